```python
import math
import jax, jax.numpy as jnp
from jax import lax
import numpy as np

D_MODEL = 1024
BATCH = 8
SEQ = 2048
DEPTH = 1

HEAD_DIM = 64
ATT_WIDTH = D_MODEL // 2
RWKV_WIDTH = D_MODEL - ATT_WIDTH
ATT_HEADS = ATT_WIDTH // HEAD_DIM
RWKV_HEADS = RWKV_WIDTH // HEAD_DIM
MIX_WIDTH = ATT_WIDTH + RWKV_WIDTH
DILATED_PATTERNS = ((128, 1), (512, 4), (2048, 16))
DECAY_LORA = 64
AAA_LORA = 64
GATE_LORA = 128
SHIFT_WIDTH = 3 * RWKV_WIDTH + DECAY_LORA + AAA_LORA + GATE_LORA
IN_WIDTH = 3 * ATT_WIDTH + SHIFT_WIDTH
PEER_N_KEYS = 128
PEER_N_EXPERTS = PEER_N_KEYS * PEER_N_KEYS
PEER_HEADS = 8
PEER_KEY_DIM = 256
PEER_TOPK = 16
PEER_TOKEN_BLOCK = 128
NORM_EPS = 1e-6
LNX_EPS = 64e-5
MASK_VALUE = -1e30

kernel_name = "hybrid_dilated_attn_rwkv7_peer_block"


def rms_norm(x, g, eps=NORM_EPS):
    x32 = x.astype(jnp.float32)
    y = x32 * lax.rsqrt(jnp.mean(x32 * x32, axis=-1, keepdims=True) + eps)
    return y * g.astype(jnp.float32)


def alibi_slopes(n_heads):
    return jnp.exp2(-8.0 * (jnp.arange(n_heads, dtype=jnp.float32) + 1.0) / n_heads)


def dilated_window_attention(q, k, v, slopes, window, dilation):
    B, S, H, E = q.shape
    d = dilation
    n = window // (2 * d)
    L = S // d
    nb = -(-L // n)
    Lp = nb * n
    qs = jnp.pad(q.reshape(B, L, d, H, E), ((0, 0), (0, Lp - L), (0, 0), (0, 0), (0, 0)))
    qs = qs.reshape(B, nb, n, d, H, E)

    def neighbourhood(t):
        tp = jnp.pad(t.reshape(B, L, d, H, E), ((0, 0), (n, Lp - L + n), (0, 0), (0, 0), (0, 0)))
        tp = tp.reshape(B, nb + 2, n, d, H, E)
        return jnp.concatenate([tp[:, :-2], tp[:, 1:-1], tp[:, 2:]], axis=2)

    kb = neighbourhood(k)
    vb = neighbourhood(v)
    qi = jnp.arange(nb)[:, None] * n + jnp.arange(n)[None, :]
    kj = (jnp.arange(nb)[:, None] - 1) * n + jnp.arange(3 * n)[None, :]
    off = kj[:, None, :] - qi[:, :, None]
    valid = (jnp.abs(off) <= n) & (kj[:, None, :] >= 0) & (kj[:, None, :] < L)
    dist = (d * jnp.abs(off)).astype(jnp.float32)

    s = jnp.einsum('bnqrhe,bnkrhe->bhrnqk', qs, kb) * (E ** -0.5)
    s = s - slopes[None, :, None, None, None, None] * dist[None, None, None]
    s = jnp.where(valid[None, None, None], s, MASK_VALUE)
    m = jnp.max(s, axis=-1, keepdims=True)
    p = jnp.exp(s - m)
    den = jnp.sum(p, axis=-1)
    lse = m[..., 0] + jnp.log(den)
    out = jnp.einsum('bhrnqk,bnkrhe->bnqrhe', p, vb)
    out = out / jnp.transpose(den, (0, 3, 4, 2, 1))[..., None]
    out = out.reshape(B, Lp, d, H, E)[:, :L].reshape(B, S, H, E)
    lse = jnp.transpose(lse, (0, 3, 4, 2, 1)).reshape(B, Lp, d, H)[:, :L].reshape(B, S, H)
    return out, lse


def dilated_mixture_attention(q, k, v):
    slopes = alibi_slopes(q.shape[2])
    outs, lses = [], []
    for window, dilation in DILATED_PATTERNS:
        o, l = dilated_window_attention(q, k, v, slopes, window, dilation)
        outs.append(o)
        lses.append(l)
    wts = jax.nn.softmax(jnp.stack(lses, axis=0), axis=0)
    return jnp.einsum('pbsh,pbshe->bshe', wts, jnp.stack(outs, axis=0))


def token_shift(z, mu_prev, mu_next):
    z_prev = jnp.pad(z, ((0, 0), (1, 0), (0, 0)))[:, :-1]
    z_next = jnp.pad(z, ((0, 0), (0, 1), (0, 0)))[:, 1:]
    return z + mu_prev * (z_prev - z) + mu_next * (z_next - z)


def wkv7_scan(r, w, k, v, kk, a, reverse):
    B, S, H, N = r.shape
    xs = tuple(jnp.moveaxis(t, 1, 0) for t in (r, w, k, v, kk, a))
    state0 = jnp.zeros((B, H, N, N), jnp.float32)

    def step(state, inp):
        r_t, w_t, k_t, v_t, kk_t, a_t = inp
        sa = jnp.einsum('bhvk,bhk->bhv', state, -kk_t)
        state = (state * w_t[:, :, None, :]
                 + sa[..., :, None] * (kk_t * a_t)[..., None, :]
                 + v_t[..., :, None] * k_t[..., None, :])
        y = jnp.einsum('bhvk,bhk->bhv', state, r_t)
        return state, y

    _, ys = lax.scan(step, state0, xs, reverse=reverse)
    return jnp.moveaxis(ys, 0, 1)


def rwkv7_bidirectional(r, k, v, xw, xa, xg, w_decay0, w_decay_up, a_gate0, a_gate_up,
                        g_up, k_k, k_a, r_k, lnx_g, lnx_b):
    B, S, _ = r.shape

    def hs(t):
        return t.reshape(B, S, RWKV_HEADS, HEAD_DIM)

    g = jax.nn.sigmoid(xg) @ g_up
    kk = hs(k * k_k)
    kk = kk * lax.rsqrt(jnp.maximum(jnp.sum(kk * kk, axis=-1, keepdims=True), 1e-12))
    ys, a_dirs = [], []
    for direction in range(2):
        w = -jax.nn.softplus(-(w_decay0[direction] + jnp.tanh(xw) @ w_decay_up[direction])) - 0.5
        decay = jnp.exp(-jnp.exp(w))
        a = jax.nn.sigmoid(a_gate0[direction] + xa @ a_gate_up[direction])
        k_dir = k * (1.0 + (a - 1.0) * k_a)
        ys.append(wkv7_scan(hs(r), hs(decay), hs(k_dir), hs(v), kk, hs(a), reverse=(direction == 1)))
        a_dirs.append(a)
    y = ys[0] + ys[1]
    mean = jnp.mean(y, axis=-1, keepdims=True)
    var = jnp.mean(jnp.square(y - mean), axis=-1, keepdims=True)
    yn = ((y - mean) * lax.rsqrt(var + LNX_EPS)).reshape(B, S, RWKV_WIDTH) * lnx_g + lnx_b
    k_bonus = hs(k * (1.0 + (0.5 * (a_dirs[0] + a_dirs[1]) - 1.0) * k_a))
    bonus = jnp.sum(hs(r) * k_bonus * r_k, axis=-1, keepdims=True) * hs(v)
    return (yn + bonus.reshape(B, S, RWKV_WIDTH)) * g


def peer_ffn(h, w_query, sub_keys1, sub_keys2, expert_u, expert_v):
    B, S, D = h.shape
    T = B * S
    half = PEER_KEY_DIM // 2
    x = h.reshape(T, D)
    q = (x @ w_query).astype(jnp.float32).reshape(T, PEER_HEADS, PEER_KEY_DIM)
    s1 = jnp.einsum('thd,nd->thn', q[..., :half], sub_keys1.astype(jnp.float32))
    s2 = jnp.einsum('thd,nd->thn', q[..., half:], sub_keys2.astype(jnp.float32))
    v1, i1 = lax.top_k(s1, PEER_TOPK)
    v2, i2 = lax.top_k(s2, PEER_TOPK)
    cand = (v1[..., :, None] + v2[..., None, :]).reshape(T, PEER_HEADS, PEER_TOPK * PEER_TOPK)
    top_s, top_p = lax.top_k(cand, PEER_TOPK)
    e_idx = (jnp.take_along_axis(i1, top_p // PEER_TOPK, axis=-1) * PEER_N_KEYS
             + jnp.take_along_axis(i2, top_p % PEER_TOPK, axis=-1))
    gate = jax.nn.softmax(top_s, axis=-1)
    HK = PEER_HEADS * PEER_TOPK
    nblk = T // PEER_TOKEN_BLOCK
    xb = x.reshape(nblk, PEER_TOKEN_BLOCK, D)
    ib = e_idx.reshape(nblk, PEER_TOKEN_BLOCK, HK)
    gb = gate.reshape(nblk, PEER_TOKEN_BLOCK, HK)

    def block(args):
        xc, ic, gc = args
        u = jnp.take(expert_u, ic, axis=0)
        act = jax.nn.gelu(jnp.einsum('cd,ced->ce', xc, u).astype(jnp.float32), approximate=False)
        vv = jnp.take(expert_v, ic, axis=0)
        return jnp.einsum('ce,ced->cd', (gc * act).astype(vv.dtype), vv)

    y = lax.map(block, (xb, ib, gb))
    return y.reshape(B, S, D)


def setup_inputs(seed: int = 0) -> dict:
    key = jax.random.key(seed)
    ks = jax.random.split(key, 32)
    f32 = jnp.float32
    nrm = lambda k, shape, s: jax.random.normal(k, shape, f32) * s
    RW = RWKV_WIDTH
    return {
        "x": nrm(ks[0], (BATCH, SEQ, D_MODEL), 1.0),
        "c": nrm(ks[1], (BATCH, D_MODEL), 1.0),
        "ada_w": nrm(ks[2], (DEPTH, D_MODEL, 6 * D_MODEL), 0.5 * D_MODEL ** -0.5),
        "ada_b": nrm(ks[3], (DEPTH, 6 * D_MODEL), 0.1),
        "norm1_g": 1.0 + nrm(ks[4], (DEPTH, D_MODEL), 0.05),
        "w_in": nrm(ks[5], (DEPTH, D_MODEL, IN_WIDTH), D_MODEL ** -0.5),
        "mu_prev": jax.random.uniform(ks[6], (DEPTH, SHIFT_WIDTH), f32, 0.0, 0.5),
        "mu_next": jax.random.uniform(ks[7], (DEPTH, SHIFT_WIDTH), f32, 0.0, 0.5),
        "q_norm_g": 1.0 + nrm(ks[8], (DEPTH, HEAD_DIM), 0.05),
        "k_norm_g": 1.0 + nrm(ks[9], (DEPTH, HEAD_DIM), 0.05),
        "w_decay0": jax.random.uniform(ks[10], (DEPTH, 2, RW), f32, -5.0, -1.0),
        "w_decay_up": nrm(ks[11], (DEPTH, 2, DECAY_LORA, RW), 0.5 * DECAY_LORA ** -0.5),
        "a_gate0": nrm(ks[12], (DEPTH, 2, RW), 0.5),
        "a_gate_up": nrm(ks[13], (DEPTH, 2, AAA_LORA, RW), AAA_LORA ** -0.5),
        "g_up": nrm(ks[14], (DEPTH, GATE_LORA, RW), GATE_LORA ** -0.5),
        "k_k": 0.85 + nrm(ks[15], (DEPTH, RW), 0.05),
        "k_a": 1.0 + nrm(ks[16], (DEPTH, RW), 0.05),
        "r_k": nrm(ks[17], (DEPTH, RWKV_HEADS, HEAD_DIM), 0.1),
        "lnx_g": 1.0 + nrm(ks[18], (DEPTH, RW), 0.05),
        "lnx_b": nrm(ks[19], (DEPTH, RW), 0.01),
        "w_out": nrm(ks[20], (DEPTH, MIX_WIDTH, D_MODEL), MIX_WIDTH ** -0.5),
        "norm2_g": 1.0 + nrm(ks[21], (DEPTH, D_MODEL), 0.05),
        "peer_w_query": nrm(ks[22], (DEPTH, D_MODEL, PEER_HEADS * PEER_KEY_DIM), D_MODEL ** -0.5),
        "peer_sub_keys1": nrm(ks[23], (DEPTH, PEER_N_KEYS, PEER_KEY_DIM // 2), (PEER_KEY_DIM // 2) ** -0.5),
        "peer_sub_keys2": nrm(ks[24], (DEPTH, PEER_N_KEYS, PEER_KEY_DIM // 2), (PEER_KEY_DIM // 2) ** -0.5),
        "peer_u": nrm(ks[25], (DEPTH, PEER_N_EXPERTS, D_MODEL), D_MODEL ** -0.5),
        "peer_v": nrm(ks[26], (DEPTH, PEER_N_EXPERTS, D_MODEL), (PEER_HEADS * PEER_TOPK) ** -0.5),
    }


def reference(x, c, ada_w, ada_b, norm1_g, w_in, mu_prev, mu_next, q_norm_g, k_norm_g,
              w_decay0, w_decay_up, a_gate0, a_gate_up, g_up, k_k, k_a, r_k, lnx_g, lnx_b,
              w_out, norm2_g, peer_w_query, peer_sub_keys1, peer_sub_keys2, peer_u, peer_v):
    in_dtype = x.dtype
    B, S, _ = x.shape
    A = ATT_WIDTH
    RW = RWKV_WIDTH
    for l in range(DEPTH):
        mod = (jax.nn.silu(c.astype(jnp.float32)) @ ada_w[l].astype(jnp.float32)
               + ada_b[l].astype(jnp.float32))[:, None, :]
        shift1, scale1, gate1, shift2, scale2, gate2 = jnp.split(mod, 6, axis=-1)

        h = rms_norm(x, norm1_g[l]) * (1.0 + scale1) + shift1
        z = (h.astype(in_dtype) @ w_in[l]).astype(jnp.float32)
        qa = rms_norm(z[..., :A].reshape(B, S, ATT_HEADS, HEAD_DIM), q_norm_g[l])
        ka = rms_norm(z[..., A:2 * A].reshape(B, S, ATT_HEADS, HEAD_DIM), k_norm_g[l])
        va = z[..., 2 * A:3 * A].reshape(B, S, ATT_HEADS, HEAD_DIM)
        att = dilated_mixture_attention(qa, ka, va).reshape(B, S, A)

        zr = token_shift(z[..., 3 * A:], mu_prev[l].astype(jnp.float32), mu_next[l].astype(jnp.float32))
        o1, o2, o3 = RW, 2 * RW, 3 * RW
        o4, o5 = o3 + DECAY_LORA, o3 + DECAY_LORA + AAA_LORA
        rw = rwkv7_bidirectional(
            zr[..., :o1], zr[..., o1:o2], zr[..., o2:o3],
            zr[..., o3:o4], zr[..., o4:o5], zr[..., o5:],
            w_decay0[l].astype(jnp.float32), w_decay_up[l].astype(jnp.float32),
            a_gate0[l].astype(jnp.float32), a_gate_up[l].astype(jnp.float32),
            g_up[l].astype(jnp.float32), k_k[l].astype(jnp.float32), k_a[l].astype(jnp.float32),
            r_k[l].astype(jnp.float32), lnx_g[l].astype(jnp.float32), lnx_b[l].astype(jnp.float32))
        mixed = jnp.concatenate([att, rw], axis=-1).astype(in_dtype) @ w_out[l]
        x = (x.astype(jnp.float32) + gate1 * mixed.astype(jnp.float32)).astype(in_dtype)

        h2 = (rms_norm(x, norm2_g[l]) * (1.0 + scale2) + shift2).astype(in_dtype)
        ff = peer_ffn(h2, peer_w_query[l], peer_sub_keys1[l], peer_sub_keys2[l], peer_u[l], peer_v[l])
        x = (x.astype(jnp.float32) + gate2 * ff.astype(jnp.float32)).astype(in_dtype)
    return x
```

```python
import functools
import math

import jax
import jax.numpy as jnp
from jax import lax
from jax.experimental import pallas as pl
from jax.experimental.pallas import tpu as pltpu

F32 = jnp.float32
BF16 = jnp.bfloat16

HEAD_DIM = 64
ATT_WIDTH = 512
RWKV_WIDTH = 512
DECAY_LORA = 64
AAA_LORA = 64
GATE_LORA = 128
SHIFT_WIDTH = 3 * RWKV_WIDTH + DECAY_LORA + AAA_LORA + GATE_LORA
PEER_N_KEYS = 128
PEER_HEADS = 8
PEER_TOPK = 16
NORM_EPS = 1e-6
LNX_EPS = 64e-5
MASK_VALUE = -1e30
ATT_HALF_WINDOWS = ((1, 64), (4, 256), (16, 1024))

LANES = 128
CHUNK = 64
ATT_BLOCK = 128
ATT_REACH = 1024 // ATT_BLOCK
VMEM_LIMIT_BYTES = 56 * 1024 * 1024

NN = (((1,), (0,)), ((), ()))
NT = (((1,), (1,)), ((), ()))
TN = (((0,), (0,)), ((), ()))

SLAB_R, SLAB_V, SLAB_KK, SLAB_DIR0, SLAB_G, SLAB_BONUS, NUM_SLABS = 0, 1, 2, 3, 9, 10, 11


def _params(*sem):
    return pltpu.CompilerParams(dimension_semantics=sem, vmem_limit_bytes=VMEM_LIMIT_BYTES)


def _dot(a, b, dims=NN):
    return lax.dot_general(a, b, dims, preferred_element_type=F32)


def _split(a):
    hi = a.astype(BF16)
    lo = (a - hi.astype(F32)).astype(BF16)
    return hi, lo


def _mm(a, b, dims=NN, passes=3):
    if passes == 1:
        return _dot(a.astype(BF16), b.astype(BF16), dims)
    ah, al = _split(a)
    bh, bl = _split(b)
    return _dot(ah, bh, dims) + (_dot(ah, bl, dims) + _dot(al, bh, dims))


def _mm_exact_rhs(a, b_exact):
    ah, al = _split(a)
    return _dot(ah, b_exact) + _dot(al, b_exact)


def _sigmoid(x):
    return 1.0 / (1.0 + jnp.exp(-x))


def _ada_kernel(c_ref, w_ref, b_ref, o_ref):
    c = c_ref[...]
    o_ref[...] = _mm(c * _sigmoid(c), w_ref[...]) + b_ref[...]


def _ada(c, ada_w, ada_b):
    bsz, d = c.shape
    n = ada_w.shape[1]
    tn = 1024
    return pl.pallas_call(
        _ada_kernel,
        out_shape=jax.ShapeDtypeStruct((bsz, n), F32),
        grid=(n // tn,),
        in_specs=[pl.BlockSpec((bsz, d), lambda j: (0, 0)),
                  pl.BlockSpec((d, tn), lambda j: (0, j)),
                  pl.BlockSpec((1, tn), lambda j: (0, j))],
        out_specs=pl.BlockSpec((bsz, tn), lambda j: (0, j)),
        compiler_params=_params("arbitrary"),
        name="ada",
    )(c, ada_w, ada_b.reshape(1, n))


def _inproj_kernel(x_ref, mod_ref, g1_ref, w_ref, gq_ref, gk_ref, bd_ref, qkv_ref, zrw_ref):
    x = x_ref[0]
    mod = mod_ref[0]
    ms = jnp.mean(x * x, axis=-1, keepdims=True)
    h = x * lax.rsqrt(ms + NORM_EPS) * g1_ref[...]
    h = (h * (1.0 + mod[1:2]) + mod[0:1]).astype(BF16)
    bd = bd_ref[...]

    def head_norm(z, g):
        ss = _mm_exact_rhs(z * z, bd) * (1.0 / HEAD_DIM)
        return z * lax.rsqrt(ss + NORM_EPS) * g

    a = ATT_WIDTH
    zq = _dot(h, w_ref[:, 0:a])
    qkv_ref[0, :, 0:a] = (head_norm(zq, gq_ref[...]) * (HEAD_DIM ** -0.5)).astype(BF16)
    zk = _dot(h, w_ref[:, a:2 * a])
    qkv_ref[0, :, a:2 * a] = head_norm(zk, gk_ref[...]).astype(BF16)
    qkv_ref[0, :, 2 * a:3 * a] = _dot(h, w_ref[:, 2 * a:3 * a]).astype(BF16)
    zrw_ref[0] = _dot(h, w_ref[:, 3 * a:])


def _inproj(x, mod6, norm1_g, w_in_bf, gq, gk, bd512, tm=256):
    bsz, s, d = x.shape
    nin = w_in_bf.shape[1]
    return pl.pallas_call(
        _inproj_kernel,
        out_shape=(jax.ShapeDtypeStruct((bsz, s, 3 * ATT_WIDTH), BF16),
                   jax.ShapeDtypeStruct((bsz, s, SHIFT_WIDTH), F32)),
        grid=(bsz, s // tm),
        in_specs=[pl.BlockSpec((1, tm, d), lambda b, i: (b, i, 0)),
                  pl.BlockSpec((1, 6, d), lambda b, i: (b, 0, 0)),
                  pl.BlockSpec((1, d), lambda b, i: (0, 0)),
                  pl.BlockSpec((d, nin), lambda b, i: (0, 0)),
                  pl.BlockSpec((1, ATT_WIDTH), lambda b, i: (0, 0)),
                  pl.BlockSpec((1, ATT_WIDTH), lambda b, i: (0, 0)),
                  pl.BlockSpec((ATT_WIDTH, ATT_WIDTH), lambda b, i: (0, 0))],
        out_specs=(pl.BlockSpec((1, tm, 3 * ATT_WIDTH), lambda b, i: (b, i, 0)),
                   pl.BlockSpec((1, tm, SHIFT_WIDTH), lambda b, i: (b, i, 0))),
        compiler_params=_params("arbitrary", "arbitrary"),
        name="inproj",
    )(x, mod6, norm1_g, w_in_bf, gq, gk, bd512)


def _attn_bias_table(n_heads):
    nd = 2 * ATT_REACH + 1
    dd = jnp.arange(nd, dtype=jnp.int32)[:, None, None]
    r = jnp.arange(ATT_BLOCK, dtype=jnp.int32)[None, :, None]
    c = jnp.arange(ATT_BLOCK, dtype=jnp.int32)[None, None, :]
    dt = r - c - (dd - ATT_REACH) * ATT_BLOCK
    adt = jnp.abs(dt)
    mult = jnp.zeros(dt.shape, F32)
    for dil, half in ATT_HALF_WINDOWS:
        mult = mult + ((adt <= half) & (dt % dil == 0)).astype(F32)
    logm = jnp.where(mult > 0, jnp.log(jnp.maximum(mult, 1.0)), MASK_VALUE)
    slopes = jnp.exp2(-8.0 * (jnp.arange(n_heads, dtype=F32) + 1.0) / n_heads)
    return logm[None] - slopes[:, None, None, None] * adt.astype(F32)[None]


def _attn_kernel(q_ref, k_ref, v_ref, bias_ref, o_ref, *, nkb):
    qi = pl.program_id(2)
    q = q_ref[0]
    lane = lax.broadcasted_iota(jnp.int32, (ATT_BLOCK, LANES), 1)
    first = lane < HEAD_DIM
    zero = jnp.zeros_like(q)
    qs = (jnp.where(first, q, zero), jnp.where(first, zero, q))
    lo = jnp.maximum(qi - ATT_REACH, 0)
    hi = jnp.minimum(qi + ATT_REACH, nkb - 1) + 1

    def body(kj, carry):
        kb = k_ref[0, pl.ds(kj * ATT_BLOCK, ATT_BLOCK), :]
        vb = v_ref[0, pl.ds(kj * ATT_BLOCK, ATT_BLOCK), :]
        dd = kj - qi + ATT_REACH
        new = []
        for hh in range(2):
            m, l, acc = carry[hh]
            s = _dot(qs[hh], kb, NT) + bias_ref[hh, dd]
            m_new = jnp.maximum(m, jnp.max(s, axis=-1, keepdims=True))
            alpha = jnp.exp(m - m_new)
            p = jnp.exp(s - m_new)
            l = alpha * l + jnp.sum(p, axis=-1, keepdims=True)
            acc = alpha * acc + _dot(p.astype(BF16), vb)
            new.append((m_new, l, acc))
        return tuple(new)

    init = tuple((jnp.full((ATT_BLOCK, 1), MASK_VALUE, F32), jnp.zeros((ATT_BLOCK, 1), F32),
                  jnp.zeros((ATT_BLOCK, LANES), F32)) for _ in range(2))
    (_, l0, a0), (_, l1, a1) = lax.fori_loop(lo, hi, body, init)
    o_ref[0] = jnp.where(first, a0 / l0, a1 / l1).astype(o_ref.dtype)


def _attention(qkv, bias):
    bsz, s, _ = qkv.shape
    nkb = s // ATT_BLOCK
    npair = ATT_WIDTH // LANES
    nd = 2 * ATT_REACH + 1
    return pl.pallas_call(
        functools.partial(_attn_kernel, nkb=nkb),
        out_shape=jax.ShapeDtypeStruct((bsz, s, ATT_WIDTH), BF16),
        grid=(npair, bsz, nkb),
        in_specs=[pl.BlockSpec((1, ATT_BLOCK, LANES), lambda hp, b, i: (b, i, hp)),
                  pl.BlockSpec((1, s, LANES), lambda hp, b, i: (b, 0, npair + hp)),
                  pl.BlockSpec((1, s, LANES), lambda hp, b, i: (b, 0, 2 * npair + hp)),
                  pl.BlockSpec((2, nd, ATT_BLOCK, ATT_BLOCK), lambda hp, b, i: (hp, 0, 0, 0))],
        out_specs=pl.BlockSpec((1, ATT_BLOCK, LANES), lambda hp, b, i: (b, i, hp)),
        compiler_params=_params("arbitrary", "arbitrary", "arbitrary"),
        name="attn",
    )(qkv, qkv, qkv, bias)


def _prep_kernel(z_ref, zp_ref, zn_ref, mup_ref, mun_ref, wup_ref, aup_ref, gup_ref, w0_ref, a0_ref,
                 kk_ref, ka_ref, rk_ref, bd_ref, o_ref, *, tq):
    i = pl.program_id(1)
    last = pl.num_programs(1) - 1
    z = z_ref[0]
    row = lax.broadcasted_iota(jnp.int32, (tq, 1), 0)
    prev_row = zp_ref[0, 7:8, :] * (i > 0).astype(F32)
    next_row = zn_ref[0, 0:1, :] * (i < last).astype(F32)
    zp = jnp.where(row == 0, prev_row, pltpu.roll(z, 1, axis=0))
    zn = jnp.where(row == tq - 1, next_row, pltpu.roll(z, tq - 1, axis=0))
    zs = z + mup_ref[...] * (zp - z) + mun_ref[...] * (zn - z)

    w = RWKV_WIDTH
    r = zs[:, 0:w]
    k = zs[:, w:2 * w]
    v = zs[:, 2 * w:3 * w]
    xwa = zs[:, 3 * w:3 * w + LANES]
    xg = zs[:, 3 * w + LANES:]
    bd = bd_ref[...]

    g = _dot(_sigmoid(xg).astype(BF16), gup_ref[...])
    kk = k * kk_ref[...]
    ss = _mm_exact_rhs(kk * kk, bd)
    kk = kk * lax.rsqrt(jnp.maximum(ss, 1e-12))
    ka = ka_ref[...]
    txw = jnp.tanh(xwa).astype(BF16)
    xab = xwa.astype(BF16)

    o_ref[0, :, SLAB_R * w:(SLAB_R + 1) * w] = r
    o_ref[0, :, SLAB_V * w:(SLAB_V + 1) * w] = v
    o_ref[0, :, SLAB_KK * w:(SLAB_KK + 1) * w] = kk
    o_ref[0, :, SLAB_G * w:(SLAB_G + 1) * w] = g
    a_sum = jnp.zeros_like(k)
    for d in range(2):
        y = w0_ref[d:d + 1, :] + _dot(txw, wup_ref[d])
        wlog = -(jnp.maximum(-y, 0.0) + jnp.log(1.0 + jnp.exp(-jnp.abs(y)))) - 0.5
        a = _sigmoid(a0_ref[d:d + 1, :] + _dot(xab, aup_ref[d]))
        a_sum = a_sum + a
        base = (SLAB_DIR0 + 3 * d) * w
        o_ref[0, :, base:base + w] = -jnp.exp(wlog)
        o_ref[0, :, base + w:base + 2 * w] = k * (1.0 + (a - 1.0) * ka)
        o_ref[0, :, base + 2 * w:base + 3 * w] = kk * a
    k_bonus = k * (1.0 + (0.5 * a_sum - 1.0) * ka)
    bsum = _mm_exact_rhs(r * k_bonus * rk_ref[...], bd)
    o_ref[0, :, SLAB_BONUS * w:(SLAB_BONUS + 1) * w] = bsum * v


def _rwkv_prep(zrw, mu_prev, mu_next, wup_pad, aup_pad, g_up_bf, w0, a0, k_k, k_a, r_k, bd512, tq=256):
    bsz, s, sw = zrw.shape
    w = RWKV_WIDTH
    nt = s // tq
    row = lambda b, i: (0, 0)
    return pl.pallas_call(
        functools.partial(_prep_kernel, tq=tq),
        out_shape=jax.ShapeDtypeStruct((bsz, s, NUM_SLABS * w), F32),
        grid=(bsz, nt),
        in_specs=[pl.BlockSpec((1, tq, sw), lambda b, i: (b, i, 0)),
                  pl.BlockSpec((1, 8, sw), lambda b, i: (b, jnp.maximum(i * (tq // 8) - 1, 0), 0)),
                  pl.BlockSpec((1, 8, sw), lambda b, i: (b, jnp.minimum((i + 1) * (tq // 8), s // 8 - 1), 0)),
                  pl.BlockSpec((1, sw), row),
                  pl.BlockSpec((1, sw), row),
                  pl.BlockSpec((2, LANES, w), lambda b, i: (0, 0, 0)),
                  pl.BlockSpec((2, LANES, w), lambda b, i: (0, 0, 0)),
                  pl.BlockSpec((GATE_LORA, w), row),
                  pl.BlockSpec((2, w), row),
                  pl.BlockSpec((2, w), row),
                  pl.BlockSpec((1, w), row),
                  pl.BlockSpec((1, w), row),
                  pl.BlockSpec((1, w), row),
                  pl.BlockSpec((w, w), row)],
        out_specs=pl.BlockSpec((1, tq, NUM_SLABS * w), lambda b, i: (b, i, 0)),
        compiler_params=_params("arbitrary", "arbitrary"),
        name="rwkv_prep",
    )(zrw, zrw, zrw, mu_prev, mu_next, wup_pad, aup_pad, g_up_bf, w0, a0, k_k, k_a, r_k, bd512)


def _block_diag(x, first):
    zero = jnp.zeros_like(x)
    return jnp.concatenate([jnp.where(first, x, zero), jnp.where(first, zero, x)], axis=0)


def _p1_kernel(r_ref, v_ref, kk_ref, lw_ref, kd_ref, be_ref, rp_ref, qp_ref, g_ref, h_ref, *, cpb):
    c = CHUNK
    sign = 1 - 2 * pl.program_id(2)
    rowi = lax.broadcasted_iota(jnp.int32, (c, LANES), 0)
    lane = lax.broadcasted_iota(jnp.int32, (c, LANES), 1)
    coli = lane & (c - 1)
    first = lane < HEAD_DIM
    before = sign * (rowi - coli)
    strict2 = before > 0
    incl2 = before >= 0
    eye2 = (coli == rowi).astype(F32)
    r64 = lax.broadcasted_iota(jnp.int32, (c, c), 0)
    c64 = lax.broadcasted_iota(jnp.int32, (c, c), 1)
    cum = (sign * (r64 - c64) >= 0).astype(BF16)
    r128 = lax.broadcasted_iota(jnp.int32, (LANES, LANES), 0)
    c128 = lax.broadcasted_iota(jnp.int32, (LANES, LANES), 1)
    same_head = (r128 // HEAD_DIM) == (c128 // HEAD_DIM)
    eye128 = (r128 == c128).astype(F32)
    zero128 = jnp.zeros((LANES, LANES), F32)

    for j in range(cpb):
        sl = pl.ds(j * c, c)
        r = r_ref[0, sl, :]
        v = v_ref[0, sl, :]
        kk = kk_ref[0, sl, :]
        lw = lw_ref[0, sl, :]
        kd = kd_ref[0, sl, :]
        be = be_ref[0, sl, :]

        l1 = lw.astype(BF16)
        rem = lw - l1.astype(F32)
        l2 = rem.astype(BF16)
        l3 = (rem - l2.astype(F32)).astype(BF16)
        cs = _dot(cum, l1) + (_dot(cum, l2) + _dot(cum, l3))
        g_incl = jnp.exp(cs)
        g_excl = jnp.exp(cs - lw)
        g_inv = jnp.exp(-cs)
        g_tot = jnp.exp(jnp.sum(lw, axis=0, keepdims=True))

        ab = -kk * g_excl
        rb = r * g_incl
        bt = be * g_inv
        kt = kd * g_inv

        lhs = jnp.concatenate([ab, rb], axis=0)
        sb = _mm(lhs, _block_diag(bt, first), NT)
        sk = _mm(lhs, _block_diag(kt, first), NT)
        m_ab = jnp.where(strict2, sb[:c], 0.0)
        n_rb = jnp.where(incl2, sb[c:], 0.0)
        m_ak = jnp.where(strict2, sk[:c], 0.0)
        n_rk = jnp.where(incl2, sk[c:], 0.0)

        t = eye2 + m_ab
        p = m_ab
        for _ in range(int(math.log2(c)) - 1):
            p = _mm(p, _block_diag(p, first))
            t = t + _mm(t, _block_diag(p, first))

        vbd = _block_diag(v, first)
        pm = _mm(m_ak, vbd)
        ap = _mm(t, _block_diag(ab, first))
        pp = _mm(t, _block_diag(pm, first))
        rp_ref[0, 0, 0, sl, :] = rb + _mm(n_rb, _block_diag(ap, first))
        qp_ref[0, 0, 0, sl, :] = _mm(n_rb, _block_diag(pp, first)) + _mm(n_rk, vbd)
        gmat = eye128 + jnp.where(same_head, _mm(ap, bt, TN), zero128)
        hmat = jnp.where(same_head, _mm(pp, bt, TN) + _mm(v, kt, TN), zero128)
        g_ref[0, 0, 0, j] = gmat * g_tot
        h_ref[0, 0, 0, j] = hmat * g_tot


def _rwkv_p1(pk, cpb=4):
    bsz, s, _ = pk.shape
    nc = s // CHUNK
    npair = RWKV_WIDTH // LANES
    rows = cpb * CHUNK

    def slab(sidx):
        return pl.BlockSpec((1, rows, LANES), lambda b, hp, d, ci: (b, ci, sidx * npair + hp))

    def dslab(off):
        return pl.BlockSpec((1, rows, LANES),
                            lambda b, hp, d, ci: (b, ci, (SLAB_DIR0 + 3 * d + off) * npair + hp))

    seq = pl.BlockSpec((1, 1, 1, rows, LANES), lambda b, hp, d, ci: (b, hp, d, ci, 0))
    mat = pl.BlockSpec((1, 1, 1, cpb, LANES, LANES), lambda b, hp, d, ci: (b, hp, d, ci, 0, 0))
    return pl.pallas_call(
        functools.partial(_p1_kernel, cpb=cpb),
        out_shape=(jax.ShapeDtypeStruct((bsz, npair, 2, s, LANES), F32),
                   jax.ShapeDtypeStruct((bsz, npair, 2, s, LANES), F32),
                   jax.ShapeDtypeStruct((bsz, npair, 2, nc, LANES, LANES), F32),
                   jax.ShapeDtypeStruct((bsz, npair, 2, nc, LANES, LANES), F32)),
        grid=(bsz, npair, 2, nc // cpb),
        in_specs=[slab(SLAB_R), slab(SLAB_V), slab(SLAB_KK), dslab(0), dslab(1), dslab(2)],
        out_specs=(seq, seq, mat, mat),
        compiler_params=_params("arbitrary", "arbitrary", "arbitrary", "arbitrary"),
        name="rwkv_p1",
    )(pk, pk, pk, pk, pk, pk)


def _p2_kernel(rp_ref, qp_ref, g_ref, h_ref, gate_ref, bonus_ref, lng_ref, lnb_ref, bd_ref, o_ref,
               yf_ref, yb_ref, *, nc, te):
    c = CHUNK

    def step(j, carry):
        sf, sb = carry
        jf = j
        jb = nc - 1 - j
        slf = pl.ds(pl.multiple_of(jf * c, c), c)
        slb = pl.ds(pl.multiple_of(jb * c, c), c)
        yf_ref[slf, :] = _mm(rp_ref[0, 0, 0, slf, :], sf, NT) + qp_ref[0, 0, 0, slf, :]
        yb_ref[slb, :] = _mm(rp_ref[0, 0, 1, slb, :], sb, NT) + qp_ref[0, 0, 1, slb, :]
        sf = _mm(sf, g_ref[0, 0, 0, jf]) + h_ref[0, 0, 0, jf]
        sb = _mm(sb, g_ref[0, 0, 1, jb]) + h_ref[0, 0, 1, jb]
        return sf, sb

    zero = jnp.zeros((LANES, LANES), F32)
    lax.fori_loop(0, nc, step, (zero, zero))

    bd = bd_ref[...]
    lng = lng_ref[...]
    lnb = lnb_ref[...]
    inv = 1.0 / HEAD_DIM

    def epi(i, _):
        sl = pl.ds(pl.multiple_of(i * te, te), te)
        y = yf_ref[sl, :] + yb_ref[sl, :]
        mean = _mm_exact_rhs(y, bd) * inv
        yc = y - mean
        var = _mm_exact_rhs(yc * yc, bd) * inv
        yn = yc * lax.rsqrt(var + LNX_EPS) * lng + lnb
        o_ref[0, sl, :] = ((yn + bonus_ref[0, sl, :]) * gate_ref[0, sl, :]).astype(o_ref.dtype)
        return 0

    lax.fori_loop(0, (nc * c) // te, epi, 0)


def _rwkv_p2(rp, qp, gm, hm, pk, lnx_g, lnx_b, bd128, te=256):
    bsz, npair, _, s, _ = rp.shape
    nc = s // CHUNK
    seq = pl.BlockSpec((1, 1, 2, s, LANES), lambda b, hp: (b, hp, 0, 0, 0))
    mat = pl.BlockSpec((1, 1, 2, nc, LANES, LANES), lambda b, hp: (b, hp, 0, 0, 0, 0))
    return pl.pallas_call(
        functools.partial(_p2_kernel, nc=nc, te=min(te, s)),
        out_shape=jax.ShapeDtypeStruct((bsz, s, RWKV_WIDTH), BF16),
        grid=(bsz, npair),
        in_specs=[seq, seq, mat, mat,
                  pl.BlockSpec((1, s, LANES), lambda b, hp: (b, 0, SLAB_G * npair + hp)),
                  pl.BlockSpec((1, s, LANES), lambda b, hp: (b, 0, SLAB_BONUS * npair + hp)),
                  pl.BlockSpec((1, LANES), lambda b, hp: (0, hp)),
                  pl.BlockSpec((1, LANES), lambda b, hp: (0, hp)),
                  pl.BlockSpec((LANES, LANES), lambda b, hp: (0, 0))],
        out_specs=pl.BlockSpec((1, s, LANES), lambda b, hp: (b, 0, hp)),
        scratch_shapes=[pltpu.VMEM((s, LANES), F32), pltpu.VMEM((s, LANES), F32)],
        compiler_params=_params("arbitrary", "arbitrary"),
        name="rwkv_p2",
    )(rp, qp, gm, hm, pk, pk, lnx_g, lnx_b, bd128)


def _outproj_kernel(att_ref, rw_ref, x_ref, mod_ref, g2_ref, wo_ref, wq_ref, k1_ref, k2_ref,
                    x1_ref, h2_ref, s1_ref, s2_ref):
    mod = mod_ref[0]
    a = ATT_WIDTH
    mixed = _dot(att_ref[0], wo_ref[0:a, :]) + _dot(rw_ref[0], wo_ref[a:, :])
    x1 = x_ref[0] + mod[2:3] * mixed
    x1_ref[0] = x1
    ms = jnp.mean(x1 * x1, axis=-1, keepdims=True)
    h2 = x1 * lax.rsqrt(ms + NORM_EPS) * g2_ref[...]
    h2 = (h2 * (1.0 + mod[4:5]) + mod[3:4]).astype(BF16)
    h2_ref[0] = h2
    q = _dot(h2, wq_ref[...])
    k1 = k1_ref[...]
    k2 = k2_ref[...]
    for h in range(PEER_HEADS):
        base = h * 2 * LANES
        s1_ref[h] = _dot(k1, q[:, base:base + LANES].astype(BF16), NT)
        s2_ref[h] = _dot(k2, q[:, base + LANES:base + 2 * LANES].astype(BF16), NT)


def _outproj(att, rw, x, mod6, norm2_g, w_out_bf, wq_bf, k1_bf, k2_bf, tm=256):
    bsz, s, d = x.shape
    nt = s // tm
    t = bsz * s
    nq = wq_bf.shape[1]
    const = lambda b, i: (0, 0)
    tok = pl.BlockSpec((PEER_HEADS, PEER_N_KEYS, tm), lambda b, i: (0, 0, b * nt + i))
    return pl.pallas_call(
        _outproj_kernel,
        out_shape=(jax.ShapeDtypeStruct((bsz, s, d), F32),
                   jax.ShapeDtypeStruct((bsz, s, d), BF16),
                   jax.ShapeDtypeStruct((PEER_HEADS, PEER_N_KEYS, t), F32),
                   jax.ShapeDtypeStruct((PEER_HEADS, PEER_N_KEYS, t), F32)),
        grid=(bsz, nt),
        in_specs=[pl.BlockSpec((1, tm, ATT_WIDTH), lambda b, i: (b, i, 0)),
                  pl.BlockSpec((1, tm, RWKV_WIDTH), lambda b, i: (b, i, 0)),
                  pl.BlockSpec((1, tm, d), lambda b, i: (b, i, 0)),
                  pl.BlockSpec((1, 6, d), lambda b, i: (b, 0, 0)),
                  pl.BlockSpec((1, d), const),
                  pl.BlockSpec((d, d), const),
                  pl.BlockSpec((d, nq), const),
                  pl.BlockSpec((PEER_N_KEYS, LANES), const),
                  pl.BlockSpec((PEER_N_KEYS, LANES), const)],
        out_specs=(pl.BlockSpec((1, tm, d), lambda b, i: (b, i, 0)),
                   pl.BlockSpec((1, tm, d), lambda b, i: (b, i, 0)),
                   tok, tok),
        compiler_params=_params("arbitrary", "arbitrary"),
        name="outproj",
    )(att, rw, x, mod6, norm2_g, w_out_bf, wq_bf, k1_bf, k2_bf)


def _topk_kernel(s1_ref, s2_ref, d1_ref, e1_ref, e2_ref, cand_ref, *, tn):
    kk = PEER_TOPK
    neg = -jnp.inf
    rows = lax.broadcasted_iota(jnp.int32, (kk, tn), 0)

    def top_values(x):
        def it(i, carry):
            x, out, _ = carry
            m = jnp.max(x, axis=0, keepdims=True)
            return jnp.where(x >= m, neg, x), jnp.where(rows == i, m, out), m
        init = jnp.max(x, axis=0, keepdims=True)
        _, out, m17 = lax.fori_loop(0, kk + 1, it, (x, jnp.full((kk, tn), neg, F32), init))
        return out, m17

    def per_head(h, _):
        s1 = s1_ref[h]
        s2 = s2_ref[h]
        a, a17 = top_values(s1)
        b, b17 = top_values(s2)
        for i in range(kk):
            cand_ref[i * kk:(i + 1) * kk, :] = a[i:i + 1, :] + b

        def it(i, carry):
            m = jnp.max(cand_ref[...], axis=0, keepdims=True)
            cand_ref[...] = jnp.where(cand_ref[...] >= m, neg, cand_ref[...])
            return m, carry[0]
        a0 = a[0:1, :]
        b0 = b[0:1, :]
        top = jnp.max(cand_ref[...], axis=0, keepdims=True)
        nxt, tau = lax.fori_loop(0, kk + 1, it, (top, top))
        nxt = jnp.maximum(nxt, jnp.maximum(a17 + b0, a0 + b17))
        thr = 0.5 * (tau + nxt)

        z = None
        for i in range(kk):
            ci = a[i:i + 1, :] + b
            zi = jnp.sum(jnp.where(ci >= tau, jnp.exp(ci - (a0 + b0)), 0.0), axis=0, keepdims=True)
            z = zi if z is None else z + zi
        d1_ref[h] = thr - s1
        e1_ref[h] = jnp.exp(s1 - a0) / z
        e2_ref[h] = jnp.exp(s2 - b0)
        return 0

    lax.fori_loop(0, PEER_HEADS, per_head, 0)


def _peer_topk(s1t, s2t, tn=256):
    nh, nk, t = s1t.shape
    blk = pl.BlockSpec((nh, nk, tn), lambda i: (0, 0, i))
    shp = jax.ShapeDtypeStruct((nh, nk, t), F32)
    return pl.pallas_call(
        functools.partial(_topk_kernel, tn=tn),
        out_shape=(shp, shp, shp),
        grid=(t // tn,),
        in_specs=[blk, blk],
        out_specs=(blk, blk, blk),
        scratch_shapes=[pltpu.VMEM((PEER_TOPK * PEER_TOPK, tn), F32)],
        compiler_params=_params("arbitrary"),
        name="peer_topk",
    )(s1t, s2t)


def _peer_kernel(h2_ref, u_ref, vt_ref, d1_ref, e1_ref, s2_ref, e2_ref, x1_ref, mod_ref, o_ref,
                 acc_ref, act_ref, p_ref, *, tn, te):
    ei = pl.program_id(1)
    nk = PEER_N_KEYS

    @pl.when(ei == 0)
    def _():
        acc_ref[...] = jnp.zeros_like(acc_ref)

    act_ref[...] = _dot(u_ref[...], h2_ref[...], NT)
    ni = te // nk
    i0 = pl.multiple_of(ei * ni, ni)
    for tg in range(tn // LANES):
        tl = slice(tg * LANES, (tg + 1) * LANES)
        d1s = [d1_ref[h, pl.ds(i0, ni), tl] for h in range(PEER_HEADS)]
        e1s = [e1_ref[h, pl.ds(i0, ni), tl] for h in range(PEER_HEADS)]
        for il in range(ni):
            gate = jnp.zeros((nk, LANES), F32)
            for h in range(PEER_HEADS):
                d1 = d1s[h][il:il + 1, :]
                e1 = e1s[h][il:il + 1, :]
                gate = gate + jnp.where(s2_ref[h, :, tl] >= d1, e2_ref[h, :, tl] * e1, 0.0)
            a = act_ref[il * nk:(il + 1) * nk, tl]
            gelu = 0.5 * a * (1.0 + lax.erf(a * (2.0 ** -0.5)))
            p_ref[il * nk:(il + 1) * nk, tl] = (gate * gelu).astype(BF16)
    acc_ref[...] += _dot(vt_ref[...], p_ref[...])

    @pl.when(ei == pl.num_programs(1) - 1)
    def _():
        o_ref[...] = x1_ref[...] + mod_ref[0, 5:6, :] * acc_ref[...].T


def _peer_ffn(h2, u_bf, vt_bf, d1, e1, s2t, e2, x1, mod6, seq, tn=512, te=1024):
    t, d = h2.shape
    ne = u_bf.shape[0]
    stat = pl.BlockSpec((PEER_HEADS, PEER_N_KEYS, tn), lambda ti, ei: (0, 0, ti))
    return pl.pallas_call(
        functools.partial(_peer_kernel, tn=tn, te=te),
        out_shape=jax.ShapeDtypeStruct((t, d), F32),
        grid=(t // tn, ne // te),
        in_specs=[pl.BlockSpec((tn, d), lambda ti, ei: (ti, 0)),
                  pl.BlockSpec((te, d), lambda ti, ei: (ei, 0)),
                  pl.BlockSpec((d, te), lambda ti, ei: (0, ei)),
                  stat, stat, stat, stat,
                  pl.BlockSpec((tn, d), lambda ti, ei: (ti, 0)),
                  pl.BlockSpec((1, 6, d), lambda ti, ei: ((ti * tn) // seq, 0, 0))],
        out_specs=pl.BlockSpec((tn, d), lambda ti, ei: (ti, 0)),
        scratch_shapes=[pltpu.VMEM((d, tn), F32), pltpu.VMEM((te, tn), F32), pltpu.VMEM((te, tn), BF16)],
        compiler_params=_params("arbitrary", "arbitrary"),
        name="peer_ffn",
    )(h2, u_bf, vt_bf, d1, e1, s2t, e2, x1, mod6)


def _head_block_diag(n):
    idx = jnp.arange(n, dtype=jnp.int32) // HEAD_DIM
    return (idx[:, None] == idx[None, :]).astype(BF16)


def _layer(x, c, ada_w, ada_b, norm1_g, w_in, mu_prev, mu_next, q_norm_g, k_norm_g, w_decay0, w_decay_up,
           a_gate0, a_gate_up, g_up, k_k, k_a, r_k, lnx_g, lnx_b, w_out, norm2_g, peer_w_query,
           peer_sub_keys1, peer_sub_keys2, peer_u, peer_v):
    bsz, s, d = x.shape
    n_att_heads = ATT_WIDTH // HEAD_DIM
    bd512 = _head_block_diag(ATT_WIDTH)
    bd128 = _head_block_diag(LANES)

    mod6 = _ada(c, ada_w, ada_b).reshape(bsz, 6, d)
    gq = jnp.tile(q_norm_g, n_att_heads).reshape(1, ATT_WIDTH)
    gk = jnp.tile(k_norm_g, n_att_heads).reshape(1, ATT_WIDTH)
    qkv, zrw = _inproj(x, mod6, norm1_g.reshape(1, d), w_in.astype(BF16), gq, gk, bd512)

    att = _attention(qkv, _attn_bias_table(n_att_heads))

    zero_w = jnp.zeros((2, LANES - DECAY_LORA, RWKV_WIDTH), F32)
    wup_pad = jnp.concatenate([w_decay_up, zero_w], axis=1).astype(BF16)
    aup_pad = jnp.concatenate([zero_w, a_gate_up], axis=1).astype(BF16)
    pk = _rwkv_prep(zrw, mu_prev.reshape(1, -1), mu_next.reshape(1, -1), wup_pad, aup_pad, g_up.astype(BF16),
                    w_decay0, a_gate0, k_k.reshape(1, -1), k_a.reshape(1, -1), r_k.reshape(1, -1), bd512)
    rp, qp, gm, hm = _rwkv_p1(pk)
    rw = _rwkv_p2(rp, qp, gm, hm, pk, lnx_g.reshape(1, -1), lnx_b.reshape(1, -1), bd128)

    x1, h2, s1t, s2t = _outproj(att, rw, x, mod6, norm2_g.reshape(1, d), w_out.astype(BF16),
                                peer_w_query.astype(BF16), peer_sub_keys1.astype(BF16),
                                peer_sub_keys2.astype(BF16))
    d1, e1, e2 = _peer_topk(s1t, s2t)
    out = _peer_ffn(h2.reshape(bsz * s, d), peer_u.astype(BF16), peer_v.T.astype(BF16), d1, e1, s2t, e2,
                    x1.reshape(bsz * s, d), mod6, s)
    return out.reshape(bsz, s, d)


def kernel(x, c, ada_w, ada_b, norm1_g, w_in, mu_prev, mu_next, q_norm_g, k_norm_g, w_decay0, w_decay_up,
           a_gate0, a_gate_up, g_up, k_k, k_a, r_k, lnx_g, lnx_b, w_out, norm2_g, peer_w_query,
           peer_sub_keys1, peer_sub_keys2, peer_u, peer_v):
    depth = ada_w.shape[0]
    for l in range(depth):
        x = _layer(x, c, ada_w[l], ada_b[l], norm1_g[l], w_in[l], mu_prev[l], mu_next[l], q_norm_g[l],
                   k_norm_g[l], w_decay0[l], w_decay_up[l], a_gate0[l], a_gate_up[l], g_up[l], k_k[l], k_a[l],
                   r_k[l], lnx_g[l], lnx_b[l], w_out[l], norm2_g[l], peer_w_query[l], peer_sub_keys1[l],
                   peer_sub_keys2[l], peer_u[l], peer_v[l])
    return x
```

```python
import functools
import math

import jax
import jax.numpy as jnp
from jax import lax
from jax.experimental import pallas as pl
from jax.experimental.pallas import tpu as pltpu

F32 = jnp.float32
BF16 = jnp.bfloat16

HEAD_DIM = 64
ATT_WIDTH = 512
RWKV_WIDTH = 512
DECAY_LORA = 64
AAA_LORA = 64
GATE_LORA = 128
SHIFT_WIDTH = 3 * RWKV_WIDTH + DECAY_LORA + AAA_LORA + GATE_LORA
PEER_N_KEYS = 128
PEER_HEADS = 8
PEER_TOPK = 16
NORM_EPS = 1e-6
LNX_EPS = 64e-5
MASK_VALUE = -1e30
LOG2E = 1.4426950408889634
ATT_HALF_WINDOWS = ((1, 64), (4, 256), (16, 1024))

LANES = 128
CHUNK = 64
ATT_BLOCK = 128
ATT_REACH = 1024 // ATT_BLOCK
VMEM_LIMIT_BYTES = 56 * 1024 * 1024
WKV_PASSES_SCORE = 1
WKV_PASSES_INV = 1
WKV_PASSES_APPLY = 1
WKV_PASSES_STATE = 1
WKV_PASSES_CHAIN = 3

NN = (((1,), (0,)), ((), ()))
NT = (((1,), (1,)), ((), ()))
TN = (((0,), (0,)), ((), ()))

SLAB_R, SLAB_V, SLAB_KK, SLAB_DIR0, SLAB_G, SLAB_BONUS, NUM_SLABS = 0, 1, 2, 3, 9, 10, 11


def _params(*sem):
    return pltpu.CompilerParams(dimension_semantics=sem, vmem_limit_bytes=VMEM_LIMIT_BYTES)


def _dot(a, b, dims=NN):
    return lax.dot_general(a, b, dims, preferred_element_type=F32)


def _split(a):
    hi = a.astype(BF16)
    lo = (a - hi.astype(F32)).astype(BF16)
    return hi, lo


def _mm(a, b, dims=NN, passes=3):
    if passes == 1:
        return _dot(a.astype(BF16), b.astype(BF16), dims)
    ah, al = _split(a)
    bh, bl = _split(b)
    return _dot(ah, bh, dims) + (_dot(ah, bl, dims) + _dot(al, bh, dims))


def _mm_exact_rhs(a, b_exact):
    ah, al = _split(a)
    return _dot(ah, b_exact) + _dot(al, b_exact)


def _sigmoid(x):
    return 1.0 / (1.0 + jnp.exp(-x))


def _ada_kernel(c_ref, w_ref, b_ref, o_ref):
    c = c_ref[...]
    o_ref[...] = _mm(c * _sigmoid(c), w_ref[...]) + b_ref[...]


def _ada(c, ada_w, ada_b):
    bsz, d = c.shape
    n = ada_w.shape[1]
    tn = 1024
    return pl.pallas_call(
        _ada_kernel,
        out_shape=jax.ShapeDtypeStruct((bsz, n), F32),
        grid=(n // tn,),
        in_specs=[pl.BlockSpec((bsz, d), lambda j: (0, 0)),
                  pl.BlockSpec((d, tn), lambda j: (0, j)),
                  pl.BlockSpec((1, tn), lambda j: (0, j))],
        out_specs=pl.BlockSpec((bsz, tn), lambda j: (0, j)),
        compiler_params=_params("arbitrary"),
        name="ada",
    )(c, ada_w, ada_b.reshape(1, n))


def _inproj_kernel(x_ref, mod_ref, g1_ref, w_ref, gq_ref, gk_ref, bd_ref, qkv_ref, zrw_ref):
    x = x_ref[0]
    mod = mod_ref[0]
    ms = jnp.mean(x * x, axis=-1, keepdims=True)
    h = x * lax.rsqrt(ms + NORM_EPS) * g1_ref[...]
    h = (h * (1.0 + mod[1:2]) + mod[0:1]).astype(BF16)
    bd = bd_ref[...]

    def head_norm(z, g):
        ss = _mm_exact_rhs(z * z, bd) * (1.0 / HEAD_DIM)
        return z * lax.rsqrt(ss + NORM_EPS) * g

    a = ATT_WIDTH
    zq = _dot(h, w_ref[:, 0:a])
    qkv_ref[0, :, 0:a] = (head_norm(zq, gq_ref[...]) * (HEAD_DIM ** -0.5 * LOG2E)).astype(BF16)
    zk = _dot(h, w_ref[:, a:2 * a])
    qkv_ref[0, :, a:2 * a] = head_norm(zk, gk_ref[...]).astype(BF16)
    qkv_ref[0, :, 2 * a:3 * a] = _dot(h, w_ref[:, 2 * a:3 * a]).astype(BF16)
    zrw_ref[0] = _dot(h, w_ref[:, 3 * a:])


def _inproj(x, mod6, norm1_g, w_in_bf, gq, gk, bd512, tm=256):
    bsz, s, d = x.shape
    nin = w_in_bf.shape[1]
    return pl.pallas_call(
        _inproj_kernel,
        out_shape=(jax.ShapeDtypeStruct((bsz, s, 3 * ATT_WIDTH), BF16),
                   jax.ShapeDtypeStruct((bsz, s, SHIFT_WIDTH), F32)),
        grid=(bsz, s // tm),
        in_specs=[pl.BlockSpec((1, tm, d), lambda b, i: (b, i, 0)),
                  pl.BlockSpec((1, 6, d), lambda b, i: (b, 0, 0)),
                  pl.BlockSpec((1, d), lambda b, i: (0, 0)),
                  pl.BlockSpec((d, nin), lambda b, i: (0, 0)),
                  pl.BlockSpec((1, ATT_WIDTH), lambda b, i: (0, 0)),
                  pl.BlockSpec((1, ATT_WIDTH), lambda b, i: (0, 0)),
                  pl.BlockSpec((ATT_WIDTH, ATT_WIDTH), lambda b, i: (0, 0))],
        out_specs=(pl.BlockSpec((1, tm, 3 * ATT_WIDTH), lambda b, i: (b, i, 0)),
                   pl.BlockSpec((1, tm, SHIFT_WIDTH), lambda b, i: (b, i, 0))),
        compiler_params=_params("arbitrary", "arbitrary"),
        name="inproj",
    )(x, mod6, norm1_g, w_in_bf, gq, gk, bd512)


def _attn_window(nkb):
    return nkb if nkb <= 2 * ATT_REACH + 2 else 2 * ATT_REACH + 2


def _attn_bias_table(n_heads, nkb):
    reach = _attn_window(nkb) - 1
    nd = 2 * reach + 1
    dd = jnp.arange(nd, dtype=jnp.int32)[None, :, None]
    r = jnp.arange(ATT_BLOCK, dtype=jnp.int32)[:, None, None]
    c = jnp.arange(ATT_BLOCK, dtype=jnp.int32)[None, None, :]
    dt = r - c - (dd - reach) * ATT_BLOCK
    adt = jnp.abs(dt)
    mult = jnp.zeros(dt.shape, F32)
    for dil, half in ATT_HALF_WINDOWS:
        mult = mult + ((adt <= half) & (dt % dil == 0)).astype(F32)
    logm = jnp.where(mult > 0, jnp.log2(jnp.maximum(mult, 1.0)), MASK_VALUE)
    slopes = jnp.exp2(-8.0 * (jnp.arange(n_heads, dtype=F32) + 1.0) / n_heads) * LOG2E
    bias = logm[None] - slopes[:, None, None, None] * adt.astype(F32)[None]
    return bias.reshape(n_heads, ATT_BLOCK, nd * ATT_BLOCK)


def _attn_kernel(q_ref, k_ref, v_ref, bias_ref, o_ref, s_ref, *, nkb, win):
    qi = pl.program_id(2)
    blk = ATT_BLOCK
    q = q_ref[0]
    lane = lax.broadcasted_iota(jnp.int32, (blk, LANES), 1)
    first = lane < HEAD_DIM
    zero = jnp.zeros_like(q)
    qab = jnp.concatenate([jnp.where(first, q, zero), jnp.where(first, zero, q)], axis=0)
    ks = 0 if win == nkb else jnp.clip(qi - ATT_REACH, 0, nkb - win)
    reach = win - 1

    m = [jnp.full((blk, LANES), -jnp.inf, F32) for _ in range(2)]
    for w0 in range(0, win, 2):
        kj0 = ks + w0
        kslab = k_ref[0, pl.ds(pl.multiple_of(kj0 * blk, blk), 2 * blk), :]
        s = _dot(qab, kslab, NT)
        x0 = pl.multiple_of((kj0 - qi + reach) * blk, blk)
        for hh in range(2):
            sb = s[hh * blk:(hh + 1) * blk] + bias_ref[hh, :, pl.ds(x0, 2 * blk)]
            s_ref[hh, :, w0 * blk:(w0 + 2) * blk] = sb
            m[hh] = jnp.maximum(m[hh], jnp.maximum(sb[:, :blk], sb[:, blk:]))
    mrow = [jnp.max(m[hh], axis=-1, keepdims=True) for hh in range(2)]

    first2 = lax.broadcasted_iota(jnp.int32, (2 * blk, LANES), 1) < HEAD_DIM
    acc = [jnp.zeros((blk, LANES), F32) for _ in range(2)]
    for w0 in range(0, win, 2):
        vslab = v_ref[0, pl.ds(pl.multiple_of((ks + w0) * blk, blk), 2 * blk), :]
        one = jnp.ones_like(vslab)
        vs = (jnp.where(first2, vslab, one), jnp.where(first2, one, vslab))
        for hh in range(2):
            p = jnp.exp2(s_ref[hh, :, w0 * blk:(w0 + 2) * blk] - mrow[hh]).astype(BF16)
            acc[hh] = acc[hh] + _dot(p, vs[hh])
    num = jnp.where(first, acc[0], acc[1])
    den = jnp.where(first, pltpu.roll(acc[0], HEAD_DIM, axis=1), pltpu.roll(acc[1], HEAD_DIM, axis=1))
    o_ref[0] = (num / den).astype(o_ref.dtype)


def _attention(qkv, bias):
    bsz, s, _ = qkv.shape
    nkb = s // ATT_BLOCK
    win = _attn_window(nkb)
    assert nkb % 2 == 0 and bias.shape[2] == (2 * win - 1) * ATT_BLOCK
    npair = ATT_WIDTH // LANES
    return pl.pallas_call(
        functools.partial(_attn_kernel, nkb=nkb, win=win),
        out_shape=jax.ShapeDtypeStruct((bsz, s, ATT_WIDTH), BF16),
        grid=(npair, bsz, nkb),
        in_specs=[pl.BlockSpec((1, ATT_BLOCK, LANES), lambda hp, b, i: (b, i, hp)),
                  pl.BlockSpec((1, s, LANES), lambda hp, b, i: (b, 0, npair + hp)),
                  pl.BlockSpec((1, s, LANES), lambda hp, b, i: (b, 0, 2 * npair + hp)),
                  pl.BlockSpec((2, ATT_BLOCK, bias.shape[2]), lambda hp, b, i: (hp, 0, 0))],
        out_specs=pl.BlockSpec((1, ATT_BLOCK, LANES), lambda hp, b, i: (b, i, hp)),
        scratch_shapes=[pltpu.VMEM((2, ATT_BLOCK, win * ATT_BLOCK), F32)],
        compiler_params=_params("arbitrary", "arbitrary", "arbitrary"),
        name="attn",
    )(qkv, qkv, qkv, bias)


def _prep_kernel(z_ref, zp_ref, zn_ref, mup_ref, mun_ref, wup_ref, aup_ref, gup_ref, w0_ref, a0_ref,
                 kk_ref, ka_ref, rk_ref, bd_ref, o_ref, *, tq):
    i = pl.program_id(1)
    last = pl.num_programs(1) - 1
    z = z_ref[0]
    row = lax.broadcasted_iota(jnp.int32, (tq, 1), 0)
    prev_row = zp_ref[0, 7:8, :] * (i > 0).astype(F32)
    next_row = zn_ref[0, 0:1, :] * (i < last).astype(F32)
    zp = jnp.where(row == 0, prev_row, pltpu.roll(z, 1, axis=0))
    zn = jnp.where(row == tq - 1, next_row, pltpu.roll(z, tq - 1, axis=0))
    zs = z + mup_ref[...] * (zp - z) + mun_ref[...] * (zn - z)

    w = RWKV_WIDTH
    r = zs[:, 0:w]
    k = zs[:, w:2 * w]
    v = zs[:, 2 * w:3 * w]
    xwa = zs[:, 3 * w:3 * w + LANES]
    xg = zs[:, 3 * w + LANES:]
    bd = bd_ref[...]

    g = _dot(_sigmoid(xg).astype(BF16), gup_ref[...])
    kk = k * kk_ref[...]
    ss = _mm_exact_rhs(kk * kk, bd)
    kk = kk * lax.rsqrt(jnp.maximum(ss, 1e-12))
    ka = ka_ref[...]
    txw = jnp.tanh(xwa).astype(BF16)
    xab = xwa.astype(BF16)

    o_ref[0, :, SLAB_R * w:(SLAB_R + 1) * w] = r
    o_ref[0, :, SLAB_V * w:(SLAB_V + 1) * w] = v
    o_ref[0, :, SLAB_KK * w:(SLAB_KK + 1) * w] = kk
    o_ref[0, :, SLAB_G * w:(SLAB_G + 1) * w] = g
    a_sum = jnp.zeros_like(k)
    for d in range(2):
        y = w0_ref[d:d + 1, :] + _dot(txw, wup_ref[d])
        wlog = -(jnp.maximum(-y, 0.0) + jnp.log(1.0 + jnp.exp(-jnp.abs(y)))) - 0.5
        a = _sigmoid(a0_ref[d:d + 1, :] + _dot(xab, aup_ref[d]))
        a_sum = a_sum + a
        base = (SLAB_DIR0 + 3 * d) * w
        o_ref[0, :, base:base + w] = -jnp.exp(wlog)
        o_ref[0, :, base + w:base + 2 * w] = k * (1.0 + (a - 1.0) * ka)
        o_ref[0, :, base + 2 * w:base + 3 * w] = kk * a
    k_bonus = k * (1.0 + (0.5 * a_sum - 1.0) * ka)
    bsum = _mm_exact_rhs(r * k_bonus * rk_ref[...], bd)
    o_ref[0, :, SLAB_BONUS * w:(SLAB_BONUS + 1) * w] = bsum * v


def _rwkv_prep(zrw, mu_prev, mu_next, wup_pad, aup_pad, g_up_bf, w0, a0, k_k, k_a, r_k, bd512, tq=256):
    bsz, s, sw = zrw.shape
    w = RWKV_WIDTH
    nt = s // tq
    row = lambda b, i: (0, 0)
    return pl.pallas_call(
        functools.partial(_prep_kernel, tq=tq),
        out_shape=jax.ShapeDtypeStruct((bsz, s, NUM_SLABS * w), F32),
        grid=(bsz, nt),
        in_specs=[pl.BlockSpec((1, tq, sw), lambda b, i: (b, i, 0)),
                  pl.BlockSpec((1, 8, sw), lambda b, i: (b, jnp.maximum(i * (tq // 8) - 1, 0), 0)),
                  pl.BlockSpec((1, 8, sw), lambda b, i: (b, jnp.minimum((i + 1) * (tq // 8), s // 8 - 1), 0)),
                  pl.BlockSpec((1, sw), row),
                  pl.BlockSpec((1, sw), row),
                  pl.BlockSpec((2, LANES, w), lambda b, i: (0, 0, 0)),
                  pl.BlockSpec((2, LANES, w), lambda b, i: (0, 0, 0)),
                  pl.BlockSpec((GATE_LORA, w), row),
                  pl.BlockSpec((2, w), row),
                  pl.BlockSpec((2, w), row),
                  pl.BlockSpec((1, w), row),
                  pl.BlockSpec((1, w), row),
                  pl.BlockSpec((1, w), row),
                  pl.BlockSpec((w, w), row)],
        out_specs=pl.BlockSpec((1, tq, NUM_SLABS * w), lambda b, i: (b, i, 0)),
        compiler_params=_params("arbitrary", "arbitrary"),
        name="rwkv_prep",
    )(zrw, zrw, zrw, mu_prev, mu_next, wup_pad, aup_pad, g_up_bf, w0, a0, k_k, k_a, r_k, bd512)


def _block_diag(x, first):
    zero = jnp.zeros_like(x)
    return jnp.concatenate([jnp.where(first, x, zero), jnp.where(first, zero, x)], axis=0)


def _p1_kernel(r_ref, v_ref, kk_ref, lw_ref, kd_ref, be_ref, rp_ref, qp_ref, g_ref, h_ref, *, cpb):
    c = CHUNK
    sign = 1 - 2 * pl.program_id(2)
    rowi = lax.broadcasted_iota(jnp.int32, (c, LANES), 0)
    lane = lax.broadcasted_iota(jnp.int32, (c, LANES), 1)
    coli = lane & (c - 1)
    first = lane < HEAD_DIM
    before = sign * (rowi - coli)
    strict2 = before > 0
    incl2 = before >= 0
    eye2 = (coli == rowi).astype(F32)
    r64 = lax.broadcasted_iota(jnp.int32, (c, c), 0)
    c64 = lax.broadcasted_iota(jnp.int32, (c, c), 1)
    cum = (sign * (r64 - c64) >= 0).astype(BF16)
    r128 = lax.broadcasted_iota(jnp.int32, (LANES, LANES), 0)
    c128 = lax.broadcasted_iota(jnp.int32, (LANES, LANES), 1)
    same_head = (r128 // HEAD_DIM) == (c128 // HEAD_DIM)
    eye128 = (r128 == c128).astype(F32)
    zero128 = jnp.zeros((LANES, LANES), F32)

    js = range(cpb)
    sls = [pl.ds(j * c, c) for j in js]
    bdg = lambda x: _block_diag(x, first)
    r = [r_ref[0, sl, :] for sl in sls]
    v = [v_ref[0, sl, :] for sl in sls]
    kk = [kk_ref[0, sl, :] for sl in sls]
    lw = [lw_ref[0, sl, :] for sl in sls]
    kd = [kd_ref[0, sl, :] for sl in sls]
    be = [be_ref[0, sl, :] for sl in sls]

    def cumsum(x):
        l1 = x.astype(BF16)
        rem = x - l1.astype(F32)
        l2 = rem.astype(BF16)
        l3 = (rem - l2.astype(F32)).astype(BF16)
        return _dot(cum, l1) + (_dot(cum, l2) + _dot(cum, l3))

    cs = [cumsum(lw[j]) for j in js]
    g_inv = [jnp.exp(-cs[j]) for j in js]
    g_tot = [jnp.exp(jnp.sum(lw[j], axis=0, keepdims=True)) for j in js]
    ab = [-kk[j] * jnp.exp(cs[j] - lw[j]) for j in js]
    rb = [r[j] * jnp.exp(cs[j]) for j in js]
    bt = [be[j] * g_inv[j] for j in js]
    kt = [kd[j] * g_inv[j] for j in js]

    lhs = [jnp.concatenate([ab[j], rb[j]], axis=0) for j in js]
    sb = [_mm(lhs[j], bdg(bt[j]), NT, WKV_PASSES_SCORE) for j in js]
    sk = [_mm(lhs[j], bdg(kt[j]), NT, WKV_PASSES_SCORE) for j in js]
    m_ab = [jnp.where(strict2, sb[j][:c], 0.0) for j in js]
    n_rb = [jnp.where(incl2, sb[j][c:], 0.0) for j in js]
    m_ak = [jnp.where(strict2, sk[j][:c], 0.0) for j in js]
    n_rk = [jnp.where(incl2, sk[j][c:], 0.0) for j in js]

    t = [eye2 + m_ab[j] for j in js]
    p = m_ab
    for _ in range(int(math.log2(c)) - 1):
        p = [_mm(p[j], bdg(p[j]), NN, WKV_PASSES_INV) for j in js]
        t = [t[j] + _mm(t[j], bdg(p[j]), NN, WKV_PASSES_INV) for j in js]
    res = [eye2 - _mm(eye2 - m_ab[j], bdg(t[j]), NN, 3) for j in js]
    t = [t[j] + _mm(t[j], bdg(res[j]), NN, 1) for j in js]

    vbd = [bdg(v[j]) for j in js]
    pm = [_mm(m_ak[j], vbd[j], NN, WKV_PASSES_APPLY) for j in js]
    ap = [_mm(t[j], bdg(ab[j]), NN, WKV_PASSES_APPLY) for j in js]
    pp = [_mm(t[j], bdg(pm[j]), NN, WKV_PASSES_APPLY) for j in js]
    for j in js:
        rp_ref[0, 0, 0, sls[j], :] = rb[j] + _mm(n_rb[j], bdg(ap[j]), NN, WKV_PASSES_APPLY)
        qp_ref[0, 0, 0, sls[j], :] = (_mm(n_rb[j], bdg(pp[j]), NN, WKV_PASSES_APPLY)
                                      + _mm(n_rk[j], vbd[j], NN, WKV_PASSES_APPLY))
    for j in js:
        gmat = eye128 + jnp.where(same_head, _mm(ap[j], bt[j], TN, WKV_PASSES_STATE), zero128)
        hmat = jnp.where(same_head, _mm(pp[j], bt[j], TN, WKV_PASSES_STATE)
                         + _mm(v[j], kt[j], TN, WKV_PASSES_STATE), zero128)
        g_ref[0, 0, 0, j] = gmat * g_tot[j]
        h_ref[0, 0, 0, j] = hmat * g_tot[j]


def _rwkv_p1(pk, cpb=4):
    bsz, s, _ = pk.shape
    nc = s // CHUNK
    npair = RWKV_WIDTH // LANES
    rows = cpb * CHUNK

    def slab(sidx):
        return pl.BlockSpec((1, rows, LANES), lambda b, hp, d, ci: (b, ci, sidx * npair + hp))

    def dslab(off):
        return pl.BlockSpec((1, rows, LANES),
                            lambda b, hp, d, ci: (b, ci, (SLAB_DIR0 + 3 * d + off) * npair + hp))

    seq = pl.BlockSpec((1, 1, 1, rows, LANES), lambda b, hp, d, ci: (b, hp, d, ci, 0))
    mat = pl.BlockSpec((1, 1, 1, cpb, LANES, LANES), lambda b, hp, d, ci: (b, hp, d, ci, 0, 0))
    return pl.pallas_call(
        functools.partial(_p1_kernel, cpb=cpb),
        out_shape=(jax.ShapeDtypeStruct((bsz, npair, 2, s, LANES), F32),
                   jax.ShapeDtypeStruct((bsz, npair, 2, s, LANES), F32),
                   jax.ShapeDtypeStruct((bsz, npair, 2, nc, LANES, LANES), F32),
                   jax.ShapeDtypeStruct((bsz, npair, 2, nc, LANES, LANES), F32)),
        grid=(bsz, npair, 2, nc // cpb),
        in_specs=[slab(SLAB_R), slab(SLAB_V), slab(SLAB_KK), dslab(0), dslab(1), dslab(2)],
        out_specs=(seq, seq, mat, mat),
        compiler_params=_params("arbitrary", "arbitrary", "arbitrary", "arbitrary"),
        name="rwkv_p1",
    )(pk, pk, pk, pk, pk, pk)


def _p2_kernel(rp_ref, qp_ref, g_ref, h_ref, gate_ref, bonus_ref, lng_ref, lnb_ref, bd_ref, o_ref,
               yf_ref, yb_ref, *, nc, te):
    c = CHUNK

    def step(j, carry):
        sf, sb = carry
        jf = j
        jb = nc - 1 - j
        slf = pl.ds(pl.multiple_of(jf * c, c), c)
        slb = pl.ds(pl.multiple_of(jb * c, c), c)
        yf_ref[slf, :] = _mm(rp_ref[0, 0, 0, slf, :], sf, NT, WKV_PASSES_CHAIN) + qp_ref[0, 0, 0, slf, :]
        yb_ref[slb, :] = _mm(rp_ref[0, 0, 1, slb, :], sb, NT, WKV_PASSES_CHAIN) + qp_ref[0, 0, 1, slb, :]
        sf = _mm(sf, g_ref[0, 0, 0, jf], NN, WKV_PASSES_CHAIN) + h_ref[0, 0, 0, jf]
        sb = _mm(sb, g_ref[0, 0, 1, jb], NN, WKV_PASSES_CHAIN) + h_ref[0, 0, 1, jb]
        return sf, sb

    zero = jnp.zeros((LANES, LANES), F32)
    lax.fori_loop(0, nc, step, (zero, zero))

    bd = bd_ref[...]
    lng = lng_ref[...]
    lnb = lnb_ref[...]
    inv = 1.0 / HEAD_DIM

    def epi(i, _):
        sl = pl.ds(pl.multiple_of(i * te, te), te)
        y = yf_ref[sl, :] + yb_ref[sl, :]
        mean = _mm_exact_rhs(y, bd) * inv
        yc = y - mean
        var = _mm_exact_rhs(yc * yc, bd) * inv
        yn = yc * lax.rsqrt(var + LNX_EPS) * lng + lnb
        o_ref[0, sl, :] = ((yn + bonus_ref[0, sl, :]) * gate_ref[0, sl, :]).astype(o_ref.dtype)
        return 0

    lax.fori_loop(0, (nc * c) // te, epi, 0)


def _rwkv_p2(rp, qp, gm, hm, pk, lnx_g, lnx_b, bd128, te=256):
    bsz, npair, _, s, _ = rp.shape
    nc = s // CHUNK
    seq = pl.BlockSpec((1, 1, 2, s, LANES), lambda b, hp: (b, hp, 0, 0, 0))
    mat = pl.BlockSpec((1, 1, 2, nc, LANES, LANES), lambda b, hp: (b, hp, 0, 0, 0, 0))
    return pl.pallas_call(
        functools.partial(_p2_kernel, nc=nc, te=min(te, s)),
        out_shape=jax.ShapeDtypeStruct((bsz, s, RWKV_WIDTH), BF16),
        grid=(bsz, npair),
        in_specs=[seq, seq, mat, mat,
                  pl.BlockSpec((1, s, LANES), lambda b, hp: (b, 0, SLAB_G * npair + hp)),
                  pl.BlockSpec((1, s, LANES), lambda b, hp: (b, 0, SLAB_BONUS * npair + hp)),
                  pl.BlockSpec((1, LANES), lambda b, hp: (0, hp)),
                  pl.BlockSpec((1, LANES), lambda b, hp: (0, hp)),
                  pl.BlockSpec((LANES, LANES), lambda b, hp: (0, 0))],
        out_specs=pl.BlockSpec((1, s, LANES), lambda b, hp: (b, 0, hp)),
        scratch_shapes=[pltpu.VMEM((s, LANES), F32), pltpu.VMEM((s, LANES), F32)],
        compiler_params=_params("arbitrary", "arbitrary"),
        name="rwkv_p2",
    )(rp, qp, gm, hm, pk, pk, lnx_g, lnx_b, bd128)


def _outproj_kernel(att_ref, rw_ref, x_ref, mod_ref, g2_ref, wo_ref, wq_ref, k1_ref, k2_ref,
                    x1_ref, h2_ref, s1_ref, s2_ref):
    mod = mod_ref[0]
    a = ATT_WIDTH
    mixed = _dot(att_ref[0], wo_ref[0:a, :]) + _dot(rw_ref[0], wo_ref[a:, :])
    x1 = x_ref[0] + mod[2:3] * mixed
    x1_ref[0] = x1
    ms = jnp.mean(x1 * x1, axis=-1, keepdims=True)
    h2 = x1 * lax.rsqrt(ms + NORM_EPS) * g2_ref[...]
    h2 = (h2 * (1.0 + mod[4:5]) + mod[3:4]).astype(BF16)
    h2_ref[0] = h2
    q = _dot(h2, wq_ref[...])
    k1 = k1_ref[...]
    k2 = k2_ref[...]
    for h in range(PEER_HEADS):
        base = h * 2 * LANES
        s1_ref[h] = _dot(k1, q[:, base:base + LANES].astype(BF16), NT)
        s2_ref[h] = _dot(k2, q[:, base + LANES:base + 2 * LANES].astype(BF16), NT)


def _outproj(att, rw, x, mod6, norm2_g, w_out_bf, wq_bf, k1_bf, k2_bf, tm=256):
    bsz, s, d = x.shape
    nt = s // tm
    t = bsz * s
    nq = wq_bf.shape[1]
    const = lambda b, i: (0, 0)
    tok = pl.BlockSpec((PEER_HEADS, PEER_N_KEYS, tm), lambda b, i: (0, 0, b * nt + i))
    return pl.pallas_call(
        _outproj_kernel,
        out_shape=(jax.ShapeDtypeStruct((bsz, s, d), F32),
                   jax.ShapeDtypeStruct((bsz, s, d), BF16),
                   jax.ShapeDtypeStruct((PEER_HEADS, PEER_N_KEYS, t), F32),
                   jax.ShapeDtypeStruct((PEER_HEADS, PEER_N_KEYS, t), F32)),
        grid=(bsz, nt),
        in_specs=[pl.BlockSpec((1, tm, ATT_WIDTH), lambda b, i: (b, i, 0)),
                  pl.BlockSpec((1, tm, RWKV_WIDTH), lambda b, i: (b, i, 0)),
                  pl.BlockSpec((1, tm, d), lambda b, i: (b, i, 0)),
                  pl.BlockSpec((1, 6, d), lambda b, i: (b, 0, 0)),
                  pl.BlockSpec((1, d), const),
                  pl.BlockSpec((d, d), const),
                  pl.BlockSpec((d, nq), const),
                  pl.BlockSpec((PEER_N_KEYS, LANES), const),
                  pl.BlockSpec((PEER_N_KEYS, LANES), const)],
        out_specs=(pl.BlockSpec((1, tm, d), lambda b, i: (b, i, 0)),
                   pl.BlockSpec((1, tm, d), lambda b, i: (b, i, 0)),
                   tok, tok),
        compiler_params=_params("arbitrary", "arbitrary"),
        name="outproj",
    )(att, rw, x, mod6, norm2_g, w_out_bf, wq_bf, k1_bf, k2_bf)


def _topk_kernel(s1_ref, s2_ref, d1_ref, e1_ref, e2_ref, cand_ref, *, tn):
    kk = PEER_TOPK
    neg = -jnp.inf
    rows = lax.broadcasted_iota(jnp.int32, (kk, tn), 0)

    def top_values(x):
        def it(i, carry):
            x, out, _ = carry
            m = jnp.max(x, axis=0, keepdims=True)
            return jnp.where(x >= m, neg, x), jnp.where(rows == i, m, out), m
        init = jnp.max(x, axis=0, keepdims=True)
        _, out, m17 = lax.fori_loop(0, kk + 1, it, (x, jnp.full((kk, tn), neg, F32), init))
        return out, m17

    def per_head(h, _):
        s1 = s1_ref[h]
        s2 = s2_ref[h]
        a, a17 = top_values(s1)
        b, b17 = top_values(s2)
        for i in range(kk):
            cand_ref[i * kk:(i + 1) * kk, :] = a[i:i + 1, :] + b

        def it(i, carry):
            m = jnp.max(cand_ref[...], axis=0, keepdims=True)
            cand_ref[...] = jnp.where(cand_ref[...] >= m, neg, cand_ref[...])
            return m, carry[0]
        a0 = a[0:1, :]
        b0 = b[0:1, :]
        top = jnp.max(cand_ref[...], axis=0, keepdims=True)
        nxt, tau = lax.fori_loop(0, kk + 1, it, (top, top))
        nxt = jnp.maximum(nxt, jnp.maximum(a17 + b0, a0 + b17))
        thr = 0.5 * (tau + nxt)

        z = None
        for i in range(kk):
            ci = a[i:i + 1, :] + b
            zi = jnp.sum(jnp.where(ci >= tau, jnp.exp(ci - (a0 + b0)), 0.0), axis=0, keepdims=True)
            z = zi if z is None else z + zi
        d1_ref[h] = thr - s1
        e1_ref[h] = jnp.exp(s1 - a0) / z
        e2_ref[h] = jnp.exp(s2 - b0)
        return 0

    lax.fori_loop(0, PEER_HEADS, per_head, 0)


def _peer_topk(s1t, s2t, tn=256):
    nh, nk, t = s1t.shape
    blk = pl.BlockSpec((nh, nk, tn), lambda i: (0, 0, i))
    shp = jax.ShapeDtypeStruct((nh, nk, t), F32)
    return pl.pallas_call(
        functools.partial(_topk_kernel, tn=tn),
        out_shape=(shp, shp, shp),
        grid=(t // tn,),
        in_specs=[blk, blk],
        out_specs=(blk, blk, blk),
        scratch_shapes=[pltpu.VMEM((PEER_TOPK * PEER_TOPK, tn), F32)],
        compiler_params=_params("arbitrary"),
        name="peer_topk",
    )(s1t, s2t)


def _peer_kernel(h2_ref, u_ref, vt_ref, d1_ref, e1_ref, s2_ref, e2_ref, x1_ref, mod_ref, o_ref,
                 acc_ref, act_ref, p_ref, *, tn, te):
    ei = pl.program_id(1)
    nk = PEER_N_KEYS

    @pl.when(ei == 0)
    def _():
        acc_ref[...] = jnp.zeros_like(acc_ref)

    act_ref[...] = _dot(u_ref[...], h2_ref[...], NT)
    ni = te // nk
    i0 = pl.multiple_of(ei * ni, ni)
    for tg in range(tn // LANES):
        tl = slice(tg * LANES, (tg + 1) * LANES)
        d1s = [d1_ref[h, pl.ds(i0, ni), tl] for h in range(PEER_HEADS)]
        e1s = [e1_ref[h, pl.ds(i0, ni), tl] for h in range(PEER_HEADS)]
        for il in range(ni):
            gate = jnp.zeros((nk, LANES), F32)
            for h in range(PEER_HEADS):
                d1 = d1s[h][il:il + 1, :]
                e1 = e1s[h][il:il + 1, :]
                gate = gate + jnp.where(s2_ref[h, :, tl] >= d1, e2_ref[h, :, tl] * e1, 0.0)
            a = act_ref[il * nk:(il + 1) * nk, tl]
            gelu = 0.5 * a * (1.0 + lax.erf(a * (2.0 ** -0.5)))
            p_ref[il * nk:(il + 1) * nk, tl] = (gate * gelu).astype(BF16)
    acc_ref[...] += _dot(vt_ref[...], p_ref[...])

    @pl.when(ei == pl.num_programs(1) - 1)
    def _():
        o_ref[...] = x1_ref[...] + mod_ref[0, 5:6, :] * acc_ref[...].T


def _peer_ffn(h2, u_bf, vt_bf, d1, e1, s2t, e2, x1, mod6, seq, tn=512, te=1024):
    t, d = h2.shape
    ne = u_bf.shape[0]
    stat = pl.BlockSpec((PEER_HEADS, PEER_N_KEYS, tn), lambda ti, ei: (0, 0, ti))
    return pl.pallas_call(
        functools.partial(_peer_kernel, tn=tn, te=te),
        out_shape=jax.ShapeDtypeStruct((t, d), F32),
        grid=(t // tn, ne // te),
        in_specs=[pl.BlockSpec((tn, d), lambda ti, ei: (ti, 0)),
                  pl.BlockSpec((te, d), lambda ti, ei: (ei, 0)),
                  pl.BlockSpec((d, te), lambda ti, ei: (0, ei)),
                  stat, stat, stat, stat,
                  pl.BlockSpec((tn, d), lambda ti, ei: (ti, 0)),
                  pl.BlockSpec((1, 6, d), lambda ti, ei: ((ti * tn) // seq, 0, 0))],
        out_specs=pl.BlockSpec((tn, d), lambda ti, ei: (ti, 0)),
        scratch_shapes=[pltpu.VMEM((d, tn), F32), pltpu.VMEM((te, tn), F32), pltpu.VMEM((te, tn), BF16)],
        compiler_params=_params("arbitrary", "arbitrary"),
        name="peer_ffn",
    )(h2, u_bf, vt_bf, d1, e1, s2t, e2, x1, mod6)


def _head_block_diag(n):
    idx = jnp.arange(n, dtype=jnp.int32) // HEAD_DIM
    return (idx[:, None] == idx[None, :]).astype(BF16)


def _layer(x, c, ada_w, ada_b, norm1_g, w_in, mu_prev, mu_next, q_norm_g, k_norm_g, w_decay0, w_decay_up,
           a_gate0, a_gate_up, g_up, k_k, k_a, r_k, lnx_g, lnx_b, w_out, norm2_g, peer_w_query,
           peer_sub_keys1, peer_sub_keys2, peer_u, peer_v):
    bsz, s, d = x.shape
    n_att_heads = ATT_WIDTH // HEAD_DIM
    bd512 = _head_block_diag(ATT_WIDTH)
    bd128 = _head_block_diag(LANES)

    mod6 = _ada(c, ada_w, ada_b).reshape(bsz, 6, d)
    gq = jnp.tile(q_norm_g, n_att_heads).reshape(1, ATT_WIDTH)
    gk = jnp.tile(k_norm_g, n_att_heads).reshape(1, ATT_WIDTH)
    qkv, zrw = _inproj(x, mod6, norm1_g.reshape(1, d), w_in.astype(BF16), gq, gk, bd512)

    att = _attention(qkv, _attn_bias_table(n_att_heads, s // ATT_BLOCK))

    zero_w = jnp.zeros((2, LANES - DECAY_LORA, RWKV_WIDTH), F32)
    wup_pad = jnp.concatenate([w_decay_up, zero_w], axis=1).astype(BF16)
    aup_pad = jnp.concatenate([zero_w, a_gate_up], axis=1).astype(BF16)
    pk = _rwkv_prep(zrw, mu_prev.reshape(1, -1), mu_next.reshape(1, -1), wup_pad, aup_pad, g_up.astype(BF16),
                    w_decay0, a_gate0, k_k.reshape(1, -1), k_a.reshape(1, -1), r_k.reshape(1, -1), bd512)
    rp, qp, gm, hm = _rwkv_p1(pk)
    rw = _rwkv_p2(rp, qp, gm, hm, pk, lnx_g.reshape(1, -1), lnx_b.reshape(1, -1), bd128)

    x1, h2, s1t, s2t = _outproj(att, rw, x, mod6, norm2_g.reshape(1, d), w_out.astype(BF16),
                                peer_w_query.astype(BF16), peer_sub_keys1.astype(BF16),
                                peer_sub_keys2.astype(BF16))
    d1, e1, e2 = _peer_topk(s1t, s2t)
    out = _peer_ffn(h2.reshape(bsz * s, d), peer_u.astype(BF16), peer_v.T.astype(BF16), d1, e1, s2t, e2,
                    x1.reshape(bsz * s, d), mod6, s)
    return out.reshape(bsz, s, d)


def kernel(x, c, ada_w, ada_b, norm1_g, w_in, mu_prev, mu_next, q_norm_g, k_norm_g, w_decay0, w_decay_up,
           a_gate0, a_gate_up, g_up, k_k, k_a, r_k, lnx_g, lnx_b, w_out, norm2_g, peer_w_query,
           peer_sub_keys1, peer_sub_keys2, peer_u, peer_v):
    depth = ada_w.shape[0]
    for l in range(depth):
        x = _layer(x, c, ada_w[l], ada_b[l], norm1_g[l], w_in[l], mu_prev[l], mu_next[l], q_norm_g[l],
                   k_norm_g[l], w_decay0[l], w_decay_up[l], a_gate0[l], a_gate_up[l], g_up[l], k_k[l], k_a[l],
                   r_k[l], lnx_g[l], lnx_b[l], w_out[l], norm2_g[l], peer_w_query[l], peer_sub_keys1[l],
                   peer_sub_keys2[l], peer_u[l], peer_v[l])
    return x
```

```python
import functools
import math

import jax
import jax.numpy as jnp
from jax import lax
from jax.experimental import pallas as pl
from jax.experimental.pallas import tpu as pltpu

F32 = jnp.float32
BF16 = jnp.bfloat16

HEAD_DIM = 64
ATT_WIDTH = 512
RWKV_WIDTH = 512
DECAY_LORA = 64
AAA_LORA = 64
GATE_LORA = 128
SHIFT_WIDTH = 3 * RWKV_WIDTH + DECAY_LORA + AAA_LORA + GATE_LORA
PEER_N_KEYS = 128
PEER_HEADS = 8
PEER_TOPK = 16
NORM_EPS = 1e-6
LNX_EPS = 64e-5
MASK_VALUE = -1e30
LOG2E = 1.4426950408889634
ATT_HALF_WINDOWS = ((1, 64), (4, 256), (16, 1024))

LANES = 128
BF16_ROWS = 16
CHUNK = 64
ATT_BLOCK = 128
ATT_REACH = 1024 // ATT_BLOCK
VMEM_LIMIT_BYTES = 56 * 1024 * 1024
WKV_PASSES_SCORE = 1
WKV_PASSES_INV = 1
WKV_PASSES_APPLY = 1
WKV_PASSES_STATE = 1
WKV_PASSES_CHAIN = 3

NN = (((1,), (0,)), ((), ()))
NT = (((1,), (1,)), ((), ()))
TN = (((0,), (0,)), ((), ()))

SLAB_R, SLAB_V, SLAB_KK, SLAB_DIR0, SLAB_G, SLAB_BONUS, NUM_SLABS = 0, 1, 2, 3, 9, 10, 11


def _params(*sem):
    return pltpu.CompilerParams(dimension_semantics=sem, vmem_limit_bytes=VMEM_LIMIT_BYTES)


def _dot(a, b, dims=NN):
    return lax.dot_general(a, b, dims, preferred_element_type=F32)


def _split(a):
    hi = a.astype(BF16)
    lo = (a - hi.astype(F32)).astype(BF16)
    return hi, lo


def _mm(a, b, dims=NN, passes=3):
    if passes == 1:
        return _dot(a.astype(BF16), b.astype(BF16), dims)
    ah, al = _split(a)
    bh, bl = _split(b)
    return _dot(ah, bh, dims) + (_dot(ah, bl, dims) + _dot(al, bh, dims))


def _mm_exact_rhs(a, b_exact):
    ah, al = _split(a)
    return _dot(ah, b_exact) + _dot(al, b_exact)


def _sigmoid(x):
    return 1.0 / (1.0 + jnp.exp(-x))


def _ada_kernel(c_ref, w_ref, b_ref, o_ref):
    c = c_ref[...]
    o_ref[...] = _mm(c * _sigmoid(c), w_ref[...]) + b_ref[...]


def _ada(c, ada_w, ada_b):
    bsz, d = c.shape
    n = ada_w.shape[1]
    tn = 1024
    return pl.pallas_call(
        _ada_kernel,
        out_shape=jax.ShapeDtypeStruct((bsz, n), F32),
        grid=(n // tn,),
        in_specs=[pl.BlockSpec((bsz, d), lambda j: (0, 0)),
                  pl.BlockSpec((d, tn), lambda j: (0, j)),
                  pl.BlockSpec((1, tn), lambda j: (0, j))],
        out_specs=pl.BlockSpec((bsz, tn), lambda j: (0, j)),
        compiler_params=_params("arbitrary"),
        name="ada",
    )(c, ada_w, ada_b.reshape(1, n))


def _inproj_kernel(x_ref, mod_ref, g1_ref, w_ref, gq_ref, gk_ref, bd_ref, qkv_ref, zrw_ref):
    x = x_ref[0]
    mod = mod_ref[0]
    ms = jnp.mean(x * x, axis=-1, keepdims=True)
    h = x * lax.rsqrt(ms + NORM_EPS) * g1_ref[...]
    h = (h * (1.0 + mod[1:2]) + mod[0:1]).astype(BF16)
    bd = bd_ref[...]

    def head_norm(z, g):
        ss = _mm_exact_rhs(z * z, bd) * (1.0 / HEAD_DIM)
        return z * lax.rsqrt(ss + NORM_EPS) * g

    a = ATT_WIDTH
    zq = _dot(h, w_ref[:, 0:a])
    qkv_ref[0, :, 0:a] = (head_norm(zq, gq_ref[...]) * (HEAD_DIM ** -0.5 * LOG2E)).astype(BF16)
    zk = _dot(h, w_ref[:, a:2 * a])
    qkv_ref[0, :, a:2 * a] = head_norm(zk, gk_ref[...]).astype(BF16)
    qkv_ref[0, :, 2 * a:3 * a] = _dot(h, w_ref[:, 2 * a:3 * a]).astype(BF16)
    zrw_ref[0] = _dot(h, w_ref[:, 3 * a:])


def _inproj(x, mod6, norm1_g, w_in_bf, gq, gk, bd512, tm=256):
    bsz, s, d = x.shape
    nin = w_in_bf.shape[1]
    return pl.pallas_call(
        _inproj_kernel,
        out_shape=(jax.ShapeDtypeStruct((bsz, s, 3 * ATT_WIDTH), BF16),
                   jax.ShapeDtypeStruct((bsz, s, SHIFT_WIDTH), F32)),
        grid=(bsz, s // tm),
        in_specs=[pl.BlockSpec((1, tm, d), lambda b, i: (b, i, 0)),
                  pl.BlockSpec((1, 6, d), lambda b, i: (b, 0, 0)),
                  pl.BlockSpec((1, d), lambda b, i: (0, 0)),
                  pl.BlockSpec((d, nin), lambda b, i: (0, 0)),
                  pl.BlockSpec((1, ATT_WIDTH), lambda b, i: (0, 0)),
                  pl.BlockSpec((1, ATT_WIDTH), lambda b, i: (0, 0)),
                  pl.BlockSpec((ATT_WIDTH, ATT_WIDTH), lambda b, i: (0, 0))],
        out_specs=(pl.BlockSpec((1, tm, 3 * ATT_WIDTH), lambda b, i: (b, i, 0)),
                   pl.BlockSpec((1, tm, SHIFT_WIDTH), lambda b, i: (b, i, 0))),
        compiler_params=_params("arbitrary", "arbitrary"),
        name="inproj",
    )(x, mod6, norm1_g, w_in_bf, gq, gk, bd512)


def _attn_window(nkb):
    return nkb if nkb <= 2 * ATT_REACH + 2 else 2 * ATT_REACH + 2


def _attn_bias_table(n_heads, nkb):
    reach = _attn_window(nkb) - 1
    nd = 2 * reach + 1
    dd = jnp.arange(nd, dtype=jnp.int32)[None, :, None]
    r = jnp.arange(ATT_BLOCK, dtype=jnp.int32)[:, None, None]
    c = jnp.arange(ATT_BLOCK, dtype=jnp.int32)[None, None, :]
    dt = r - c - (dd - reach) * ATT_BLOCK
    adt = jnp.abs(dt)
    mult = jnp.zeros(dt.shape, F32)
    for dil, half in ATT_HALF_WINDOWS:
        mult = mult + ((adt <= half) & (dt % dil == 0)).astype(F32)
    logm = jnp.where(mult > 0, jnp.log2(jnp.maximum(mult, 1.0)), MASK_VALUE)
    slopes = jnp.exp2(-8.0 * (jnp.arange(n_heads, dtype=F32) + 1.0) / n_heads) * LOG2E
    bias = logm[None] - slopes[:, None, None, None] * adt.astype(F32)[None]
    return bias.reshape(n_heads, ATT_BLOCK, nd * ATT_BLOCK)


def _attn_kernel(q_ref, k_ref, v_ref, bias_ref, o_ref, s_ref, *, nkb, win):
    qi = pl.program_id(2)
    blk = ATT_BLOCK
    q = q_ref[0]
    lane = lax.broadcasted_iota(jnp.int32, (blk, LANES), 1)
    first = lane < HEAD_DIM
    zero = jnp.zeros_like(q)
    qab = jnp.concatenate([jnp.where(first, q, zero), jnp.where(first, zero, q)], axis=0)
    ks = 0 if win == nkb else jnp.clip(qi - ATT_REACH, 0, nkb - win)
    reach = win - 1

    m = [jnp.full((blk, LANES), -jnp.inf, F32) for _ in range(2)]
    for w0 in range(0, win, 2):
        kj0 = ks + w0
        kslab = k_ref[0, pl.ds(pl.multiple_of(kj0 * blk, blk), 2 * blk), :]
        s = _dot(qab, kslab, NT)
        x0 = pl.multiple_of((kj0 - qi + reach) * blk, blk)
        for hh in range(2):
            sb = s[hh * blk:(hh + 1) * blk] + bias_ref[hh, :, pl.ds(x0, 2 * blk)]
            s_ref[hh, :, w0 * blk:(w0 + 2) * blk] = sb
            m[hh] = jnp.maximum(m[hh], jnp.maximum(sb[:, :blk], sb[:, blk:]))
    mrow = [jnp.max(m[hh], axis=-1, keepdims=True) for hh in range(2)]

    first2 = lax.broadcasted_iota(jnp.int32, (2 * blk, LANES), 1) < HEAD_DIM
    acc = [jnp.zeros((blk, LANES), F32) for _ in range(2)]
    for w0 in range(0, win, 2):
        vslab = v_ref[0, pl.ds(pl.multiple_of((ks + w0) * blk, blk), 2 * blk), :]
        one = jnp.ones_like(vslab)
        vs = (jnp.where(first2, vslab, one), jnp.where(first2, one, vslab))
        for hh in range(2):
            p = jnp.exp2(s_ref[hh, :, w0 * blk:(w0 + 2) * blk] - mrow[hh]).astype(BF16)
            acc[hh] = acc[hh] + _dot(p, vs[hh])
    num = jnp.where(first, acc[0], acc[1])
    den = jnp.where(first, pltpu.roll(acc[0], HEAD_DIM, axis=1), pltpu.roll(acc[1], HEAD_DIM, axis=1))
    o_ref[0] = (num / den).astype(o_ref.dtype)


def _attention(qkv, bias):
    bsz, s, _ = qkv.shape
    nkb = s // ATT_BLOCK
    win = _attn_window(nkb)
    assert nkb % 2 == 0 and bias.shape[2] == (2 * win - 1) * ATT_BLOCK
    npair = ATT_WIDTH // LANES
    return pl.pallas_call(
        functools.partial(_attn_kernel, nkb=nkb, win=win),
        out_shape=jax.ShapeDtypeStruct((bsz, s, ATT_WIDTH), BF16),
        grid=(npair, bsz, nkb),
        in_specs=[pl.BlockSpec((1, ATT_BLOCK, LANES), lambda hp, b, i: (b, i, hp)),
                  pl.BlockSpec((1, s, LANES), lambda hp, b, i: (b, 0, npair + hp)),
                  pl.BlockSpec((1, s, LANES), lambda hp, b, i: (b, 0, 2 * npair + hp)),
                  pl.BlockSpec((2, ATT_BLOCK, bias.shape[2]), lambda hp, b, i: (hp, 0, 0))],
        out_specs=pl.BlockSpec((1, ATT_BLOCK, LANES), lambda hp, b, i: (b, i, hp)),
        scratch_shapes=[pltpu.VMEM((2, ATT_BLOCK, win * ATT_BLOCK), F32)],
        compiler_params=_params("arbitrary", "arbitrary", "arbitrary"),
        name="attn",
    )(qkv, qkv, qkv, bias)


def _prep_kernel(z_ref, zp_ref, zn_ref, mup_ref, mun_ref, wup_ref, aup_ref, gup_ref, w0_ref, a0_ref,
                 kk_ref, ka_ref, rk_ref, bd_ref, o_ref, *, tq):
    i = pl.program_id(1)
    last = pl.num_programs(1) - 1
    z = z_ref[0]
    row = lax.broadcasted_iota(jnp.int32, (tq, 1), 0)
    prev_row = zp_ref[0, 7:8, :] * (i > 0).astype(F32)
    next_row = zn_ref[0, 0:1, :] * (i < last).astype(F32)
    zp = jnp.where(row == 0, prev_row, pltpu.roll(z, 1, axis=0))
    zn = jnp.where(row == tq - 1, next_row, pltpu.roll(z, tq - 1, axis=0))
    zs = z + mup_ref[...] * (zp - z) + mun_ref[...] * (zn - z)

    w = RWKV_WIDTH
    r = zs[:, 0:w]
    k = zs[:, w:2 * w]
    v = zs[:, 2 * w:3 * w]
    xwa = zs[:, 3 * w:3 * w + LANES]
    xg = zs[:, 3 * w + LANES:]
    bd = bd_ref[...]

    g = _dot(_sigmoid(xg).astype(BF16), gup_ref[...])
    kk = k * kk_ref[...]
    ss = _mm_exact_rhs(kk * kk, bd)
    kk = kk * lax.rsqrt(jnp.maximum(ss, 1e-12))
    ka = ka_ref[...]
    txw = jnp.tanh(xwa).astype(BF16)
    xab = xwa.astype(BF16)

    o_ref[0, :, SLAB_R * w:(SLAB_R + 1) * w] = r
    o_ref[0, :, SLAB_V * w:(SLAB_V + 1) * w] = v
    o_ref[0, :, SLAB_KK * w:(SLAB_KK + 1) * w] = kk
    o_ref[0, :, SLAB_G * w:(SLAB_G + 1) * w] = g
    a_sum = jnp.zeros_like(k)
    for d in range(2):
        y = w0_ref[d:d + 1, :] + _dot(txw, wup_ref[d])
        wlog = -(jnp.maximum(-y, 0.0) + jnp.log(1.0 + jnp.exp(-jnp.abs(y)))) - 0.5
        a = _sigmoid(a0_ref[d:d + 1, :] + _dot(xab, aup_ref[d]))
        a_sum = a_sum + a
        base = (SLAB_DIR0 + 3 * d) * w
        o_ref[0, :, base:base + w] = -jnp.exp(wlog)
        o_ref[0, :, base + w:base + 2 * w] = k * (1.0 + (a - 1.0) * ka)
        o_ref[0, :, base + 2 * w:base + 3 * w] = kk * a
    k_bonus = k * (1.0 + (0.5 * a_sum - 1.0) * ka)
    bsum = _mm_exact_rhs(r * k_bonus * rk_ref[...], bd)
    o_ref[0, :, SLAB_BONUS * w:(SLAB_BONUS + 1) * w] = bsum * v


def _rwkv_prep(zrw, mu_prev, mu_next, wup_pad, aup_pad, g_up_bf, w0, a0, k_k, k_a, r_k, bd512, tq=256):
    bsz, s, sw = zrw.shape
    w = RWKV_WIDTH
    nt = s // tq
    row = lambda b, i: (0, 0)
    return pl.pallas_call(
        functools.partial(_prep_kernel, tq=tq),
        out_shape=jax.ShapeDtypeStruct((bsz, s, NUM_SLABS * w), F32),
        grid=(bsz, nt),
        in_specs=[pl.BlockSpec((1, tq, sw), lambda b, i: (b, i, 0)),
                  pl.BlockSpec((1, 8, sw), lambda b, i: (b, jnp.maximum(i * (tq // 8) - 1, 0), 0)),
                  pl.BlockSpec((1, 8, sw), lambda b, i: (b, jnp.minimum((i + 1) * (tq // 8), s // 8 - 1), 0)),
                  pl.BlockSpec((1, sw), row),
                  pl.BlockSpec((1, sw), row),
                  pl.BlockSpec((2, LANES, w), lambda b, i: (0, 0, 0)),
                  pl.BlockSpec((2, LANES, w), lambda b, i: (0, 0, 0)),
                  pl.BlockSpec((GATE_LORA, w), row),
                  pl.BlockSpec((2, w), row),
                  pl.BlockSpec((2, w), row),
                  pl.BlockSpec((1, w), row),
                  pl.BlockSpec((1, w), row),
                  pl.BlockSpec((1, w), row),
                  pl.BlockSpec((w, w), row)],
        out_specs=pl.BlockSpec((1, tq, NUM_SLABS * w), lambda b, i: (b, i, 0)),
        compiler_params=_params("arbitrary", "arbitrary"),
        name="rwkv_prep",
    )(zrw, zrw, zrw, mu_prev, mu_next, wup_pad, aup_pad, g_up_bf, w0, a0, k_k, k_a, r_k, bd512)


def _block_diag(x, first):
    zero = jnp.zeros_like(x)
    return jnp.concatenate([jnp.where(first, x, zero), jnp.where(first, zero, x)], axis=0)


def _p1_kernel(r_ref, v_ref, kk_ref, lw_ref, kd_ref, be_ref, rp_ref, qp_ref, g_ref, h_ref, *, cpb):
    c = CHUNK
    sign = 1 - 2 * pl.program_id(2)
    rowi = lax.broadcasted_iota(jnp.int32, (c, LANES), 0)
    lane = lax.broadcasted_iota(jnp.int32, (c, LANES), 1)
    coli = lane & (c - 1)
    first = lane < HEAD_DIM
    before = sign * (rowi - coli)
    strict2 = before > 0
    incl2 = before >= 0
    eye2 = (coli == rowi).astype(F32)
    r64 = lax.broadcasted_iota(jnp.int32, (c, c), 0)
    c64 = lax.broadcasted_iota(jnp.int32, (c, c), 1)
    cum = (sign * (r64 - c64) >= 0).astype(BF16)
    r128 = lax.broadcasted_iota(jnp.int32, (LANES, LANES), 0)
    c128 = lax.broadcasted_iota(jnp.int32, (LANES, LANES), 1)
    same_head = (r128 // HEAD_DIM) == (c128 // HEAD_DIM)
    eye128 = (r128 == c128).astype(F32)
    zero128 = jnp.zeros((LANES, LANES), F32)

    js = range(cpb)
    sls = [pl.ds(j * c, c) for j in js]
    bdg = lambda x: _block_diag(x, first)
    r = [r_ref[0, sl, :] for sl in sls]
    v = [v_ref[0, sl, :] for sl in sls]
    kk = [kk_ref[0, sl, :] for sl in sls]
    lw = [lw_ref[0, sl, :] for sl in sls]
    kd = [kd_ref[0, sl, :] for sl in sls]
    be = [be_ref[0, sl, :] for sl in sls]

    def cumsum(x):
        l1 = x.astype(BF16)
        rem = x - l1.astype(F32)
        l2 = rem.astype(BF16)
        l3 = (rem - l2.astype(F32)).astype(BF16)
        return _dot(cum, l1) + (_dot(cum, l2) + _dot(cum, l3))

    cs = [cumsum(lw[j]) for j in js]
    g_inv = [jnp.exp(-cs[j]) for j in js]
    g_tot = [jnp.exp(jnp.sum(lw[j], axis=0, keepdims=True)) for j in js]
    ab = [-kk[j] * jnp.exp(cs[j] - lw[j]) for j in js]
    rb = [r[j] * jnp.exp(cs[j]) for j in js]
    bt = [be[j] * g_inv[j] for j in js]
    kt = [kd[j] * g_inv[j] for j in js]

    lhs = [jnp.concatenate([ab[j], rb[j]], axis=0) for j in js]
    sb = [_mm(lhs[j], bdg(bt[j]), NT, WKV_PASSES_SCORE) for j in js]
    sk = [_mm(lhs[j], bdg(kt[j]), NT, WKV_PASSES_SCORE) for j in js]
    m_ab = [jnp.where(strict2, sb[j][:c], 0.0) for j in js]
    n_rb = [jnp.where(incl2, sb[j][c:], 0.0) for j in js]
    m_ak = [jnp.where(strict2, sk[j][:c], 0.0) for j in js]
    n_rk = [jnp.where(incl2, sk[j][c:], 0.0) for j in js]

    t = [eye2 + m_ab[j] for j in js]
    p = m_ab
    for _ in range(int(math.log2(c)) - 1):
        p = [_mm(p[j], bdg(p[j]), NN, WKV_PASSES_INV) for j in js]
        t = [t[j] + _mm(t[j], bdg(p[j]), NN, WKV_PASSES_INV) for j in js]
    res = [eye2 - _mm(eye2 - m_ab[j], bdg(t[j]), NN, 3) for j in js]
    t = [t[j] + _mm(t[j], bdg(res[j]), NN, 1) for j in js]

    vbd = [bdg(v[j]) for j in js]
    pm = [_mm(m_ak[j], vbd[j], NN, WKV_PASSES_APPLY) for j in js]
    ap = [_mm(t[j], bdg(ab[j]), NN, WKV_PASSES_APPLY) for j in js]
    pp = [_mm(t[j], bdg(pm[j]), NN, WKV_PASSES_APPLY) for j in js]
    for j in js:
        rp_ref[0, 0, 0, sls[j], :] = rb[j] + _mm(n_rb[j], bdg(ap[j]), NN, WKV_PASSES_APPLY)
        qp_ref[0, 0, 0, sls[j], :] = (_mm(n_rb[j], bdg(pp[j]), NN, WKV_PASSES_APPLY)
                                      + _mm(n_rk[j], vbd[j], NN, WKV_PASSES_APPLY))
    for j in js:
        gmat = eye128 + jnp.where(same_head, _mm(ap[j], bt[j], TN, WKV_PASSES_STATE), zero128)
        hmat = jnp.where(same_head, _mm(pp[j], bt[j], TN, WKV_PASSES_STATE)
                         + _mm(v[j], kt[j], TN, WKV_PASSES_STATE), zero128)
        g_ref[0, 0, 0, j] = gmat * g_tot[j]
        h_ref[0, 0, 0, j] = hmat * g_tot[j]


def _rwkv_p1(pk, cpb=4):
    bsz, s, _ = pk.shape
    nc = s // CHUNK
    npair = RWKV_WIDTH // LANES
    rows = cpb * CHUNK

    def slab(sidx):
        return pl.BlockSpec((1, rows, LANES), lambda b, hp, d, ci: (b, ci, sidx * npair + hp))

    def dslab(off):
        return pl.BlockSpec((1, rows, LANES),
                            lambda b, hp, d, ci: (b, ci, (SLAB_DIR0 + 3 * d + off) * npair + hp))

    seq = pl.BlockSpec((1, 1, 1, rows, LANES), lambda b, hp, d, ci: (b, hp, d, ci, 0))
    mat = pl.BlockSpec((1, 1, 1, cpb, LANES, LANES), lambda b, hp, d, ci: (b, hp, d, ci, 0, 0))
    return pl.pallas_call(
        functools.partial(_p1_kernel, cpb=cpb),
        out_shape=(jax.ShapeDtypeStruct((bsz, npair, 2, s, LANES), F32),
                   jax.ShapeDtypeStruct((bsz, npair, 2, s, LANES), F32),
                   jax.ShapeDtypeStruct((bsz, npair, 2, nc, LANES, LANES), F32),
                   jax.ShapeDtypeStruct((bsz, npair, 2, nc, LANES, LANES), F32)),
        grid=(bsz, npair, 2, nc // cpb),
        in_specs=[slab(SLAB_R), slab(SLAB_V), slab(SLAB_KK), dslab(0), dslab(1), dslab(2)],
        out_specs=(seq, seq, mat, mat),
        compiler_params=_params("arbitrary", "arbitrary", "arbitrary", "arbitrary"),
        name="rwkv_p1",
    )(pk, pk, pk, pk, pk, pk)


def _p2_kernel(rp_ref, qp_ref, g_ref, h_ref, gate_ref, bonus_ref, lng_ref, lnb_ref, bd_ref, o_ref,
               yf_ref, yb_ref, *, nc, te):
    c = CHUNK

    def step(j, carry):
        sf, sb = carry
        jf = j
        jb = nc - 1 - j
        slf = pl.ds(pl.multiple_of(jf * c, c), c)
        slb = pl.ds(pl.multiple_of(jb * c, c), c)
        yf_ref[slf, :] = _mm(rp_ref[0, 0, 0, slf, :], sf, NT, WKV_PASSES_CHAIN) + qp_ref[0, 0, 0, slf, :]
        yb_ref[slb, :] = _mm(rp_ref[0, 0, 1, slb, :], sb, NT, WKV_PASSES_CHAIN) + qp_ref[0, 0, 1, slb, :]
        sf = _mm(sf, g_ref[0, 0, 0, jf], NN, WKV_PASSES_CHAIN) + h_ref[0, 0, 0, jf]
        sb = _mm(sb, g_ref[0, 0, 1, jb], NN, WKV_PASSES_CHAIN) + h_ref[0, 0, 1, jb]
        return sf, sb

    zero = jnp.zeros((LANES, LANES), F32)
    lax.fori_loop(0, nc, step, (zero, zero))

    bd = bd_ref[...]
    lng = lng_ref[...]
    lnb = lnb_ref[...]
    inv = 1.0 / HEAD_DIM

    def epi(i, _):
        sl = pl.ds(pl.multiple_of(i * te, te), te)
        y = yf_ref[sl, :] + yb_ref[sl, :]
        mean = _mm_exact_rhs(y, bd) * inv
        yc = y - mean
        var = _mm_exact_rhs(yc * yc, bd) * inv
        yn = yc * lax.rsqrt(var + LNX_EPS) * lng + lnb
        o_ref[0, sl, :] = ((yn + bonus_ref[0, sl, :]) * gate_ref[0, sl, :]).astype(o_ref.dtype)
        return 0

    lax.fori_loop(0, (nc * c) // te, epi, 0)


def _rwkv_p2(rp, qp, gm, hm, pk, lnx_g, lnx_b, bd128, te=256):
    bsz, npair, _, s, _ = rp.shape
    nc = s // CHUNK
    seq = pl.BlockSpec((1, 1, 2, s, LANES), lambda b, hp: (b, hp, 0, 0, 0))
    mat = pl.BlockSpec((1, 1, 2, nc, LANES, LANES), lambda b, hp: (b, hp, 0, 0, 0, 0))
    return pl.pallas_call(
        functools.partial(_p2_kernel, nc=nc, te=min(te, s)),
        out_shape=jax.ShapeDtypeStruct((bsz, s, RWKV_WIDTH), BF16),
        grid=(bsz, npair),
        in_specs=[seq, seq, mat, mat,
                  pl.BlockSpec((1, s, LANES), lambda b, hp: (b, 0, SLAB_G * npair + hp)),
                  pl.BlockSpec((1, s, LANES), lambda b, hp: (b, 0, SLAB_BONUS * npair + hp)),
                  pl.BlockSpec((1, LANES), lambda b, hp: (0, hp)),
                  pl.BlockSpec((1, LANES), lambda b, hp: (0, hp)),
                  pl.BlockSpec((LANES, LANES), lambda b, hp: (0, 0))],
        out_specs=pl.BlockSpec((1, s, LANES), lambda b, hp: (b, 0, hp)),
        scratch_shapes=[pltpu.VMEM((s, LANES), F32), pltpu.VMEM((s, LANES), F32)],
        compiler_params=_params("arbitrary", "arbitrary"),
        name="rwkv_p2",
    )(rp, qp, gm, hm, pk, pk, lnx_g, lnx_b, bd128)


def _outproj_kernel(att_ref, rw_ref, x_ref, mod_ref, g2_ref, wo_ref, wq_ref, k1_ref, k2_ref,
                    x1_ref, h2_ref, s1_ref, s2_ref):
    mod = mod_ref[0]
    a = ATT_WIDTH
    mixed = _dot(att_ref[0], wo_ref[0:a, :]) + _dot(rw_ref[0], wo_ref[a:, :])
    x1 = x_ref[0] + mod[2:3] * mixed
    x1_ref[0] = x1
    ms = jnp.mean(x1 * x1, axis=-1, keepdims=True)
    h2 = x1 * lax.rsqrt(ms + NORM_EPS) * g2_ref[...]
    h2 = (h2 * (1.0 + mod[4:5]) + mod[3:4]).astype(BF16)
    h2_ref[0] = h2
    q = _dot(h2, wq_ref[...])
    k1 = k1_ref[...]
    k2 = k2_ref[...]
    for h in range(PEER_HEADS):
        base = h * 2 * LANES
        s1 = _dot(k1, q[:, base:base + LANES].astype(BF16), NT)
        s2 = _dot(k2, q[:, base + LANES:base + 2 * LANES].astype(BF16), NT)
        for g in range(s1.shape[1] // LANES):
            s1_ref[h, g] = s1[:, g * LANES:(g + 1) * LANES]
            s2_ref[h, g] = s2[:, g * LANES:(g + 1) * LANES]


def _outproj(att, rw, x, mod6, norm2_g, w_out_bf, wq_bf, k1_bf, k2_bf, tm=256):
    bsz, s, d = x.shape
    nt = s // tm
    t = bsz * s
    nq = wq_bf.shape[1]
    const = lambda b, i: (0, 0)
    gpt = tm // LANES
    tok = pl.BlockSpec((PEER_HEADS, gpt, PEER_N_KEYS, LANES), lambda b, i: (0, b * nt + i, 0, 0))
    stat = jax.ShapeDtypeStruct((PEER_HEADS, t // LANES, PEER_N_KEYS, LANES), F32)
    return pl.pallas_call(
        _outproj_kernel,
        out_shape=(jax.ShapeDtypeStruct((bsz, s, d), F32),
                   jax.ShapeDtypeStruct((bsz, s, d), BF16),
                   stat, stat),
        grid=(bsz, nt),
        in_specs=[pl.BlockSpec((1, tm, ATT_WIDTH), lambda b, i: (b, i, 0)),
                  pl.BlockSpec((1, tm, RWKV_WIDTH), lambda b, i: (b, i, 0)),
                  pl.BlockSpec((1, tm, d), lambda b, i: (b, i, 0)),
                  pl.BlockSpec((1, 6, d), lambda b, i: (b, 0, 0)),
                  pl.BlockSpec((1, d), const),
                  pl.BlockSpec((d, d), const),
                  pl.BlockSpec((d, nq), const),
                  pl.BlockSpec((PEER_N_KEYS, LANES), const),
                  pl.BlockSpec((PEER_N_KEYS, LANES), const)],
        out_specs=(pl.BlockSpec((1, tm, d), lambda b, i: (b, i, 0)),
                   pl.BlockSpec((1, tm, d), lambda b, i: (b, i, 0)),
                   tok, tok),
        compiler_params=_params("arbitrary", "arbitrary"),
        name="outproj",
    )(att, rw, x, mod6, norm2_g, w_out_bf, wq_bf, k1_bf, k2_bf)


def _topk_kernel(s1_ref, s2_ref, c1_ref, e1_ref, r2_ref, e2_ref, *, groups):
    kk = PEER_TOPK
    neg = -jnp.inf
    rows16 = lax.broadcasted_iota(jnp.int32, (kk, LANES), 0)
    rows8 = lax.broadcasted_iota(jnp.int32, (8, LANES), 0)

    def top_values(x):
        def it(i, carry):
            x, out, _, rank = carry
            m = jnp.max(x, axis=0, keepdims=True)
            hit = x >= m
            return (jnp.where(hit, neg, x), jnp.where(rows16 == i, m, out), m,
                    jnp.where(hit, i.astype(F32), rank))
        init = jnp.max(x, axis=0, keepdims=True)
        _, out, m17, rank = lax.fori_loop(
            0, kk + 1, it, (x, jnp.full((kk, LANES), neg, F32), init, jnp.full(x.shape, float(kk), F32)))
        return out, m17, rank

    def candidates(a, b):
        b8 = b[0:8, :]
        parts = [a[0:1, :] + b, a[1:2, :] + b8]
        for i in range(2, 8):
            parts.append(jnp.where(rows8 < kk // (i + 1), a[i:i + 1, :] + b8, neg))
        parts.append(a[8:16, :] + b[0:1, :])
        return jnp.concatenate(parts, axis=0)

    def per_tile(it_idx, _):
        g = it_idx // PEER_HEADS
        h = it_idx % PEER_HEADS
        s1 = s1_ref[h, g]
        s2 = s2_ref[h, g]
        a, a17, _ = top_values(s1)
        b, b17, rank2 = top_values(s2)
        a0 = a[0:1, :]
        b0 = b[0:1, :]
        cand = candidates(a, b)

        def it(i, carry):
            c, m_prev, _ = carry
            m = jnp.max(c, axis=0, keepdims=True)
            return jnp.where(c >= m, neg, c), m, m_prev
        top = jnp.max(cand, axis=0, keepdims=True)
        _, nxt, tau = lax.fori_loop(0, kk + 1, it, (cand, top, top))
        nxt = jnp.maximum(nxt, jnp.maximum(a17 + b0, a0 + b17))
        thr = 0.5 * (tau + nxt)
        z = jnp.sum(jnp.where(cand >= tau, jnp.exp(cand - top), 0.0), axis=0, keepdims=True)
        count = jnp.zeros_like(s1)
        for r in range(kk):
            ar = a[r:r + 1, :]
            n_r = jnp.sum(jnp.where(b >= thr - ar, 1.0, 0.0), axis=0, keepdims=True)
            count = jnp.where(s1 == ar, n_r, count)
        c1_ref[h, g] = count
        e1_ref[h, g] = jnp.exp(s1 - a0) / z
        r2_ref[h, g] = pltpu.bitcast(rank2.astype(BF16), jnp.uint32)
        e2_ref[h, g] = pltpu.bitcast(jnp.exp(s2 - b0).astype(BF16), jnp.uint32)
        return 0

    lax.fori_loop(0, groups * PEER_HEADS, per_tile, 0)


def _peer_topk(s1t, s2t, tn=512):
    nh, ng, nk, _ = s1t.shape
    groups = tn // LANES
    blk = pl.BlockSpec((nh, groups, nk, LANES), lambda i: (0, i, 0, 0))
    f32 = jax.ShapeDtypeStruct(s1t.shape, F32)
    b16 = jax.ShapeDtypeStruct((nh, ng, nk // 2, LANES), jnp.uint32)
    pblk = pl.BlockSpec((nh, groups, nk // 2, LANES), lambda i: (0, i, 0, 0))
    return pl.pallas_call(
        functools.partial(_topk_kernel, groups=groups),
        out_shape=(f32, f32, b16, b16),
        grid=(ng // groups,),
        in_specs=[blk, blk],
        out_specs=(blk, blk, pblk, pblk),
        compiler_params=_params("arbitrary"),
        name="peer_topk",
    )(s1t, s2t)


def _peer_kernel(h2_ref, u_ref, vt_ref, c1_ref, e1_ref, r2_ref, e2_ref, x1_ref, mod_ref, o_ref,
                 acc_ref, act0_ref, act1_ref, p0_ref, p1_ref, *, tn, te):
    k = pl.program_id(1)
    n_tiles = pl.num_programs(1) - 2
    nk = PEER_N_KEYS
    ngroups = tn // LANES

    @pl.when(k == 0)
    def _():
        acc_ref[...] = jnp.zeros_like(acc_ref)
        for ref in (act0_ref, act1_ref, p0_ref, p1_ref):
            ref[...] = jnp.zeros_like(ref)

    ni = te // nk
    i0 = pl.multiple_of(jnp.clip(k - 1, 0, n_tiles - 1) * ni, ni)

    d_model = acc_ref.shape[0]
    halves = [(mh, nh, kh) for mh in range(2) for nh in range(ngroups // 2) for kh in range(2)]

    def stage_c(p_a, mh, nh, kh):
        rows = slice(mh * (d_model // 2), (mh + 1) * (d_model // 2))
        cols = slice(nh * 2 * LANES, (nh + 1) * 2 * LANES)
        ks = slice(kh * (te // 2), (kh + 1) * (te // 2))
        p_prev = jnp.concatenate([p_a[2 * nh, ks, :], p_a[2 * nh + 1, ks, :]], axis=1)
        acc_ref[rows, cols] += _dot(vt_ref[rows, ks], p_prev)

    def stage_a(act_a, mh, nh, kh):
        rows = slice(mh * (te // 2), (mh + 1) * (te // 2))
        ks = slice(kh * (d_model // 2), (kh + 1) * (d_model // 2))
        act = _dot(u_ref[rows, ks], h2_ref[nh * 2 * LANES:(nh + 1) * 2 * LANES, ks], NT)
        if kh == 0:
            act_a[2 * nh, rows, :] = act[:, :LANES]
            act_a[2 * nh + 1, rows, :] = act[:, LANES:]
        else:
            act_a[2 * nh, rows, :] += act[:, :LANES]
            act_a[2 * nh + 1, rows, :] += act[:, LANES:]

    def rows_bf16(row):
        tile = jnp.broadcast_to(row, (BF16_ROWS, LANES)).astype(BF16)
        return jnp.concatenate([tile] * (nk // BF16_ROWS), axis=0)

    quad = 4

    def stage_b(act_b, p_b, tg, iq, pieces):
        gates = [jnp.zeros((nk, LANES), BF16) for _ in range(quad)]
        for h in range(PEER_HEADS):
            if h % (PEER_HEADS // 2) == 0 and pieces:
                pieces.pop(0)()
            c1 = c1_ref[h, tg, pl.ds(i0, ni), :]
            e1 = e1_ref[h, tg, pl.ds(i0, ni), :]
            rank2 = pltpu.bitcast(r2_ref[h, tg], BF16)
            e2 = pltpu.bitcast(e2_ref[h, tg], BF16)
            for q in range(quad):
                il = iq * quad + q
                sel = rank2 < rows_bf16(c1[il:il + 1, :])
                gates[q] = gates[q] + jnp.where(sel, e2 * rows_bf16(e1[il:il + 1, :]), jnp.zeros_like(e2))
        for q in range(quad):
            il = iq * quad + q
            a = act_b[tg, il * nk:(il + 1) * nk, :]
            gelu = 0.5 * a * (1.0 + lax.erf(a * (2.0 ** -0.5)))
            p_b[tg, il * nk:(il + 1) * nk, :] = gates[q] * gelu.astype(BF16)

    def stages(act_a, act_b, p_a, p_b):
        pieces = ([functools.partial(stage_c, p_a, *hv) for hv in halves]
                  + [functools.partial(stage_a, act_a, *hv) for hv in halves])
        for tg in range(ngroups):
            for iq in range(ni // quad):
                stage_b(act_b, p_b, tg, iq, pieces)
        while pieces:
            pieces.pop(0)()

    @pl.when(k % 2 == 0)
    def _():
        stages(act0_ref, act1_ref, p0_ref, p1_ref)

    @pl.when(k % 2 == 1)
    def _():
        stages(act1_ref, act0_ref, p1_ref, p0_ref)

    @pl.when(k == pl.num_programs(1) - 1)
    def _():
        o_ref[...] = x1_ref[...] + mod_ref[0, 5:6, :] * acc_ref[...].T


def _peer_ffn(h2, u_bf, vt_bf, c1, e1, r2, e2, x1, mod6, seq, tn=512, te=1024):
    t, d = h2.shape
    ne = u_bf.shape[0]
    ngroups = tn // LANES
    n_tiles = ne // te
    stat = pl.BlockSpec((PEER_HEADS, ngroups, PEER_N_KEYS, LANES), lambda ti, k: (0, ti, 0, 0))
    pstat = pl.BlockSpec((PEER_HEADS, ngroups, PEER_N_KEYS // 2, LANES), lambda ti, k: (0, ti, 0, 0))
    return pl.pallas_call(
        functools.partial(_peer_kernel, tn=tn, te=te),
        out_shape=jax.ShapeDtypeStruct((t, d), F32),
        grid=(t // tn, n_tiles + 2),
        in_specs=[pl.BlockSpec((tn, d), lambda ti, k: (ti, 0)),
                  pl.BlockSpec((te, d), lambda ti, k: (jnp.minimum(k, n_tiles - 1), 0)),
                  pl.BlockSpec((d, te), lambda ti, k: (0, jnp.clip(k - 2, 0, n_tiles - 1))),
                  stat, stat, pstat, pstat,
                  pl.BlockSpec((tn, d), lambda ti, k: (ti, 0)),
                  pl.BlockSpec((1, 6, d), lambda ti, k: ((ti * tn) // seq, 0, 0))],
        out_specs=pl.BlockSpec((tn, d), lambda ti, k: (ti, 0)),
        scratch_shapes=[pltpu.VMEM((d, tn), F32),
                        pltpu.VMEM((ngroups, te, LANES), F32), pltpu.VMEM((ngroups, te, LANES), F32),
                        pltpu.VMEM((ngroups, te, LANES), BF16), pltpu.VMEM((ngroups, te, LANES), BF16)],
        compiler_params=_params("arbitrary", "arbitrary"),
        name="peer_ffn",
    )(h2, u_bf, vt_bf, c1, e1, r2, e2, x1, mod6)


def _head_block_diag(n):
    idx = jnp.arange(n, dtype=jnp.int32) // HEAD_DIM
    return (idx[:, None] == idx[None, :]).astype(BF16)


def _layer(x, c, ada_w, ada_b, norm1_g, w_in, mu_prev, mu_next, q_norm_g, k_norm_g, w_decay0, w_decay_up,
           a_gate0, a_gate_up, g_up, k_k, k_a, r_k, lnx_g, lnx_b, w_out, norm2_g, peer_w_query,
           peer_sub_keys1, peer_sub_keys2, peer_u, peer_v):
    bsz, s, d = x.shape
    n_att_heads = ATT_WIDTH // HEAD_DIM
    bd512 = _head_block_diag(ATT_WIDTH)
    bd128 = _head_block_diag(LANES)

    mod6 = _ada(c, ada_w, ada_b).reshape(bsz, 6, d)
    gq = jnp.tile(q_norm_g, n_att_heads).reshape(1, ATT_WIDTH)
    gk = jnp.tile(k_norm_g, n_att_heads).reshape(1, ATT_WIDTH)
    qkv, zrw = _inproj(x, mod6, norm1_g.reshape(1, d), w_in.astype(BF16), gq, gk, bd512)

    att = _attention(qkv, _attn_bias_table(n_att_heads, s // ATT_BLOCK))

    zero_w = jnp.zeros((2, LANES - DECAY_LORA, RWKV_WIDTH), F32)
    wup_pad = jnp.concatenate([w_decay_up, zero_w], axis=1).astype(BF16)
    aup_pad = jnp.concatenate([zero_w, a_gate_up], axis=1).astype(BF16)
    pk = _rwkv_prep(zrw, mu_prev.reshape(1, -1), mu_next.reshape(1, -1), wup_pad, aup_pad, g_up.astype(BF16),
                    w_decay0, a_gate0, k_k.reshape(1, -1), k_a.reshape(1, -1), r_k.reshape(1, -1), bd512)
    rp, qp, gm, hm = _rwkv_p1(pk)
    rw = _rwkv_p2(rp, qp, gm, hm, pk, lnx_g.reshape(1, -1), lnx_b.reshape(1, -1), bd128)

    x1, h2, s1t, s2t = _outproj(att, rw, x, mod6, norm2_g.reshape(1, d), w_out.astype(BF16),
                                peer_w_query.astype(BF16), peer_sub_keys1.astype(BF16),
                                peer_sub_keys2.astype(BF16))
    c1, e1, r2, e2 = _peer_topk(s1t, s2t)
    out = _peer_ffn(h2.reshape(bsz * s, d), peer_u.astype(BF16), peer_v.T.astype(BF16), c1, e1, r2, e2,
                    x1.reshape(bsz * s, d), mod6, s)
    return out.reshape(bsz, s, d)


def kernel(x, c, ada_w, ada_b, norm1_g, w_in, mu_prev, mu_next, q_norm_g, k_norm_g, w_decay0, w_decay_up,
           a_gate0, a_gate_up, g_up, k_k, k_a, r_k, lnx_g, lnx_b, w_out, norm2_g, peer_w_query,
           peer_sub_keys1, peer_sub_keys2, peer_u, peer_v):
    depth = ada_w.shape[0]
    for l in range(depth):
        x = _layer(x, c, ada_w[l], ada_b[l], norm1_g[l], w_in[l], mu_prev[l], mu_next[l], q_norm_g[l],
                   k_norm_g[l], w_decay0[l], w_decay_up[l], a_gate0[l], a_gate_up[l], g_up[l], k_k[l], k_a[l],
                   r_k[l], lnx_g[l], lnx_b[l], w_out[l], norm2_g[l], peer_w_query[l], peer_sub_keys1[l],
                   peer_sub_keys2[l], peer_u[l], peer_v[l])
    return x
```

```python
import functools
import math

import jax
import jax.numpy as jnp
from jax import lax
from jax.experimental import pallas as pl
from jax.experimental.pallas import tpu as pltpu

F32 = jnp.float32
BF16 = jnp.bfloat16

HEAD_DIM = 64
ATT_WIDTH = 512
RWKV_WIDTH = 512
DECAY_LORA = 64
AAA_LORA = 64
GATE_LORA = 128
SHIFT_WIDTH = 3 * RWKV_WIDTH + DECAY_LORA + AAA_LORA + GATE_LORA
PEER_N_KEYS = 128
PEER_HEADS = 8
PEER_TOPK = 16
NORM_EPS = 1e-6
LNX_EPS = 64e-5
MASK_VALUE = -1e30
LOG2E = 1.4426950408889634
ATT_HALF_WINDOWS = ((1, 64), (4, 256), (16, 1024))

LANES = 128
BF16_ROWS = 16
CHUNK = 64
ATT_BLOCK = 128
ATT_REACH = 1024 // ATT_BLOCK
VMEM_LIMIT_BYTES = 56 * 1024 * 1024
WKV_PASSES_SCORE = 1
WKV_PASSES_INV = 1
WKV_PASSES_APPLY = 1
WKV_PASSES_STATE = 1
WKV_PASSES_CHAIN = 3

NN = (((1,), (0,)), ((), ()))
NT = (((1,), (1,)), ((), ()))
TN = (((0,), (0,)), ((), ()))

SLAB_R, SLAB_V, SLAB_KK, SLAB_DIR0, SLAB_G, SLAB_BONUS, NUM_SLABS = 0, 1, 2, 3, 9, 10, 11


def _params(*sem):
    return pltpu.CompilerParams(dimension_semantics=sem, vmem_limit_bytes=VMEM_LIMIT_BYTES)


def _dot(a, b, dims=NN):
    return lax.dot_general(a, b, dims, preferred_element_type=F32)


def _split(a):
    hi = a.astype(BF16)
    lo = (a - hi.astype(F32)).astype(BF16)
    return hi, lo


def _mm(a, b, dims=NN, passes=3):
    if passes == 1:
        return _dot(a.astype(BF16), b.astype(BF16), dims)
    ah, al = _split(a)
    bh, bl = _split(b)
    return _dot(ah, bh, dims) + (_dot(ah, bl, dims) + _dot(al, bh, dims))


def _mm_exact_rhs(a, b_exact):
    ah, al = _split(a)
    return _dot(ah, b_exact) + _dot(al, b_exact)


def _sigmoid(x):
    return 1.0 / (1.0 + jnp.exp(-x))


def _ada_kernel(c_ref, w_ref, b_ref, o_ref):
    c = c_ref[...]
    o_ref[...] = _mm(c * _sigmoid(c), w_ref[...]) + b_ref[...]


def _ada(c, ada_w, ada_b):
    bsz, d = c.shape
    n = ada_w.shape[1]
    tn = 1024
    return pl.pallas_call(
        _ada_kernel,
        out_shape=jax.ShapeDtypeStruct((bsz, n), F32),
        grid=(n // tn,),
        in_specs=[pl.BlockSpec((bsz, d), lambda j: (0, 0)),
                  pl.BlockSpec((d, tn), lambda j: (0, j)),
                  pl.BlockSpec((1, tn), lambda j: (0, j))],
        out_specs=pl.BlockSpec((bsz, tn), lambda j: (0, j)),
        compiler_params=_params("arbitrary"),
        name="ada",
    )(c, ada_w, ada_b.reshape(1, n))


def _inproj_kernel(x_ref, mod_ref, g1_ref, w_ref, gq_ref, gk_ref, bd_ref, qkv_ref, zrw_ref):
    x = x_ref[0]
    mod = mod_ref[0]
    ms = jnp.mean(x * x, axis=-1, keepdims=True)
    h = x * lax.rsqrt(ms + NORM_EPS) * g1_ref[...]
    h = (h * (1.0 + mod[1:2]) + mod[0:1]).astype(BF16)
    bd = bd_ref[...]

    def head_norm(z, g):
        ss = _mm_exact_rhs(z * z, bd) * (1.0 / HEAD_DIM)
        return z * lax.rsqrt(ss + NORM_EPS) * g

    a = ATT_WIDTH
    zq = _dot(h, w_ref[:, 0:a])
    qkv_ref[0, :, 0:a] = (head_norm(zq, gq_ref[...]) * (HEAD_DIM ** -0.5 * LOG2E)).astype(BF16)
    zk = _dot(h, w_ref[:, a:2 * a])
    qkv_ref[0, :, a:2 * a] = head_norm(zk, gk_ref[...]).astype(BF16)
    qkv_ref[0, :, 2 * a:3 * a] = _dot(h, w_ref[:, 2 * a:3 * a]).astype(BF16)
    zrw_ref[0] = _dot(h, w_ref[:, 3 * a:])


def _inproj(x, mod6, norm1_g, w_in_bf, gq, gk, bd512, tm=256):
    bsz, s, d = x.shape
    nin = w_in_bf.shape[1]
    return pl.pallas_call(
        _inproj_kernel,
        out_shape=(jax.ShapeDtypeStruct((bsz, s, 3 * ATT_WIDTH), BF16),
                   jax.ShapeDtypeStruct((bsz, s, SHIFT_WIDTH), F32)),
        grid=(bsz, s // tm),
        in_specs=[pl.BlockSpec((1, tm, d), lambda b, i: (b, i, 0)),
                  pl.BlockSpec((1, 6, d), lambda b, i: (b, 0, 0)),
                  pl.BlockSpec((1, d), lambda b, i: (0, 0)),
                  pl.BlockSpec((d, nin), lambda b, i: (0, 0)),
                  pl.BlockSpec((1, ATT_WIDTH), lambda b, i: (0, 0)),
                  pl.BlockSpec((1, ATT_WIDTH), lambda b, i: (0, 0)),
                  pl.BlockSpec((ATT_WIDTH, ATT_WIDTH), lambda b, i: (0, 0))],
        out_specs=(pl.BlockSpec((1, tm, 3 * ATT_WIDTH), lambda b, i: (b, i, 0)),
                   pl.BlockSpec((1, tm, SHIFT_WIDTH), lambda b, i: (b, i, 0))),
        compiler_params=_params("arbitrary", "arbitrary"),
        name="inproj",
    )(x, mod6, norm1_g, w_in_bf, gq, gk, bd512)


def _attn_window(nkb):
    return nkb if nkb <= 2 * ATT_REACH + 2 else 2 * ATT_REACH + 2


def _attn_bias_table(n_heads, nkb):
    reach = _attn_window(nkb) - 1
    nd = 2 * reach + 1
    dd = jnp.arange(nd, dtype=jnp.int32)[None, :, None]
    r = jnp.arange(ATT_BLOCK, dtype=jnp.int32)[:, None, None]
    c = jnp.arange(ATT_BLOCK, dtype=jnp.int32)[None, None, :]
    dt = r - c - (dd - reach) * ATT_BLOCK
    adt = jnp.abs(dt)
    mult = jnp.zeros(dt.shape, F32)
    for dil, half in ATT_HALF_WINDOWS:
        mult = mult + ((adt <= half) & (dt % dil == 0)).astype(F32)
    logm = jnp.where(mult > 0, jnp.log2(jnp.maximum(mult, 1.0)), MASK_VALUE)
    slopes = jnp.exp2(-8.0 * (jnp.arange(n_heads, dtype=F32) + 1.0) / n_heads) * LOG2E
    bias = logm[None] - slopes[:, None, None, None] * adt.astype(F32)[None]
    return bias.reshape(n_heads, ATT_BLOCK, nd * ATT_BLOCK)


def _attn_kernel(q_ref, k_ref, v_ref, bias_ref, o_ref, s_ref, *, nkb, win):
    qi = pl.program_id(2)
    blk = ATT_BLOCK
    q = q_ref[0]
    lane = lax.broadcasted_iota(jnp.int32, (blk, LANES), 1)
    first = lane < HEAD_DIM
    zero = jnp.zeros_like(q)
    qab = jnp.concatenate([jnp.where(first, q, zero), jnp.where(first, zero, q)], axis=0)
    ks = 0 if win == nkb else jnp.clip(qi - ATT_REACH, 0, nkb - win)
    reach = win - 1

    m = [jnp.full((blk, LANES), -jnp.inf, F32) for _ in range(2)]
    for w0 in range(0, win, 2):
        kj0 = ks + w0
        kslab = k_ref[0, pl.ds(pl.multiple_of(kj0 * blk, blk), 2 * blk), :]
        s = _dot(qab, kslab, NT)
        x0 = pl.multiple_of((kj0 - qi + reach) * blk, blk)
        for hh in range(2):
            sb = s[hh * blk:(hh + 1) * blk] + bias_ref[hh, :, pl.ds(x0, 2 * blk)]
            s_ref[hh, :, w0 * blk:(w0 + 2) * blk] = sb
            m[hh] = jnp.maximum(m[hh], jnp.maximum(sb[:, :blk], sb[:, blk:]))
    mrow = [jnp.max(m[hh], axis=-1, keepdims=True) for hh in range(2)]

    first2 = lax.broadcasted_iota(jnp.int32, (2 * blk, LANES), 1) < HEAD_DIM
    acc = [jnp.zeros((blk, LANES), F32) for _ in range(2)]
    for w0 in range(0, win, 2):
        vslab = v_ref[0, pl.ds(pl.multiple_of((ks + w0) * blk, blk), 2 * blk), :]
        one = jnp.ones_like(vslab)
        vs = (jnp.where(first2, vslab, one), jnp.where(first2, one, vslab))
        for hh in range(2):
            p = jnp.exp2(s_ref[hh, :, w0 * blk:(w0 + 2) * blk] - mrow[hh]).astype(BF16)
            acc[hh] = acc[hh] + _dot(p, vs[hh])
    num = jnp.where(first, acc[0], acc[1])
    den = jnp.where(first, pltpu.roll(acc[0], HEAD_DIM, axis=1), pltpu.roll(acc[1], HEAD_DIM, axis=1))
    o_ref[0] = (num / den).astype(o_ref.dtype)


def _attention(qkv, bias):
    bsz, s, _ = qkv.shape
    nkb = s // ATT_BLOCK
    win = _attn_window(nkb)
    assert nkb % 2 == 0 and bias.shape[2] == (2 * win - 1) * ATT_BLOCK
    npair = ATT_WIDTH // LANES
    return pl.pallas_call(
        functools.partial(_attn_kernel, nkb=nkb, win=win),
        out_shape=jax.ShapeDtypeStruct((bsz, s, ATT_WIDTH), BF16),
        grid=(npair, bsz, nkb),
        in_specs=[pl.BlockSpec((1, ATT_BLOCK, LANES), lambda hp, b, i: (b, i, hp)),
                  pl.BlockSpec((1, s, LANES), lambda hp, b, i: (b, 0, npair + hp)),
                  pl.BlockSpec((1, s, LANES), lambda hp, b, i: (b, 0, 2 * npair + hp)),
                  pl.BlockSpec((2, ATT_BLOCK, bias.shape[2]), lambda hp, b, i: (hp, 0, 0))],
        out_specs=pl.BlockSpec((1, ATT_BLOCK, LANES), lambda hp, b, i: (b, i, hp)),
        scratch_shapes=[pltpu.VMEM((2, ATT_BLOCK, win * ATT_BLOCK), F32)],
        compiler_params=_params("arbitrary", "arbitrary", "arbitrary"),
        name="attn",
    )(qkv, qkv, qkv, bias)


def _prep_kernel(z_ref, zp_ref, zn_ref, mup_ref, mun_ref, wup_ref, aup_ref, gup_ref, w0_ref, a0_ref,
                 kk_ref, ka_ref, rk_ref, bd_ref, o_ref, *, tq):
    i = pl.program_id(1)
    last = pl.num_programs(1) - 1
    z = z_ref[0]
    row = lax.broadcasted_iota(jnp.int32, (tq, 1), 0)
    prev_row = zp_ref[0, 7:8, :] * (i > 0).astype(F32)
    next_row = zn_ref[0, 0:1, :] * (i < last).astype(F32)
    zp = jnp.where(row == 0, prev_row, pltpu.roll(z, 1, axis=0))
    zn = jnp.where(row == tq - 1, next_row, pltpu.roll(z, tq - 1, axis=0))
    zs = z + mup_ref[...] * (zp - z) + mun_ref[...] * (zn - z)

    w = RWKV_WIDTH
    r = zs[:, 0:w]
    k = zs[:, w:2 * w]
    v = zs[:, 2 * w:3 * w]
    xwa = zs[:, 3 * w:3 * w + LANES]
    xg = zs[:, 3 * w + LANES:]
    bd = bd_ref[...]

    g = _dot(_sigmoid(xg).astype(BF16), gup_ref[...])
    kk = k * kk_ref[...]
    ss = _mm_exact_rhs(kk * kk, bd)
    kk = kk * lax.rsqrt(jnp.maximum(ss, 1e-12))
    ka = ka_ref[...]
    txw = jnp.tanh(xwa).astype(BF16)
    xab = xwa.astype(BF16)

    o_ref[0, :, SLAB_R * w:(SLAB_R + 1) * w] = r
    o_ref[0, :, SLAB_V * w:(SLAB_V + 1) * w] = v
    o_ref[0, :, SLAB_KK * w:(SLAB_KK + 1) * w] = kk
    o_ref[0, :, SLAB_G * w:(SLAB_G + 1) * w] = g
    a_sum = jnp.zeros_like(k)
    for d in range(2):
        y = w0_ref[d:d + 1, :] + _dot(txw, wup_ref[d])
        wlog = -(jnp.maximum(-y, 0.0) + jnp.log(1.0 + jnp.exp(-jnp.abs(y)))) - 0.5
        a = _sigmoid(a0_ref[d:d + 1, :] + _dot(xab, aup_ref[d]))
        a_sum = a_sum + a
        base = (SLAB_DIR0 + 3 * d) * w
        o_ref[0, :, base:base + w] = -jnp.exp(wlog)
        o_ref[0, :, base + w:base + 2 * w] = k * (1.0 + (a - 1.0) * ka)
        o_ref[0, :, base + 2 * w:base + 3 * w] = kk * a
    k_bonus = k * (1.0 + (0.5 * a_sum - 1.0) * ka)
    bsum = _mm_exact_rhs(r * k_bonus * rk_ref[...], bd)
    o_ref[0, :, SLAB_BONUS * w:(SLAB_BONUS + 1) * w] = bsum * v


def _rwkv_prep(zrw, mu_prev, mu_next, wup_pad, aup_pad, g_up_bf, w0, a0, k_k, k_a, r_k, bd512, tq=256):
    bsz, s, sw = zrw.shape
    w = RWKV_WIDTH
    nt = s // tq
    row = lambda b, i: (0, 0)
    return pl.pallas_call(
        functools.partial(_prep_kernel, tq=tq),
        out_shape=jax.ShapeDtypeStruct((bsz, s, NUM_SLABS * w), F32),
        grid=(bsz, nt),
        in_specs=[pl.BlockSpec((1, tq, sw), lambda b, i: (b, i, 0)),
                  pl.BlockSpec((1, 8, sw), lambda b, i: (b, jnp.maximum(i * (tq // 8) - 1, 0), 0)),
                  pl.BlockSpec((1, 8, sw), lambda b, i: (b, jnp.minimum((i + 1) * (tq // 8), s // 8 - 1), 0)),
                  pl.BlockSpec((1, sw), row),
                  pl.BlockSpec((1, sw), row),
                  pl.BlockSpec((2, LANES, w), lambda b, i: (0, 0, 0)),
                  pl.BlockSpec((2, LANES, w), lambda b, i: (0, 0, 0)),
                  pl.BlockSpec((GATE_LORA, w), row),
                  pl.BlockSpec((2, w), row),
                  pl.BlockSpec((2, w), row),
                  pl.BlockSpec((1, w), row),
                  pl.BlockSpec((1, w), row),
                  pl.BlockSpec((1, w), row),
                  pl.BlockSpec((w, w), row)],
        out_specs=pl.BlockSpec((1, tq, NUM_SLABS * w), lambda b, i: (b, i, 0)),
        compiler_params=_params("arbitrary", "arbitrary"),
        name="rwkv_prep",
    )(zrw, zrw, zrw, mu_prev, mu_next, wup_pad, aup_pad, g_up_bf, w0, a0, k_k, k_a, r_k, bd512)


def _block_diag(x, first):
    zero = jnp.zeros_like(x)
    return jnp.concatenate([jnp.where(first, x, zero), jnp.where(first, zero, x)], axis=0)


def _p1_kernel(r_ref, v_ref, kk_ref, lw_ref, kd_ref, be_ref, rp_ref, qp_ref, g_ref, h_ref, *, cpb):
    c = CHUNK
    sign = 1 - 2 * pl.program_id(2)
    rowi = lax.broadcasted_iota(jnp.int32, (c, LANES), 0)
    lane = lax.broadcasted_iota(jnp.int32, (c, LANES), 1)
    coli = lane & (c - 1)
    first = lane < HEAD_DIM
    before = sign * (rowi - coli)
    strict2 = before > 0
    incl2 = before >= 0
    eye2 = (coli == rowi).astype(F32)
    r64 = lax.broadcasted_iota(jnp.int32, (c, c), 0)
    c64 = lax.broadcasted_iota(jnp.int32, (c, c), 1)
    cum = (sign * (r64 - c64) >= 0).astype(BF16)
    r128 = lax.broadcasted_iota(jnp.int32, (LANES, LANES), 0)
    c128 = lax.broadcasted_iota(jnp.int32, (LANES, LANES), 1)
    same_head = (r128 // HEAD_DIM) == (c128 // HEAD_DIM)
    eye128 = (r128 == c128).astype(F32)
    zero128 = jnp.zeros((LANES, LANES), F32)

    js = range(cpb)
    sls = [pl.ds(j * c, c) for j in js]
    bdg = lambda x: _block_diag(x, first)
    r = [r_ref[0, sl, :] for sl in sls]
    v = [v_ref[0, sl, :] for sl in sls]
    kk = [kk_ref[0, sl, :] for sl in sls]
    lw = [lw_ref[0, sl, :] for sl in sls]
    kd = [kd_ref[0, sl, :] for sl in sls]
    be = [be_ref[0, sl, :] for sl in sls]

    def cumsum(x):
        l1 = x.astype(BF16)
        rem = x - l1.astype(F32)
        l2 = rem.astype(BF16)
        l3 = (rem - l2.astype(F32)).astype(BF16)
        return _dot(cum, l1) + (_dot(cum, l2) + _dot(cum, l3))

    cs = [cumsum(lw[j]) for j in js]
    g_inv = [jnp.exp(-cs[j]) for j in js]
    g_tot = [jnp.exp(jnp.sum(lw[j], axis=0, keepdims=True)) for j in js]
    ab = [-kk[j] * jnp.exp(cs[j] - lw[j]) for j in js]
    rb = [r[j] * jnp.exp(cs[j]) for j in js]
    bt = [be[j] * g_inv[j] for j in js]
    kt = [kd[j] * g_inv[j] for j in js]

    lhs = [jnp.concatenate([ab[j], rb[j]], axis=0) for j in js]
    sb = [_mm(lhs[j], bdg(bt[j]), NT, WKV_PASSES_SCORE) for j in js]
    sk = [_mm(lhs[j], bdg(kt[j]), NT, WKV_PASSES_SCORE) for j in js]
    m_ab = [jnp.where(strict2, sb[j][:c], 0.0) for j in js]
    n_rb = [jnp.where(incl2, sb[j][c:], 0.0) for j in js]
    m_ak = [jnp.where(strict2, sk[j][:c], 0.0) for j in js]
    n_rk = [jnp.where(incl2, sk[j][c:], 0.0) for j in js]

    t = [eye2 + m_ab[j] for j in js]
    p = m_ab
    for _ in range(int(math.log2(c)) - 2):
        p = [_mm(p[j], bdg(p[j]), NN, WKV_PASSES_INV) for j in js]
        t = [t[j] + _mm(t[j], bdg(p[j]), NN, WKV_PASSES_INV) for j in js]
    res = [eye2 - _mm(eye2 - m_ab[j], bdg(t[j]), NN, 3) for j in js]
    t = [t[j] + _mm(t[j], bdg(res[j]), NN, 1) for j in js]

    vbd = [bdg(v[j]) for j in js]
    pm = [_mm(m_ak[j], vbd[j], NN, WKV_PASSES_APPLY) for j in js]
    ap = [_mm(t[j], bdg(ab[j]), NN, WKV_PASSES_APPLY) for j in js]
    pp = [_mm(t[j], bdg(pm[j]), NN, WKV_PASSES_APPLY) for j in js]
    for j in js:
        rp_ref[0, 0, 0, sls[j], :] = rb[j] + _mm(n_rb[j], bdg(ap[j]), NN, WKV_PASSES_APPLY)
        qp_ref[0, 0, 0, sls[j], :] = (_mm(n_rb[j], bdg(pp[j]), NN, WKV_PASSES_APPLY)
                                      + _mm(n_rk[j], vbd[j], NN, WKV_PASSES_APPLY))
    for j in js:
        gmat = eye128 + jnp.where(same_head, _mm(ap[j], bt[j], TN, WKV_PASSES_STATE), zero128)
        hmat = jnp.where(same_head, _mm(pp[j], bt[j], TN, WKV_PASSES_STATE)
                         + _mm(v[j], kt[j], TN, WKV_PASSES_STATE), zero128)
        g_ref[0, 0, 0, j] = gmat * g_tot[j]
        h_ref[0, 0, 0, j] = hmat * g_tot[j]


def _rwkv_p1(pk, cpb=16):
    bsz, s, _ = pk.shape
    nc = s // CHUNK
    npair = RWKV_WIDTH // LANES
    rows = cpb * CHUNK

    def slab(sidx):
        return pl.BlockSpec((1, rows, LANES), lambda b, hp, d, ci: (b, ci, sidx * npair + hp))

    def dslab(off):
        return pl.BlockSpec((1, rows, LANES),
                            lambda b, hp, d, ci: (b, ci, (SLAB_DIR0 + 3 * d + off) * npair + hp))

    seq = pl.BlockSpec((1, 1, 1, rows, LANES), lambda b, hp, d, ci: (b, hp, d, ci, 0))
    mat = pl.BlockSpec((1, 1, 1, cpb, LANES, LANES), lambda b, hp, d, ci: (b, hp, d, ci, 0, 0))
    return pl.pallas_call(
        functools.partial(_p1_kernel, cpb=cpb),
        out_shape=(jax.ShapeDtypeStruct((bsz, npair, 2, s, LANES), F32),
                   jax.ShapeDtypeStruct((bsz, npair, 2, s, LANES), F32),
                   jax.ShapeDtypeStruct((bsz, npair, 2, nc, LANES, LANES), F32),
                   jax.ShapeDtypeStruct((bsz, npair, 2, nc, LANES, LANES), F32)),
        grid=(bsz, npair, 2, nc // cpb),
        in_specs=[slab(SLAB_R), slab(SLAB_V), slab(SLAB_KK), dslab(0), dslab(1), dslab(2)],
        out_specs=(seq, seq, mat, mat),
        compiler_params=_params("arbitrary", "arbitrary", "arbitrary", "arbitrary"),
        name="rwkv_p1",
    )(pk, pk, pk, pk, pk, pk)


def _p2_kernel(rp_ref, qp_ref, g_ref, h_ref, gate_ref, bonus_ref, lng_ref, lnb_ref, bd_ref, o_ref,
               yf_ref, yb_ref, *, nc, te):
    c = CHUNK

    def step(j, carry):
        sf, sb = carry
        jf = j
        jb = nc - 1 - j
        slf = pl.ds(pl.multiple_of(jf * c, c), c)
        slb = pl.ds(pl.multiple_of(jb * c, c), c)
        yf_ref[slf, :] = _mm(rp_ref[0, 0, 0, slf, :], sf, NT, WKV_PASSES_CHAIN) + qp_ref[0, 0, 0, slf, :]
        yb_ref[slb, :] = _mm(rp_ref[0, 0, 1, slb, :], sb, NT, WKV_PASSES_CHAIN) + qp_ref[0, 0, 1, slb, :]
        sf = _mm(sf, g_ref[0, 0, 0, jf], NN, WKV_PASSES_CHAIN) + h_ref[0, 0, 0, jf]
        sb = _mm(sb, g_ref[0, 0, 1, jb], NN, WKV_PASSES_CHAIN) + h_ref[0, 0, 1, jb]
        return sf, sb

    zero = jnp.zeros((LANES, LANES), F32)
    lax.fori_loop(0, nc, step, (zero, zero))

    bd = bd_ref[...]
    lng = lng_ref[...]
    lnb = lnb_ref[...]
    inv = 1.0 / HEAD_DIM

    def epi(i, _):
        sl = pl.ds(pl.multiple_of(i * te, te), te)
        y = yf_ref[sl, :] + yb_ref[sl, :]
        mean = _mm_exact_rhs(y, bd) * inv
        yc = y - mean
        var = _mm_exact_rhs(yc * yc, bd) * inv
        yn = yc * lax.rsqrt(var + LNX_EPS) * lng + lnb
        o_ref[0, sl, :] = ((yn + bonus_ref[0, sl, :]) * gate_ref[0, sl, :]).astype(o_ref.dtype)
        return 0

    lax.fori_loop(0, (nc * c) // te, epi, 0)


def _rwkv_p2(rp, qp, gm, hm, pk, lnx_g, lnx_b, bd128, te=256):
    bsz, npair, _, s, _ = rp.shape
    nc = s // CHUNK
    seq = pl.BlockSpec((1, 1, 2, s, LANES), lambda b, hp: (b, hp, 0, 0, 0))
    mat = pl.BlockSpec((1, 1, 2, nc, LANES, LANES), lambda b, hp: (b, hp, 0, 0, 0, 0))
    return pl.pallas_call(
        functools.partial(_p2_kernel, nc=nc, te=min(te, s)),
        out_shape=jax.ShapeDtypeStruct((bsz, s, RWKV_WIDTH), BF16),
        grid=(bsz, npair),
        in_specs=[seq, seq, mat, mat,
                  pl.BlockSpec((1, s, LANES), lambda b, hp: (b, 0, SLAB_G * npair + hp)),
                  pl.BlockSpec((1, s, LANES), lambda b, hp: (b, 0, SLAB_BONUS * npair + hp)),
                  pl.BlockSpec((1, LANES), lambda b, hp: (0, hp)),
                  pl.BlockSpec((1, LANES), lambda b, hp: (0, hp)),
                  pl.BlockSpec((LANES, LANES), lambda b, hp: (0, 0))],
        out_specs=pl.BlockSpec((1, s, LANES), lambda b, hp: (b, 0, hp)),
        scratch_shapes=[pltpu.VMEM((s, LANES), F32), pltpu.VMEM((s, LANES), F32)],
        compiler_params=_params("arbitrary", "arbitrary"),
        name="rwkv_p2",
    )(rp, qp, gm, hm, pk, pk, lnx_g, lnx_b, bd128)


def _outproj_kernel(att_ref, rw_ref, x_ref, mod_ref, g2_ref, wo_ref, wq_ref, k1_ref, k2_ref,
                    x1_ref, h2_ref, s1_ref, s2_ref):
    mod = mod_ref[0]
    a = ATT_WIDTH
    mixed = _dot(att_ref[0], wo_ref[0:a, :]) + _dot(rw_ref[0], wo_ref[a:, :])
    x1 = x_ref[0] + mod[2:3] * mixed
    x1_ref[0] = x1
    ms = jnp.mean(x1 * x1, axis=-1, keepdims=True)
    h2 = x1 * lax.rsqrt(ms + NORM_EPS) * g2_ref[...]
    h2 = (h2 * (1.0 + mod[4:5]) + mod[3:4]).astype(BF16)
    h2_ref[0] = h2
    q = _dot(h2, wq_ref[...])
    k1 = k1_ref[...]
    k2 = k2_ref[...]
    for h in range(PEER_HEADS):
        base = h * 2 * LANES
        s1 = _dot(k1, q[:, base:base + LANES].astype(BF16), NT)
        s2 = _dot(k2, q[:, base + LANES:base + 2 * LANES].astype(BF16), NT)
        for g in range(s1.shape[1] // LANES):
            s1_ref[h, g] = s1[:, g * LANES:(g + 1) * LANES]
            s2_ref[h, g] = s2[:, g * LANES:(g + 1) * LANES]


def _outproj(att, rw, x, mod6, norm2_g, w_out_bf, wq_bf, k1_bf, k2_bf, tm=256):
    bsz, s, d = x.shape
    nt = s // tm
    t = bsz * s
    nq = wq_bf.shape[1]
    const = lambda b, i: (0, 0)
    gpt = tm // LANES
    tok = pl.BlockSpec((PEER_HEADS, gpt, PEER_N_KEYS, LANES), lambda b, i: (0, b * nt + i, 0, 0))
    stat = jax.ShapeDtypeStruct((PEER_HEADS, t // LANES, PEER_N_KEYS, LANES), F32)
    return pl.pallas_call(
        _outproj_kernel,
        out_shape=(jax.ShapeDtypeStruct((bsz, s, d), F32),
                   jax.ShapeDtypeStruct((bsz, s, d), BF16),
                   stat, stat),
        grid=(bsz, nt),
        in_specs=[pl.BlockSpec((1, tm, ATT_WIDTH), lambda b, i: (b, i, 0)),
                  pl.BlockSpec((1, tm, RWKV_WIDTH), lambda b, i: (b, i, 0)),
                  pl.BlockSpec((1, tm, d), lambda b, i: (b, i, 0)),
                  pl.BlockSpec((1, 6, d), lambda b, i: (b, 0, 0)),
                  pl.BlockSpec((1, d), const),
                  pl.BlockSpec((d, d), const),
                  pl.BlockSpec((d, nq), const),
                  pl.BlockSpec((PEER_N_KEYS, LANES), const),
                  pl.BlockSpec((PEER_N_KEYS, LANES), const)],
        out_specs=(pl.BlockSpec((1, tm, d), lambda b, i: (b, i, 0)),
                   pl.BlockSpec((1, tm, d), lambda b, i: (b, i, 0)),
                   tok, tok),
        compiler_params=_params("arbitrary", "arbitrary"),
        name="outproj",
    )(att, rw, x, mod6, norm2_g, w_out_bf, wq_bf, k1_bf, k2_bf)


def _topk_kernel(s1_ref, s2_ref, c1_ref, e1_ref, r2_ref, e2_ref, *, groups):
    kk = PEER_TOPK
    neg = -jnp.inf
    rows16 = lax.broadcasted_iota(jnp.int32, (kk, LANES), 0)
    rows8 = lax.broadcasted_iota(jnp.int32, (8, LANES), 0)

    def top_values(x):
        def it(i, carry):
            x, out, _, rank = carry
            m = jnp.max(x, axis=0, keepdims=True)
            hit = x >= m
            return (jnp.where(hit, neg, x), jnp.where(rows16 == i, m, out), m,
                    jnp.where(hit, jnp.asarray(i, F32), rank))
        init = jnp.max(x, axis=0, keepdims=True)
        _, out, m17, rank = lax.fori_loop(
            0, kk + 1, it, (x, jnp.full((kk, LANES), neg, F32), init, jnp.full(x.shape, float(kk), F32)))
        return out, m17, rank

    def candidates(a, b):
        b8 = b[0:8, :]
        parts = [a[0:1, :] + b, a[1:2, :] + b8]
        for i in range(2, 8):
            parts.append(jnp.where(rows8 < kk // (i + 1), a[i:i + 1, :] + b8, neg))
        parts.append(a[8:16, :] + b[0:1, :])
        return jnp.concatenate(parts, axis=0)

    def per_tile(it_idx, _):
        g = it_idx // PEER_HEADS
        h = it_idx % PEER_HEADS
        s1 = s1_ref[h, g]
        s2 = s2_ref[h, g]
        a, a17, _ = top_values(s1)
        b, b17, rank2 = top_values(s2)
        a0 = a[0:1, :]
        b0 = b[0:1, :]
        cand = candidates(a, b)

        def it(i, carry):
            c, m_prev, _ = carry
            m = jnp.max(c, axis=0, keepdims=True)
            return jnp.where(c >= m, neg, c), m, m_prev
        top = jnp.max(cand, axis=0, keepdims=True)
        _, nxt, tau = lax.fori_loop(0, kk + 1, it, (cand, top, top))
        nxt = jnp.maximum(nxt, jnp.maximum(a17 + b0, a0 + b17))
        thr = 0.5 * (tau + nxt)
        z = jnp.sum(jnp.where(cand >= tau, jnp.exp(cand - top), 0.0), axis=0, keepdims=True)
        count = jnp.zeros_like(s1)
        for r in range(kk):
            ar = a[r:r + 1, :]
            n_r = jnp.sum(jnp.where(b >= thr - ar, 1.0, 0.0), axis=0, keepdims=True)
            count = jnp.where(s1 == ar, n_r, count)
        c1_ref[h, g] = count
        e1_ref[h, g] = jnp.exp(s1 - a0) / z
        r2_ref[h, g] = pltpu.bitcast(rank2.astype(BF16), jnp.uint32)
        e2_ref[h, g] = pltpu.bitcast(jnp.exp(s2 - b0).astype(BF16), jnp.uint32)
        return 0

    lax.fori_loop(0, groups * PEER_HEADS, per_tile, 0)


def _peer_topk(s1t, s2t, tn=512):
    nh, ng, nk, _ = s1t.shape
    groups = tn // LANES
    blk = pl.BlockSpec((nh, groups, nk, LANES), lambda i: (0, i, 0, 0))
    f32 = jax.ShapeDtypeStruct(s1t.shape, F32)
    b16 = jax.ShapeDtypeStruct((nh, ng, nk // 2, LANES), jnp.uint32)
    pblk = pl.BlockSpec((nh, groups, nk // 2, LANES), lambda i: (0, i, 0, 0))
    return pl.pallas_call(
        functools.partial(_topk_kernel, groups=groups),
        out_shape=(f32, f32, b16, b16),
        grid=(ng // groups,),
        in_specs=[blk, blk],
        out_specs=(blk, blk, pblk, pblk),
        compiler_params=_params("arbitrary"),
        name="peer_topk",
    )(s1t, s2t)


def _peer_kernel(h2_ref, u_ref, vt_ref, c1_ref, e1_ref, r2_ref, e2_ref, x1_ref, mod_ref, o_ref,
                 acc_ref, act0_ref, act1_ref, p0_ref, p1_ref, *, tn, te, n_tiles):
    k = pl.program_id(0)
    n_items = pl.num_programs(0) - 2
    nk = PEER_N_KEYS
    ngroups = tn // LANES

    @pl.when(k == 0)
    def _():
        acc_ref[...] = jnp.zeros_like(acc_ref)
        for ref in (act0_ref, act1_ref, p0_ref, p1_ref):
            ref[...] = jnp.zeros_like(ref)

    ni = te // nk
    i0 = pl.multiple_of((jnp.clip(k - 1, 0, n_items - 1) % n_tiles) * ni, ni)
    tile_c = jnp.clip(k - 2, 0, n_items - 1) % n_tiles
    first_c = tile_c == 0

    d_model = acc_ref.shape[0]
    halves = [(mh, nh, kh) for mh in range(2) for nh in range(ngroups // 2) for kh in range(2)]

    def stage_c(p_a, mh, nh, kh):
        rows = slice(mh * (d_model // 2), (mh + 1) * (d_model // 2))
        cols = slice(nh * 2 * LANES, (nh + 1) * 2 * LANES)
        ks = slice(kh * (te // 2), (kh + 1) * (te // 2))
        p_prev = jnp.concatenate([p_a[2 * nh, ks, :], p_a[2 * nh + 1, ks, :]], axis=1)
        prev = acc_ref[rows, cols]
        if kh == 0:
            prev = jnp.where(first_c, jnp.zeros_like(prev), prev)
        acc_ref[rows, cols] = prev + _dot(vt_ref[rows, ks], p_prev)

    def stage_a(act_a, mh, nh, kh):
        rows = slice(mh * (te // 2), (mh + 1) * (te // 2))
        ks = slice(kh * (d_model // 2), (kh + 1) * (d_model // 2))
        act = _dot(u_ref[rows, ks], h2_ref[nh * 2 * LANES:(nh + 1) * 2 * LANES, ks], NT)
        if kh == 0:
            act_a[2 * nh, rows, :] = act[:, :LANES]
            act_a[2 * nh + 1, rows, :] = act[:, LANES:]
        else:
            act_a[2 * nh, rows, :] += act[:, :LANES]
            act_a[2 * nh + 1, rows, :] += act[:, LANES:]

    def rows_bf16(row):
        tile = jnp.broadcast_to(row, (BF16_ROWS, LANES)).astype(BF16)
        return jnp.concatenate([tile] * (nk // BF16_ROWS), axis=0)

    quad = 4

    def stage_b(act_b, p_b, tg, iq, pieces):
        gates = [jnp.zeros((nk, LANES), BF16) for _ in range(quad)]
        for h in range(PEER_HEADS):
            if h % (PEER_HEADS // 2) == 0 and pieces:
                pieces.pop(0)()
            c1 = c1_ref[h, tg, pl.ds(i0, ni), :]
            e1 = e1_ref[h, tg, pl.ds(i0, ni), :]
            rank2 = pltpu.bitcast(r2_ref[h, tg], BF16)
            e2 = pltpu.bitcast(e2_ref[h, tg], BF16)
            for q in range(quad):
                il = iq * quad + q
                sel = rank2 < rows_bf16(c1[il:il + 1, :])
                gates[q] = gates[q] + jnp.where(sel, e2 * rows_bf16(e1[il:il + 1, :]), jnp.zeros_like(e2))
        for q in range(quad):
            il = iq * quad + q
            a = act_b[tg, il * nk:(il + 1) * nk, :]
            gelu = 0.5 * a * (1.0 + lax.erf(a * (2.0 ** -0.5)))
            p_b[tg, il * nk:(il + 1) * nk, :] = gates[q] * gelu.astype(BF16)

    def stages(act_a, act_b, p_a, p_b):
        pieces = ([functools.partial(stage_c, p_a, *hv) for hv in halves]
                  + [functools.partial(stage_a, act_a, *hv) for hv in halves])
        for tg in range(ngroups):
            for iq in range(ni // quad):
                stage_b(act_b, p_b, tg, iq, pieces)
        while pieces:
            pieces.pop(0)()

    @pl.when(k % 2 == 0)
    def _():
        stages(act0_ref, act1_ref, p0_ref, p1_ref)

    @pl.when(k % 2 == 1)
    def _():
        stages(act1_ref, act0_ref, p1_ref, p0_ref)

    @pl.when((tile_c == n_tiles - 1) & (k >= 2))
    def _():
        o_ref[...] = x1_ref[...] + mod_ref[0, 5:6, :] * acc_ref[...].T


def _peer_ffn(h2, u_bf, vt_bf, c1, e1, r2, e2, x1, mod6, seq, tn=512, te=1024):
    t, d = h2.shape
    ne = u_bf.shape[0]
    ngroups = tn // LANES
    n_tiles = ne // te
    n_items = (t // tn) * n_tiles

    def item(k, lag):
        w = jnp.clip(k - lag, 0, n_items - 1)
        return w // n_tiles, w % n_tiles

    stat = pl.BlockSpec((PEER_HEADS, ngroups, PEER_N_KEYS, LANES), lambda k: (0, item(k, 1)[0], 0, 0))
    pstat = pl.BlockSpec((PEER_HEADS, ngroups, PEER_N_KEYS // 2, LANES), lambda k: (0, item(k, 1)[0], 0, 0))
    return pl.pallas_call(
        functools.partial(_peer_kernel, tn=tn, te=te, n_tiles=n_tiles),
        out_shape=jax.ShapeDtypeStruct((t, d), F32),
        grid=(n_items + 2,),
        in_specs=[pl.BlockSpec((tn, d), lambda k: (item(k, 0)[0], 0)),
                  pl.BlockSpec((te, d), lambda k: (item(k, 0)[1], 0)),
                  pl.BlockSpec((d, te), lambda k: (0, item(k, 2)[1])),
                  stat, stat, pstat, pstat,
                  pl.BlockSpec((tn, d), lambda k: (item(k, 2)[0], 0)),
                  pl.BlockSpec((1, 6, d), lambda k: ((item(k, 2)[0] * tn) // seq, 0, 0))],
        out_specs=pl.BlockSpec((tn, d), lambda k: (item(k, 2)[0], 0)),
        scratch_shapes=[pltpu.VMEM((d, tn), F32),
                        pltpu.VMEM((ngroups, te, LANES), F32), pltpu.VMEM((ngroups, te, LANES), F32),
                        pltpu.VMEM((ngroups, te, LANES), BF16), pltpu.VMEM((ngroups, te, LANES), BF16)],
        compiler_params=_params("arbitrary"),
        name="peer_ffn",
    )(h2, u_bf, vt_bf, c1, e1, r2, e2, x1, mod6)


def _head_block_diag(n):
    idx = jnp.arange(n, dtype=jnp.int32) // HEAD_DIM
    return (idx[:, None] == idx[None, :]).astype(BF16)


def _layer(x, c, ada_w, ada_b, norm1_g, w_in, mu_prev, mu_next, q_norm_g, k_norm_g, w_decay0, w_decay_up,
           a_gate0, a_gate_up, g_up, k_k, k_a, r_k, lnx_g, lnx_b, w_out, norm2_g, peer_w_query,
           peer_sub_keys1, peer_sub_keys2, peer_u, peer_v):
    bsz, s, d = x.shape
    n_att_heads = ATT_WIDTH // HEAD_DIM
    bd512 = _head_block_diag(ATT_WIDTH)
    bd128 = _head_block_diag(LANES)

    mod6 = _ada(c, ada_w, ada_b).reshape(bsz, 6, d)
    gq = jnp.tile(q_norm_g, n_att_heads).reshape(1, ATT_WIDTH)
    gk = jnp.tile(k_norm_g, n_att_heads).reshape(1, ATT_WIDTH)
    qkv, zrw = _inproj(x, mod6, norm1_g.reshape(1, d), w_in.astype(BF16), gq, gk, bd512)

    att = _attention(qkv, _attn_bias_table(n_att_heads, s // ATT_BLOCK))

    zero_w = jnp.zeros((2, LANES - DECAY_LORA, RWKV_WIDTH), F32)
    wup_pad = jnp.concatenate([w_decay_up, zero_w], axis=1).astype(BF16)
    aup_pad = jnp.concatenate([zero_w, a_gate_up], axis=1).astype(BF16)
    pk = _rwkv_prep(zrw, mu_prev.reshape(1, -1), mu_next.reshape(1, -1), wup_pad, aup_pad, g_up.astype(BF16),
                    w_decay0, a_gate0, k_k.reshape(1, -1), k_a.reshape(1, -1), r_k.reshape(1, -1), bd512)
    rp, qp, gm, hm = _rwkv_p1(pk)
    rw = _rwkv_p2(rp, qp, gm, hm, pk, lnx_g.reshape(1, -1), lnx_b.reshape(1, -1), bd128)

    x1, h2, s1t, s2t = _outproj(att, rw, x, mod6, norm2_g.reshape(1, d), w_out.astype(BF16),
                                peer_w_query.astype(BF16), peer_sub_keys1.astype(BF16),
                                peer_sub_keys2.astype(BF16))
    c1, e1, r2, e2 = _peer_topk(s1t, s2t)
    out = _peer_ffn(h2.reshape(bsz * s, d), peer_u.astype(BF16), peer_v.T.astype(BF16), c1, e1, r2, e2,
                    x1.reshape(bsz * s, d), mod6, s)
    return out.reshape(bsz, s, d)


def kernel(x, c, ada_w, ada_b, norm1_g, w_in, mu_prev, mu_next, q_norm_g, k_norm_g, w_decay0, w_decay_up,
           a_gate0, a_gate_up, g_up, k_k, k_a, r_k, lnx_g, lnx_b, w_out, norm2_g, peer_w_query,
           peer_sub_keys1, peer_sub_keys2, peer_u, peer_v):
    depth = ada_w.shape[0]
    for l in range(depth):
        x = _layer(x, c, ada_w[l], ada_b[l], norm1_g[l], w_in[l], mu_prev[l], mu_next[l], q_norm_g[l],
                   k_norm_g[l], w_decay0[l], w_decay_up[l], a_gate0[l], a_gate_up[l], g_up[l], k_k[l], k_a[l],
                   r_k[l], lnx_g[l], lnx_b[l], w_out[l], norm2_g[l], peer_w_query[l], peer_sub_keys1[l],
                   peer_sub_keys2[l], peer_u[l], peer_v[l])
    return x
```

```python
import functools
import math

import jax
import jax.numpy as jnp
from jax import lax
from jax.experimental import pallas as pl
from jax.experimental.pallas import tpu as pltpu

F32 = jnp.float32
BF16 = jnp.bfloat16

HEAD_DIM = 64
ATT_WIDTH = 512
RWKV_WIDTH = 512
DECAY_LORA = 64
AAA_LORA = 64
GATE_LORA = 128
SHIFT_WIDTH = 3 * RWKV_WIDTH + DECAY_LORA + AAA_LORA + GATE_LORA
PEER_N_KEYS = 128
PEER_HEADS = 8
PEER_TOPK = 16
NORM_EPS = 1e-6
LNX_EPS = 64e-5
MASK_VALUE = -1e30
LOG2E = 1.4426950408889634
ATT_HALF_WINDOWS = ((1, 64), (4, 256), (16, 1024))

LANES = 128
BF16_ROWS = 16
CHUNK = 64
ATT_BLOCK = 128
ATT_REACH = 1024 // ATT_BLOCK
VMEM_LIMIT_BYTES = 56 * 1024 * 1024
WKV_PASSES_SCORE = 1
WKV_PASSES_INV = 1
WKV_PASSES_APPLY = 1
WKV_PASSES_STATE = 1
WKV_PASSES_CHAIN = 3

NN = (((1,), (0,)), ((), ()))
NT = (((1,), (1,)), ((), ()))
TN = (((0,), (0,)), ((), ()))

SLAB_R, SLAB_V, SLAB_KK, SLAB_DIR0, SLAB_G, SLAB_BONUS, NUM_SLABS = 0, 1, 2, 3, 9, 10, 11


def _params(*sem):
    return pltpu.CompilerParams(dimension_semantics=sem, vmem_limit_bytes=VMEM_LIMIT_BYTES)


def _dot(a, b, dims=NN):
    return lax.dot_general(a, b, dims, preferred_element_type=F32)


def _split(a):
    hi = a.astype(BF16)
    lo = (a - hi.astype(F32)).astype(BF16)
    return hi, lo


def _mm(a, b, dims=NN, passes=3):
    if passes == 1:
        return _dot(a.astype(BF16), b.astype(BF16), dims)
    ah, al = _split(a)
    bh, bl = _split(b)
    return _dot(ah, bh, dims) + (_dot(ah, bl, dims) + _dot(al, bh, dims))


def _mm_exact_rhs(a, b_exact):
    ah, al = _split(a)
    return _dot(ah, b_exact) + _dot(al, b_exact)


def _sigmoid(x):
    return 1.0 / (1.0 + jnp.exp(-x))


def _ada_kernel(c_ref, w_ref, b_ref, o_ref):
    c = c_ref[...]
    o_ref[...] = _mm(c * _sigmoid(c), w_ref[...]) + b_ref[...]


def _ada(c, ada_w, ada_b):
    bsz, d = c.shape
    n = ada_w.shape[1]
    tn = 1024
    return pl.pallas_call(
        _ada_kernel,
        out_shape=jax.ShapeDtypeStruct((bsz, n), F32),
        grid=(n // tn,),
        in_specs=[pl.BlockSpec((bsz, d), lambda j: (0, 0)),
                  pl.BlockSpec((d, tn), lambda j: (0, j)),
                  pl.BlockSpec((1, tn), lambda j: (0, j))],
        out_specs=pl.BlockSpec((bsz, tn), lambda j: (0, j)),
        compiler_params=_params("arbitrary"),
        name="ada",
    )(c, ada_w, ada_b.reshape(1, n))


def _inproj_kernel(x_ref, mod_ref, g1_ref, w_ref, gq_ref, gk_ref, bd_ref, qkv_ref, zrw_ref):
    x = x_ref[0]
    mod = mod_ref[0]
    ms = jnp.mean(x * x, axis=-1, keepdims=True)
    h = x * lax.rsqrt(ms + NORM_EPS) * g1_ref[...]
    h = (h * (1.0 + mod[1:2]) + mod[0:1]).astype(BF16)
    bd = bd_ref[...]

    def head_norm(z, g):
        ss = _mm_exact_rhs(z * z, bd) * (1.0 / HEAD_DIM)
        return z * lax.rsqrt(ss + NORM_EPS) * g

    a = ATT_WIDTH
    zq = _dot(h, w_ref[:, 0:a])
    qkv_ref[0, :, 0:a] = (head_norm(zq, gq_ref[...]) * (HEAD_DIM ** -0.5 * LOG2E)).astype(BF16)
    zk = _dot(h, w_ref[:, a:2 * a])
    qkv_ref[0, :, a:2 * a] = head_norm(zk, gk_ref[...]).astype(BF16)
    qkv_ref[0, :, 2 * a:3 * a] = _dot(h, w_ref[:, 2 * a:3 * a]).astype(BF16)
    zrw_ref[0] = _dot(h, w_ref[:, 3 * a:])


def _inproj(x, mod6, norm1_g, w_in_bf, gq, gk, bd512, tm=256):
    bsz, s, d = x.shape
    nin = w_in_bf.shape[1]
    return pl.pallas_call(
        _inproj_kernel,
        out_shape=(jax.ShapeDtypeStruct((bsz, s, 3 * ATT_WIDTH), BF16),
                   jax.ShapeDtypeStruct((bsz, s, SHIFT_WIDTH), F32)),
        grid=(bsz, s // tm),
        in_specs=[pl.BlockSpec((1, tm, d), lambda b, i: (b, i, 0)),
                  pl.BlockSpec((1, 6, d), lambda b, i: (b, 0, 0)),
                  pl.BlockSpec((1, d), lambda b, i: (0, 0)),
                  pl.BlockSpec((d, nin), lambda b, i: (0, 0)),
                  pl.BlockSpec((1, ATT_WIDTH), lambda b, i: (0, 0)),
                  pl.BlockSpec((1, ATT_WIDTH), lambda b, i: (0, 0)),
                  pl.BlockSpec((ATT_WIDTH, ATT_WIDTH), lambda b, i: (0, 0))],
        out_specs=(pl.BlockSpec((1, tm, 3 * ATT_WIDTH), lambda b, i: (b, i, 0)),
                   pl.BlockSpec((1, tm, SHIFT_WIDTH), lambda b, i: (b, i, 0))),
        compiler_params=_params("arbitrary", "arbitrary"),
        name="inproj",
    )(x, mod6, norm1_g, w_in_bf, gq, gk, bd512)


def _attn_window(nkb):
    return nkb if nkb <= 2 * ATT_REACH + 2 else 2 * ATT_REACH + 2


def _attn_bias_table(n_heads, nkb):
    reach = _attn_window(nkb) - 1
    nd = 2 * reach + 1
    dd = jnp.arange(nd, dtype=jnp.int32)[None, :, None]
    r = jnp.arange(ATT_BLOCK, dtype=jnp.int32)[:, None, None]
    c = jnp.arange(ATT_BLOCK, dtype=jnp.int32)[None, None, :]
    dt = r - c - (dd - reach) * ATT_BLOCK
    adt = jnp.abs(dt)
    mult = jnp.zeros(dt.shape, F32)
    for dil, half in ATT_HALF_WINDOWS:
        mult = mult + ((adt <= half) & (dt % dil == 0)).astype(F32)
    logm = jnp.where(mult > 0, jnp.log2(jnp.maximum(mult, 1.0)), MASK_VALUE)
    slopes = jnp.exp2(-8.0 * (jnp.arange(n_heads, dtype=F32) + 1.0) / n_heads) * LOG2E
    bias = logm[None] - slopes[:, None, None, None] * adt.astype(F32)[None]
    return bias.reshape(n_heads, ATT_BLOCK, nd * ATT_BLOCK)


def _attn_kernel(q_ref, k_ref, v_ref, bias_ref, o_ref, s_ref, *, nkb, win):
    qi = pl.program_id(2)
    blk = ATT_BLOCK
    q = q_ref[0]
    lane = lax.broadcasted_iota(jnp.int32, (blk, LANES), 1)
    first = lane < HEAD_DIM
    zero = jnp.zeros_like(q)
    qab = jnp.concatenate([jnp.where(first, q, zero), jnp.where(first, zero, q)], axis=0)
    ks = 0 if win == nkb else jnp.clip(qi - ATT_REACH, 0, nkb - win)
    reach = win - 1

    m = [jnp.full((blk, LANES), -jnp.inf, F32) for _ in range(2)]
    for w0 in range(0, win, 2):
        kj0 = ks + w0
        kslab = k_ref[0, pl.ds(pl.multiple_of(kj0 * blk, blk), 2 * blk), :]
        s = _dot(qab, kslab, NT)
        x0 = pl.multiple_of((kj0 - qi + reach) * blk, blk)
        for hh in range(2):
            sb = s[hh * blk:(hh + 1) * blk] + bias_ref[hh, :, pl.ds(x0, 2 * blk)]
            s_ref[hh, :, w0 * blk:(w0 + 2) * blk] = sb
            m[hh] = jnp.maximum(m[hh], jnp.maximum(sb[:, :blk], sb[:, blk:]))
    mrow = [jnp.max(m[hh], axis=-1, keepdims=True) for hh in range(2)]

    first2 = lax.broadcasted_iota(jnp.int32, (2 * blk, LANES), 1) < HEAD_DIM
    acc = [jnp.zeros((blk, LANES), F32) for _ in range(2)]
    for w0 in range(0, win, 2):
        vslab = v_ref[0, pl.ds(pl.multiple_of((ks + w0) * blk, blk), 2 * blk), :]
        one = jnp.ones_like(vslab)
        vs = (jnp.where(first2, vslab, one), jnp.where(first2, one, vslab))
        for hh in range(2):
            p = jnp.exp2(s_ref[hh, :, w0 * blk:(w0 + 2) * blk] - mrow[hh]).astype(BF16)
            acc[hh] = acc[hh] + _dot(p, vs[hh])
    num = jnp.where(first, acc[0], acc[1])
    den = jnp.where(first, pltpu.roll(acc[0], HEAD_DIM, axis=1), pltpu.roll(acc[1], HEAD_DIM, axis=1))
    o_ref[0] = (num / den).astype(o_ref.dtype)


def _attention(qkv, bias):
    bsz, s, _ = qkv.shape
    nkb = s // ATT_BLOCK
    win = _attn_window(nkb)
    assert nkb % 2 == 0 and bias.shape[2] == (2 * win - 1) * ATT_BLOCK
    npair = ATT_WIDTH // LANES
    return pl.pallas_call(
        functools.partial(_attn_kernel, nkb=nkb, win=win),
        out_shape=jax.ShapeDtypeStruct((bsz, s, ATT_WIDTH), BF16),
        grid=(npair, bsz, nkb),
        in_specs=[pl.BlockSpec((1, ATT_BLOCK, LANES), lambda hp, b, i: (b, i, hp)),
                  pl.BlockSpec((1, s, LANES), lambda hp, b, i: (b, 0, npair + hp)),
                  pl.BlockSpec((1, s, LANES), lambda hp, b, i: (b, 0, 2 * npair + hp)),
                  pl.BlockSpec((2, ATT_BLOCK, bias.shape[2]), lambda hp, b, i: (hp, 0, 0))],
        out_specs=pl.BlockSpec((1, ATT_BLOCK, LANES), lambda hp, b, i: (b, i, hp)),
        scratch_shapes=[pltpu.VMEM((2, ATT_BLOCK, win * ATT_BLOCK), F32)],
        compiler_params=_params("arbitrary", "arbitrary", "arbitrary"),
        name="attn",
    )(qkv, qkv, qkv, bias)


def _prep_kernel(z_ref, zp_ref, zn_ref, mup_ref, mun_ref, wup_ref, aup_ref, gup_ref, w0_ref, a0_ref,
                 kk_ref, ka_ref, rk_ref, bd_ref, o_ref, *, tq):
    i = pl.program_id(1)
    last = pl.num_programs(1) - 1
    z = z_ref[0]
    row = lax.broadcasted_iota(jnp.int32, (tq, 1), 0)
    prev_row = zp_ref[0, 7:8, :] * (i > 0).astype(F32)
    next_row = zn_ref[0, 0:1, :] * (i < last).astype(F32)
    zp = jnp.where(row == 0, prev_row, pltpu.roll(z, 1, axis=0))
    zn = jnp.where(row == tq - 1, next_row, pltpu.roll(z, tq - 1, axis=0))
    zs = z + mup_ref[...] * (zp - z) + mun_ref[...] * (zn - z)

    w = RWKV_WIDTH
    r = zs[:, 0:w]
    k = zs[:, w:2 * w]
    v = zs[:, 2 * w:3 * w]
    xwa = zs[:, 3 * w:3 * w + LANES]
    xg = zs[:, 3 * w + LANES:]
    bd = bd_ref[...]

    g = _dot(_sigmoid(xg).astype(BF16), gup_ref[...])
    kk = k * kk_ref[...]
    ss = _mm_exact_rhs(kk * kk, bd)
    kk = kk * lax.rsqrt(jnp.maximum(ss, 1e-12))
    ka = ka_ref[...]
    txw = jnp.tanh(xwa).astype(BF16)
    xab = xwa.astype(BF16)

    o_ref[0, :, SLAB_R * w:(SLAB_R + 1) * w] = r
    o_ref[0, :, SLAB_V * w:(SLAB_V + 1) * w] = v
    o_ref[0, :, SLAB_KK * w:(SLAB_KK + 1) * w] = kk
    o_ref[0, :, SLAB_G * w:(SLAB_G + 1) * w] = g
    a_sum = jnp.zeros_like(k)
    for d in range(2):
        y = w0_ref[d:d + 1, :] + _dot(txw, wup_ref[d])
        wlog = -(jnp.maximum(-y, 0.0) + jnp.log(1.0 + jnp.exp(-jnp.abs(y)))) - 0.5
        a = _sigmoid(a0_ref[d:d + 1, :] + _dot(xab, aup_ref[d]))
        a_sum = a_sum + a
        base = (SLAB_DIR0 + 3 * d) * w
        o_ref[0, :, base:base + w] = -jnp.exp(wlog)
        o_ref[0, :, base + w:base + 2 * w] = k * (1.0 + (a - 1.0) * ka)
        o_ref[0, :, base + 2 * w:base + 3 * w] = kk * a
    k_bonus = k * (1.0 + (0.5 * a_sum - 1.0) * ka)
    bsum = _mm_exact_rhs(r * k_bonus * rk_ref[...], bd)
    o_ref[0, :, SLAB_BONUS * w:(SLAB_BONUS + 1) * w] = bsum * v


def _rwkv_prep(zrw, mu_prev, mu_next, wup_pad, aup_pad, g_up_bf, w0, a0, k_k, k_a, r_k, bd512, tq=256):
    bsz, s, sw = zrw.shape
    w = RWKV_WIDTH
    nt = s // tq
    row = lambda b, i: (0, 0)
    return pl.pallas_call(
        functools.partial(_prep_kernel, tq=tq),
        out_shape=jax.ShapeDtypeStruct((bsz, s, NUM_SLABS * w), F32),
        grid=(bsz, nt),
        in_specs=[pl.BlockSpec((1, tq, sw), lambda b, i: (b, i, 0)),
                  pl.BlockSpec((1, 8, sw), lambda b, i: (b, jnp.maximum(i * (tq // 8) - 1, 0), 0)),
                  pl.BlockSpec((1, 8, sw), lambda b, i: (b, jnp.minimum((i + 1) * (tq // 8), s // 8 - 1), 0)),
                  pl.BlockSpec((1, sw), row),
                  pl.BlockSpec((1, sw), row),
                  pl.BlockSpec((2, LANES, w), lambda b, i: (0, 0, 0)),
                  pl.BlockSpec((2, LANES, w), lambda b, i: (0, 0, 0)),
                  pl.BlockSpec((GATE_LORA, w), row),
                  pl.BlockSpec((2, w), row),
                  pl.BlockSpec((2, w), row),
                  pl.BlockSpec((1, w), row),
                  pl.BlockSpec((1, w), row),
                  pl.BlockSpec((1, w), row),
                  pl.BlockSpec((w, w), row)],
        out_specs=pl.BlockSpec((1, tq, NUM_SLABS * w), lambda b, i: (b, i, 0)),
        compiler_params=_params("arbitrary", "arbitrary"),
        name="rwkv_prep",
    )(zrw, zrw, zrw, mu_prev, mu_next, wup_pad, aup_pad, g_up_bf, w0, a0, k_k, k_a, r_k, bd512)


def _block_diag(x, first):
    zero = jnp.zeros_like(x)
    return jnp.concatenate([jnp.where(first, x, zero), jnp.where(first, zero, x)], axis=0)


def _p1_kernel(r_ref, v_ref, kk_ref, lw_ref, kd_ref, be_ref, rp_ref, qp_ref, g_ref, h_ref, *, cpb):
    c = CHUNK
    sign = 1 - 2 * pl.program_id(2)
    rowi = lax.broadcasted_iota(jnp.int32, (c, LANES), 0)
    lane = lax.broadcasted_iota(jnp.int32, (c, LANES), 1)
    coli = lane & (c - 1)
    first = lane < HEAD_DIM
    before = sign * (rowi - coli)
    strict2 = before > 0
    incl2 = before >= 0
    eye2 = (coli == rowi).astype(F32)
    r64 = lax.broadcasted_iota(jnp.int32, (c, c), 0)
    c64 = lax.broadcasted_iota(jnp.int32, (c, c), 1)
    cum = (sign * (r64 - c64) >= 0).astype(BF16)
    r128 = lax.broadcasted_iota(jnp.int32, (LANES, LANES), 0)
    c128 = lax.broadcasted_iota(jnp.int32, (LANES, LANES), 1)
    same_head = (r128 // HEAD_DIM) == (c128 // HEAD_DIM)
    eye128 = (r128 == c128).astype(F32)
    zero128 = jnp.zeros((LANES, LANES), F32)

    js = range(cpb)
    sls = [pl.ds(j * c, c) for j in js]
    bdg = lambda x: _block_diag(x, first)
    r = [r_ref[0, sl, :] for sl in sls]
    v = [v_ref[0, sl, :] for sl in sls]
    kk = [kk_ref[0, sl, :] for sl in sls]
    lw = [lw_ref[0, sl, :] for sl in sls]
    kd = [kd_ref[0, sl, :] for sl in sls]
    be = [be_ref[0, sl, :] for sl in sls]

    def cumsum(x):
        l1 = x.astype(BF16)
        rem = x - l1.astype(F32)
        l2 = rem.astype(BF16)
        l3 = (rem - l2.astype(F32)).astype(BF16)
        return _dot(cum, l1) + (_dot(cum, l2) + _dot(cum, l3))

    cs = [cumsum(lw[j]) for j in js]
    g_inv = [jnp.exp(-cs[j]) for j in js]
    g_tot = [jnp.exp(jnp.sum(lw[j], axis=0, keepdims=True)) for j in js]
    ab = [-kk[j] * jnp.exp(cs[j] - lw[j]) for j in js]
    rb = [r[j] * jnp.exp(cs[j]) for j in js]
    bt = [be[j] * g_inv[j] for j in js]
    kt = [kd[j] * g_inv[j] for j in js]

    lhs = [jnp.concatenate([ab[j], rb[j]], axis=0) for j in js]
    sb = [_mm(lhs[j], bdg(bt[j]), NT, WKV_PASSES_SCORE) for j in js]
    sk = [_mm(lhs[j], bdg(kt[j]), NT, WKV_PASSES_SCORE) for j in js]
    m_ab = [jnp.where(strict2, sb[j][:c], 0.0) for j in js]
    n_rb = [jnp.where(incl2, sb[j][c:], 0.0) for j in js]
    m_ak = [jnp.where(strict2, sk[j][:c], 0.0) for j in js]
    n_rk = [jnp.where(incl2, sk[j][c:], 0.0) for j in js]

    t = [eye2 + m_ab[j] for j in js]
    p = m_ab
    for _ in range(int(math.log2(c)) - 2):
        p = [_mm(p[j], bdg(p[j]), NN, WKV_PASSES_INV) for j in js]
        t = [t[j] + _mm(t[j], bdg(p[j]), NN, WKV_PASSES_INV) for j in js]
    res = [eye2 - _mm(eye2 - m_ab[j], bdg(t[j]), NN, 3) for j in js]
    t = [t[j] + _mm(t[j], bdg(res[j]), NN, 1) for j in js]

    vbd = [bdg(v[j]) for j in js]
    pm = [_mm(m_ak[j], vbd[j], NN, WKV_PASSES_APPLY) for j in js]
    ap = [_mm(t[j], bdg(ab[j]), NN, WKV_PASSES_APPLY) for j in js]
    pp = [_mm(t[j], bdg(pm[j]), NN, WKV_PASSES_APPLY) for j in js]
    for j in js:
        rp_ref[0, 0, 0, sls[j], :] = rb[j] + _mm(n_rb[j], bdg(ap[j]), NN, WKV_PASSES_APPLY)
        qp_ref[0, 0, 0, sls[j], :] = (_mm(n_rb[j], bdg(pp[j]), NN, WKV_PASSES_APPLY)
                                      + _mm(n_rk[j], vbd[j], NN, WKV_PASSES_APPLY))
    for j in js:
        gmat = eye128 + jnp.where(same_head, _mm(ap[j], bt[j], TN, WKV_PASSES_STATE), zero128)
        hmat = jnp.where(same_head, _mm(pp[j], bt[j], TN, WKV_PASSES_STATE)
                         + _mm(v[j], kt[j], TN, WKV_PASSES_STATE), zero128)
        g_ref[0, 0, 0, j] = gmat * g_tot[j]
        h_ref[0, 0, 0, j] = hmat * g_tot[j]


def _rwkv_p1(pk, cpb=16):
    bsz, s, _ = pk.shape
    nc = s // CHUNK
    npair = RWKV_WIDTH // LANES
    rows = cpb * CHUNK

    def slab(sidx):
        return pl.BlockSpec((1, rows, LANES), lambda b, hp, d, ci: (b, ci, sidx * npair + hp))

    def dslab(off):
        return pl.BlockSpec((1, rows, LANES),
                            lambda b, hp, d, ci: (b, ci, (SLAB_DIR0 + 3 * d + off) * npair + hp))

    seq = pl.BlockSpec((1, 1, 1, rows, LANES), lambda b, hp, d, ci: (b, hp, d, ci, 0))
    mat = pl.BlockSpec((1, 1, 1, cpb, LANES, LANES), lambda b, hp, d, ci: (b, hp, d, ci, 0, 0))
    return pl.pallas_call(
        functools.partial(_p1_kernel, cpb=cpb),
        out_shape=(jax.ShapeDtypeStruct((bsz, npair, 2, s, LANES), F32),
                   jax.ShapeDtypeStruct((bsz, npair, 2, s, LANES), F32),
                   jax.ShapeDtypeStruct((bsz, npair, 2, nc, LANES, LANES), F32),
                   jax.ShapeDtypeStruct((bsz, npair, 2, nc, LANES, LANES), F32)),
        grid=(bsz, npair, 2, nc // cpb),
        in_specs=[slab(SLAB_R), slab(SLAB_V), slab(SLAB_KK), dslab(0), dslab(1), dslab(2)],
        out_specs=(seq, seq, mat, mat),
        compiler_params=_params("arbitrary", "arbitrary", "arbitrary", "arbitrary"),
        name="rwkv_p1",
    )(pk, pk, pk, pk, pk, pk)


def _p2_kernel(rp_ref, qp_ref, g_ref, h_ref, gate_ref, bonus_ref, lng_ref, lnb_ref, bd_ref, o_ref,
               yf_ref, yb_ref, *, nc, te):
    c = CHUNK

    def step(j, carry):
        sf, sb = carry
        jf = j
        jb = nc - 1 - j
        slf = pl.ds(pl.multiple_of(jf * c, c), c)
        slb = pl.ds(pl.multiple_of(jb * c, c), c)
        yf_ref[slf, :] = _mm(rp_ref[0, 0, 0, slf, :], sf, NT, WKV_PASSES_CHAIN) + qp_ref[0, 0, 0, slf, :]
        yb_ref[slb, :] = _mm(rp_ref[0, 0, 1, slb, :], sb, NT, WKV_PASSES_CHAIN) + qp_ref[0, 0, 1, slb, :]
        sf = _mm(sf, g_ref[0, 0, 0, jf], NN, WKV_PASSES_CHAIN) + h_ref[0, 0, 0, jf]
        sb = _mm(sb, g_ref[0, 0, 1, jb], NN, WKV_PASSES_CHAIN) + h_ref[0, 0, 1, jb]
        return sf, sb

    zero = jnp.zeros((LANES, LANES), F32)
    lax.fori_loop(0, nc, step, (zero, zero))

    bd = bd_ref[...]
    lng = lng_ref[...]
    lnb = lnb_ref[...]
    inv = 1.0 / HEAD_DIM

    def epi(i, _):
        sl = pl.ds(pl.multiple_of(i * te, te), te)
        y = yf_ref[sl, :] + yb_ref[sl, :]
        mean = _mm_exact_rhs(y, bd) * inv
        yc = y - mean
        var = _mm_exact_rhs(yc * yc, bd) * inv
        yn = yc * lax.rsqrt(var + LNX_EPS) * lng + lnb
        o_ref[0, sl, :] = ((yn + bonus_ref[0, sl, :]) * gate_ref[0, sl, :]).astype(o_ref.dtype)
        return 0

    lax.fori_loop(0, (nc * c) // te, epi, 0)


def _rwkv_p2(rp, qp, gm, hm, pk, lnx_g, lnx_b, bd128, te=256):
    bsz, npair, _, s, _ = rp.shape
    nc = s // CHUNK
    seq = pl.BlockSpec((1, 1, 2, s, LANES), lambda b, hp: (b, hp, 0, 0, 0))
    mat = pl.BlockSpec((1, 1, 2, nc, LANES, LANES), lambda b, hp: (b, hp, 0, 0, 0, 0))
    return pl.pallas_call(
        functools.partial(_p2_kernel, nc=nc, te=min(te, s)),
        out_shape=jax.ShapeDtypeStruct((bsz, s, RWKV_WIDTH), BF16),
        grid=(bsz, npair),
        in_specs=[seq, seq, mat, mat,
                  pl.BlockSpec((1, s, LANES), lambda b, hp: (b, 0, SLAB_G * npair + hp)),
                  pl.BlockSpec((1, s, LANES), lambda b, hp: (b, 0, SLAB_BONUS * npair + hp)),
                  pl.BlockSpec((1, LANES), lambda b, hp: (0, hp)),
                  pl.BlockSpec((1, LANES), lambda b, hp: (0, hp)),
                  pl.BlockSpec((LANES, LANES), lambda b, hp: (0, 0))],
        out_specs=pl.BlockSpec((1, s, LANES), lambda b, hp: (b, 0, hp)),
        scratch_shapes=[pltpu.VMEM((s, LANES), F32), pltpu.VMEM((s, LANES), F32)],
        compiler_params=_params("arbitrary", "arbitrary"),
        name="rwkv_p2",
    )(rp, qp, gm, hm, pk, pk, lnx_g, lnx_b, bd128)


def _outproj_kernel(att_ref, rw_ref, x_ref, mod_ref, g2_ref, wo_ref, wq_ref, k1_ref, k2_ref,
                    x1_ref, h2_ref, s1_ref, s2_ref):
    mod = mod_ref[0]
    a = ATT_WIDTH
    mixed = _dot(att_ref[0], wo_ref[0:a, :]) + _dot(rw_ref[0], wo_ref[a:, :])
    x1 = x_ref[0] + mod[2:3] * mixed
    x1_ref[0] = x1
    ms = jnp.mean(x1 * x1, axis=-1, keepdims=True)
    h2 = x1 * lax.rsqrt(ms + NORM_EPS) * g2_ref[...]
    h2 = (h2 * (1.0 + mod[4:5]) + mod[3:4]).astype(BF16)
    h2_ref[0] = h2
    q = _dot(h2, wq_ref[...])
    k1 = k1_ref[...]
    k2 = k2_ref[...]
    for h in range(PEER_HEADS):
        base = h * 2 * LANES
        s1 = _dot(k1, q[:, base:base + LANES].astype(BF16), NT)
        s2 = _dot(k2, q[:, base + LANES:base + 2 * LANES].astype(BF16), NT)
        for g in range(s1.shape[1] // LANES):
            s1_ref[h, g] = s1[:, g * LANES:(g + 1) * LANES]
            s2_ref[h, g] = s2[:, g * LANES:(g + 1) * LANES]


def _outproj(att, rw, x, mod6, norm2_g, w_out_bf, wq_bf, k1_bf, k2_bf, tm=256):
    bsz, s, d = x.shape
    nt = s // tm
    t = bsz * s
    nq = wq_bf.shape[1]
    const = lambda b, i: (0, 0)
    gpt = tm // LANES
    tok = pl.BlockSpec((PEER_HEADS, gpt, PEER_N_KEYS, LANES), lambda b, i: (0, b * nt + i, 0, 0))
    stat = jax.ShapeDtypeStruct((PEER_HEADS, t // LANES, PEER_N_KEYS, LANES), F32)
    return pl.pallas_call(
        _outproj_kernel,
        out_shape=(jax.ShapeDtypeStruct((bsz, s, d), F32),
                   jax.ShapeDtypeStruct((bsz, s, d), BF16),
                   stat, stat),
        grid=(bsz, nt),
        in_specs=[pl.BlockSpec((1, tm, ATT_WIDTH), lambda b, i: (b, i, 0)),
                  pl.BlockSpec((1, tm, RWKV_WIDTH), lambda b, i: (b, i, 0)),
                  pl.BlockSpec((1, tm, d), lambda b, i: (b, i, 0)),
                  pl.BlockSpec((1, 6, d), lambda b, i: (b, 0, 0)),
                  pl.BlockSpec((1, d), const),
                  pl.BlockSpec((d, d), const),
                  pl.BlockSpec((d, nq), const),
                  pl.BlockSpec((PEER_N_KEYS, LANES), const),
                  pl.BlockSpec((PEER_N_KEYS, LANES), const)],
        out_specs=(pl.BlockSpec((1, tm, d), lambda b, i: (b, i, 0)),
                   pl.BlockSpec((1, tm, d), lambda b, i: (b, i, 0)),
                   tok, tok),
        compiler_params=_params("arbitrary", "arbitrary"),
        name="outproj",
    )(att, rw, x, mod6, norm2_g, w_out_bf, wq_bf, k1_bf, k2_bf)


def _oddeven_sort_pairs(n):
    pairs = []
    p = 1
    while p < n:
        k = p
        while k >= 1:
            for j in range(k % p, n - k, 2 * k):
                for i in range(min(k, n - j - k)):
                    if (i + j) // (2 * p) == (i + j + k) // (2 * p):
                        pairs.append((i + j, i + j + k))
            k //= 2
        p *= 2
    return pairs


def _bitonic_merge_pairs(n):
    pairs = []
    stride = n // 2
    while stride >= 1:
        pairs += [(i, i + stride) for i in range(n) if not i & stride]
        stride //= 2
    return pairs


def _compare_exchange(vals, pairs):
    vals = list(vals)
    for i, j in pairs:
        a, b = vals[i], vals[j]
        if b is None:
            continue
        if a is None:
            vals[i], vals[j] = b, None
        else:
            vals[i], vals[j] = jnp.maximum(a, b), jnp.minimum(a, b)
    return vals


def _top16_sorted(vregs):
    kk = PEER_TOPK
    vals = list(vregs) + [None] * (kk - len(vregs))
    vals = _compare_exchange(vals, _oddeven_sort_pairs(kk))
    for shift in (4, 2, 1):
        other = [None if v is None else pltpu.roll(v, shift, axis=0) for v in vals]
        merged = []
        for k in range(kk):
            a, b = vals[k], other[kk - 1 - k]
            merged.append(b if a is None else a if b is None else jnp.maximum(a, b))
        vals = _compare_exchange(merged, _bitonic_merge_pairs(kk))
    return vals


def _topk_kernel(s1_ref, s2_ref, c1_ref, e1_ref, r2_ref, e2_ref, *, groups):
    kk = PEER_TOPK
    sub = 8
    neg = -jnp.inf
    rows8 = lax.broadcasted_iota(jnp.int32, (sub, LANES), 0)

    def rows_of(reps):
        out = reps[0]
        for r in range(1, sub):
            out = jnp.where(rows8 == r, reps[r], out)
        return out

    def all_sublanes_sum(x):
        for shift in (4, 2, 1):
            x = x + pltpu.roll(x, shift, axis=0)
        return x

    def per_tile(it_idx, _):
        g = it_idx // PEER_HEADS
        h = it_idx % PEER_HEADS
        s1 = [s1_ref[h, g, pl.ds(v * sub, sub), :] for v in range(PEER_N_KEYS // sub)]
        s2 = [s2_ref[h, g, pl.ds(v * sub, sub), :] for v in range(PEER_N_KEYS // sub)]
        a = _top16_sorted(s1)
        b = _top16_sorted(s2)
        a_lo, a_hi = rows_of(a[:sub]), rows_of(a[sub:])
        b_lo, b_hi = rows_of(b[:sub]), rows_of(b[sub:])

        cand = [a[0] + b_lo, a[0] + b_hi, a[1] + b_lo]
        for i in range(2, sub):
            cand.append(jnp.where(rows8 < kk // (i + 1), a[i] + b_lo, neg))
        cand.append(a_hi + b[0])
        top = _top16_sorted(cand)
        tau = top[kk - 1]
        z = None
        for t in top:
            ez = jnp.exp(t - top[0])
            z = ez if z is None else z + ez
        inv_z = 1.0 / z

        counts = []
        for r in range(kk):
            hits = (jnp.where(a[r] + b_lo >= tau, 1.0, 0.0) + jnp.where(a[r] + b_hi >= tau, 1.0, 0.0))
            counts.append(all_sublanes_sum(hits))
        for v in range(PEER_N_KEYS // sub):
            rows = pl.ds(v * sub, sub)
            cnt = jnp.zeros((sub, LANES), F32)
            rank = jnp.full((sub, LANES), float(kk), F32)
            for r in range(kk):
                cnt = jnp.where(s1[v] == a[r], counts[r], cnt)
                rank = jnp.where(s2[v] == b[r], float(r), rank)
            c1_ref[h, g, rows, :] = cnt
            e1_ref[h, g, rows, :] = jnp.exp(s1[v] - a[0]) * inv_z
            s2[v] = (rank, jnp.exp(s2[v] - b[0]))
        rank2 = jnp.concatenate([rv for rv, _ in s2], axis=0).astype(BF16)
        e2 = jnp.concatenate([ev for _, ev in s2], axis=0).astype(BF16)
        r2_ref[h, g] = pltpu.bitcast(rank2, jnp.uint32)
        e2_ref[h, g] = pltpu.bitcast(e2, jnp.uint32)
        return 0

    lax.fori_loop(0, groups * PEER_HEADS, per_tile, 0)


def _peer_topk(s1t, s2t, tn=512):
    nh, ng, nk, _ = s1t.shape
    groups = tn // LANES
    blk = pl.BlockSpec((nh, groups, nk, LANES), lambda i: (0, i, 0, 0))
    f32 = jax.ShapeDtypeStruct(s1t.shape, F32)
    b16 = jax.ShapeDtypeStruct((nh, ng, nk // 2, LANES), jnp.uint32)
    pblk = pl.BlockSpec((nh, groups, nk // 2, LANES), lambda i: (0, i, 0, 0))
    return pl.pallas_call(
        functools.partial(_topk_kernel, groups=groups),
        out_shape=(f32, f32, b16, b16),
        grid=(ng // groups,),
        in_specs=[blk, blk],
        out_specs=(blk, blk, pblk, pblk),
        compiler_params=_params("arbitrary"),
        name="peer_topk",
    )(s1t, s2t)


def _peer_kernel(h2_ref, u_ref, vt_ref, c1_ref, e1_ref, r2_ref, e2_ref, x1_ref, mod_ref, o_ref,
                 acc_ref, act0_ref, act1_ref, p0_ref, p1_ref, *, tn, te, n_tiles):
    k = pl.program_id(0)
    n_items = pl.num_programs(0) - 2
    nk = PEER_N_KEYS
    ngroups = tn // LANES

    @pl.when(k == 0)
    def _():
        acc_ref[...] = jnp.zeros_like(acc_ref)
        for ref in (act0_ref, act1_ref, p0_ref, p1_ref):
            ref[...] = jnp.zeros_like(ref)

    ni = te // nk
    i0 = pl.multiple_of((jnp.clip(k - 1, 0, n_items - 1) % n_tiles) * ni, ni)
    tile_c = jnp.clip(k - 2, 0, n_items - 1) % n_tiles
    first_c = tile_c == 0

    d_model = acc_ref.shape[0]
    halves = [(mh, nh, kh) for mh in range(2) for nh in range(ngroups // 2) for kh in range(2)]

    def stage_c(p_a, mh, nh, kh):
        rows = slice(mh * (d_model // 2), (mh + 1) * (d_model // 2))
        cols = slice(nh * 2 * LANES, (nh + 1) * 2 * LANES)
        ks = slice(kh * (te // 2), (kh + 1) * (te // 2))
        p_prev = jnp.concatenate([p_a[2 * nh, ks, :], p_a[2 * nh + 1, ks, :]], axis=1)
        prev = acc_ref[rows, cols]
        if kh == 0:
            prev = jnp.where(first_c, jnp.zeros_like(prev), prev)
        acc_ref[rows, cols] = prev + _dot(vt_ref[rows, ks], p_prev)

    def stage_a(act_a, mh, nh, kh):
        rows = slice(mh * (te // 2), (mh + 1) * (te // 2))
        ks = slice(kh * (d_model // 2), (kh + 1) * (d_model // 2))
        act = _dot(u_ref[rows, ks], h2_ref[nh * 2 * LANES:(nh + 1) * 2 * LANES, ks], NT)
        if kh == 0:
            act_a[2 * nh, rows, :] = act[:, :LANES]
            act_a[2 * nh + 1, rows, :] = act[:, LANES:]
        else:
            act_a[2 * nh, rows, :] += act[:, :LANES]
            act_a[2 * nh + 1, rows, :] += act[:, LANES:]

    def rows_bf16(row):
        tile = jnp.broadcast_to(row, (BF16_ROWS, LANES)).astype(BF16)
        return jnp.concatenate([tile] * (nk // BF16_ROWS), axis=0)

    quad = 4

    def stage_b(act_b, p_b, tg, iq, pieces):
        gates = [jnp.zeros((nk, LANES), BF16) for _ in range(quad)]
        for h in range(PEER_HEADS):
            if h % (PEER_HEADS // 2) == 0 and pieces:
                pieces.pop(0)()
            c1 = c1_ref[h, tg, pl.ds(i0, ni), :]
            e1 = e1_ref[h, tg, pl.ds(i0, ni), :]
            rank2 = pltpu.bitcast(r2_ref[h, tg], BF16)
            e2 = pltpu.bitcast(e2_ref[h, tg], BF16)
            for q in range(quad):
                il = iq * quad + q
                sel = rank2 < rows_bf16(c1[il:il + 1, :])
                gates[q] = gates[q] + jnp.where(sel, e2 * rows_bf16(e1[il:il + 1, :]), jnp.zeros_like(e2))
        for q in range(quad):
            il = iq * quad + q
            a = act_b[tg, il * nk:(il + 1) * nk, :]
            gelu = 0.5 * a * (1.0 + lax.erf(a * (2.0 ** -0.5)))
            p_b[tg, il * nk:(il + 1) * nk, :] = gates[q] * gelu.astype(BF16)

    def stages(act_a, act_b, p_a, p_b):
        pieces = ([functools.partial(stage_c, p_a, *hv) for hv in halves]
                  + [functools.partial(stage_a, act_a, *hv) for hv in halves])
        for tg in range(ngroups):
            for iq in range(ni // quad):
                stage_b(act_b, p_b, tg, iq, pieces)
        while pieces:
            pieces.pop(0)()

    @pl.when(k % 2 == 0)
    def _():
        stages(act0_ref, act1_ref, p0_ref, p1_ref)

    @pl.when(k % 2 == 1)
    def _():
        stages(act1_ref, act0_ref, p1_ref, p0_ref)

    @pl.when((tile_c == n_tiles - 1) & (k >= 2))
    def _():
        o_ref[...] = x1_ref[...] + mod_ref[0, 5:6, :] * acc_ref[...].T


def _peer_ffn(h2, u_bf, vt_bf, c1, e1, r2, e2, x1, mod6, seq, tn=512, te=1024):
    t, d = h2.shape
    ne = u_bf.shape[0]
    ngroups = tn // LANES
    n_tiles = ne // te
    n_items = (t // tn) * n_tiles

    def item(k, lag):
        w = jnp.clip(k - lag, 0, n_items - 1)
        return w // n_tiles, w % n_tiles

    stat = pl.BlockSpec((PEER_HEADS, ngroups, PEER_N_KEYS, LANES), lambda k: (0, item(k, 1)[0], 0, 0))
    pstat = pl.BlockSpec((PEER_HEADS, ngroups, PEER_N_KEYS // 2, LANES), lambda k: (0, item(k, 1)[0], 0, 0))
    return pl.pallas_call(
        functools.partial(_peer_kernel, tn=tn, te=te, n_tiles=n_tiles),
        out_shape=jax.ShapeDtypeStruct((t, d), F32),
        grid=(n_items + 2,),
        in_specs=[pl.BlockSpec((tn, d), lambda k: (item(k, 0)[0], 0)),
                  pl.BlockSpec((te, d), lambda k: (item(k, 0)[1], 0)),
                  pl.BlockSpec((d, te), lambda k: (0, item(k, 2)[1])),
                  stat, stat, pstat, pstat,
                  pl.BlockSpec((tn, d), lambda k: (item(k, 2)[0], 0)),
                  pl.BlockSpec((1, 6, d), lambda k: ((item(k, 2)[0] * tn) // seq, 0, 0))],
        out_specs=pl.BlockSpec((tn, d), lambda k: (item(k, 2)[0], 0)),
        scratch_shapes=[pltpu.VMEM((d, tn), F32),
                        pltpu.VMEM((ngroups, te, LANES), F32), pltpu.VMEM((ngroups, te, LANES), F32),
                        pltpu.VMEM((ngroups, te, LANES), BF16), pltpu.VMEM((ngroups, te, LANES), BF16)],
        compiler_params=_params("arbitrary"),
        name="peer_ffn",
    )(h2, u_bf, vt_bf, c1, e1, r2, e2, x1, mod6)


def _head_block_diag(n):
    idx = jnp.arange(n, dtype=jnp.int32) // HEAD_DIM
    return (idx[:, None] == idx[None, :]).astype(BF16)


def _layer(x, c, ada_w, ada_b, norm1_g, w_in, mu_prev, mu_next, q_norm_g, k_norm_g, w_decay0, w_decay_up,
           a_gate0, a_gate_up, g_up, k_k, k_a, r_k, lnx_g, lnx_b, w_out, norm2_g, peer_w_query,
           peer_sub_keys1, peer_sub_keys2, peer_u, peer_v):
    bsz, s, d = x.shape
    n_att_heads = ATT_WIDTH // HEAD_DIM
    bd512 = _head_block_diag(ATT_WIDTH)
    bd128 = _head_block_diag(LANES)

    mod6 = _ada(c, ada_w, ada_b).reshape(bsz, 6, d)
    gq = jnp.tile(q_norm_g, n_att_heads).reshape(1, ATT_WIDTH)
    gk = jnp.tile(k_norm_g, n_att_heads).reshape(1, ATT_WIDTH)
    qkv, zrw = _inproj(x, mod6, norm1_g.reshape(1, d), w_in.astype(BF16), gq, gk, bd512)

    att = _attention(qkv, _attn_bias_table(n_att_heads, s // ATT_BLOCK))

    zero_w = jnp.zeros((2, LANES - DECAY_LORA, RWKV_WIDTH), F32)
    wup_pad = jnp.concatenate([w_decay_up, zero_w], axis=1).astype(BF16)
    aup_pad = jnp.concatenate([zero_w, a_gate_up], axis=1).astype(BF16)
    pk = _rwkv_prep(zrw, mu_prev.reshape(1, -1), mu_next.reshape(1, -1), wup_pad, aup_pad, g_up.astype(BF16),
                    w_decay0, a_gate0, k_k.reshape(1, -1), k_a.reshape(1, -1), r_k.reshape(1, -1), bd512)
    rp, qp, gm, hm = _rwkv_p1(pk)
    rw = _rwkv_p2(rp, qp, gm, hm, pk, lnx_g.reshape(1, -1), lnx_b.reshape(1, -1), bd128)

    x1, h2, s1t, s2t = _outproj(att, rw, x, mod6, norm2_g.reshape(1, d), w_out.astype(BF16),
                                peer_w_query.astype(BF16), peer_sub_keys1.astype(BF16),
                                peer_sub_keys2.astype(BF16))
    c1, e1, r2, e2 = _peer_topk(s1t, s2t)
    out = _peer_ffn(h2.reshape(bsz * s, d), peer_u.astype(BF16), peer_v.T.astype(BF16), c1, e1, r2, e2,
                    x1.reshape(bsz * s, d), mod6, s)
    return out.reshape(bsz, s, d)


def kernel(x, c, ada_w, ada_b, norm1_g, w_in, mu_prev, mu_next, q_norm_g, k_norm_g, w_decay0, w_decay_up,
           a_gate0, a_gate_up, g_up, k_k, k_a, r_k, lnx_g, lnx_b, w_out, norm2_g, peer_w_query,
           peer_sub_keys1, peer_sub_keys2, peer_u, peer_v):
    depth = ada_w.shape[0]
    for l in range(depth):
        x = _layer(x, c, ada_w[l], ada_b[l], norm1_g[l], w_in[l], mu_prev[l], mu_next[l], q_norm_g[l],
                   k_norm_g[l], w_decay0[l], w_decay_up[l], a_gate0[l], a_gate_up[l], g_up[l], k_k[l], k_a[l],
                   r_k[l], lnx_g[l], lnx_b[l], w_out[l], norm2_g[l], peer_w_query[l], peer_sub_keys1[l],
                   peer_sub_keys2[l], peer_u[l], peer_v[l])
    return x
```

```python
import functools
import math

import jax
import jax.numpy as jnp
from jax import lax
from jax.experimental import pallas as pl
from jax.experimental.pallas import tpu as pltpu

F32 = jnp.float32
BF16 = jnp.bfloat16

HEAD_DIM = 64
ATT_WIDTH = 512
RWKV_WIDTH = 512
DECAY_LORA = 64
AAA_LORA = 64
GATE_LORA = 128
SHIFT_WIDTH = 3 * RWKV_WIDTH + DECAY_LORA + AAA_LORA + GATE_LORA
PEER_N_KEYS = 128
PEER_HEADS = 8
PEER_TOPK = 16
NORM_EPS = 1e-6
LNX_EPS = 64e-5
MASK_VALUE = -1e30
LOG2E = 1.4426950408889634
ATT_HALF_WINDOWS = ((1, 64), (4, 256), (16, 1024))

LANES = 128
BF16_ROWS = 16
CHUNK = 64
ATT_BLOCK = 128
ATT_REACH = 1024 // ATT_BLOCK
VMEM_LIMIT_BYTES = 56 * 1024 * 1024
WKV_PASSES_CHAIN = 1

NN = (((1,), (0,)), ((), ()))
NT = (((1,), (1,)), ((), ()))
TN = (((0,), (0,)), ((), ()))

SLAB_R, SLAB_V, SLAB_KK, SLAB_DIR0, SLAB_G, SLAB_BONUS, NUM_SLABS = 0, 1, 2, 3, 9, 10, 11


def _params(*sem):
    return pltpu.CompilerParams(dimension_semantics=sem, vmem_limit_bytes=VMEM_LIMIT_BYTES)


def _dot(a, b, dims=NN):
    return lax.dot_general(a, b, dims, preferred_element_type=F32)


def _split(a):
    hi = a.astype(BF16)
    lo = (a - hi.astype(F32)).astype(BF16)
    return hi, lo


def _mm(a, b, dims=NN, passes=3):
    if passes == 1:
        return _dot(a.astype(BF16), b.astype(BF16), dims)
    ah, al = _split(a)
    bh, bl = _split(b)
    return _dot(ah, bh, dims) + (_dot(ah, bl, dims) + _dot(al, bh, dims))


def _mm_exact_rhs(a, b_exact):
    ah, al = _split(a)
    return _dot(ah, b_exact) + _dot(al, b_exact)


def _sigmoid(x):
    return 1.0 / (1.0 + jnp.exp(-x))


def _ada_kernel(c_ref, w_ref, b_ref, o_ref):
    c = c_ref[...]
    o_ref[...] = _mm(c * _sigmoid(c), w_ref[...]) + b_ref[...]


def _ada(c, ada_w, ada_b):
    bsz, d = c.shape
    n = ada_w.shape[1]
    tn = 1024
    return pl.pallas_call(
        _ada_kernel,
        out_shape=jax.ShapeDtypeStruct((bsz, n), F32),
        grid=(n // tn,),
        in_specs=[pl.BlockSpec((bsz, d), lambda j: (0, 0)),
                  pl.BlockSpec((d, tn), lambda j: (0, j)),
                  pl.BlockSpec((1, tn), lambda j: (0, j))],
        out_specs=pl.BlockSpec((bsz, tn), lambda j: (0, j)),
        compiler_params=_params("arbitrary"),
        name="ada",
    )(c, ada_w, ada_b.reshape(1, n))


def _inproj_kernel(x_ref, mod_ref, g1_ref, w_ref, gq_ref, gk_ref, bd_ref, qkv_ref, zrw_ref):
    x = x_ref[0]
    mod = mod_ref[0]
    ms = jnp.mean(x * x, axis=-1, keepdims=True)
    h = x * lax.rsqrt(ms + NORM_EPS) * g1_ref[...]
    h = (h * (1.0 + mod[1:2]) + mod[0:1]).astype(BF16)
    bd = bd_ref[...]

    def head_norm(z, g):
        ss = _mm_exact_rhs(z * z, bd) * (1.0 / HEAD_DIM)
        return z * lax.rsqrt(ss + NORM_EPS) * g

    a = ATT_WIDTH
    zq = _dot(h, w_ref[:, 0:a])
    qkv_ref[0, :, 0:a] = (head_norm(zq, gq_ref[...]) * (HEAD_DIM ** -0.5 * LOG2E)).astype(BF16)
    zk = _dot(h, w_ref[:, a:2 * a])
    qkv_ref[0, :, a:2 * a] = head_norm(zk, gk_ref[...]).astype(BF16)
    qkv_ref[0, :, 2 * a:3 * a] = _dot(h, w_ref[:, 2 * a:3 * a]).astype(BF16)
    zrw_ref[0] = _dot(h, w_ref[:, 3 * a:])


def _inproj(x, mod6, norm1_g, w_in_bf, gq, gk, bd512, tm=256):
    bsz, s, d = x.shape
    nin = w_in_bf.shape[1]
    return pl.pallas_call(
        _inproj_kernel,
        out_shape=(jax.ShapeDtypeStruct((bsz, s, 3 * ATT_WIDTH), BF16),
                   jax.ShapeDtypeStruct((bsz, s, SHIFT_WIDTH), F32)),
        grid=(bsz, s // tm),
        in_specs=[pl.BlockSpec((1, tm, d), lambda b, i: (b, i, 0)),
                  pl.BlockSpec((1, 6, d), lambda b, i: (b, 0, 0)),
                  pl.BlockSpec((1, d), lambda b, i: (0, 0)),
                  pl.BlockSpec((d, nin), lambda b, i: (0, 0)),
                  pl.BlockSpec((1, ATT_WIDTH), lambda b, i: (0, 0)),
                  pl.BlockSpec((1, ATT_WIDTH), lambda b, i: (0, 0)),
                  pl.BlockSpec((ATT_WIDTH, ATT_WIDTH), lambda b, i: (0, 0))],
        out_specs=(pl.BlockSpec((1, tm, 3 * ATT_WIDTH), lambda b, i: (b, i, 0)),
                   pl.BlockSpec((1, tm, SHIFT_WIDTH), lambda b, i: (b, i, 0))),
        compiler_params=_params("arbitrary", "arbitrary"),
        name="inproj",
    )(x, mod6, norm1_g, w_in_bf, gq, gk, bd512)


def _attn_window(nkb):
    return nkb if nkb <= 2 * ATT_REACH + 2 else 2 * ATT_REACH + 2


def _attn_bias_table(n_heads, nkb):
    reach = _attn_window(nkb) - 1
    nd = 2 * reach + 1
    dd = jnp.arange(nd, dtype=jnp.int32)[None, :, None]
    r = jnp.arange(ATT_BLOCK, dtype=jnp.int32)[:, None, None]
    c = jnp.arange(ATT_BLOCK, dtype=jnp.int32)[None, None, :]
    dt = r - c - (dd - reach) * ATT_BLOCK
    adt = jnp.abs(dt)
    mult = jnp.zeros(dt.shape, F32)
    for dil, half in ATT_HALF_WINDOWS:
        mult = mult + ((adt <= half) & (dt % dil == 0)).astype(F32)
    logm = jnp.where(mult > 0, jnp.log2(jnp.maximum(mult, 1.0)), MASK_VALUE)
    slopes = jnp.exp2(-8.0 * (jnp.arange(n_heads, dtype=F32) + 1.0) / n_heads) * LOG2E
    bias = logm[None] - slopes[:, None, None, None] * adt.astype(F32)[None]
    return bias.reshape(n_heads, ATT_BLOCK, nd * ATT_BLOCK)


def _attn_kernel(q_ref, k_ref, v_ref, bias_ref, o_ref, s_ref, *, nkb, win):
    qi = pl.program_id(2)
    blk = ATT_BLOCK
    q = q_ref[0]
    lane = lax.broadcasted_iota(jnp.int32, (blk, LANES), 1)
    first = lane < HEAD_DIM
    zero = jnp.zeros_like(q)
    qab = jnp.concatenate([jnp.where(first, q, zero), jnp.where(first, zero, q)], axis=0)
    ks = 0 if win == nkb else jnp.clip(qi - ATT_REACH, 0, nkb - win)
    reach = win - 1

    m = [jnp.full((blk, LANES), -jnp.inf, F32) for _ in range(2)]
    for w0 in range(0, win, 2):
        kj0 = ks + w0
        kslab = k_ref[0, pl.ds(pl.multiple_of(kj0 * blk, blk), 2 * blk), :]
        s = _dot(qab, kslab, NT)
        x0 = pl.multiple_of((kj0 - qi + reach) * blk, blk)
        for hh in range(2):
            sb = s[hh * blk:(hh + 1) * blk] + bias_ref[hh, :, pl.ds(x0, 2 * blk)]
            s_ref[hh, :, w0 * blk:(w0 + 2) * blk] = sb
            m[hh] = jnp.maximum(m[hh], jnp.maximum(sb[:, :blk], sb[:, blk:]))
    mrow = [jnp.max(m[hh], axis=-1, keepdims=True) for hh in range(2)]

    first2 = lax.broadcasted_iota(jnp.int32, (2 * blk, LANES), 1) < HEAD_DIM
    acc = [jnp.zeros((blk, LANES), F32) for _ in range(2)]
    for w0 in range(0, win, 2):
        vslab = v_ref[0, pl.ds(pl.multiple_of((ks + w0) * blk, blk), 2 * blk), :]
        one = jnp.ones_like(vslab)
        vs = (jnp.where(first2, vslab, one), jnp.where(first2, one, vslab))
        for hh in range(2):
            p = jnp.exp2(s_ref[hh, :, w0 * blk:(w0 + 2) * blk] - mrow[hh]).astype(BF16)
            acc[hh] = acc[hh] + _dot(p, vs[hh])
    num = jnp.where(first, acc[0], acc[1])
    den = jnp.where(first, pltpu.roll(acc[0], HEAD_DIM, axis=1), pltpu.roll(acc[1], HEAD_DIM, axis=1))
    o_ref[0] = (num / den).astype(o_ref.dtype)


def _attention(qkv, bias):
    bsz, s, _ = qkv.shape
    nkb = s // ATT_BLOCK
    win = _attn_window(nkb)
    assert nkb % 2 == 0 and bias.shape[2] == (2 * win - 1) * ATT_BLOCK
    npair = ATT_WIDTH // LANES
    return pl.pallas_call(
        functools.partial(_attn_kernel, nkb=nkb, win=win),
        out_shape=jax.ShapeDtypeStruct((bsz, s, ATT_WIDTH), BF16),
        grid=(npair, bsz, nkb),
        in_specs=[pl.BlockSpec((1, ATT_BLOCK, LANES), lambda hp, b, i: (b, i, hp)),
                  pl.BlockSpec((1, s, LANES), lambda hp, b, i: (b, 0, npair + hp)),
                  pl.BlockSpec((1, s, LANES), lambda hp, b, i: (b, 0, 2 * npair + hp)),
                  pl.BlockSpec((2, ATT_BLOCK, bias.shape[2]), lambda hp, b, i: (hp, 0, 0))],
        out_specs=pl.BlockSpec((1, ATT_BLOCK, LANES), lambda hp, b, i: (b, i, hp)),
        scratch_shapes=[pltpu.VMEM((2, ATT_BLOCK, win * ATT_BLOCK), F32)],
        compiler_params=_params("arbitrary", "arbitrary", "arbitrary"),
        name="attn",
    )(qkv, qkv, qkv, bias)


def _prep_kernel(z_ref, zp_ref, zn_ref, mup_ref, mun_ref, wup_ref, aup_ref, gup_ref, w0_ref, a0_ref,
                 kk_ref, ka_ref, rk_ref, bd_ref, o_ref, *, tq):
    i = pl.program_id(1)
    last = pl.num_programs(1) - 1
    z = z_ref[0]
    row = lax.broadcasted_iota(jnp.int32, (tq, 1), 0)
    prev_row = zp_ref[0, 7:8, :] * (i > 0).astype(F32)
    next_row = zn_ref[0, 0:1, :] * (i < last).astype(F32)
    zp = jnp.where(row == 0, prev_row, pltpu.roll(z, 1, axis=0))
    zn = jnp.where(row == tq - 1, next_row, pltpu.roll(z, tq - 1, axis=0))
    zs = z + mup_ref[...] * (zp - z) + mun_ref[...] * (zn - z)

    w = RWKV_WIDTH
    r = zs[:, 0:w]
    k = zs[:, w:2 * w]
    v = zs[:, 2 * w:3 * w]
    xwa = zs[:, 3 * w:3 * w + LANES]
    xg = zs[:, 3 * w + LANES:]
    bd = bd_ref[...]

    g = _dot(_sigmoid(xg).astype(BF16), gup_ref[...])
    kk = k * kk_ref[...]
    ss = _mm_exact_rhs(kk * kk, bd)
    kk = kk * lax.rsqrt(jnp.maximum(ss, 1e-12))
    ka = ka_ref[...]
    txw = jnp.tanh(xwa).astype(BF16)
    xab = xwa.astype(BF16)

    o_ref[0, :, SLAB_R * w:(SLAB_R + 1) * w] = r
    o_ref[0, :, SLAB_V * w:(SLAB_V + 1) * w] = v
    o_ref[0, :, SLAB_KK * w:(SLAB_KK + 1) * w] = kk
    o_ref[0, :, SLAB_G * w:(SLAB_G + 1) * w] = g
    a_sum = jnp.zeros_like(k)
    for d in range(2):
        y = w0_ref[d:d + 1, :] + _dot(txw, wup_ref[d])
        wlog = -(jnp.maximum(-y, 0.0) + jnp.log(1.0 + jnp.exp(-jnp.abs(y)))) - 0.5
        a = _sigmoid(a0_ref[d:d + 1, :] + _dot(xab, aup_ref[d]))
        a_sum = a_sum + a
        base = (SLAB_DIR0 + 3 * d) * w
        o_ref[0, :, base:base + w] = -jnp.exp(wlog)
        o_ref[0, :, base + w:base + 2 * w] = k * (1.0 + (a - 1.0) * ka)
        o_ref[0, :, base + 2 * w:base + 3 * w] = kk * a
    k_bonus = k * (1.0 + (0.5 * a_sum - 1.0) * ka)
    bsum = _mm_exact_rhs(r * k_bonus * rk_ref[...], bd)
    o_ref[0, :, SLAB_BONUS * w:(SLAB_BONUS + 1) * w] = bsum * v


def _rwkv_prep(zrw, mu_prev, mu_next, wup_pad, aup_pad, g_up_bf, w0, a0, k_k, k_a, r_k, bd512, tq=256):
    bsz, s, sw = zrw.shape
    w = RWKV_WIDTH
    nt = s // tq
    row = lambda b, i: (0, 0)
    return pl.pallas_call(
        functools.partial(_prep_kernel, tq=tq),
        out_shape=jax.ShapeDtypeStruct((bsz, s, NUM_SLABS * w), F32),
        grid=(bsz, nt),
        in_specs=[pl.BlockSpec((1, tq, sw), lambda b, i: (b, i, 0)),
                  pl.BlockSpec((1, 8, sw), lambda b, i: (b, jnp.maximum(i * (tq // 8) - 1, 0), 0)),
                  pl.BlockSpec((1, 8, sw), lambda b, i: (b, jnp.minimum((i + 1) * (tq // 8), s // 8 - 1), 0)),
                  pl.BlockSpec((1, sw), row),
                  pl.BlockSpec((1, sw), row),
                  pl.BlockSpec((2, LANES, w), lambda b, i: (0, 0, 0)),
                  pl.BlockSpec((2, LANES, w), lambda b, i: (0, 0, 0)),
                  pl.BlockSpec((GATE_LORA, w), row),
                  pl.BlockSpec((2, w), row),
                  pl.BlockSpec((2, w), row),
                  pl.BlockSpec((1, w), row),
                  pl.BlockSpec((1, w), row),
                  pl.BlockSpec((1, w), row),
                  pl.BlockSpec((w, w), row)],
        out_specs=pl.BlockSpec((1, tq, NUM_SLABS * w), lambda b, i: (b, i, 0)),
        compiler_params=_params("arbitrary", "arbitrary"),
        name="rwkv_prep",
    )(zrw, zrw, zrw, mu_prev, mu_next, wup_pad, aup_pad, g_up_bf, w0, a0, k_k, k_a, r_k, bd512)


def _block_diag(x, first):
    zero = jnp.zeros_like(x)
    return jnp.concatenate([jnp.where(first, x, zero), jnp.where(first, zero, x)], axis=0)


def _p1_kernel(r_ref, v_ref, kk_ref, lw_ref, kd_ref, be_ref, rp_ref, qp_ref, g_ref, h_ref, *, cpb):
    c = CHUNK
    sign = 1 - 2 * pl.program_id(2)
    rowi = lax.broadcasted_iota(jnp.int32, (c, LANES), 0)
    lane = lax.broadcasted_iota(jnp.int32, (c, LANES), 1)
    coli = lane & (c - 1)
    first = lane < HEAD_DIM
    before = sign * (rowi - coli)
    strict2 = before > 0
    incl2 = before >= 0
    eye2 = (coli == rowi).astype(F32)
    r64 = lax.broadcasted_iota(jnp.int32, (c, c), 0)
    c64 = lax.broadcasted_iota(jnp.int32, (c, c), 1)
    cum = (sign * (r64 - c64) >= 0).astype(BF16)
    r128 = lax.broadcasted_iota(jnp.int32, (LANES, LANES), 0)
    c128 = lax.broadcasted_iota(jnp.int32, (LANES, LANES), 1)
    same_head = (r128 // HEAD_DIM) == (c128 // HEAD_DIM)
    eye128 = (r128 == c128).astype(F32)
    zero128 = jnp.zeros((LANES, LANES), F32)

    js = range(cpb)
    sls = [pl.ds(j * c, c) for j in js]
    bdg = lambda x: _block_diag(x, first)
    r = [r_ref[0, sl, :] for sl in sls]
    v = [v_ref[0, sl, :] for sl in sls]
    kk = [kk_ref[0, sl, :] for sl in sls]
    lw = [lw_ref[0, sl, :] for sl in sls]
    kd = [kd_ref[0, sl, :] for sl in sls]
    be = [be_ref[0, sl, :] for sl in sls]

    one = lambda x, y, dims=NN: _mm(x, y, dims, 1)
    rows2 = lambda x, y: jnp.concatenate([x, y], axis=0)
    cols2 = lambda x, y: jnp.concatenate([x, y], axis=1)

    def cumsum(x):
        l1 = x.astype(BF16)
        rem = x - l1.astype(F32)
        l2 = rem.astype(BF16)
        l3 = (rem - l2.astype(F32)).astype(BF16)
        y = _dot(cum, jnp.concatenate([l1, l2, l3], axis=1))
        return y[:, :LANES] + (y[:, LANES:2 * LANES] + y[:, 2 * LANES:])

    cs = [cumsum(lw[j]) for j in js]
    g_inv = [jnp.exp(-cs[j]) for j in js]
    g_tot = [jnp.exp(jnp.sum(lw[j], axis=0, keepdims=True)) for j in js]
    ab = [-kk[j] * jnp.exp(cs[j] - lw[j]) for j in js]
    rb = [r[j] * jnp.exp(cs[j]) for j in js]
    bt = [be[j] * g_inv[j] for j in js]
    kt = [kd[j] * g_inv[j] for j in js]

    sc = [one(rows2(ab[j], rb[j]), rows2(bdg(bt[j]), bdg(kt[j])), NT) for j in js]
    m_ab = [jnp.where(strict2, sc[j][:c, :LANES], 0.0) for j in js]
    n_rb = [jnp.where(incl2, sc[j][c:, :LANES], 0.0) for j in js]
    m_ak = [jnp.where(strict2, sc[j][:c, LANES:], 0.0) for j in js]
    n_rk = [jnp.where(incl2, sc[j][c:, LANES:], 0.0) for j in js]

    levels = int(math.log2(c)) - 1
    t = [eye2 + m_ab[j] for j in js]
    p = [one(m_ab[j], bdg(m_ab[j])) for j in js]
    for _ in range(1, levels - 1):
        y = [one(rows2(p[j], t[j]), bdg(p[j])) for j in js]
        p = [y[j][:c] for j in js]
        t = [t[j] + y[j][c:] for j in js]
    t = [t[j] + one(t[j], bdg(p[j])) for j in js]

    def residual(a, tt):
        ah, al = _split(a)
        th, tl = _split(bdg(tt))
        y = _dot(rows2(ah, al), th)
        return eye2 - (y[:c] + (y[c:] + _dot(ah, tl)))

    res = [residual(eye2 - m_ab[j], t[j]) for j in js]
    t = [t[j] + one(t[j], bdg(res[j])) for j in js]

    vbd = [bdg(v[j]) for j in js]
    mv = [one(rows2(m_ak[j], n_rk[j]), vbd[j]) for j in js]
    tp = [one(t[j], cols2(bdg(ab[j]), bdg(mv[j][:c]))) for j in js]
    ap = [tp[j][:, :LANES] for j in js]
    pp = [tp[j][:, LANES:] for j in js]
    nr = [one(n_rb[j], cols2(bdg(ap[j]), bdg(pp[j]))) for j in js]
    for j in js:
        rp_ref[0, 0, 0, sls[j], :] = rb[j] + nr[j][:, :LANES]
        qp_ref[0, 0, 0, sls[j], :] = nr[j][:, LANES:] + mv[j][c:]
    for j in js:
        st = one(cols2(ap[j], pp[j]), bt[j], TN)
        gmat = eye128 + jnp.where(same_head, st[:LANES], zero128)
        hmat = jnp.where(same_head, st[LANES:] + one(v[j], kt[j], TN), zero128)
        g_ref[0, 0, 0, j] = gmat * g_tot[j]
        h_ref[0, 0, 0, j] = hmat * g_tot[j]


def _rwkv_p1(pk, cpb=16):
    bsz, s, _ = pk.shape
    nc = s // CHUNK
    npair = RWKV_WIDTH // LANES
    rows = cpb * CHUNK

    def slab(sidx):
        return pl.BlockSpec((1, rows, LANES), lambda b, hp, d, ci: (b, ci, sidx * npair + hp))

    def dslab(off):
        return pl.BlockSpec((1, rows, LANES),
                            lambda b, hp, d, ci: (b, ci, (SLAB_DIR0 + 3 * d + off) * npair + hp))

    seq = pl.BlockSpec((1, 1, 1, rows, LANES), lambda b, hp, d, ci: (b, hp, d, ci, 0))
    mat = pl.BlockSpec((1, 1, 1, cpb, LANES, LANES), lambda b, hp, d, ci: (b, hp, d, ci, 0, 0))
    return pl.pallas_call(
        functools.partial(_p1_kernel, cpb=cpb),
        out_shape=(jax.ShapeDtypeStruct((bsz, npair, 2, s, LANES), F32),
                   jax.ShapeDtypeStruct((bsz, npair, 2, s, LANES), F32),
                   jax.ShapeDtypeStruct((bsz, npair, 2, nc, LANES, LANES), F32),
                   jax.ShapeDtypeStruct((bsz, npair, 2, nc, LANES, LANES), F32)),
        grid=(bsz, npair, 2, nc // cpb),
        in_specs=[slab(SLAB_R), slab(SLAB_V), slab(SLAB_KK), dslab(0), dslab(1), dslab(2)],
        out_specs=(seq, seq, mat, mat),
        compiler_params=_params("arbitrary", "arbitrary", "arbitrary", "arbitrary"),
        name="rwkv_p1",
    )(pk, pk, pk, pk, pk, pk)


def _p2_kernel(rp_ref, qp_ref, g_ref, h_ref, gate_ref, bonus_ref, lng_ref, lnb_ref, bd_ref, o_ref,
               yf_ref, yb_ref, sf_ref, sb_ref, *, nc, te):
    c = CHUNK
    sf_ref[...] = jnp.zeros_like(sf_ref)
    sb_ref[...] = jnp.zeros_like(sb_ref)

    def step(j, _):
        jf = j
        jb = nc - 1 - j
        slf = pl.ds(pl.multiple_of(jf * c, c), c)
        slb = pl.ds(pl.multiple_of(jb * c, c), c)
        sf = sf_ref[...]
        sb = sb_ref[...]
        yf_ref[slf, :] = _mm(rp_ref[0, 0, 0, slf, :], sf, NT, WKV_PASSES_CHAIN) + qp_ref[0, 0, 0, slf, :]
        yb_ref[slb, :] = _mm(rp_ref[0, 0, 1, slb, :], sb, NT, WKV_PASSES_CHAIN) + qp_ref[0, 0, 1, slb, :]
        sf_ref[...] = _mm(sf, g_ref[0, 0, 0, jf], NN, WKV_PASSES_CHAIN) + h_ref[0, 0, 0, jf]
        sb_ref[...] = _mm(sb, g_ref[0, 0, 1, jb], NN, WKV_PASSES_CHAIN) + h_ref[0, 0, 1, jb]
        return 0

    lax.fori_loop(0, nc, step, 0)

    bd = bd_ref[...]
    lng = lng_ref[...]
    lnb = lnb_ref[...]
    inv = 1.0 / HEAD_DIM

    def epi(i, _):
        sl = pl.ds(pl.multiple_of(i * te, te), te)
        y = yf_ref[sl, :] + yb_ref[sl, :]
        mean = _mm_exact_rhs(y, bd) * inv
        yc = y - mean
        var = _mm_exact_rhs(yc * yc, bd) * inv
        yn = yc * lax.rsqrt(var + LNX_EPS) * lng + lnb
        o_ref[0, sl, :] = ((yn + bonus_ref[0, sl, :]) * gate_ref[0, sl, :]).astype(o_ref.dtype)
        return 0

    lax.fori_loop(0, (nc * c) // te, epi, 0)


def _rwkv_p2(rp, qp, gm, hm, pk, lnx_g, lnx_b, bd128, te=1024):
    bsz, npair, _, s, _ = rp.shape
    nc = s // CHUNK
    seq = pl.BlockSpec((1, 1, 2, s, LANES), lambda b, hp: (b, hp, 0, 0, 0))
    mat = pl.BlockSpec((1, 1, 2, nc, LANES, LANES), lambda b, hp: (b, hp, 0, 0, 0, 0))
    return pl.pallas_call(
        functools.partial(_p2_kernel, nc=nc, te=min(te, s)),
        out_shape=jax.ShapeDtypeStruct((bsz, s, RWKV_WIDTH), BF16),
        grid=(bsz, npair),
        in_specs=[seq, seq, mat, mat,
                  pl.BlockSpec((1, s, LANES), lambda b, hp: (b, 0, SLAB_G * npair + hp)),
                  pl.BlockSpec((1, s, LANES), lambda b, hp: (b, 0, SLAB_BONUS * npair + hp)),
                  pl.BlockSpec((1, LANES), lambda b, hp: (0, hp)),
                  pl.BlockSpec((1, LANES), lambda b, hp: (0, hp)),
                  pl.BlockSpec((LANES, LANES), lambda b, hp: (0, 0))],
        out_specs=pl.BlockSpec((1, s, LANES), lambda b, hp: (b, 0, hp)),
        scratch_shapes=[pltpu.VMEM((s, LANES), F32), pltpu.VMEM((s, LANES), F32),
                        pltpu.VMEM((LANES, LANES), F32), pltpu.VMEM((LANES, LANES), F32)],
        compiler_params=_params("arbitrary", "arbitrary"),
        name="rwkv_p2",
    )(rp, qp, gm, hm, pk, pk, lnx_g, lnx_b, bd128)


def _outproj_kernel(att_ref, rw_ref, x_ref, mod_ref, g2_ref, wo_ref, wq_ref, k1_ref, k2_ref,
                    x1_ref, h2_ref, s1_ref, s2_ref):
    mod = mod_ref[0]
    a = ATT_WIDTH
    mixed = _dot(att_ref[0], wo_ref[0:a, :]) + _dot(rw_ref[0], wo_ref[a:, :])
    x1 = x_ref[0] + mod[2:3] * mixed
    x1_ref[0] = x1
    ms = jnp.mean(x1 * x1, axis=-1, keepdims=True)
    h2 = x1 * lax.rsqrt(ms + NORM_EPS) * g2_ref[...]
    h2 = (h2 * (1.0 + mod[4:5]) + mod[3:4]).astype(BF16)
    h2_ref[0] = h2
    q = _dot(h2, wq_ref[...])
    k1 = k1_ref[...]
    k2 = k2_ref[...]
    for h in range(PEER_HEADS):
        base = h * 2 * LANES
        s1 = _dot(k1, q[:, base:base + LANES].astype(BF16), NT)
        s2 = _dot(k2, q[:, base + LANES:base + 2 * LANES].astype(BF16), NT)
        for g in range(s1.shape[1] // LANES):
            s1_ref[h, g] = s1[:, g * LANES:(g + 1) * LANES]
            s2_ref[h, g] = s2[:, g * LANES:(g + 1) * LANES]


def _outproj(att, rw, x, mod6, norm2_g, w_out_bf, wq_bf, k1_bf, k2_bf, tm=256):
    bsz, s, d = x.shape
    nt = s // tm
    t = bsz * s
    nq = wq_bf.shape[1]
    const = lambda b, i: (0, 0)
    gpt = tm // LANES
    tok = pl.BlockSpec((PEER_HEADS, gpt, PEER_N_KEYS, LANES), lambda b, i: (0, b * nt + i, 0, 0))
    stat = jax.ShapeDtypeStruct((PEER_HEADS, t // LANES, PEER_N_KEYS, LANES), F32)
    return pl.pallas_call(
        _outproj_kernel,
        out_shape=(jax.ShapeDtypeStruct((bsz, s, d), F32),
                   jax.ShapeDtypeStruct((bsz, s, d), BF16),
                   stat, stat),
        grid=(bsz, nt),
        in_specs=[pl.BlockSpec((1, tm, ATT_WIDTH), lambda b, i: (b, i, 0)),
                  pl.BlockSpec((1, tm, RWKV_WIDTH), lambda b, i: (b, i, 0)),
                  pl.BlockSpec((1, tm, d), lambda b, i: (b, i, 0)),
                  pl.BlockSpec((1, 6, d), lambda b, i: (b, 0, 0)),
                  pl.BlockSpec((1, d), const),
                  pl.BlockSpec((d, d), const),
                  pl.BlockSpec((d, nq), const),
                  pl.BlockSpec((PEER_N_KEYS, LANES), const),
                  pl.BlockSpec((PEER_N_KEYS, LANES), const)],
        out_specs=(pl.BlockSpec((1, tm, d), lambda b, i: (b, i, 0)),
                   pl.BlockSpec((1, tm, d), lambda b, i: (b, i, 0)),
                   tok, tok),
        compiler_params=_params("arbitrary", "arbitrary"),
        name="outproj",
    )(att, rw, x, mod6, norm2_g, w_out_bf, wq_bf, k1_bf, k2_bf)


def _oddeven_sort_pairs(n):
    pairs = []
    p = 1
    while p < n:
        k = p
        while k >= 1:
            for j in range(k % p, n - k, 2 * k):
                for i in range(min(k, n - j - k)):
                    if (i + j) // (2 * p) == (i + j + k) // (2 * p):
                        pairs.append((i + j, i + j + k))
            k //= 2
        p *= 2
    return pairs


def _bitonic_merge_pairs(n):
    pairs = []
    stride = n // 2
    while stride >= 1:
        pairs += [(i, i + stride) for i in range(n) if not i & stride]
        stride //= 2
    return pairs


def _compare_exchange(vals, pairs):
    vals = list(vals)
    for i, j in pairs:
        a, b = vals[i], vals[j]
        if b is None:
            continue
        if a is None:
            vals[i], vals[j] = b, None
        else:
            vals[i], vals[j] = jnp.maximum(a, b), jnp.minimum(a, b)
    return vals


def _top16_sorted(vregs):
    kk = PEER_TOPK
    vals = list(vregs) + [None] * (kk - len(vregs))
    vals = _compare_exchange(vals, _oddeven_sort_pairs(kk))
    for shift in (4, 2, 1):
        other = [None if v is None else pltpu.roll(v, shift, axis=0) for v in vals]
        merged = []
        for k in range(kk):
            a, b = vals[k], other[kk - 1 - k]
            merged.append(b if a is None else a if b is None else jnp.maximum(a, b))
        vals = _compare_exchange(merged, _bitonic_merge_pairs(kk))
    return vals


def _topk_kernel(s1_ref, s2_ref, c1_ref, e1_ref, r2_ref, e2_ref, *, groups):
    kk = PEER_TOPK
    sub = 8
    neg = -jnp.inf
    rows8 = lax.broadcasted_iota(jnp.int32, (sub, LANES), 0)

    def rows_of(reps):
        out = reps[0]
        for r in range(1, sub):
            out = jnp.where(rows8 == r, reps[r], out)
        return out

    def all_sublanes_sum(x):
        for shift in (4, 2, 1):
            x = x + pltpu.roll(x, shift, axis=0)
        return x

    def per_tile(it_idx, _):
        g = it_idx // PEER_HEADS
        h = it_idx % PEER_HEADS
        s1 = [s1_ref[h, g, pl.ds(v * sub, sub), :] for v in range(PEER_N_KEYS // sub)]
        s2 = [s2_ref[h, g, pl.ds(v * sub, sub), :] for v in range(PEER_N_KEYS // sub)]
        a = _top16_sorted(s1)
        b = _top16_sorted(s2)
        a_lo, a_hi = rows_of(a[:sub]), rows_of(a[sub:])
        b_lo, b_hi = rows_of(b[:sub]), rows_of(b[sub:])

        cand = [a[0] + b_lo, a[0] + b_hi, a[1] + b_lo]
        for i in range(2, sub):
            cand.append(jnp.where(rows8 < kk // (i + 1), a[i] + b_lo, neg))
        cand.append(a_hi + b[0])
        top = _top16_sorted(cand)
        tau = top[kk - 1]
        z = None
        for t in top:
            ez = jnp.exp(t - top[0])
            z = ez if z is None else z + ez
        inv_z = 1.0 / z

        counts = []
        for r in range(kk):
            hits = (jnp.where(a[r] + b_lo >= tau, 1.0, 0.0) + jnp.where(a[r] + b_hi >= tau, 1.0, 0.0))
            counts.append(all_sublanes_sum(hits))
        for v in range(PEER_N_KEYS // sub):
            rows = pl.ds(v * sub, sub)
            cnt = jnp.zeros((sub, LANES), F32)
            rank = jnp.full((sub, LANES), float(kk), F32)
            for r in range(kk):
                cnt = jnp.where(s1[v] == a[r], counts[r], cnt)
                rank = jnp.where(s2[v] == b[r], float(r), rank)
            c1_ref[h, g, rows, :] = cnt
            e1_ref[h, g, rows, :] = jnp.exp(s1[v] - a[0]) * inv_z
            s2[v] = (rank, jnp.exp(s2[v] - b[0]))
        rank2 = jnp.concatenate([rv for rv, _ in s2], axis=0).astype(BF16)
        e2 = jnp.concatenate([ev for _, ev in s2], axis=0).astype(BF16)
        r2_ref[h, g] = pltpu.bitcast(rank2, jnp.uint32)
        e2_ref[h, g] = pltpu.bitcast(e2, jnp.uint32)
        return 0

    lax.fori_loop(0, groups * PEER_HEADS, per_tile, 0)


def _peer_topk(s1t, s2t, tn=512):
    nh, ng, nk, _ = s1t.shape
    groups = tn // LANES
    blk = pl.BlockSpec((nh, groups, nk, LANES), lambda i: (0, i, 0, 0))
    f32 = jax.ShapeDtypeStruct(s1t.shape, F32)
    b16 = jax.ShapeDtypeStruct((nh, ng, nk // 2, LANES), jnp.uint32)
    pblk = pl.BlockSpec((nh, groups, nk // 2, LANES), lambda i: (0, i, 0, 0))
    return pl.pallas_call(
        functools.partial(_topk_kernel, groups=groups),
        out_shape=(f32, f32, b16, b16),
        grid=(ng // groups,),
        in_specs=[blk, blk],
        out_specs=(blk, blk, pblk, pblk),
        compiler_params=_params("arbitrary"),
        name="peer_topk",
    )(s1t, s2t)


def _peer_kernel(h2_ref, u_ref, vt_ref, c1_ref, e1_ref, r2_ref, e2_ref, x1_ref, mod_ref, o_ref,
                 acc_ref, act0_ref, act1_ref, p0_ref, p1_ref, *, tn, te, n_tiles):
    k = pl.program_id(0)
    n_items = pl.num_programs(0) - 2
    nk = PEER_N_KEYS
    ngroups = tn // LANES

    @pl.when(k == 0)
    def _():
        acc_ref[...] = jnp.zeros_like(acc_ref)
        for ref in (act0_ref, act1_ref, p0_ref, p1_ref):
            ref[...] = jnp.zeros_like(ref)

    ni = te // nk
    i0 = pl.multiple_of((jnp.clip(k - 1, 0, n_items - 1) % n_tiles) * ni, ni)
    tile_c = jnp.clip(k - 2, 0, n_items - 1) % n_tiles
    first_c = tile_c == 0

    d_model = acc_ref.shape[0]
    halves = [(mh, nh, kh) for mh in range(2) for nh in range(ngroups // 2) for kh in range(2)]

    def stage_c(p_a, mh, nh, kh):
        rows = slice(mh * (d_model // 2), (mh + 1) * (d_model // 2))
        cols = slice(nh * 2 * LANES, (nh + 1) * 2 * LANES)
        ks = slice(kh * (te // 2), (kh + 1) * (te // 2))
        p_prev = jnp.concatenate([p_a[2 * nh, ks, :], p_a[2 * nh + 1, ks, :]], axis=1)
        prev = acc_ref[rows, cols]
        if kh == 0:
            prev = jnp.where(first_c, jnp.zeros_like(prev), prev)
        acc_ref[rows, cols] = prev + _dot(vt_ref[rows, ks], p_prev)

    def stage_a(act_a, mh, nh, kh):
        rows = slice(mh * (te // 2), (mh + 1) * (te // 2))
        ks = slice(kh * (d_model // 2), (kh + 1) * (d_model // 2))
        act = _dot(u_ref[rows, ks], h2_ref[nh * 2 * LANES:(nh + 1) * 2 * LANES, ks], NT)
        if kh == 0:
            act_a[2 * nh, rows, :] = act[:, :LANES]
            act_a[2 * nh + 1, rows, :] = act[:, LANES:]
        else:
            act_a[2 * nh, rows, :] += act[:, :LANES]
            act_a[2 * nh + 1, rows, :] += act[:, LANES:]

    def rows_bf16(row):
        tile = jnp.broadcast_to(row, (BF16_ROWS, LANES)).astype(BF16)
        return jnp.concatenate([tile] * (nk // BF16_ROWS), axis=0)

    quad = 4

    def stage_b(act_b, p_b, tg, iq, pieces):
        gates = [jnp.zeros((nk, LANES), BF16) for _ in range(quad)]
        for h in range(PEER_HEADS):
            if h % (PEER_HEADS // 2) == 0 and pieces:
                pieces.pop(0)()
            c1 = c1_ref[h, tg, pl.ds(i0, ni), :]
            e1 = e1_ref[h, tg, pl.ds(i0, ni), :]
            rank2 = pltpu.bitcast(r2_ref[h, tg], BF16)
            e2 = pltpu.bitcast(e2_ref[h, tg], BF16)
            for q in range(quad):
                il = iq * quad + q
                sel = rank2 < rows_bf16(c1[il:il + 1, :])
                gates[q] = gates[q] + jnp.where(sel, e2 * rows_bf16(e1[il:il + 1, :]), jnp.zeros_like(e2))
        for q in range(quad):
            il = iq * quad + q
            a = act_b[tg, il * nk:(il + 1) * nk, :]
            gelu = 0.5 * a * (1.0 + lax.erf(a * (2.0 ** -0.5)))
            p_b[tg, il * nk:(il + 1) * nk, :] = gates[q] * gelu.astype(BF16)

    def stages(act_a, act_b, p_a, p_b):
        pieces = ([functools.partial(stage_c, p_a, *hv) for hv in halves]
                  + [functools.partial(stage_a, act_a, *hv) for hv in halves])
        for tg in range(ngroups):
            for iq in range(ni // quad):
                stage_b(act_b, p_b, tg, iq, pieces)
        while pieces:
            pieces.pop(0)()

    @pl.when(k % 2 == 0)
    def _():
        stages(act0_ref, act1_ref, p0_ref, p1_ref)

    @pl.when(k % 2 == 1)
    def _():
        stages(act1_ref, act0_ref, p1_ref, p0_ref)

    @pl.when((tile_c == n_tiles - 1) & (k >= 2))
    def _():
        o_ref[...] = x1_ref[...] + mod_ref[0, 5:6, :] * acc_ref[...].T


def _peer_ffn(h2, u_bf, vt_bf, c1, e1, r2, e2, x1, mod6, seq, tn=512, te=1024):
    t, d = h2.shape
    ne = u_bf.shape[0]
    ngroups = tn // LANES
    n_tiles = ne // te
    n_items = (t // tn) * n_tiles

    def item(k, lag):
        w = jnp.clip(k - lag, 0, n_items - 1)
        return w // n_tiles, w % n_tiles

    stat = pl.BlockSpec((PEER_HEADS, ngroups, PEER_N_KEYS, LANES), lambda k: (0, item(k, 1)[0], 0, 0))
    pstat = pl.BlockSpec((PEER_HEADS, ngroups, PEER_N_KEYS // 2, LANES), lambda k: (0, item(k, 1)[0], 0, 0))
    return pl.pallas_call(
        functools.partial(_peer_kernel, tn=tn, te=te, n_tiles=n_tiles),
        out_shape=jax.ShapeDtypeStruct((t, d), F32),
        grid=(n_items + 2,),
        in_specs=[pl.BlockSpec((tn, d), lambda k: (item(k, 0)[0], 0)),
                  pl.BlockSpec((te, d), lambda k: (item(k, 0)[1], 0)),
                  pl.BlockSpec((d, te), lambda k: (0, item(k, 2)[1])),
                  stat, stat, pstat, pstat,
                  pl.BlockSpec((tn, d), lambda k: (item(k, 2)[0], 0)),
                  pl.BlockSpec((1, 6, d), lambda k: ((item(k, 2)[0] * tn) // seq, 0, 0))],
        out_specs=pl.BlockSpec((tn, d), lambda k: (item(k, 2)[0], 0)),
        scratch_shapes=[pltpu.VMEM((d, tn), F32),
                        pltpu.VMEM((ngroups, te, LANES), F32), pltpu.VMEM((ngroups, te, LANES), F32),
                        pltpu.VMEM((ngroups, te, LANES), BF16), pltpu.VMEM((ngroups, te, LANES), BF16)],
        compiler_params=_params("arbitrary"),
        name="peer_ffn",
    )(h2, u_bf, vt_bf, c1, e1, r2, e2, x1, mod6)


def _head_block_diag(n):
    idx = jnp.arange(n, dtype=jnp.int32) // HEAD_DIM
    return (idx[:, None] == idx[None, :]).astype(BF16)


def _layer(x, c, ada_w, ada_b, norm1_g, w_in, mu_prev, mu_next, q_norm_g, k_norm_g, w_decay0, w_decay_up,
           a_gate0, a_gate_up, g_up, k_k, k_a, r_k, lnx_g, lnx_b, w_out, norm2_g, peer_w_query,
           peer_sub_keys1, peer_sub_keys2, peer_u, peer_v):
    bsz, s, d = x.shape
    n_att_heads = ATT_WIDTH // HEAD_DIM
    bd512 = _head_block_diag(ATT_WIDTH)
    bd128 = _head_block_diag(LANES)

    mod6 = _ada(c, ada_w, ada_b).reshape(bsz, 6, d)
    gq = jnp.tile(q_norm_g, n_att_heads).reshape(1, ATT_WIDTH)
    gk = jnp.tile(k_norm_g, n_att_heads).reshape(1, ATT_WIDTH)
    qkv, zrw = _inproj(x, mod6, norm1_g.reshape(1, d), w_in.astype(BF16), gq, gk, bd512)

    att = _attention(qkv, _attn_bias_table(n_att_heads, s // ATT_BLOCK))

    zero_w = jnp.zeros((2, LANES - DECAY_LORA, RWKV_WIDTH), F32)
    wup_pad = jnp.concatenate([w_decay_up, zero_w], axis=1).astype(BF16)
    aup_pad = jnp.concatenate([zero_w, a_gate_up], axis=1).astype(BF16)
    pk = _rwkv_prep(zrw, mu_prev.reshape(1, -1), mu_next.reshape(1, -1), wup_pad, aup_pad, g_up.astype(BF16),
                    w_decay0, a_gate0, k_k.reshape(1, -1), k_a.reshape(1, -1), r_k.reshape(1, -1), bd512)
    rp, qp, gm, hm = _rwkv_p1(pk)
    rw = _rwkv_p2(rp, qp, gm, hm, pk, lnx_g.reshape(1, -1), lnx_b.reshape(1, -1), bd128)

    x1, h2, s1t, s2t = _outproj(att, rw, x, mod6, norm2_g.reshape(1, d), w_out.astype(BF16),
                                peer_w_query.astype(BF16), peer_sub_keys1.astype(BF16),
                                peer_sub_keys2.astype(BF16))
    c1, e1, r2, e2 = _peer_topk(s1t, s2t)
    out = _peer_ffn(h2.reshape(bsz * s, d), peer_u.astype(BF16), peer_v.T.astype(BF16), c1, e1, r2, e2,
                    x1.reshape(bsz * s, d), mod6, s)
    return out.reshape(bsz, s, d)


def kernel(x, c, ada_w, ada_b, norm1_g, w_in, mu_prev, mu_next, q_norm_g, k_norm_g, w_decay0, w_decay_up,
           a_gate0, a_gate_up, g_up, k_k, k_a, r_k, lnx_g, lnx_b, w_out, norm2_g, peer_w_query,
           peer_sub_keys1, peer_sub_keys2, peer_u, peer_v):
    depth = ada_w.shape[0]
    for l in range(depth):
        x = _layer(x, c, ada_w[l], ada_b[l], norm1_g[l], w_in[l], mu_prev[l], mu_next[l], q_norm_g[l],
                   k_norm_g[l], w_decay0[l], w_decay_up[l], a_gate0[l], a_gate_up[l], g_up[l], k_k[l], k_a[l],
                   r_k[l], lnx_g[l], lnx_b[l], w_out[l], norm2_g[l], peer_w_query[l], peer_sub_keys1[l],
                   peer_sub_keys2[l], peer_u[l], peer_v[l])
    return x
```

```python
import functools
import math

import jax
import jax.numpy as jnp
from jax import lax
from jax.experimental import pallas as pl
from jax.experimental.pallas import tpu as pltpu

F32 = jnp.float32
BF16 = jnp.bfloat16

HEAD_DIM = 64
ATT_WIDTH = 512
RWKV_WIDTH = 512
DECAY_LORA = 64
AAA_LORA = 64
GATE_LORA = 128
SHIFT_WIDTH = 3 * RWKV_WIDTH + DECAY_LORA + AAA_LORA + GATE_LORA
PEER_N_KEYS = 128
PEER_HEADS = 8
PEER_TOPK = 16
NORM_EPS = 1e-6
LNX_EPS = 64e-5
MASK_VALUE = -1e30
LOG2E = 1.4426950408889634
ATT_HALF_WINDOWS = ((1, 64), (4, 256), (16, 1024))

LANES = 128
BF16_ROWS = 16
CHUNK = 64
ATT_BLOCK = 128
ATT_REACH = 1024 // ATT_BLOCK
VMEM_LIMIT_BYTES = 56 * 1024 * 1024
WKV_PASSES_CHAIN = 1

NN = (((1,), (0,)), ((), ()))
NT = (((1,), (1,)), ((), ()))
TN = (((0,), (0,)), ((), ()))

SLAB_R, SLAB_V, SLAB_KK, SLAB_DIR0, SLAB_G, SLAB_BONUS, NUM_SLABS = 0, 1, 2, 3, 9, 10, 11


def _params(*sem):
    return pltpu.CompilerParams(dimension_semantics=sem, vmem_limit_bytes=VMEM_LIMIT_BYTES)


def _dot(a, b, dims=NN):
    return lax.dot_general(a, b, dims, preferred_element_type=F32)


def _split(a):
    hi = a.astype(BF16)
    lo = (a - hi.astype(F32)).astype(BF16)
    return hi, lo


def _mm(a, b, dims=NN, passes=3):
    if passes == 1:
        return _dot(a.astype(BF16), b.astype(BF16), dims)
    ah, al = _split(a)
    bh, bl = _split(b)
    return _dot(ah, bh, dims) + (_dot(ah, bl, dims) + _dot(al, bh, dims))


def _mm_exact_rhs(a, b_exact):
    ah, al = _split(a)
    return _dot(ah, b_exact) + _dot(al, b_exact)


def _sigmoid(x):
    return 1.0 / (1.0 + jnp.exp(-x))


def _ada_kernel(c_ref, w_ref, b_ref, o_ref):
    c = c_ref[...]
    o_ref[...] = _mm(c * _sigmoid(c), w_ref[...]) + b_ref[...]


def _ada(c, ada_w, ada_b):
    bsz, d = c.shape
    n = ada_w.shape[1]
    tn = 1024
    return pl.pallas_call(
        _ada_kernel,
        out_shape=jax.ShapeDtypeStruct((bsz, n), F32),
        grid=(n // tn,),
        in_specs=[pl.BlockSpec((bsz, d), lambda j: (0, 0)),
                  pl.BlockSpec((d, tn), lambda j: (0, j)),
                  pl.BlockSpec((1, tn), lambda j: (0, j))],
        out_specs=pl.BlockSpec((bsz, tn), lambda j: (0, j)),
        compiler_params=_params("arbitrary"),
        name="ada",
    )(c, ada_w, ada_b.reshape(1, n))


def _inproj_kernel(x_ref, mod_ref, g1_ref, w_ref, gq_ref, gk_ref, bd_ref, qkv_ref, zrw_ref):
    x = x_ref[0]
    mod = mod_ref[0]
    ms = jnp.mean(x * x, axis=-1, keepdims=True)
    h = x * lax.rsqrt(ms + NORM_EPS) * g1_ref[...]
    h = (h * (1.0 + mod[1:2]) + mod[0:1]).astype(BF16)
    bd = bd_ref[...]

    def head_norm(z, g):
        ss = _mm_exact_rhs(z * z, bd) * (1.0 / HEAD_DIM)
        return z * lax.rsqrt(ss + NORM_EPS) * g

    a = ATT_WIDTH
    zq = _dot(h, w_ref[:, 0:a])
    qkv_ref[0, :, 0:a] = (head_norm(zq, gq_ref[...]) * (HEAD_DIM ** -0.5 * LOG2E)).astype(BF16)
    zk = _dot(h, w_ref[:, a:2 * a])
    qkv_ref[0, :, a:2 * a] = head_norm(zk, gk_ref[...]).astype(BF16)
    qkv_ref[0, :, 2 * a:3 * a] = _dot(h, w_ref[:, 2 * a:3 * a]).astype(BF16)
    zrw_ref[0] = _dot(h, w_ref[:, 3 * a:])


def _inproj(x, mod6, norm1_g, w_in_bf, gq, gk, bd512, tm=512):
    bsz, s, d = x.shape
    nin = w_in_bf.shape[1]
    return pl.pallas_call(
        _inproj_kernel,
        out_shape=(jax.ShapeDtypeStruct((bsz, s, 3 * ATT_WIDTH), BF16),
                   jax.ShapeDtypeStruct((bsz, s, SHIFT_WIDTH), F32)),
        grid=(bsz, s // tm),
        in_specs=[pl.BlockSpec((1, tm, d), lambda b, i: (b, i, 0)),
                  pl.BlockSpec((1, 6, d), lambda b, i: (b, 0, 0)),
                  pl.BlockSpec((1, d), lambda b, i: (0, 0)),
                  pl.BlockSpec((d, nin), lambda b, i: (0, 0)),
                  pl.BlockSpec((1, ATT_WIDTH), lambda b, i: (0, 0)),
                  pl.BlockSpec((1, ATT_WIDTH), lambda b, i: (0, 0)),
                  pl.BlockSpec((ATT_WIDTH, ATT_WIDTH), lambda b, i: (0, 0))],
        out_specs=(pl.BlockSpec((1, tm, 3 * ATT_WIDTH), lambda b, i: (b, i, 0)),
                   pl.BlockSpec((1, tm, SHIFT_WIDTH), lambda b, i: (b, i, 0))),
        compiler_params=_params("arbitrary", "arbitrary"),
        name="inproj",
    )(x, mod6, norm1_g, w_in_bf, gq, gk, bd512)


def _attn_window(nkb):
    return nkb if nkb <= 2 * ATT_REACH + 2 else 2 * ATT_REACH + 2


def _attn_bias_table(n_heads, nkb):
    reach = _attn_window(nkb) - 1
    nd = 2 * reach + 1
    dd = jnp.arange(nd, dtype=jnp.int32)[None, :, None]
    r = jnp.arange(ATT_BLOCK, dtype=jnp.int32)[:, None, None]
    c = jnp.arange(ATT_BLOCK, dtype=jnp.int32)[None, None, :]
    dt = r - c - (dd - reach) * ATT_BLOCK
    adt = jnp.abs(dt)
    mult = jnp.zeros(dt.shape, F32)
    for dil, half in ATT_HALF_WINDOWS:
        mult = mult + ((adt <= half) & (dt % dil == 0)).astype(F32)
    logm = jnp.where(mult > 0, jnp.log2(jnp.maximum(mult, 1.0)), MASK_VALUE)
    slopes = jnp.exp2(-8.0 * (jnp.arange(n_heads, dtype=F32) + 1.0) / n_heads) * LOG2E
    bias = logm[None] - slopes[:, None, None, None] * adt.astype(F32)[None]
    return bias.reshape(n_heads, ATT_BLOCK, nd * ATT_BLOCK)


def _attn_kernel(q_ref, k_ref, v_ref, bias_ref, o_ref, s_ref, v1_ref, *, nkb, win):
    blk = ATT_BLOCK
    reach = win - 1
    lane = lax.broadcasted_iota(jnp.int32, (blk, LANES), 1)
    first = lane < HEAD_DIM

    v = v_ref[0]
    one = jnp.ones_like(v)
    first_all = lax.broadcasted_iota(jnp.int32, v.shape, 1) < HEAD_DIM
    v1_ref[0] = jnp.where(first_all, v, one)
    v1_ref[1] = jnp.where(first_all, one, v)

    def query_block(qi, _):
        rows = pl.ds(pl.multiple_of(qi * blk, blk), blk)
        q = q_ref[0, rows, :]
        zero = jnp.zeros_like(q)
        qab = jnp.concatenate([jnp.where(first, q, zero), jnp.where(first, zero, q)], axis=0)
        ks = 0 if win == nkb else jnp.clip(qi - ATT_REACH, 0, nkb - win)

        m = [jnp.full((blk, LANES), -jnp.inf, F32) for _ in range(2)]
        for w0 in range(0, win, 2):
            kj0 = ks + w0
            kslab = k_ref[0, pl.ds(pl.multiple_of(kj0 * blk, blk), 2 * blk), :]
            s = _dot(qab, kslab, NT)
            x0 = pl.multiple_of((kj0 - qi + reach) * blk, blk)
            for hh in range(2):
                sb = s[hh * blk:(hh + 1) * blk] + bias_ref[hh, :, pl.ds(x0, 2 * blk)]
                s_ref[hh, :, w0 * blk:(w0 + 2) * blk] = sb
                m[hh] = jnp.maximum(m[hh], jnp.maximum(sb[:, :blk], sb[:, blk:]))
        mrow = [jnp.max(m[hh], axis=-1, keepdims=True) for hh in range(2)]

        acc = [jnp.zeros((blk, LANES), F32) for _ in range(2)]
        for w0 in range(0, win, 2):
            keys = pl.ds(pl.multiple_of((ks + w0) * blk, blk), 2 * blk)
            for hh in range(2):
                p = jnp.exp2(s_ref[hh, :, w0 * blk:(w0 + 2) * blk] - mrow[hh]).astype(BF16)
                acc[hh] = acc[hh] + _dot(p, v1_ref[hh, keys, :])
        num = jnp.where(first, acc[0], acc[1])
        den = jnp.where(first, pltpu.roll(acc[0], HEAD_DIM, axis=1), pltpu.roll(acc[1], HEAD_DIM, axis=1))
        o_ref[0, rows, :] = (num / den).astype(o_ref.dtype)
        return 0

    lax.fori_loop(0, nkb, query_block, 0)


def _attention(qkv, bias):
    bsz, s, _ = qkv.shape
    nkb = s // ATT_BLOCK
    win = _attn_window(nkb)
    assert nkb % 2 == 0 and bias.shape[2] == (2 * win - 1) * ATT_BLOCK
    npair = ATT_WIDTH // LANES
    return pl.pallas_call(
        functools.partial(_attn_kernel, nkb=nkb, win=win),
        out_shape=jax.ShapeDtypeStruct((bsz, s, ATT_WIDTH), BF16),
        grid=(npair, bsz),
        in_specs=[pl.BlockSpec((1, s, LANES), lambda hp, b: (b, 0, hp)),
                  pl.BlockSpec((1, s, LANES), lambda hp, b: (b, 0, npair + hp)),
                  pl.BlockSpec((1, s, LANES), lambda hp, b: (b, 0, 2 * npair + hp)),
                  pl.BlockSpec((2, ATT_BLOCK, bias.shape[2]), lambda hp, b: (hp, 0, 0))],
        out_specs=pl.BlockSpec((1, s, LANES), lambda hp, b: (b, 0, hp)),
        scratch_shapes=[pltpu.VMEM((2, ATT_BLOCK, win * ATT_BLOCK), F32), pltpu.VMEM((2, s, LANES), BF16)],
        compiler_params=_params("arbitrary", "arbitrary"),
        name="attn",
    )(qkv, qkv, qkv, bias)


def _prep_kernel(z_ref, zp_ref, zn_ref, mup_ref, mun_ref, wup_ref, aup_ref, gup_ref, w0_ref, a0_ref,
                 kk_ref, ka_ref, rk_ref, bd_ref, o_ref, *, tq):
    i = pl.program_id(1)
    last = pl.num_programs(1) - 1
    z = z_ref[0]
    row = lax.broadcasted_iota(jnp.int32, (tq, 1), 0)
    prev_row = zp_ref[0, 7:8, :] * (i > 0).astype(F32)
    next_row = zn_ref[0, 0:1, :] * (i < last).astype(F32)
    zp = jnp.where(row == 0, prev_row, pltpu.roll(z, 1, axis=0))
    zn = jnp.where(row == tq - 1, next_row, pltpu.roll(z, tq - 1, axis=0))
    zs = z + mup_ref[...] * (zp - z) + mun_ref[...] * (zn - z)

    w = RWKV_WIDTH
    r = zs[:, 0:w]
    k = zs[:, w:2 * w]
    v = zs[:, 2 * w:3 * w]
    xwa = zs[:, 3 * w:3 * w + LANES]
    xg = zs[:, 3 * w + LANES:]
    bd = bd_ref[...]

    g = _dot(_sigmoid(xg).astype(BF16), gup_ref[...])
    kk = k * kk_ref[...]
    ss = _mm_exact_rhs(kk * kk, bd)
    kk = kk * lax.rsqrt(jnp.maximum(ss, 1e-12))
    ka = ka_ref[...]
    txw = jnp.tanh(xwa).astype(BF16)
    xab = xwa.astype(BF16)

    o_ref[0, :, SLAB_R * w:(SLAB_R + 1) * w] = r
    o_ref[0, :, SLAB_V * w:(SLAB_V + 1) * w] = v
    o_ref[0, :, SLAB_KK * w:(SLAB_KK + 1) * w] = kk
    o_ref[0, :, SLAB_G * w:(SLAB_G + 1) * w] = g
    a_sum = jnp.zeros_like(k)
    for d in range(2):
        y = w0_ref[d:d + 1, :] + _dot(txw, wup_ref[d])
        wlog = -(jnp.maximum(-y, 0.0) + jnp.log(1.0 + jnp.exp(-jnp.abs(y)))) - 0.5
        a = _sigmoid(a0_ref[d:d + 1, :] + _dot(xab, aup_ref[d]))
        a_sum = a_sum + a
        base = (SLAB_DIR0 + 3 * d) * w
        o_ref[0, :, base:base + w] = -jnp.exp(wlog)
        o_ref[0, :, base + w:base + 2 * w] = k * (1.0 + (a - 1.0) * ka)
        o_ref[0, :, base + 2 * w:base + 3 * w] = kk * a
    k_bonus = k * (1.0 + (0.5 * a_sum - 1.0) * ka)
    bsum = _mm_exact_rhs(r * k_bonus * rk_ref[...], bd)
    o_ref[0, :, SLAB_BONUS * w:(SLAB_BONUS + 1) * w] = bsum * v


def _rwkv_prep(zrw, mu_prev, mu_next, wup_pad, aup_pad, g_up_bf, w0, a0, k_k, k_a, r_k, bd512, tq=256):
    bsz, s, sw = zrw.shape
    w = RWKV_WIDTH
    nt = s // tq
    row = lambda b, i: (0, 0)
    return pl.pallas_call(
        functools.partial(_prep_kernel, tq=tq),
        out_shape=jax.ShapeDtypeStruct((bsz, s, NUM_SLABS * w), F32),
        grid=(bsz, nt),
        in_specs=[pl.BlockSpec((1, tq, sw), lambda b, i: (b, i, 0)),
                  pl.BlockSpec((1, 8, sw), lambda b, i: (b, jnp.maximum(i * (tq // 8) - 1, 0), 0)),
                  pl.BlockSpec((1, 8, sw), lambda b, i: (b, jnp.minimum((i + 1) * (tq // 8), s // 8 - 1), 0)),
                  pl.BlockSpec((1, sw), row),
                  pl.BlockSpec((1, sw), row),
                  pl.BlockSpec((2, LANES, w), lambda b, i: (0, 0, 0)),
                  pl.BlockSpec((2, LANES, w), lambda b, i: (0, 0, 0)),
                  pl.BlockSpec((GATE_LORA, w), row),
                  pl.BlockSpec((2, w), row),
                  pl.BlockSpec((2, w), row),
                  pl.BlockSpec((1, w), row),
                  pl.BlockSpec((1, w), row),
                  pl.BlockSpec((1, w), row),
                  pl.BlockSpec((w, w), row)],
        out_specs=pl.BlockSpec((1, tq, NUM_SLABS * w), lambda b, i: (b, i, 0)),
        compiler_params=_params("arbitrary", "arbitrary"),
        name="rwkv_prep",
    )(zrw, zrw, zrw, mu_prev, mu_next, wup_pad, aup_pad, g_up_bf, w0, a0, k_k, k_a, r_k, bd512)


def _block_diag(x, first):
    zero = jnp.zeros_like(x)
    return jnp.concatenate([jnp.where(first, x, zero), jnp.where(first, zero, x)], axis=0)


def _p1_kernel(r_ref, v_ref, kk_ref, lw_ref, kd_ref, be_ref, rp_ref, qp_ref, g_ref, h_ref, *, cpb):
    c = CHUNK
    sign = 1 - 2 * pl.program_id(2)
    rowi = lax.broadcasted_iota(jnp.int32, (c, LANES), 0)
    lane = lax.broadcasted_iota(jnp.int32, (c, LANES), 1)
    coli = lane & (c - 1)
    first = lane < HEAD_DIM
    before = sign * (rowi - coli)
    strict2 = before > 0
    incl2 = before >= 0
    eye2 = (coli == rowi).astype(F32)
    r64 = lax.broadcasted_iota(jnp.int32, (c, c), 0)
    c64 = lax.broadcasted_iota(jnp.int32, (c, c), 1)
    cum = (sign * (r64 - c64) >= 0).astype(BF16)
    r128 = lax.broadcasted_iota(jnp.int32, (LANES, LANES), 0)
    c128 = lax.broadcasted_iota(jnp.int32, (LANES, LANES), 1)
    same_head = (r128 // HEAD_DIM) == (c128 // HEAD_DIM)
    eye128 = (r128 == c128).astype(F32)
    zero128 = jnp.zeros((LANES, LANES), F32)

    js = range(cpb)
    sls = [pl.ds(j * c, c) for j in js]
    bdg = lambda x: _block_diag(x, first)
    r = [r_ref[0, sl, :] for sl in sls]
    v = [v_ref[0, sl, :] for sl in sls]
    kk = [kk_ref[0, sl, :] for sl in sls]
    lw = [lw_ref[0, sl, :] for sl in sls]
    kd = [kd_ref[0, sl, :] for sl in sls]
    be = [be_ref[0, sl, :] for sl in sls]

    one = lambda x, y, dims=NN: _mm(x, y, dims, 1)
    rows2 = lambda x, y: jnp.concatenate([x, y], axis=0)
    cols2 = lambda x, y: jnp.concatenate([x, y], axis=1)

    def cumsum(x):
        l1 = x.astype(BF16)
        rem = x - l1.astype(F32)
        l2 = rem.astype(BF16)
        l3 = (rem - l2.astype(F32)).astype(BF16)
        y = _dot(cum, jnp.concatenate([l1, l2, l3], axis=1))
        return y[:, :LANES] + (y[:, LANES:2 * LANES] + y[:, 2 * LANES:])

    cs = [cumsum(lw[j]) for j in js]
    g_inv = [jnp.exp(-cs[j]) for j in js]
    g_tot = [jnp.exp(jnp.sum(lw[j], axis=0, keepdims=True)) for j in js]
    ab = [-kk[j] * jnp.exp(cs[j] - lw[j]) for j in js]
    rb = [r[j] * jnp.exp(cs[j]) for j in js]
    bt = [be[j] * g_inv[j] for j in js]
    kt = [kd[j] * g_inv[j] for j in js]

    sc = [one(rows2(ab[j], rb[j]), rows2(bdg(bt[j]), bdg(kt[j])), NT) for j in js]
    m_ab = [jnp.where(strict2, sc[j][:c, :LANES], 0.0) for j in js]
    n_rb = [jnp.where(incl2, sc[j][c:, :LANES], 0.0) for j in js]
    m_ak = [jnp.where(strict2, sc[j][:c, LANES:], 0.0) for j in js]
    n_rk = [jnp.where(incl2, sc[j][c:, LANES:], 0.0) for j in js]

    levels = int(math.log2(c)) - 1
    t = [eye2 + m_ab[j] for j in js]
    p = [one(m_ab[j], bdg(m_ab[j])) for j in js]
    for _ in range(1, levels - 1):
        y = [one(rows2(p[j], t[j]), bdg(p[j])) for j in js]
        p = [y[j][:c] for j in js]
        t = [t[j] + y[j][c:] for j in js]
    t = [t[j] + one(t[j], bdg(p[j])) for j in js]

    def residual(a, tt):
        ah, al = _split(a)
        th, tl = _split(bdg(tt))
        y = _dot(rows2(ah, al), th)
        return eye2 - (y[:c] + (y[c:] + _dot(ah, tl)))

    res = [residual(eye2 - m_ab[j], t[j]) for j in js]
    t = [t[j] + one(t[j], bdg(res[j])) for j in js]

    vbd = [bdg(v[j]) for j in js]
    mv = [one(rows2(m_ak[j], n_rk[j]), vbd[j]) for j in js]
    tp = [one(t[j], cols2(bdg(ab[j]), bdg(mv[j][:c]))) for j in js]
    ap = [tp[j][:, :LANES] for j in js]
    pp = [tp[j][:, LANES:] for j in js]
    nr = [one(n_rb[j], cols2(bdg(ap[j]), bdg(pp[j]))) for j in js]
    for j in js:
        rp_ref[0, 0, 0, sls[j], :] = rb[j] + nr[j][:, :LANES]
        qp_ref[0, 0, 0, sls[j], :] = nr[j][:, LANES:] + mv[j][c:]
    for j in js:
        st = one(cols2(ap[j], pp[j]), bt[j], TN)
        gmat = eye128 + jnp.where(same_head, st[:LANES], zero128)
        hmat = jnp.where(same_head, st[LANES:] + one(v[j], kt[j], TN), zero128)
        g_ref[0, 0, 0, j] = gmat * g_tot[j]
        h_ref[0, 0, 0, j] = hmat * g_tot[j]


def _rwkv_p1(pk, cpb=16):
    bsz, s, _ = pk.shape
    nc = s // CHUNK
    npair = RWKV_WIDTH // LANES
    rows = cpb * CHUNK

    def slab(sidx):
        return pl.BlockSpec((1, rows, LANES), lambda b, hp, d, ci: (b, ci, sidx * npair + hp))

    def dslab(off):
        return pl.BlockSpec((1, rows, LANES),
                            lambda b, hp, d, ci: (b, ci, (SLAB_DIR0 + 3 * d + off) * npair + hp))

    seq = pl.BlockSpec((1, 1, 1, rows, LANES), lambda b, hp, d, ci: (b, hp, d, ci, 0))
    mat = pl.BlockSpec((1, 1, 1, cpb, LANES, LANES), lambda b, hp, d, ci: (b, hp, d, ci, 0, 0))
    return pl.pallas_call(
        functools.partial(_p1_kernel, cpb=cpb),
        out_shape=(jax.ShapeDtypeStruct((bsz, npair, 2, s, LANES), F32),
                   jax.ShapeDtypeStruct((bsz, npair, 2, s, LANES), F32),
                   jax.ShapeDtypeStruct((bsz, npair, 2, nc, LANES, LANES), F32),
                   jax.ShapeDtypeStruct((bsz, npair, 2, nc, LANES, LANES), F32)),
        grid=(bsz, npair, 2, nc // cpb),
        in_specs=[slab(SLAB_R), slab(SLAB_V), slab(SLAB_KK), dslab(0), dslab(1), dslab(2)],
        out_specs=(seq, seq, mat, mat),
        compiler_params=_params("arbitrary", "arbitrary", "arbitrary", "arbitrary"),
        name="rwkv_p1",
    )(pk, pk, pk, pk, pk, pk)


def _p2_kernel(rp_ref, qp_ref, g_ref, h_ref, gate_ref, bonus_ref, lng_ref, lnb_ref, bd_ref, o_ref,
               yf_ref, yb_ref, sf_ref, sb_ref, *, nc, te):
    c = CHUNK
    sf_ref[...] = jnp.zeros_like(sf_ref)
    sb_ref[...] = jnp.zeros_like(sb_ref)

    def step(j, _):
        jf = j
        jb = nc - 1 - j
        slf = pl.ds(pl.multiple_of(jf * c, c), c)
        slb = pl.ds(pl.multiple_of(jb * c, c), c)
        sf = sf_ref[...]
        sb = sb_ref[...]
        yf_ref[slf, :] = _mm(rp_ref[0, 0, 0, slf, :], sf, NT, WKV_PASSES_CHAIN) + qp_ref[0, 0, 0, slf, :]
        yb_ref[slb, :] = _mm(rp_ref[0, 0, 1, slb, :], sb, NT, WKV_PASSES_CHAIN) + qp_ref[0, 0, 1, slb, :]
        sf_ref[...] = _mm(sf, g_ref[0, 0, 0, jf], NN, WKV_PASSES_CHAIN) + h_ref[0, 0, 0, jf]
        sb_ref[...] = _mm(sb, g_ref[0, 0, 1, jb], NN, WKV_PASSES_CHAIN) + h_ref[0, 0, 1, jb]
        return 0

    lax.fori_loop(0, nc, step, 0)

    bd = bd_ref[...]
    lng = lng_ref[...]
    lnb = lnb_ref[...]
    inv = 1.0 / HEAD_DIM

    def epi(i, _):
        sl = pl.ds(pl.multiple_of(i * te, te), te)
        y = yf_ref[sl, :] + yb_ref[sl, :]
        mean = _mm_exact_rhs(y, bd) * inv
        yc = y - mean
        var = _mm_exact_rhs(yc * yc, bd) * inv
        yn = yc * lax.rsqrt(var + LNX_EPS) * lng + lnb
        o_ref[0, sl, :] = ((yn + bonus_ref[0, sl, :]) * gate_ref[0, sl, :]).astype(o_ref.dtype)
        return 0

    lax.fori_loop(0, (nc * c) // te, epi, 0)


def _rwkv_p2(rp, qp, gm, hm, pk, lnx_g, lnx_b, bd128, te=1024):
    bsz, npair, _, s, _ = rp.shape
    nc = s // CHUNK
    seq = pl.BlockSpec((1, 1, 2, s, LANES), lambda b, hp: (b, hp, 0, 0, 0))
    mat = pl.BlockSpec((1, 1, 2, nc, LANES, LANES), lambda b, hp: (b, hp, 0, 0, 0, 0))
    return pl.pallas_call(
        functools.partial(_p2_kernel, nc=nc, te=min(te, s)),
        out_shape=jax.ShapeDtypeStruct((bsz, s, RWKV_WIDTH), BF16),
        grid=(bsz, npair),
        in_specs=[seq, seq, mat, mat,
                  pl.BlockSpec((1, s, LANES), lambda b, hp: (b, 0, SLAB_G * npair + hp)),
                  pl.BlockSpec((1, s, LANES), lambda b, hp: (b, 0, SLAB_BONUS * npair + hp)),
                  pl.BlockSpec((1, LANES), lambda b, hp: (0, hp)),
                  pl.BlockSpec((1, LANES), lambda b, hp: (0, hp)),
                  pl.BlockSpec((LANES, LANES), lambda b, hp: (0, 0))],
        out_specs=pl.BlockSpec((1, s, LANES), lambda b, hp: (b, 0, hp)),
        scratch_shapes=[pltpu.VMEM((s, LANES), F32), pltpu.VMEM((s, LANES), F32),
                        pltpu.VMEM((LANES, LANES), F32), pltpu.VMEM((LANES, LANES), F32)],
        compiler_params=_params("arbitrary", "arbitrary"),
        name="rwkv_p2",
    )(rp, qp, gm, hm, pk, pk, lnx_g, lnx_b, bd128)


def _outproj_kernel(att_ref, rw_ref, x_ref, mod_ref, g2_ref, wo_ref, wq_ref, k1_ref, k2_ref,
                    x1_ref, h2_ref, s1_ref, s2_ref):
    mod = mod_ref[0]
    a = ATT_WIDTH
    mixed = _dot(att_ref[0], wo_ref[0:a, :]) + _dot(rw_ref[0], wo_ref[a:, :])
    x1 = x_ref[0] + mod[2:3] * mixed
    x1_ref[0] = x1
    ms = jnp.mean(x1 * x1, axis=-1, keepdims=True)
    h2 = x1 * lax.rsqrt(ms + NORM_EPS) * g2_ref[...]
    h2 = (h2 * (1.0 + mod[4:5]) + mod[3:4]).astype(BF16)
    h2_ref[0] = h2
    q = _dot(h2, wq_ref[...])
    k1 = k1_ref[...]
    k2 = k2_ref[...]
    for h in range(PEER_HEADS):
        base = h * 2 * LANES
        s1 = _dot(k1, q[:, base:base + LANES].astype(BF16), NT)
        s2 = _dot(k2, q[:, base + LANES:base + 2 * LANES].astype(BF16), NT)
        for g in range(s1.shape[1] // LANES):
            s1_ref[h, g] = s1[:, g * LANES:(g + 1) * LANES]
            s2_ref[h, g] = s2[:, g * LANES:(g + 1) * LANES]


def _outproj(att, rw, x, mod6, norm2_g, w_out_bf, wq_bf, k1_bf, k2_bf, tm=512):
    bsz, s, d = x.shape
    nt = s // tm
    t = bsz * s
    nq = wq_bf.shape[1]
    const = lambda b, i: (0, 0)
    gpt = tm // LANES
    tok = pl.BlockSpec((PEER_HEADS, gpt, PEER_N_KEYS, LANES), lambda b, i: (0, b * nt + i, 0, 0))
    stat = jax.ShapeDtypeStruct((PEER_HEADS, t // LANES, PEER_N_KEYS, LANES), F32)
    return pl.pallas_call(
        _outproj_kernel,
        out_shape=(jax.ShapeDtypeStruct((bsz, s, d), F32),
                   jax.ShapeDtypeStruct((bsz, s, d), BF16),
                   stat, stat),
        grid=(bsz, nt),
        in_specs=[pl.BlockSpec((1, tm, ATT_WIDTH), lambda b, i: (b, i, 0)),
                  pl.BlockSpec((1, tm, RWKV_WIDTH), lambda b, i: (b, i, 0)),
                  pl.BlockSpec((1, tm, d), lambda b, i: (b, i, 0)),
                  pl.BlockSpec((1, 6, d), lambda b, i: (b, 0, 0)),
                  pl.BlockSpec((1, d), const),
                  pl.BlockSpec((d, d), const),
                  pl.BlockSpec((d, nq), const),
                  pl.BlockSpec((PEER_N_KEYS, LANES), const),
                  pl.BlockSpec((PEER_N_KEYS, LANES), const)],
        out_specs=(pl.BlockSpec((1, tm, d), lambda b, i: (b, i, 0)),
                   pl.BlockSpec((1, tm, d), lambda b, i: (b, i, 0)),
                   tok, tok),
        compiler_params=_params("arbitrary", "arbitrary"),
        name="outproj",
    )(att, rw, x, mod6, norm2_g, w_out_bf, wq_bf, k1_bf, k2_bf)


def _oddeven_sort_pairs(n):
    pairs = []
    p = 1
    while p < n:
        k = p
        while k >= 1:
            for j in range(k % p, n - k, 2 * k):
                for i in range(min(k, n - j - k)):
                    if (i + j) // (2 * p) == (i + j + k) // (2 * p):
                        pairs.append((i + j, i + j + k))
            k //= 2
        p *= 2
    return pairs


def _bitonic_merge_pairs(n):
    pairs = []
    stride = n // 2
    while stride >= 1:
        pairs += [(i, i + stride) for i in range(n) if not i & stride]
        stride //= 2
    return pairs


def _compare_exchange(vals, pairs):
    vals = list(vals)
    for i, j in pairs:
        a, b = vals[i], vals[j]
        if b is None:
            continue
        if a is None:
            vals[i], vals[j] = b, None
        else:
            vals[i], vals[j] = jnp.maximum(a, b), jnp.minimum(a, b)
    return vals


def _top16_sorted(vregs):
    kk = PEER_TOPK
    vals = list(vregs) + [None] * (kk - len(vregs))
    vals = _compare_exchange(vals, _oddeven_sort_pairs(kk))
    for shift in (4, 2, 1):
        other = [None if v is None else pltpu.roll(v, shift, axis=0) for v in vals]
        merged = []
        for k in range(kk):
            a, b = vals[k], other[kk - 1 - k]
            merged.append(b if a is None else a if b is None else jnp.maximum(a, b))
        vals = _compare_exchange(merged, _bitonic_merge_pairs(kk))
    return vals


def _topk_kernel(s1_ref, s2_ref, c1_ref, e1_ref, r2_ref, e2_ref, *, groups):
    kk = PEER_TOPK
    sub = 8
    neg = -jnp.inf
    rows8 = lax.broadcasted_iota(jnp.int32, (sub, LANES), 0)

    def rows_of(reps):
        out = reps[0]
        for r in range(1, sub):
            out = jnp.where(rows8 == r, reps[r], out)
        return out

    def all_sublanes_sum(x):
        for shift in (4, 2, 1):
            x = x + pltpu.roll(x, shift, axis=0)
        return x

    def per_tile(it_idx, _):
        g = it_idx // PEER_HEADS
        h = it_idx % PEER_HEADS
        s1 = [s1_ref[h, g, pl.ds(v * sub, sub), :] for v in range(PEER_N_KEYS // sub)]
        s2 = [s2_ref[h, g, pl.ds(v * sub, sub), :] for v in range(PEER_N_KEYS // sub)]
        a = _top16_sorted(s1)
        b = _top16_sorted(s2)
        a_lo, a_hi = rows_of(a[:sub]), rows_of(a[sub:])
        b_lo, b_hi = rows_of(b[:sub]), rows_of(b[sub:])

        cand = [a[0] + b_lo, a[0] + b_hi, a[1] + b_lo]
        for i in range(2, sub):
            cand.append(jnp.where(rows8 < kk // (i + 1), a[i] + b_lo, neg))
        cand.append(a_hi + b[0])
        top = _top16_sorted(cand)
        tau = top[kk - 1]
        z = None
        for t in top:
            ez = jnp.exp(t - top[0])
            z = ez if z is None else z + ez
        inv_z = 1.0 / z

        counts = []
        for r in range(kk):
            hits = (jnp.where(a[r] + b_lo >= tau, 1.0, 0.0) + jnp.where(a[r] + b_hi >= tau, 1.0, 0.0))
            counts.append(all_sublanes_sum(hits))
        for v in range(PEER_N_KEYS // sub):
            rows = pl.ds(v * sub, sub)
            cnt = jnp.zeros((sub, LANES), F32)
            rank = jnp.full((sub, LANES), float(kk), F32)
            for r in range(kk):
                cnt = jnp.where(s1[v] == a[r], counts[r], cnt)
                rank = jnp.where(s2[v] == b[r], float(r), rank)
            c1_ref[h, g, rows, :] = cnt
            e1_ref[h, g, rows, :] = jnp.exp(s1[v] - a[0]) * inv_z
            s2[v] = (rank, jnp.exp(s2[v] - b[0]))
        rank2 = jnp.concatenate([rv for rv, _ in s2], axis=0).astype(BF16)
        e2 = jnp.concatenate([ev for _, ev in s2], axis=0).astype(BF16)
        r2_ref[h, g] = pltpu.bitcast(rank2, jnp.uint32)
        e2_ref[h, g] = pltpu.bitcast(e2, jnp.uint32)
        return 0

    lax.fori_loop(0, groups * PEER_HEADS, per_tile, 0)


def _peer_topk(s1t, s2t, tn=512):
    nh, ng, nk, _ = s1t.shape
    groups = tn // LANES
    blk = pl.BlockSpec((nh, groups, nk, LANES), lambda i: (0, i, 0, 0))
    f32 = jax.ShapeDtypeStruct(s1t.shape, F32)
    b16 = jax.ShapeDtypeStruct((nh, ng, nk // 2, LANES), jnp.uint32)
    pblk = pl.BlockSpec((nh, groups, nk // 2, LANES), lambda i: (0, i, 0, 0))
    return pl.pallas_call(
        functools.partial(_topk_kernel, groups=groups),
        out_shape=(f32, f32, b16, b16),
        grid=(ng // groups,),
        in_specs=[blk, blk],
        out_specs=(blk, blk, pblk, pblk),
        compiler_params=_params("arbitrary"),
        name="peer_topk",
    )(s1t, s2t)


def _peer_kernel(h2_ref, u_ref, vt_ref, c1_ref, e1_ref, r2_ref, e2_ref, x1_ref, mod_ref, o_ref,
                 acc_ref, act0_ref, act1_ref, p0_ref, p1_ref, *, tn, te, n_tiles):
    k = pl.program_id(0)
    n_items = pl.num_programs(0) - 2
    nk = PEER_N_KEYS
    ngroups = tn // LANES

    @pl.when(k == 0)
    def _():
        acc_ref[...] = jnp.zeros_like(acc_ref)
        for ref in (act0_ref, act1_ref, p0_ref, p1_ref):
            ref[...] = jnp.zeros_like(ref)

    ni = te // nk
    i0 = pl.multiple_of((jnp.clip(k - 1, 0, n_items - 1) % n_tiles) * ni, ni)
    tile_c = jnp.clip(k - 2, 0, n_items - 1) % n_tiles
    first_c = tile_c == 0

    d_model = acc_ref.shape[0]
    halves = [(mh, nh, kh) for mh in range(2) for nh in range(ngroups // 2) for kh in range(2)]

    def stage_c(p_a, mh, nh, kh):
        rows = slice(mh * (d_model // 2), (mh + 1) * (d_model // 2))
        cols = slice(nh * 2 * LANES, (nh + 1) * 2 * LANES)
        ks = slice(kh * (te // 2), (kh + 1) * (te // 2))
        p_prev = jnp.concatenate([p_a[2 * nh, ks, :], p_a[2 * nh + 1, ks, :]], axis=1)
        prev = acc_ref[rows, cols]
        if kh == 0:
            prev = jnp.where(first_c, jnp.zeros_like(prev), prev)
        acc_ref[rows, cols] = prev + _dot(vt_ref[rows, ks], p_prev)

    def stage_a(act_a, mh, nh, kh):
        rows = slice(mh * (te // 2), (mh + 1) * (te // 2))
        ks = slice(kh * (d_model // 2), (kh + 1) * (d_model // 2))
        act = _dot(u_ref[rows, ks], h2_ref[nh * 2 * LANES:(nh + 1) * 2 * LANES, ks], NT)
        if kh == 0:
            act_a[2 * nh, rows, :] = act[:, :LANES]
            act_a[2 * nh + 1, rows, :] = act[:, LANES:]
        else:
            act_a[2 * nh, rows, :] += act[:, :LANES]
            act_a[2 * nh + 1, rows, :] += act[:, LANES:]

    def rows_bf16(row):
        tile = jnp.broadcast_to(row, (BF16_ROWS, LANES)).astype(BF16)
        return jnp.concatenate([tile] * (nk // BF16_ROWS), axis=0)

    quad = 4

    def stage_b(act_b, p_b, tg, iq, pieces):
        gates = [jnp.zeros((nk, LANES), BF16) for _ in range(quad)]
        for h in range(PEER_HEADS):
            if h % (PEER_HEADS // 2) == 0 and pieces:
                pieces.pop(0)()
            c1 = c1_ref[h, tg, pl.ds(i0, ni), :]
            e1 = e1_ref[h, tg, pl.ds(i0, ni), :]
            rank2 = pltpu.bitcast(r2_ref[h, tg], BF16)
            e2 = pltpu.bitcast(e2_ref[h, tg], BF16)
            for q in range(quad):
                il = iq * quad + q
                sel = rank2 < rows_bf16(c1[il:il + 1, :])
                gates[q] = gates[q] + jnp.where(sel, e2 * rows_bf16(e1[il:il + 1, :]), jnp.zeros_like(e2))
        for q in range(quad):
            il = iq * quad + q
            a = act_b[tg, il * nk:(il + 1) * nk, :]
            gelu = 0.5 * a * (1.0 + lax.erf(a * (2.0 ** -0.5)))
            p_b[tg, il * nk:(il + 1) * nk, :] = gates[q] * gelu.astype(BF16)

    def stages(act_a, act_b, p_a, p_b):
        pieces = ([functools.partial(stage_c, p_a, *hv) for hv in halves]
                  + [functools.partial(stage_a, act_a, *hv) for hv in halves])
        for tg in range(ngroups):
            for iq in range(ni // quad):
                stage_b(act_b, p_b, tg, iq, pieces)
        while pieces:
            pieces.pop(0)()

    @pl.when(k % 2 == 0)
    def _():
        stages(act0_ref, act1_ref, p0_ref, p1_ref)

    @pl.when(k % 2 == 1)
    def _():
        stages(act1_ref, act0_ref, p1_ref, p0_ref)

    @pl.when((tile_c == n_tiles - 1) & (k >= 2))
    def _():
        o_ref[...] = x1_ref[...] + mod_ref[0, 5:6, :] * acc_ref[...].T


def _peer_ffn(h2, u_bf, vt_bf, c1, e1, r2, e2, x1, mod6, seq, tn=512, te=1024):
    t, d = h2.shape
    ne = u_bf.shape[0]
    ngroups = tn // LANES
    n_tiles = ne // te
    n_items = (t // tn) * n_tiles

    def item(k, lag):
        w = jnp.clip(k - lag, 0, n_items - 1)
        return w // n_tiles, w % n_tiles

    stat = pl.BlockSpec((PEER_HEADS, ngroups, PEER_N_KEYS, LANES), lambda k: (0, item(k, 1)[0], 0, 0))
    pstat = pl.BlockSpec((PEER_HEADS, ngroups, PEER_N_KEYS // 2, LANES), lambda k: (0, item(k, 1)[0], 0, 0))
    return pl.pallas_call(
        functools.partial(_peer_kernel, tn=tn, te=te, n_tiles=n_tiles),
        out_shape=jax.ShapeDtypeStruct((t, d), F32),
        grid=(n_items + 2,),
        in_specs=[pl.BlockSpec((tn, d), lambda k: (item(k, 0)[0], 0)),
                  pl.BlockSpec((te, d), lambda k: (item(k, 0)[1], 0)),
                  pl.BlockSpec((d, te), lambda k: (0, item(k, 2)[1])),
                  stat, stat, pstat, pstat,
                  pl.BlockSpec((tn, d), lambda k: (item(k, 2)[0], 0)),
                  pl.BlockSpec((1, 6, d), lambda k: ((item(k, 2)[0] * tn) // seq, 0, 0))],
        out_specs=pl.BlockSpec((tn, d), lambda k: (item(k, 2)[0], 0)),
        scratch_shapes=[pltpu.VMEM((d, tn), F32),
                        pltpu.VMEM((ngroups, te, LANES), F32), pltpu.VMEM((ngroups, te, LANES), F32),
                        pltpu.VMEM((ngroups, te, LANES), BF16), pltpu.VMEM((ngroups, te, LANES), BF16)],
        compiler_params=_params("arbitrary"),
        name="peer_ffn",
    )(h2, u_bf, vt_bf, c1, e1, r2, e2, x1, mod6)


def _head_block_diag(n):
    idx = jnp.arange(n, dtype=jnp.int32) // HEAD_DIM
    return (idx[:, None] == idx[None, :]).astype(BF16)


def _layer(x, c, ada_w, ada_b, norm1_g, w_in, mu_prev, mu_next, q_norm_g, k_norm_g, w_decay0, w_decay_up,
           a_gate0, a_gate_up, g_up, k_k, k_a, r_k, lnx_g, lnx_b, w_out, norm2_g, peer_w_query,
           peer_sub_keys1, peer_sub_keys2, peer_u, peer_v):
    bsz, s, d = x.shape
    n_att_heads = ATT_WIDTH // HEAD_DIM
    bd512 = _head_block_diag(ATT_WIDTH)
    bd128 = _head_block_diag(LANES)

    mod6 = _ada(c, ada_w, ada_b).reshape(bsz, 6, d)
    gq = jnp.tile(q_norm_g, n_att_heads).reshape(1, ATT_WIDTH)
    gk = jnp.tile(k_norm_g, n_att_heads).reshape(1, ATT_WIDTH)
    qkv, zrw = _inproj(x, mod6, norm1_g.reshape(1, d), w_in.astype(BF16), gq, gk, bd512)

    att = _attention(qkv, _attn_bias_table(n_att_heads, s // ATT_BLOCK))

    zero_w = jnp.zeros((2, LANES - DECAY_LORA, RWKV_WIDTH), F32)
    wup_pad = jnp.concatenate([w_decay_up, zero_w], axis=1).astype(BF16)
    aup_pad = jnp.concatenate([zero_w, a_gate_up], axis=1).astype(BF16)
    pk = _rwkv_prep(zrw, mu_prev.reshape(1, -1), mu_next.reshape(1, -1), wup_pad, aup_pad, g_up.astype(BF16),
                    w_decay0, a_gate0, k_k.reshape(1, -1), k_a.reshape(1, -1), r_k.reshape(1, -1), bd512)
    rp, qp, gm, hm = _rwkv_p1(pk)
    rw = _rwkv_p2(rp, qp, gm, hm, pk, lnx_g.reshape(1, -1), lnx_b.reshape(1, -1), bd128)

    x1, h2, s1t, s2t = _outproj(att, rw, x, mod6, norm2_g.reshape(1, d), w_out.astype(BF16),
                                peer_w_query.astype(BF16), peer_sub_keys1.astype(BF16),
                                peer_sub_keys2.astype(BF16))
    c1, e1, r2, e2 = _peer_topk(s1t, s2t)
    out = _peer_ffn(h2.reshape(bsz * s, d), peer_u.astype(BF16), peer_v.T.astype(BF16), c1, e1, r2, e2,
                    x1.reshape(bsz * s, d), mod6, s)
    return out.reshape(bsz, s, d)


def kernel(x, c, ada_w, ada_b, norm1_g, w_in, mu_prev, mu_next, q_norm_g, k_norm_g, w_decay0, w_decay_up,
           a_gate0, a_gate_up, g_up, k_k, k_a, r_k, lnx_g, lnx_b, w_out, norm2_g, peer_w_query,
           peer_sub_keys1, peer_sub_keys2, peer_u, peer_v):
    depth = ada_w.shape[0]
    for l in range(depth):
        x = _layer(x, c, ada_w[l], ada_b[l], norm1_g[l], w_in[l], mu_prev[l], mu_next[l], q_norm_g[l],
                   k_norm_g[l], w_decay0[l], w_decay_up[l], a_gate0[l], a_gate_up[l], g_up[l], k_k[l], k_a[l],
                   r_k[l], lnx_g[l], lnx_b[l], w_out[l], norm2_g[l], peer_w_query[l], peer_sub_keys1[l],
                   peer_sub_keys2[l], peer_u[l], peer_v[l])
    return x
```

```python
import functools
import math

import jax
import jax.numpy as jnp
from jax import lax
from jax.experimental import pallas as pl
from jax.experimental.pallas import tpu as pltpu

F32 = jnp.float32
BF16 = jnp.bfloat16

HEAD_DIM = 64
ATT_WIDTH = 512
RWKV_WIDTH = 512
DECAY_LORA = 64
AAA_LORA = 64
GATE_LORA = 128
SHIFT_WIDTH = 3 * RWKV_WIDTH + DECAY_LORA + AAA_LORA + GATE_LORA
PEER_N_KEYS = 128
PEER_HEADS = 8
PEER_TOPK = 16
NORM_EPS = 1e-6
LNX_EPS = 64e-5
MASK_VALUE = -1e30
LOG2E = 1.4426950408889634
ATT_HALF_WINDOWS = ((1, 64), (4, 256), (16, 1024))

LANES = 128
BF16_ROWS = 16
CHUNK = 64
ATT_BLOCK = 128
ATT_REACH = 1024 // ATT_BLOCK
VMEM_LIMIT_BYTES = 56 * 1024 * 1024
WKV_PASSES_CHAIN = 1

NN = (((1,), (0,)), ((), ()))
NT = (((1,), (1,)), ((), ()))
TN = (((0,), (0,)), ((), ()))

SLAB_R, SLAB_V, SLAB_KK, SLAB_DIR0, SLAB_G, SLAB_BONUS, NUM_SLABS = 0, 1, 2, 3, 9, 10, 11


def _params(*sem):
    return pltpu.CompilerParams(dimension_semantics=sem, vmem_limit_bytes=VMEM_LIMIT_BYTES)


def _dot(a, b, dims=NN):
    return lax.dot_general(a, b, dims, preferred_element_type=F32)


def _split(a):
    hi = a.astype(BF16)
    lo = (a - hi.astype(F32)).astype(BF16)
    return hi, lo


def _mm(a, b, dims=NN, passes=3):
    if passes == 1:
        return _dot(a.astype(BF16), b.astype(BF16), dims)
    ah, al = _split(a)
    bh, bl = _split(b)
    return _dot(ah, bh, dims) + (_dot(ah, bl, dims) + _dot(al, bh, dims))


def _mm_exact_rhs(a, b_exact):
    ah, al = _split(a)
    return _dot(ah, b_exact) + _dot(al, b_exact)


def _sigmoid(x):
    return 1.0 / (1.0 + jnp.exp(-x))


def _ada_kernel(c_ref, w_ref, b_ref, o_ref):
    c = c_ref[...]
    o_ref[...] = _mm(c * _sigmoid(c), w_ref[...]) + b_ref[...]


def _ada(c, ada_w, ada_b):
    bsz, d = c.shape
    n = ada_w.shape[1]
    tn = 1024
    return pl.pallas_call(
        _ada_kernel,
        out_shape=jax.ShapeDtypeStruct((bsz, n), F32),
        grid=(n // tn,),
        in_specs=[pl.BlockSpec((bsz, d), lambda j: (0, 0)),
                  pl.BlockSpec((d, tn), lambda j: (0, j)),
                  pl.BlockSpec((1, tn), lambda j: (0, j))],
        out_specs=pl.BlockSpec((bsz, tn), lambda j: (0, j)),
        compiler_params=_params("arbitrary"),
        name="ada",
    )(c, ada_w, ada_b.reshape(1, n))


def _inproj_kernel(x_ref, mod_ref, g1_ref, w_ref, gq_ref, gk_ref, bd_ref, qkv_ref, zrw_ref):
    x = x_ref[0]
    mod = mod_ref[0]
    ms = jnp.mean(x * x, axis=-1, keepdims=True)
    h = x * lax.rsqrt(ms + NORM_EPS) * g1_ref[...]
    h = (h * (1.0 + mod[1:2]) + mod[0:1]).astype(BF16)
    bd = bd_ref[...]

    def head_norm(z, g):
        ss = _mm_exact_rhs(z * z, bd) * (1.0 / HEAD_DIM)
        return z * lax.rsqrt(ss + NORM_EPS) * g

    a = ATT_WIDTH
    zq = _dot(h, w_ref[:, 0:a])
    qkv_ref[0, :, 0:a] = (head_norm(zq, gq_ref[...]) * (HEAD_DIM ** -0.5 * LOG2E)).astype(BF16)
    zk = _dot(h, w_ref[:, a:2 * a])
    qkv_ref[0, :, a:2 * a] = head_norm(zk, gk_ref[...]).astype(BF16)
    qkv_ref[0, :, 2 * a:3 * a] = _dot(h, w_ref[:, 2 * a:3 * a]).astype(BF16)
    zrw_ref[0] = _dot(h, w_ref[:, 3 * a:])


def _inproj(x, mod6, norm1_g, w_in_bf, gq, gk, bd512, tm=512):
    bsz, s, d = x.shape
    nin = w_in_bf.shape[1]
    return pl.pallas_call(
        _inproj_kernel,
        out_shape=(jax.ShapeDtypeStruct((bsz, s, 3 * ATT_WIDTH), BF16),
                   jax.ShapeDtypeStruct((bsz, s, SHIFT_WIDTH), F32)),
        grid=(bsz, s // tm),
        in_specs=[pl.BlockSpec((1, tm, d), lambda b, i: (b, i, 0)),
                  pl.BlockSpec((1, 6, d), lambda b, i: (b, 0, 0)),
                  pl.BlockSpec((1, d), lambda b, i: (0, 0)),
                  pl.BlockSpec((d, nin), lambda b, i: (0, 0)),
                  pl.BlockSpec((1, ATT_WIDTH), lambda b, i: (0, 0)),
                  pl.BlockSpec((1, ATT_WIDTH), lambda b, i: (0, 0)),
                  pl.BlockSpec((ATT_WIDTH, ATT_WIDTH), lambda b, i: (0, 0))],
        out_specs=(pl.BlockSpec((1, tm, 3 * ATT_WIDTH), lambda b, i: (b, i, 0)),
                   pl.BlockSpec((1, tm, SHIFT_WIDTH), lambda b, i: (b, i, 0))),
        compiler_params=_params("arbitrary", "arbitrary"),
        name="inproj",
    )(x, mod6, norm1_g, w_in_bf, gq, gk, bd512)


ATT_FAR_DILATION, ATT_FAR_HALF = ATT_HALF_WINDOWS[-1]
ATT_NEAR_BLOCKS = 6


def _attn_near_window(nkb):
    return min(nkb, ATT_NEAR_BLOCKS)


def _attn_near_start(qb, nkb):
    return jnp.clip(((qb - 2) // 2) * 2, 0, nkb - _attn_near_window(nkb))


def _attn_far_bias(n_heads, nkb):
    per_block = ATT_BLOCK // ATT_FAR_DILATION
    i = jnp.arange(ATT_BLOCK, dtype=jnp.int32)[:, None]
    ip = jnp.arange(ATT_BLOCK, dtype=jnp.int32)[None, :]
    start = _attn_near_start(i // per_block, nkb)
    kb = ip // per_block
    dist = ATT_FAR_DILATION * jnp.abs(i - ip)
    far = (dist <= ATT_FAR_HALF) & ~((kb >= start) & (kb < start + _attn_near_window(nkb)))
    slopes = jnp.exp2(-8.0 * (jnp.arange(n_heads, dtype=F32) + 1.0) / n_heads) * LOG2E
    return jnp.where(far[None], -slopes[:, None, None] * dist.astype(F32)[None], MASK_VALUE)


def _attn_bias_table(n_heads, nkb):
    reach = _attn_near_window(nkb) - 1
    nd = 2 * reach + 1
    dd = jnp.arange(nd, dtype=jnp.int32)[None, :, None]
    r = jnp.arange(ATT_BLOCK, dtype=jnp.int32)[:, None, None]
    c = jnp.arange(ATT_BLOCK, dtype=jnp.int32)[None, None, :]
    dt = r - c - (dd - reach) * ATT_BLOCK
    adt = jnp.abs(dt)
    mult = jnp.zeros(dt.shape, F32)
    for dil, half in ATT_HALF_WINDOWS:
        mult = mult + ((adt <= half) & (dt % dil == 0)).astype(F32)
    logm = jnp.where(mult > 0, jnp.log2(jnp.maximum(mult, 1.0)), MASK_VALUE)
    slopes = jnp.exp2(-8.0 * (jnp.arange(n_heads, dtype=F32) + 1.0) / n_heads) * LOG2E
    bias = logm[None] - slopes[:, None, None, None] * adt.astype(F32)[None]
    return bias.reshape(n_heads, ATT_BLOCK, nd * ATT_BLOCK)


def _attn_kernel(q_ref, k_ref, v_ref, bias_ref, far_ref, o_ref, s_ref, v1_ref, g_ref, accf_ref, mf_ref,
                 *, nkb, win):
    blk = ATT_BLOCK
    reach = win - 1
    has_far = win < nkb
    lane = lax.broadcasted_iota(jnp.int32, (blk, LANES), 1)
    first = lane < HEAD_DIM

    def two_heads(q):
        zero = jnp.zeros_like(q)
        return jnp.concatenate([jnp.where(first, q, zero), jnp.where(first, zero, q)], axis=0)

    def with_ones(v, hh):
        is_own = (lax.broadcasted_iota(jnp.int32, v.shape, 1) < HEAD_DIM) == (hh == 0)
        return jnp.where(is_own, v, jnp.ones_like(v))

    v = v_ref[0]
    for hh in range(2):
        v1_ref[hh] = with_ones(v, hh)

    if has_far:
        g_ref[0] = q_ref[0].astype(F32)
        g_ref[1] = k_ref[0].astype(F32)
        g_ref[2] = v.astype(F32)
        n_far = nkb * blk // ATT_FAR_DILATION

        def residue(r, _):
            rows = pl.ds(r, n_far, stride=ATT_FAR_DILATION)
            qr = g_ref[0, rows, :].astype(BF16)
            kr = g_ref[1, rows, :].astype(BF16)
            vr = g_ref[2, rows, :].astype(BF16)
            s = _dot(two_heads(qr), kr, NT)
            for hh in range(2):
                sb = s[hh * blk:(hh + 1) * blk] + far_ref[hh]
                mx = jnp.max(sb, axis=-1, keepdims=True)
                p = jnp.exp2(sb - mx).astype(BF16)
                accf_ref[hh, rows, :] = _dot(p, with_ones(vr, hh))
                mf_ref[hh, rows, :] = jnp.broadcast_to(mx, (blk, LANES))
            return 0

        lax.fori_loop(0, ATT_FAR_DILATION, residue, 0, unroll=4)

    group = s_ref.shape[0]
    slabs = range(0, win, 2)

    def query_block_group(i, _):
        us = range(group)
        qis = [group * i + u for u in us]
        rows = [pl.ds(pl.multiple_of(qi * blk, blk), blk) for qi in qis]
        qab = [two_heads(q_ref[0, rows[u], :]) for u in us]
        ks = [_attn_near_start(qi, nkb) if has_far else 0 for qi in qis]

        m = [[jnp.full((blk, LANES), -jnp.inf, F32) for _ in range(2)] for _ in us]
        for w0 in slabs:
            for u in us:
                kj0 = ks[u] + w0
                kslab = k_ref[0, pl.ds(pl.multiple_of(kj0 * blk, blk), 2 * blk), :]
                s = _dot(qab[u], kslab, NT)
                x0 = pl.multiple_of((kj0 - qis[u] + reach) * blk, blk)
                for hh in range(2):
                    sb = s[hh * blk:(hh + 1) * blk] + bias_ref[hh, :, pl.ds(x0, 2 * blk)]
                    s_ref[u, hh, :, w0 * blk:(w0 + 2) * blk] = sb
                    m[u][hh] = jnp.maximum(m[u][hh], jnp.maximum(sb[:, :blk], sb[:, blk:]))
        mrow = [[jnp.max(m[u][hh], axis=-1, keepdims=True) for hh in range(2)] for u in us]

        if has_far:
            acc = [[None, None] for _ in us]
            for u in us:
                for hh in range(2):
                    m_far = mf_ref[hh, rows[u], :][:, 0:1]
                    m_all = jnp.maximum(mrow[u][hh], m_far)
                    acc[u][hh] = accf_ref[hh, rows[u], :] * jnp.exp2(m_far - m_all)
                    mrow[u][hh] = m_all
        else:
            acc = [[jnp.zeros((blk, LANES), F32) for _ in range(2)] for _ in us]
        for w0 in slabs:
            for u in us:
                keys = pl.ds(pl.multiple_of((ks[u] + w0) * blk, blk), 2 * blk)
                for hh in range(2):
                    p = jnp.exp2(s_ref[u, hh, :, w0 * blk:(w0 + 2) * blk] - mrow[u][hh]).astype(BF16)
                    acc[u][hh] = acc[u][hh] + _dot(p, v1_ref[hh, keys, :])
        for u in us:
            num = jnp.where(first, acc[u][0], acc[u][1])
            den = jnp.where(first, pltpu.roll(acc[u][0], HEAD_DIM, axis=1), pltpu.roll(acc[u][1], HEAD_DIM, axis=1))
            o_ref[0, rows[u], :] = (num / den).astype(o_ref.dtype)
        return 0

    lax.fori_loop(0, nkb // group, query_block_group, 0)


def _attention(qkv, n_heads):
    bsz, s, _ = qkv.shape
    nkb = s // ATT_BLOCK
    win = _attn_near_window(nkb)
    assert nkb % 2 == 0 and (win == nkb or s // ATT_FAR_DILATION == ATT_BLOCK)
    bias = _attn_bias_table(n_heads, nkb)
    far = _attn_far_bias(n_heads, nkb)
    npair = ATT_WIDTH // LANES
    return pl.pallas_call(
        functools.partial(_attn_kernel, nkb=nkb, win=win),
        out_shape=jax.ShapeDtypeStruct((bsz, s, ATT_WIDTH), BF16),
        grid=(npair, bsz),
        in_specs=[pl.BlockSpec((1, s, LANES), lambda hp, b: (b, 0, hp)),
                  pl.BlockSpec((1, s, LANES), lambda hp, b: (b, 0, npair + hp)),
                  pl.BlockSpec((1, s, LANES), lambda hp, b: (b, 0, 2 * npair + hp)),
                  pl.BlockSpec((2, ATT_BLOCK, bias.shape[2]), lambda hp, b: (hp, 0, 0)),
                  pl.BlockSpec((2, ATT_BLOCK, ATT_BLOCK), lambda hp, b: (hp, 0, 0))],
        out_specs=pl.BlockSpec((1, s, LANES), lambda hp, b: (b, 0, hp)),
        scratch_shapes=[pltpu.VMEM((math.gcd(nkb, 4), 2, ATT_BLOCK, win * ATT_BLOCK), F32), pltpu.VMEM((2, s, LANES), BF16),
                        pltpu.VMEM((3, s, LANES), F32), pltpu.VMEM((2, s, LANES), F32),
                        pltpu.VMEM((2, s, LANES), F32)],
        compiler_params=_params("arbitrary", "arbitrary"),
        name="attn",
    )(qkv, qkv, qkv, bias, far)


def _prep_kernel(z_ref, zp_ref, zn_ref, mup_ref, mun_ref, wup_ref, aup_ref, gup_ref, w0_ref, a0_ref,
                 kk_ref, ka_ref, rk_ref, bd_ref, o_ref, *, tq):
    i = pl.program_id(1)
    last = pl.num_programs(1) - 1
    z = z_ref[0]
    row = lax.broadcasted_iota(jnp.int32, (tq, 1), 0)
    prev_row = zp_ref[0, 7:8, :] * (i > 0).astype(F32)
    next_row = zn_ref[0, 0:1, :] * (i < last).astype(F32)
    zp = jnp.where(row == 0, prev_row, pltpu.roll(z, 1, axis=0))
    zn = jnp.where(row == tq - 1, next_row, pltpu.roll(z, tq - 1, axis=0))
    zs = z + mup_ref[...] * (zp - z) + mun_ref[...] * (zn - z)

    w = RWKV_WIDTH
    r = zs[:, 0:w]
    k = zs[:, w:2 * w]
    v = zs[:, 2 * w:3 * w]
    xwa = zs[:, 3 * w:3 * w + LANES]
    xg = zs[:, 3 * w + LANES:]
    bd = bd_ref[...]

    g = _dot(_sigmoid(xg).astype(BF16), gup_ref[...])
    kk = k * kk_ref[...]
    ss = _mm_exact_rhs(kk * kk, bd)
    kk = kk * lax.rsqrt(jnp.maximum(ss, 1e-12))
    ka = ka_ref[...]
    txw = jnp.tanh(xwa).astype(BF16)
    xab = xwa.astype(BF16)

    o_ref[0, :, SLAB_R * w:(SLAB_R + 1) * w] = r
    o_ref[0, :, SLAB_V * w:(SLAB_V + 1) * w] = v
    o_ref[0, :, SLAB_KK * w:(SLAB_KK + 1) * w] = kk
    o_ref[0, :, SLAB_G * w:(SLAB_G + 1) * w] = g
    a_sum = jnp.zeros_like(k)
    for d in range(2):
        y = w0_ref[d:d + 1, :] + _dot(txw, wup_ref[d])
        wlog = -(jnp.maximum(-y, 0.0) + jnp.log(1.0 + jnp.exp(-jnp.abs(y)))) - 0.5
        a = _sigmoid(a0_ref[d:d + 1, :] + _dot(xab, aup_ref[d]))
        a_sum = a_sum + a
        base = (SLAB_DIR0 + 3 * d) * w
        o_ref[0, :, base:base + w] = -jnp.exp(wlog)
        o_ref[0, :, base + w:base + 2 * w] = k * (1.0 + (a - 1.0) * ka)
        o_ref[0, :, base + 2 * w:base + 3 * w] = kk * a
    k_bonus = k * (1.0 + (0.5 * a_sum - 1.0) * ka)
    bsum = _mm_exact_rhs(r * k_bonus * rk_ref[...], bd)
    o_ref[0, :, SLAB_BONUS * w:(SLAB_BONUS + 1) * w] = bsum * v


def _rwkv_prep(zrw, mu_prev, mu_next, wup_pad, aup_pad, g_up_bf, w0, a0, k_k, k_a, r_k, bd512, tq=256):
    bsz, s, sw = zrw.shape
    w = RWKV_WIDTH
    nt = s // tq
    row = lambda b, i: (0, 0)
    return pl.pallas_call(
        functools.partial(_prep_kernel, tq=tq),
        out_shape=jax.ShapeDtypeStruct((bsz, s, NUM_SLABS * w), F32),
        grid=(bsz, nt),
        in_specs=[pl.BlockSpec((1, tq, sw), lambda b, i: (b, i, 0)),
                  pl.BlockSpec((1, 8, sw), lambda b, i: (b, jnp.maximum(i * (tq // 8) - 1, 0), 0)),
                  pl.BlockSpec((1, 8, sw), lambda b, i: (b, jnp.minimum((i + 1) * (tq // 8), s // 8 - 1), 0)),
                  pl.BlockSpec((1, sw), row),
                  pl.BlockSpec((1, sw), row),
                  pl.BlockSpec((2, LANES, w), lambda b, i: (0, 0, 0)),
                  pl.BlockSpec((2, LANES, w), lambda b, i: (0, 0, 0)),
                  pl.BlockSpec((GATE_LORA, w), row),
                  pl.BlockSpec((2, w), row),
                  pl.BlockSpec((2, w), row),
                  pl.BlockSpec((1, w), row),
                  pl.BlockSpec((1, w), row),
                  pl.BlockSpec((1, w), row),
                  pl.BlockSpec((w, w), row)],
        out_specs=pl.BlockSpec((1, tq, NUM_SLABS * w), lambda b, i: (b, i, 0)),
        compiler_params=_params("arbitrary", "arbitrary"),
        name="rwkv_prep",
    )(zrw, zrw, zrw, mu_prev, mu_next, wup_pad, aup_pad, g_up_bf, w0, a0, k_k, k_a, r_k, bd512)


def _block_diag(x, first):
    zero = jnp.zeros_like(x)
    return jnp.concatenate([jnp.where(first, x, zero), jnp.where(first, zero, x)], axis=0)


def _p1_kernel(r_ref, v_ref, kk_ref, lw_ref, kd_ref, be_ref, rp_ref, qp_ref, g_ref, h_ref, *, cpb):
    c = CHUNK
    sign = 1 - 2 * pl.program_id(2)
    rowi = lax.broadcasted_iota(jnp.int32, (c, LANES), 0)
    lane = lax.broadcasted_iota(jnp.int32, (c, LANES), 1)
    coli = lane & (c - 1)
    first = lane < HEAD_DIM
    before = sign * (rowi - coli)
    strict2 = before > 0
    incl2 = before >= 0
    eye2 = (coli == rowi).astype(F32)
    r64 = lax.broadcasted_iota(jnp.int32, (c, c), 0)
    c64 = lax.broadcasted_iota(jnp.int32, (c, c), 1)
    cum = (sign * (r64 - c64) >= 0).astype(BF16)
    r128 = lax.broadcasted_iota(jnp.int32, (LANES, LANES), 0)
    c128 = lax.broadcasted_iota(jnp.int32, (LANES, LANES), 1)
    same_head = (r128 // HEAD_DIM) == (c128 // HEAD_DIM)
    eye128 = (r128 == c128).astype(F32)
    zero128 = jnp.zeros((LANES, LANES), F32)

    js = range(cpb)
    sls = [pl.ds(j * c, c) for j in js]
    bdg = lambda x: _block_diag(x, first)
    r = [r_ref[0, sl, :] for sl in sls]
    v = [v_ref[0, sl, :] for sl in sls]
    kk = [kk_ref[0, sl, :] for sl in sls]
    lw = [lw_ref[0, sl, :] for sl in sls]
    kd = [kd_ref[0, sl, :] for sl in sls]
    be = [be_ref[0, sl, :] for sl in sls]

    one = lambda x, y, dims=NN: _mm(x, y, dims, 1)
    rows2 = lambda x, y: jnp.concatenate([x, y], axis=0)
    cols2 = lambda x, y: jnp.concatenate([x, y], axis=1)

    def cumsum(x):
        l1 = x.astype(BF16)
        rem = x - l1.astype(F32)
        l2 = rem.astype(BF16)
        l3 = (rem - l2.astype(F32)).astype(BF16)
        y = _dot(cum, jnp.concatenate([l1, l2, l3], axis=1))
        return y[:, :LANES] + (y[:, LANES:2 * LANES] + y[:, 2 * LANES:])

    cs = [cumsum(lw[j]) for j in js]
    g_inv = [jnp.exp(-cs[j]) for j in js]
    g_tot = [jnp.exp(jnp.sum(lw[j], axis=0, keepdims=True)) for j in js]
    ab = [-kk[j] * jnp.exp(cs[j] - lw[j]) for j in js]
    rb = [r[j] * jnp.exp(cs[j]) for j in js]
    bt = [be[j] * g_inv[j] for j in js]
    kt = [kd[j] * g_inv[j] for j in js]

    sc = [one(rows2(ab[j], rb[j]), rows2(bdg(bt[j]), bdg(kt[j])), NT) for j in js]
    m_ab = [jnp.where(strict2, sc[j][:c, :LANES], 0.0) for j in js]
    n_rb = [jnp.where(incl2, sc[j][c:, :LANES], 0.0) for j in js]
    m_ak = [jnp.where(strict2, sc[j][:c, LANES:], 0.0) for j in js]
    n_rk = [jnp.where(incl2, sc[j][c:, LANES:], 0.0) for j in js]

    levels = int(math.log2(c)) - 1
    t = [eye2 + m_ab[j] for j in js]
    p = [one(m_ab[j], bdg(m_ab[j])) for j in js]
    for _ in range(1, levels - 1):
        y = [one(rows2(p[j], t[j]), bdg(p[j])) for j in js]
        p = [y[j][:c] for j in js]
        t = [t[j] + y[j][c:] for j in js]
    t = [t[j] + one(t[j], bdg(p[j])) for j in js]

    def residual(a, tt):
        ah, al = _split(a)
        th, tl = _split(bdg(tt))
        y = _dot(rows2(ah, al), th)
        return eye2 - (y[:c] + (y[c:] + _dot(ah, tl)))

    res = [residual(eye2 - m_ab[j], t[j]) for j in js]
    t = [t[j] + one(t[j], bdg(res[j])) for j in js]

    vbd = [bdg(v[j]) for j in js]
    mv = [one(rows2(m_ak[j], n_rk[j]), vbd[j]) for j in js]
    tp = [one(t[j], cols2(bdg(ab[j]), bdg(mv[j][:c]))) for j in js]
    ap = [tp[j][:, :LANES] for j in js]
    pp = [tp[j][:, LANES:] for j in js]
    nr = [one(n_rb[j], cols2(bdg(ap[j]), bdg(pp[j]))) for j in js]
    for j in js:
        rp_ref[0, 0, 0, sls[j], :] = rb[j] + nr[j][:, :LANES]
        qp_ref[0, 0, 0, sls[j], :] = nr[j][:, LANES:] + mv[j][c:]
    for j in js:
        st = one(cols2(ap[j], pp[j]), bt[j], TN)
        gmat = eye128 + jnp.where(same_head, st[:LANES], zero128)
        hmat = jnp.where(same_head, st[LANES:] + one(v[j], kt[j], TN), zero128)
        g_ref[0, 0, 0, j] = gmat * g_tot[j]
        h_ref[0, 0, 0, j] = hmat * g_tot[j]


def _rwkv_p1(pk, cpb=16):
    bsz, s, _ = pk.shape
    nc = s // CHUNK
    npair = RWKV_WIDTH // LANES
    rows = cpb * CHUNK

    def slab(sidx):
        return pl.BlockSpec((1, rows, LANES), lambda b, hp, d, ci: (b, ci, sidx * npair + hp))

    def dslab(off):
        return pl.BlockSpec((1, rows, LANES),
                            lambda b, hp, d, ci: (b, ci, (SLAB_DIR0 + 3 * d + off) * npair + hp))

    seq = pl.BlockSpec((1, 1, 1, rows, LANES), lambda b, hp, d, ci: (b, hp, d, ci, 0))
    mat = pl.BlockSpec((1, 1, 1, cpb, LANES, LANES), lambda b, hp, d, ci: (b, hp, d, ci, 0, 0))
    return pl.pallas_call(
        functools.partial(_p1_kernel, cpb=cpb),
        out_shape=(jax.ShapeDtypeStruct((bsz, npair, 2, s, LANES), F32),
                   jax.ShapeDtypeStruct((bsz, npair, 2, s, LANES), F32),
                   jax.ShapeDtypeStruct((bsz, npair, 2, nc, LANES, LANES), F32),
                   jax.ShapeDtypeStruct((bsz, npair, 2, nc, LANES, LANES), F32)),
        grid=(bsz, npair, 2, nc // cpb),
        in_specs=[slab(SLAB_R), slab(SLAB_V), slab(SLAB_KK), dslab(0), dslab(1), dslab(2)],
        out_specs=(seq, seq, mat, mat),
        compiler_params=_params("arbitrary", "arbitrary", "arbitrary", "arbitrary"),
        name="rwkv_p1",
    )(pk, pk, pk, pk, pk, pk)


def _p2_kernel(rp_ref, qp_ref, g_ref, h_ref, gate_ref, bonus_ref, lng_ref, lnb_ref, bd_ref, o_ref,
               yf_ref, yb_ref, sf_ref, sb_ref, *, nc, te):
    c = CHUNK
    sf_ref[...] = jnp.zeros_like(sf_ref)
    sb_ref[...] = jnp.zeros_like(sb_ref)

    def step(j, _):
        jf = j
        jb = nc - 1 - j
        slf = pl.ds(pl.multiple_of(jf * c, c), c)
        slb = pl.ds(pl.multiple_of(jb * c, c), c)
        sf = sf_ref[...]
        sb = sb_ref[...]
        yf_ref[slf, :] = _mm(rp_ref[0, 0, 0, slf, :], sf, NT, WKV_PASSES_CHAIN) + qp_ref[0, 0, 0, slf, :]
        yb_ref[slb, :] = _mm(rp_ref[0, 0, 1, slb, :], sb, NT, WKV_PASSES_CHAIN) + qp_ref[0, 0, 1, slb, :]
        sf_ref[...] = _mm(sf, g_ref[0, 0, 0, jf], NN, WKV_PASSES_CHAIN) + h_ref[0, 0, 0, jf]
        sb_ref[...] = _mm(sb, g_ref[0, 0, 1, jb], NN, WKV_PASSES_CHAIN) + h_ref[0, 0, 1, jb]
        return 0

    lax.fori_loop(0, nc, step, 0)

    bd = bd_ref[...]
    lng = lng_ref[...]
    lnb = lnb_ref[...]
    inv = 1.0 / HEAD_DIM

    def epi(i, _):
        sl = pl.ds(pl.multiple_of(i * te, te), te)
        y = yf_ref[sl, :] + yb_ref[sl, :]
        mean = _mm_exact_rhs(y, bd) * inv
        yc = y - mean
        var = _mm_exact_rhs(yc * yc, bd) * inv
        yn = yc * lax.rsqrt(var + LNX_EPS) * lng + lnb
        o_ref[0, sl, :] = ((yn + bonus_ref[0, sl, :]) * gate_ref[0, sl, :]).astype(o_ref.dtype)
        return 0

    lax.fori_loop(0, (nc * c) // te, epi, 0)


def _rwkv_p2(rp, qp, gm, hm, pk, lnx_g, lnx_b, bd128, te=1024):
    bsz, npair, _, s, _ = rp.shape
    nc = s // CHUNK
    seq = pl.BlockSpec((1, 1, 2, s, LANES), lambda b, hp: (b, hp, 0, 0, 0))
    mat = pl.BlockSpec((1, 1, 2, nc, LANES, LANES), lambda b, hp: (b, hp, 0, 0, 0, 0))
    return pl.pallas_call(
        functools.partial(_p2_kernel, nc=nc, te=min(te, s)),
        out_shape=jax.ShapeDtypeStruct((bsz, s, RWKV_WIDTH), BF16),
        grid=(bsz, npair),
        in_specs=[seq, seq, mat, mat,
                  pl.BlockSpec((1, s, LANES), lambda b, hp: (b, 0, SLAB_G * npair + hp)),
                  pl.BlockSpec((1, s, LANES), lambda b, hp: (b, 0, SLAB_BONUS * npair + hp)),
                  pl.BlockSpec((1, LANES), lambda b, hp: (0, hp)),
                  pl.BlockSpec((1, LANES), lambda b, hp: (0, hp)),
                  pl.BlockSpec((LANES, LANES), lambda b, hp: (0, 0))],
        out_specs=pl.BlockSpec((1, s, LANES), lambda b, hp: (b, 0, hp)),
        scratch_shapes=[pltpu.VMEM((s, LANES), F32), pltpu.VMEM((s, LANES), F32),
                        pltpu.VMEM((LANES, LANES), F32), pltpu.VMEM((LANES, LANES), F32)],
        compiler_params=_params("arbitrary", "arbitrary"),
        name="rwkv_p2",
    )(rp, qp, gm, hm, pk, pk, lnx_g, lnx_b, bd128)


def _outproj_kernel(att_ref, rw_ref, x_ref, mod_ref, g2_ref, wo_ref, wq_ref, k1_ref, k2_ref,
                    x1_ref, h2_ref, s1_ref, s2_ref):
    mod = mod_ref[0]
    a = ATT_WIDTH
    mixed = _dot(att_ref[0], wo_ref[0:a, :]) + _dot(rw_ref[0], wo_ref[a:, :])
    x1 = x_ref[0] + mod[2:3] * mixed
    x1_ref[0] = x1
    ms = jnp.mean(x1 * x1, axis=-1, keepdims=True)
    h2 = x1 * lax.rsqrt(ms + NORM_EPS) * g2_ref[...]
    h2 = (h2 * (1.0 + mod[4:5]) + mod[3:4]).astype(BF16)
    h2_ref[0] = h2
    q = _dot(h2, wq_ref[...])
    k1 = k1_ref[...]
    k2 = k2_ref[...]
    for h in range(PEER_HEADS):
        base = h * 2 * LANES
        s1 = _dot(k1, q[:, base:base + LANES].astype(BF16), NT)
        s2 = _dot(k2, q[:, base + LANES:base + 2 * LANES].astype(BF16), NT)
        for g in range(s1.shape[1] // LANES):
            s1_ref[h, g] = s1[:, g * LANES:(g + 1) * LANES]
            s2_ref[h, g] = s2[:, g * LANES:(g + 1) * LANES]


def _outproj(att, rw, x, mod6, norm2_g, w_out_bf, wq_bf, k1_bf, k2_bf, tm=512):
    bsz, s, d = x.shape
    nt = s // tm
    t = bsz * s
    nq = wq_bf.shape[1]
    const = lambda b, i: (0, 0)
    gpt = tm // LANES
    tok = pl.BlockSpec((PEER_HEADS, gpt, PEER_N_KEYS, LANES), lambda b, i: (0, b * nt + i, 0, 0))
    stat = jax.ShapeDtypeStruct((PEER_HEADS, t // LANES, PEER_N_KEYS, LANES), F32)
    return pl.pallas_call(
        _outproj_kernel,
        out_shape=(jax.ShapeDtypeStruct((bsz, s, d), F32),
                   jax.ShapeDtypeStruct((bsz, s, d), BF16),
                   stat, stat),
        grid=(bsz, nt),
        in_specs=[pl.BlockSpec((1, tm, ATT_WIDTH), lambda b, i: (b, i, 0)),
                  pl.BlockSpec((1, tm, RWKV_WIDTH), lambda b, i: (b, i, 0)),
                  pl.BlockSpec((1, tm, d), lambda b, i: (b, i, 0)),
                  pl.BlockSpec((1, 6, d), lambda b, i: (b, 0, 0)),
                  pl.BlockSpec((1, d), const),
                  pl.BlockSpec((d, d), const),
                  pl.BlockSpec((d, nq), const),
                  pl.BlockSpec((PEER_N_KEYS, LANES), const),
                  pl.BlockSpec((PEER_N_KEYS, LANES), const)],
        out_specs=(pl.BlockSpec((1, tm, d), lambda b, i: (b, i, 0)),
                   pl.BlockSpec((1, tm, d), lambda b, i: (b, i, 0)),
                   tok, tok),
        compiler_params=_params("arbitrary", "arbitrary"),
        name="outproj",
    )(att, rw, x, mod6, norm2_g, w_out_bf, wq_bf, k1_bf, k2_bf)


def _oddeven_sort_pairs(n):
    pairs = []
    p = 1
    while p < n:
        k = p
        while k >= 1:
            for j in range(k % p, n - k, 2 * k):
                for i in range(min(k, n - j - k)):
                    if (i + j) // (2 * p) == (i + j + k) // (2 * p):
                        pairs.append((i + j, i + j + k))
            k //= 2
        p *= 2
    return pairs


def _bitonic_merge_pairs(n):
    pairs = []
    stride = n // 2
    while stride >= 1:
        pairs += [(i, i + stride) for i in range(n) if not i & stride]
        stride //= 2
    return pairs


def _compare_exchange(vals, pairs):
    vals = list(vals)
    for i, j in pairs:
        a, b = vals[i], vals[j]
        if b is None:
            continue
        if a is None:
            vals[i], vals[j] = b, None
        else:
            vals[i], vals[j] = jnp.maximum(a, b), jnp.minimum(a, b)
    return vals


def _top16_sorted(vregs):
    kk = PEER_TOPK
    vals = list(vregs) + [None] * (kk - len(vregs))
    vals = _compare_exchange(vals, _oddeven_sort_pairs(kk))
    for shift in (4, 2, 1):
        other = [None if v is None else pltpu.roll(v, shift, axis=0) for v in vals]
        merged = []
        for k in range(kk):
            a, b = vals[k], other[kk - 1 - k]
            merged.append(b if a is None else a if b is None else jnp.maximum(a, b))
        vals = _compare_exchange(merged, _bitonic_merge_pairs(kk))
    return vals


def _topk_kernel(s1_ref, s2_ref, c1_ref, e1_ref, r2_ref, e2_ref, *, groups):
    kk = PEER_TOPK
    sub = 8
    neg = -jnp.inf
    rows8 = lax.broadcasted_iota(jnp.int32, (sub, LANES), 0)

    def rows_of(reps):
        out = reps[0]
        for r in range(1, sub):
            out = jnp.where(rows8 == r, reps[r], out)
        return out

    def all_sublanes_sum(x):
        for shift in (4, 2, 1):
            x = x + pltpu.roll(x, shift, axis=0)
        return x

    def per_tile(it_idx, _):
        g = it_idx // PEER_HEADS
        h = it_idx % PEER_HEADS
        s1 = [s1_ref[h, g, pl.ds(v * sub, sub), :] for v in range(PEER_N_KEYS // sub)]
        s2 = [s2_ref[h, g, pl.ds(v * sub, sub), :] for v in range(PEER_N_KEYS // sub)]
        a = _top16_sorted(s1)
        b = _top16_sorted(s2)
        a_lo, a_hi = rows_of(a[:sub]), rows_of(a[sub:])
        b_lo, b_hi = rows_of(b[:sub]), rows_of(b[sub:])

        cand = [a[0] + b_lo, a[0] + b_hi, a[1] + b_lo]
        for i in range(2, sub):
            cand.append(jnp.where(rows8 < kk // (i + 1), a[i] + b_lo, neg))
        cand.append(a_hi + b[0])
        top = _top16_sorted(cand)
        tau = top[kk - 1]
        z = None
        for t in top:
            ez = jnp.exp(t - top[0])
            z = ez if z is None else z + ez
        inv_z = 1.0 / z

        counts = []
        for r in range(kk):
            hits = (jnp.where(a[r] + b_lo >= tau, 1.0, 0.0) + jnp.where(a[r] + b_hi >= tau, 1.0, 0.0))
            counts.append(all_sublanes_sum(hits))
        for v in range(PEER_N_KEYS // sub):
            rows = pl.ds(v * sub, sub)
            cnt = jnp.zeros((sub, LANES), F32)
            rank = jnp.full((sub, LANES), float(kk), F32)
            for r in range(kk):
                cnt = jnp.where(s1[v] == a[r], counts[r], cnt)
                rank = jnp.where(s2[v] == b[r], float(r), rank)
            c1_ref[h, g, rows, :] = cnt
            e1_ref[h, g, rows, :] = jnp.exp(s1[v] - a[0]) * inv_z
            s2[v] = (rank, jnp.exp(s2[v] - b[0]))
        rank2 = jnp.concatenate([rv for rv, _ in s2], axis=0).astype(BF16)
        e2 = jnp.concatenate([ev for _, ev in s2], axis=0).astype(BF16)
        r2_ref[h, g] = pltpu.bitcast(rank2, jnp.uint32)
        e2_ref[h, g] = pltpu.bitcast(e2, jnp.uint32)
        return 0

    lax.fori_loop(0, groups * PEER_HEADS, per_tile, 0)


def _peer_topk(s1t, s2t, tn=512):
    nh, ng, nk, _ = s1t.shape
    groups = tn // LANES
    blk = pl.BlockSpec((nh, groups, nk, LANES), lambda i: (0, i, 0, 0))
    f32 = jax.ShapeDtypeStruct(s1t.shape, F32)
    b16 = jax.ShapeDtypeStruct((nh, ng, nk // 2, LANES), jnp.uint32)
    pblk = pl.BlockSpec((nh, groups, nk // 2, LANES), lambda i: (0, i, 0, 0))
    return pl.pallas_call(
        functools.partial(_topk_kernel, groups=groups),
        out_shape=(f32, f32, b16, b16),
        grid=(ng // groups,),
        in_specs=[blk, blk],
        out_specs=(blk, blk, pblk, pblk),
        compiler_params=_params("arbitrary"),
        name="peer_topk",
    )(s1t, s2t)


class _Pieces(list):
    every = 1
    phase = 0


def _peer_kernel(h2_ref, u_ref, vt_ref, c1_ref, e1_ref, r2_ref, e2_ref, x1_ref, mod_ref, o_ref,
                 acc_ref, act0_ref, act1_ref, p0_ref, p1_ref, *, tn, te, n_tiles):
    k = pl.program_id(0)
    n_items = pl.num_programs(0) - 2
    nk = PEER_N_KEYS
    ngroups = tn // LANES

    @pl.when(k == 0)
    def _():
        acc_ref[...] = jnp.zeros_like(acc_ref)
        for ref in (act0_ref, act1_ref, p0_ref, p1_ref):
            ref[...] = jnp.zeros_like(ref)

    ni = te // nk
    i0 = pl.multiple_of((jnp.clip(k - 1, 0, n_items - 1) % n_tiles) * ni, ni)
    tile_c = jnp.clip(k - 2, 0, n_items - 1) % n_tiles
    first_c = tile_c == 0

    d_model = acc_ref.shape[0]
    ksplit = 4
    halves = [(mh, nh, kh) for mh in range(2) for nh in range(ngroups // 2) for kh in range(ksplit)]

    def stage_c(p_a, mh, nh, kh):
        rows = slice(mh * (d_model // 2), (mh + 1) * (d_model // 2))
        cols = slice(nh * 2 * LANES, (nh + 1) * 2 * LANES)
        ks = slice(kh * (te // ksplit), (kh + 1) * (te // ksplit))
        p_prev = jnp.concatenate([p_a[2 * nh, ks, :], p_a[2 * nh + 1, ks, :]], axis=1)
        prev = acc_ref[rows, cols]
        if kh == 0:
            prev = jnp.where(first_c, jnp.zeros_like(prev), prev)
        acc_ref[rows, cols] = prev + _dot(vt_ref[rows, ks], p_prev)

    def stage_a(act_a, mh, nh, kh):
        rows = slice(mh * (te // 2), (mh + 1) * (te // 2))
        ks = slice(kh * (d_model // ksplit), (kh + 1) * (d_model // ksplit))
        act = _dot(u_ref[rows, ks], h2_ref[nh * 2 * LANES:(nh + 1) * 2 * LANES, ks], NT)
        if kh == 0:
            act_a[2 * nh, rows, :] = act[:, :LANES]
            act_a[2 * nh + 1, rows, :] = act[:, LANES:]
        else:
            act_a[2 * nh, rows, :] += act[:, :LANES]
            act_a[2 * nh + 1, rows, :] += act[:, LANES:]

    def rows_bf16(row):
        tile = jnp.broadcast_to(row, (BF16_ROWS, LANES)).astype(BF16)
        return jnp.concatenate([tile] * (nk // BF16_ROWS), axis=0)

    quad = 4

    def stage_b(act_b, p_b, tg, iq, pieces):
        gates = [jnp.zeros((nk, LANES), BF16) for _ in range(quad)]
        for h in range(PEER_HEADS):
            if h % pieces.every == pieces.phase and pieces:
                pieces.pop(0)()
            c1 = c1_ref[h, tg, pl.ds(i0, ni), :]
            e1 = e1_ref[h, tg, pl.ds(i0, ni), :]
            rank2 = pltpu.bitcast(r2_ref[h, tg], BF16)
            e2 = pltpu.bitcast(e2_ref[h, tg], BF16)
            for q in range(quad):
                il = iq * quad + q
                sel = rank2 < rows_bf16(c1[il:il + 1, :])
                gates[q] = gates[q] + jnp.where(sel, e2 * rows_bf16(e1[il:il + 1, :]), jnp.zeros_like(e2))
        for q in range(quad):
            il = iq * quad + q
            a = act_b[tg, il * nk:(il + 1) * nk, :]
            gelu = 0.5 * a * (1.0 + lax.erf(a * (2.0 ** -0.5)))
            p_b[tg, il * nk:(il + 1) * nk, :] = gates[q] * gelu.astype(BF16)

    def stages(act_a, act_b, p_a, p_b):
        pieces = _Pieces([functools.partial(stage_c, p_a, *hv) for hv in halves]
                         + [functools.partial(stage_a, act_a, *hv) for hv in halves])
        n_blocks = ngroups * (ni // quad)
        pieces.every = max(1, PEER_HEADS * n_blocks // len(pieces))
        pieces.phase = 0
        for tg in range(ngroups):
            for iq in range(ni // quad):
                stage_b(act_b, p_b, tg, iq, pieces)
        while pieces:
            pieces.pop(0)()

    @pl.when(k % 2 == 0)
    def _():
        stages(act0_ref, act1_ref, p0_ref, p1_ref)

    @pl.when(k % 2 == 1)
    def _():
        stages(act1_ref, act0_ref, p1_ref, p0_ref)

    @pl.when((tile_c == n_tiles - 1) & (k >= 2))
    def _():
        o_ref[...] = x1_ref[...] + mod_ref[0, 5:6, :] * acc_ref[...].T


def _peer_ffn(h2, u_bf, vt_bf, c1, e1, r2, e2, x1, mod6, seq, tn=512, te=1024):
    t, d = h2.shape
    ne = u_bf.shape[0]
    ngroups = tn // LANES
    n_tiles = ne // te
    n_items = (t // tn) * n_tiles

    def item(k, lag):
        w = jnp.clip(k - lag, 0, n_items - 1)
        return w // n_tiles, w % n_tiles

    stat = pl.BlockSpec((PEER_HEADS, ngroups, PEER_N_KEYS, LANES), lambda k: (0, item(k, 1)[0], 0, 0))
    pstat = pl.BlockSpec((PEER_HEADS, ngroups, PEER_N_KEYS // 2, LANES), lambda k: (0, item(k, 1)[0], 0, 0))
    return pl.pallas_call(
        functools.partial(_peer_kernel, tn=tn, te=te, n_tiles=n_tiles),
        out_shape=jax.ShapeDtypeStruct((t, d), F32),
        grid=(n_items + 2,),
        in_specs=[pl.BlockSpec((tn, d), lambda k: (item(k, 0)[0], 0)),
                  pl.BlockSpec((te, d), lambda k: (item(k, 0)[1], 0)),
                  pl.BlockSpec((d, te), lambda k: (0, item(k, 2)[1])),
                  stat, stat, pstat, pstat,
                  pl.BlockSpec((tn, d), lambda k: (item(k, 2)[0], 0)),
                  pl.BlockSpec((1, 6, d), lambda k: ((item(k, 2)[0] * tn) // seq, 0, 0))],
        out_specs=pl.BlockSpec((tn, d), lambda k: (item(k, 2)[0], 0)),
        scratch_shapes=[pltpu.VMEM((d, tn), F32),
                        pltpu.VMEM((ngroups, te, LANES), F32), pltpu.VMEM((ngroups, te, LANES), F32),
                        pltpu.VMEM((ngroups, te, LANES), BF16), pltpu.VMEM((ngroups, te, LANES), BF16)],
        compiler_params=_params("arbitrary"),
        name="peer_ffn",
    )(h2, u_bf, vt_bf, c1, e1, r2, e2, x1, mod6)


def _head_block_diag(n):
    idx = jnp.arange(n, dtype=jnp.int32) // HEAD_DIM
    return (idx[:, None] == idx[None, :]).astype(BF16)


def _layer(x, c, ada_w, ada_b, norm1_g, w_in, mu_prev, mu_next, q_norm_g, k_norm_g, w_decay0, w_decay_up,
           a_gate0, a_gate_up, g_up, k_k, k_a, r_k, lnx_g, lnx_b, w_out, norm2_g, peer_w_query,
           peer_sub_keys1, peer_sub_keys2, peer_u, peer_v):
    bsz, s, d = x.shape
    n_att_heads = ATT_WIDTH // HEAD_DIM
    bd512 = _head_block_diag(ATT_WIDTH)
    bd128 = _head_block_diag(LANES)

    mod6 = _ada(c, ada_w, ada_b).reshape(bsz, 6, d)
    gq = jnp.tile(q_norm_g, n_att_heads).reshape(1, ATT_WIDTH)
    gk = jnp.tile(k_norm_g, n_att_heads).reshape(1, ATT_WIDTH)
    qkv, zrw = _inproj(x, mod6, norm1_g.reshape(1, d), w_in.astype(BF16), gq, gk, bd512)

    att = _attention(qkv, n_att_heads)

    zero_w = jnp.zeros((2, LANES - DECAY_LORA, RWKV_WIDTH), F32)
    wup_pad = jnp.concatenate([w_decay_up, zero_w], axis=1).astype(BF16)
    aup_pad = jnp.concatenate([zero_w, a_gate_up], axis=1).astype(BF16)
    pk = _rwkv_prep(zrw, mu_prev.reshape(1, -1), mu_next.reshape(1, -1), wup_pad, aup_pad, g_up.astype(BF16),
                    w_decay0, a_gate0, k_k.reshape(1, -1), k_a.reshape(1, -1), r_k.reshape(1, -1), bd512)
    rp, qp, gm, hm = _rwkv_p1(pk)
    rw = _rwkv_p2(rp, qp, gm, hm, pk, lnx_g.reshape(1, -1), lnx_b.reshape(1, -1), bd128)

    x1, h2, s1t, s2t = _outproj(att, rw, x, mod6, norm2_g.reshape(1, d), w_out.astype(BF16),
                                peer_w_query.astype(BF16), peer_sub_keys1.astype(BF16),
                                peer_sub_keys2.astype(BF16))
    c1, e1, r2, e2 = _peer_topk(s1t, s2t)
    out = _peer_ffn(h2.reshape(bsz * s, d), peer_u.astype(BF16), peer_v.T.astype(BF16), c1, e1, r2, e2,
                    x1.reshape(bsz * s, d), mod6, s)
    return out.reshape(bsz, s, d)


def kernel(x, c, ada_w, ada_b, norm1_g, w_in, mu_prev, mu_next, q_norm_g, k_norm_g, w_decay0, w_decay_up,
           a_gate0, a_gate_up, g_up, k_k, k_a, r_k, lnx_g, lnx_b, w_out, norm2_g, peer_w_query,
           peer_sub_keys1, peer_sub_keys2, peer_u, peer_v):
    depth = ada_w.shape[0]
    for l in range(depth):
        x = _layer(x, c, ada_w[l], ada_b[l], norm1_g[l], w_in[l], mu_prev[l], mu_next[l], q_norm_g[l],
                   k_norm_g[l], w_decay0[l], w_decay_up[l], a_gate0[l], a_gate_up[l], g_up[l], k_k[l], k_a[l],
                   r_k[l], lnx_g[l], lnx_b[l], w_out[l], norm2_g[l], peer_w_query[l], peer_sub_keys1[l],
                   peer_sub_keys2[l], peer_u[l], peer_v[l])
    return x
```

```python
import functools
import math

import jax
import jax.numpy as jnp
from jax import lax
from jax.experimental import pallas as pl
from jax.experimental.pallas import tpu as pltpu

F32 = jnp.float32
BF16 = jnp.bfloat16

HEAD_DIM = 64
ATT_WIDTH = 512
RWKV_WIDTH = 512
DECAY_LORA = 64
AAA_LORA = 64
GATE_LORA = 128
SHIFT_WIDTH = 3 * RWKV_WIDTH + DECAY_LORA + AAA_LORA + GATE_LORA
PEER_N_KEYS = 128
PEER_HEADS = 8
PEER_TOPK = 16
NORM_EPS = 1e-6
LNX_EPS = 64e-5
MASK_VALUE = -1e30
LOG2E = 1.4426950408889634
ATT_HALF_WINDOWS = ((1, 64), (4, 256), (16, 1024))

LANES = 128
BF16_ROWS = 16
CHUNK = 64
ATT_BLOCK = 128
ATT_REACH = 1024 // ATT_BLOCK
VMEM_LIMIT_BYTES = 56 * 1024 * 1024
WKV_PASSES_CHAIN = 1

NN = (((1,), (0,)), ((), ()))
NT = (((1,), (1,)), ((), ()))
TN = (((0,), (0,)), ((), ()))

SLAB_R, SLAB_V, SLAB_KK, SLAB_DIR0, SLAB_G, SLAB_BONUS, NUM_SLABS = 0, 1, 2, 3, 9, 10, 11


def _params(*sem):
    return pltpu.CompilerParams(dimension_semantics=sem, vmem_limit_bytes=VMEM_LIMIT_BYTES)


def _dot(a, b, dims=NN):
    return lax.dot_general(a, b, dims, preferred_element_type=F32)


def _split(a):
    hi = a.astype(BF16)
    lo = (a - hi.astype(F32)).astype(BF16)
    return hi, lo


def _mm(a, b, dims=NN, passes=3):
    if passes == 1:
        return _dot(a.astype(BF16), b.astype(BF16), dims)
    ah, al = _split(a)
    bh, bl = _split(b)
    return _dot(ah, bh, dims) + (_dot(ah, bl, dims) + _dot(al, bh, dims))


def _mm_exact_rhs(a, b_exact):
    ah, al = _split(a)
    return _dot(ah, b_exact) + _dot(al, b_exact)


def _sigmoid(x):
    return 1.0 / (1.0 + jnp.exp(-x))


def _ada_kernel(c_ref, w_ref, b_ref, o_ref):
    c = c_ref[...]
    o_ref[...] = _mm(c * _sigmoid(c), w_ref[...]) + b_ref[...]


def _ada(c, ada_w, ada_b):
    bsz, d = c.shape
    n = ada_w.shape[1]
    tn = 1024
    return pl.pallas_call(
        _ada_kernel,
        out_shape=jax.ShapeDtypeStruct((bsz, n), F32),
        grid=(n // tn,),
        in_specs=[pl.BlockSpec((bsz, d), lambda j: (0, 0)),
                  pl.BlockSpec((d, tn), lambda j: (0, j)),
                  pl.BlockSpec((1, tn), lambda j: (0, j))],
        out_specs=pl.BlockSpec((bsz, tn), lambda j: (0, j)),
        compiler_params=_params("arbitrary"),
        name="ada",
    )(c, ada_w, ada_b.reshape(1, n))


def _inproj_kernel(x_ref, mod_ref, g1_ref, w_ref, gq_ref, gk_ref, bd_ref, qkv_ref, zrw_ref):
    x = x_ref[0]
    mod = mod_ref[0]
    ms = jnp.mean(x * x, axis=-1, keepdims=True)
    h = x * lax.rsqrt(ms + NORM_EPS) * g1_ref[...]
    h = (h * (1.0 + mod[1:2]) + mod[0:1]).astype(BF16)
    bd = bd_ref[...]

    def head_norm(z, g):
        ss = _mm_exact_rhs(z * z, bd) * (1.0 / HEAD_DIM)
        return z * lax.rsqrt(ss + NORM_EPS) * g

    a = ATT_WIDTH
    zq = _dot(h, w_ref[:, 0:a])
    qkv_ref[0, :, 0:a] = (head_norm(zq, gq_ref[...]) * (HEAD_DIM ** -0.5 * LOG2E)).astype(BF16)
    zk = _dot(h, w_ref[:, a:2 * a])
    qkv_ref[0, :, a:2 * a] = head_norm(zk, gk_ref[...]).astype(BF16)
    qkv_ref[0, :, 2 * a:3 * a] = _dot(h, w_ref[:, 2 * a:3 * a]).astype(BF16)
    zrw_ref[0] = _dot(h, w_ref[:, 3 * a:])


def _inproj(x, mod6, norm1_g, w_in_bf, gq, gk, bd512, tm=512):
    bsz, s, d = x.shape
    nin = w_in_bf.shape[1]
    return pl.pallas_call(
        _inproj_kernel,
        out_shape=(jax.ShapeDtypeStruct((bsz, s, 3 * ATT_WIDTH), BF16),
                   jax.ShapeDtypeStruct((bsz, s, SHIFT_WIDTH), F32)),
        grid=(bsz, s // tm),
        in_specs=[pl.BlockSpec((1, tm, d), lambda b, i: (b, i, 0)),
                  pl.BlockSpec((1, 6, d), lambda b, i: (b, 0, 0)),
                  pl.BlockSpec((1, d), lambda b, i: (0, 0)),
                  pl.BlockSpec((d, nin), lambda b, i: (0, 0)),
                  pl.BlockSpec((1, ATT_WIDTH), lambda b, i: (0, 0)),
                  pl.BlockSpec((1, ATT_WIDTH), lambda b, i: (0, 0)),
                  pl.BlockSpec((ATT_WIDTH, ATT_WIDTH), lambda b, i: (0, 0))],
        out_specs=(pl.BlockSpec((1, tm, 3 * ATT_WIDTH), lambda b, i: (b, i, 0)),
                   pl.BlockSpec((1, tm, SHIFT_WIDTH), lambda b, i: (b, i, 0))),
        compiler_params=_params("arbitrary", "arbitrary"),
        name="inproj",
    )(x, mod6, norm1_g, w_in_bf, gq, gk, bd512)


ATT_FAR_DILATION, ATT_FAR_HALF = ATT_HALF_WINDOWS[-1]
ATT_NEAR_BLOCKS = 6


def _attn_near_window(nkb):
    return min(nkb, ATT_NEAR_BLOCKS)


def _attn_near_start(qb, nkb):
    return jnp.clip(((qb - 2) // 2) * 2, 0, nkb - _attn_near_window(nkb))


def _attn_far_bias(n_heads, nkb):
    per_block = ATT_BLOCK // ATT_FAR_DILATION
    i = jnp.arange(ATT_BLOCK, dtype=jnp.int32)[:, None]
    ip = jnp.arange(ATT_BLOCK, dtype=jnp.int32)[None, :]
    start = _attn_near_start(i // per_block, nkb)
    kb = ip // per_block
    dist = ATT_FAR_DILATION * jnp.abs(i - ip)
    far = (dist <= ATT_FAR_HALF) & ~((kb >= start) & (kb < start + _attn_near_window(nkb)))
    slopes = jnp.exp2(-8.0 * (jnp.arange(n_heads, dtype=F32) + 1.0) / n_heads) * LOG2E
    return jnp.where(far[None], -slopes[:, None, None] * dist.astype(F32)[None], MASK_VALUE)


def _attn_bias_table(n_heads, nkb):
    reach = _attn_near_window(nkb) - 1
    nd = 2 * reach + 1
    dd = jnp.arange(nd, dtype=jnp.int32)[None, :, None]
    r = jnp.arange(ATT_BLOCK, dtype=jnp.int32)[:, None, None]
    c = jnp.arange(ATT_BLOCK, dtype=jnp.int32)[None, None, :]
    dt = r - c - (dd - reach) * ATT_BLOCK
    adt = jnp.abs(dt)
    mult = jnp.zeros(dt.shape, F32)
    for dil, half in ATT_HALF_WINDOWS:
        mult = mult + ((adt <= half) & (dt % dil == 0)).astype(F32)
    logm = jnp.where(mult > 0, jnp.log2(jnp.maximum(mult, 1.0)), MASK_VALUE)
    slopes = jnp.exp2(-8.0 * (jnp.arange(n_heads, dtype=F32) + 1.0) / n_heads) * LOG2E
    bias = logm[None] - slopes[:, None, None, None] * adt.astype(F32)[None]
    return bias.reshape(n_heads, ATT_BLOCK, nd * ATT_BLOCK)


def _attn_kernel(q_ref, k_ref, v_ref, bias_ref, far_ref, o_ref, s_ref, v1_ref, g_ref, accf_ref, mf_ref,
                 *, nkb, win):
    blk = ATT_BLOCK
    reach = win - 1
    has_far = win < nkb
    lane = lax.broadcasted_iota(jnp.int32, (blk, LANES), 1)
    first = lane < HEAD_DIM

    def two_heads(q):
        zero = jnp.zeros_like(q)
        return jnp.concatenate([jnp.where(first, q, zero), jnp.where(first, zero, q)], axis=0)

    def with_ones(v, hh):
        is_own = (lax.broadcasted_iota(jnp.int32, v.shape, 1) < HEAD_DIM) == (hh == 0)
        return jnp.where(is_own, v, jnp.ones_like(v))

    v = v_ref[0]
    for hh in range(2):
        v1_ref[hh] = with_ones(v, hh)

    if has_far:
        g_ref[0] = q_ref[0].astype(F32)
        g_ref[1] = k_ref[0].astype(F32)
        g_ref[2] = v.astype(F32)
        n_far = nkb * blk // ATT_FAR_DILATION

        def residue(r, _):
            rows = pl.ds(r, n_far, stride=ATT_FAR_DILATION)
            qr = g_ref[0, rows, :].astype(BF16)
            kr = g_ref[1, rows, :].astype(BF16)
            vr = g_ref[2, rows, :].astype(BF16)
            s = _dot(two_heads(qr), kr, NT)
            for hh in range(2):
                sb = s[hh * blk:(hh + 1) * blk] + far_ref[hh]
                mx = jnp.max(sb, axis=-1, keepdims=True)
                p = jnp.exp2(sb - mx).astype(BF16)
                accf_ref[hh, rows, :] = _dot(p, with_ones(vr, hh))
                mf_ref[hh, rows, :] = jnp.broadcast_to(mx, (blk, LANES))
            return 0

        lax.fori_loop(0, ATT_FAR_DILATION, residue, 0, unroll=4)

    group = s_ref.shape[0]
    slabs = range(0, win, 2)

    def query_block_group(i, _):
        us = range(group)
        qis = [group * i + u for u in us]
        rows = [pl.ds(pl.multiple_of(qi * blk, blk), blk) for qi in qis]
        qab = [two_heads(q_ref[0, rows[u], :]) for u in us]
        ks = [_attn_near_start(qi, nkb) if has_far else 0 for qi in qis]

        m = [[jnp.full((blk, LANES), -jnp.inf, F32) for _ in range(2)] for _ in us]
        for w0 in slabs:
            for u in us:
                kj0 = ks[u] + w0
                kslab = k_ref[0, pl.ds(pl.multiple_of(kj0 * blk, blk), 2 * blk), :]
                s = _dot(qab[u], kslab, NT)
                x0 = pl.multiple_of((kj0 - qis[u] + reach) * blk, blk)
                for hh in range(2):
                    sb = s[hh * blk:(hh + 1) * blk] + bias_ref[hh, :, pl.ds(x0, 2 * blk)]
                    s_ref[u, hh, :, w0 * blk:(w0 + 2) * blk] = sb
                    m[u][hh] = jnp.maximum(m[u][hh], jnp.maximum(sb[:, :blk], sb[:, blk:]))
        mrow = [[jnp.max(m[u][hh], axis=-1, keepdims=True) for hh in range(2)] for u in us]

        if has_far:
            acc = [[None, None] for _ in us]
            for u in us:
                for hh in range(2):
                    m_far = mf_ref[hh, rows[u], :][:, 0:1]
                    m_all = jnp.maximum(mrow[u][hh], m_far)
                    acc[u][hh] = accf_ref[hh, rows[u], :] * jnp.exp2(m_far - m_all)
                    mrow[u][hh] = m_all
        else:
            acc = [[jnp.zeros((blk, LANES), F32) for _ in range(2)] for _ in us]
        for w0 in slabs:
            for u in us:
                keys = pl.ds(pl.multiple_of((ks[u] + w0) * blk, blk), 2 * blk)
                for hh in range(2):
                    p = jnp.exp2(s_ref[u, hh, :, w0 * blk:(w0 + 2) * blk] - mrow[u][hh]).astype(BF16)
                    acc[u][hh] = acc[u][hh] + _dot(p, v1_ref[hh, keys, :])
        for u in us:
            num = jnp.where(first, acc[u][0], acc[u][1])
            den = jnp.where(first, pltpu.roll(acc[u][0], HEAD_DIM, axis=1), pltpu.roll(acc[u][1], HEAD_DIM, axis=1))
            o_ref[0, rows[u], :] = (num / den).astype(o_ref.dtype)
        return 0

    lax.fori_loop(0, nkb // group, query_block_group, 0)


def _attention(qkv, n_heads):
    bsz, s, _ = qkv.shape
    nkb = s // ATT_BLOCK
    win = _attn_near_window(nkb)
    assert nkb % 2 == 0 and (win == nkb or s // ATT_FAR_DILATION == ATT_BLOCK)
    bias = _attn_bias_table(n_heads, nkb)
    far = _attn_far_bias(n_heads, nkb)
    npair = ATT_WIDTH // LANES
    return pl.pallas_call(
        functools.partial(_attn_kernel, nkb=nkb, win=win),
        out_shape=jax.ShapeDtypeStruct((bsz, s, ATT_WIDTH), BF16),
        grid=(npair, bsz),
        in_specs=[pl.BlockSpec((1, s, LANES), lambda hp, b: (b, 0, hp)),
                  pl.BlockSpec((1, s, LANES), lambda hp, b: (b, 0, npair + hp)),
                  pl.BlockSpec((1, s, LANES), lambda hp, b: (b, 0, 2 * npair + hp)),
                  pl.BlockSpec((2, ATT_BLOCK, bias.shape[2]), lambda hp, b: (hp, 0, 0)),
                  pl.BlockSpec((2, ATT_BLOCK, ATT_BLOCK), lambda hp, b: (hp, 0, 0))],
        out_specs=pl.BlockSpec((1, s, LANES), lambda hp, b: (b, 0, hp)),
        scratch_shapes=[pltpu.VMEM((math.gcd(nkb, 4), 2, ATT_BLOCK, win * ATT_BLOCK), F32), pltpu.VMEM((2, s, LANES), BF16),
                        pltpu.VMEM((3, s, LANES), F32), pltpu.VMEM((2, s, LANES), F32),
                        pltpu.VMEM((2, s, LANES), F32)],
        compiler_params=_params("arbitrary", "arbitrary"),
        name="attn",
    )(qkv, qkv, qkv, bias, far)


def _prep_kernel(z_ref, zp_ref, zn_ref, mup_ref, mun_ref, wup_ref, aup_ref, gup_ref, w0_ref, a0_ref,
                 kk_ref, ka_ref, rk_ref, bd_ref, o_ref, *, tq):
    i = pl.program_id(1)
    last = pl.num_programs(1) - 1
    z = z_ref[0]
    row = lax.broadcasted_iota(jnp.int32, (tq, 1), 0)
    prev_row = zp_ref[0, 7:8, :] * (i > 0).astype(F32)
    next_row = zn_ref[0, 0:1, :] * (i < last).astype(F32)
    zp = jnp.where(row == 0, prev_row, pltpu.roll(z, 1, axis=0))
    zn = jnp.where(row == tq - 1, next_row, pltpu.roll(z, tq - 1, axis=0))
    zs = z + mup_ref[...] * (zp - z) + mun_ref[...] * (zn - z)

    w = RWKV_WIDTH
    r = zs[:, 0:w]
    k = zs[:, w:2 * w]
    v = zs[:, 2 * w:3 * w]
    xwa = zs[:, 3 * w:3 * w + LANES]
    xg = zs[:, 3 * w + LANES:]
    bd = bd_ref[...]

    g = _dot(_sigmoid(xg).astype(BF16), gup_ref[...])
    kk = k * kk_ref[...]
    ss = _mm_exact_rhs(kk * kk, bd)
    kk = kk * lax.rsqrt(jnp.maximum(ss, 1e-12))
    ka = ka_ref[...]
    txw = jnp.tanh(xwa).astype(BF16)
    xab = xwa.astype(BF16)

    o_ref[0, :, SLAB_R * w:(SLAB_R + 1) * w] = r
    o_ref[0, :, SLAB_V * w:(SLAB_V + 1) * w] = v
    o_ref[0, :, SLAB_KK * w:(SLAB_KK + 1) * w] = kk
    o_ref[0, :, SLAB_G * w:(SLAB_G + 1) * w] = g
    a_sum = jnp.zeros_like(k)
    for d in range(2):
        y = w0_ref[d:d + 1, :] + _dot(txw, wup_ref[d])
        wlog = -(jnp.maximum(-y, 0.0) + jnp.log(1.0 + jnp.exp(-jnp.abs(y)))) - 0.5
        a = _sigmoid(a0_ref[d:d + 1, :] + _dot(xab, aup_ref[d]))
        a_sum = a_sum + a
        base = (SLAB_DIR0 + 3 * d) * w
        o_ref[0, :, base:base + w] = -jnp.exp(wlog)
        o_ref[0, :, base + w:base + 2 * w] = k * (1.0 + (a - 1.0) * ka)
        o_ref[0, :, base + 2 * w:base + 3 * w] = kk * a
    k_bonus = k * (1.0 + (0.5 * a_sum - 1.0) * ka)
    bsum = _mm_exact_rhs(r * k_bonus * rk_ref[...], bd)
    o_ref[0, :, SLAB_BONUS * w:(SLAB_BONUS + 1) * w] = bsum * v


def _rwkv_prep(zrw, mu_prev, mu_next, wup_pad, aup_pad, g_up_bf, w0, a0, k_k, k_a, r_k, bd512, tq=256):
    bsz, s, sw = zrw.shape
    w = RWKV_WIDTH
    nt = s // tq
    row = lambda b, i: (0, 0)
    return pl.pallas_call(
        functools.partial(_prep_kernel, tq=tq),
        out_shape=jax.ShapeDtypeStruct((bsz, s, NUM_SLABS * w), F32),
        grid=(bsz, nt),
        in_specs=[pl.BlockSpec((1, tq, sw), lambda b, i: (b, i, 0)),
                  pl.BlockSpec((1, 8, sw), lambda b, i: (b, jnp.maximum(i * (tq // 8) - 1, 0), 0)),
                  pl.BlockSpec((1, 8, sw), lambda b, i: (b, jnp.minimum((i + 1) * (tq // 8), s // 8 - 1), 0)),
                  pl.BlockSpec((1, sw), row),
                  pl.BlockSpec((1, sw), row),
                  pl.BlockSpec((2, LANES, w), lambda b, i: (0, 0, 0)),
                  pl.BlockSpec((2, LANES, w), lambda b, i: (0, 0, 0)),
                  pl.BlockSpec((GATE_LORA, w), row),
                  pl.BlockSpec((2, w), row),
                  pl.BlockSpec((2, w), row),
                  pl.BlockSpec((1, w), row),
                  pl.BlockSpec((1, w), row),
                  pl.BlockSpec((1, w), row),
                  pl.BlockSpec((w, w), row)],
        out_specs=pl.BlockSpec((1, tq, NUM_SLABS * w), lambda b, i: (b, i, 0)),
        compiler_params=_params("arbitrary", "arbitrary"),
        name="rwkv_prep",
    )(zrw, zrw, zrw, mu_prev, mu_next, wup_pad, aup_pad, g_up_bf, w0, a0, k_k, k_a, r_k, bd512)


def _block_diag(x, first):
    zero = jnp.zeros_like(x)
    return jnp.concatenate([jnp.where(first, x, zero), jnp.where(first, zero, x)], axis=0)


def _p1_kernel(r_ref, v_ref, kk_ref, lw_ref, kd_ref, be_ref, rp_ref, qp_ref, g_ref, h_ref, *, cpb):
    c = CHUNK
    sign = 1 - 2 * pl.program_id(2)
    rowi = lax.broadcasted_iota(jnp.int32, (c, LANES), 0)
    lane = lax.broadcasted_iota(jnp.int32, (c, LANES), 1)
    coli = lane & (c - 1)
    first = lane < HEAD_DIM
    before = sign * (rowi - coli)
    strict2 = before > 0
    incl2 = before >= 0
    eye2 = (coli == rowi).astype(F32)
    r64 = lax.broadcasted_iota(jnp.int32, (c, c), 0)
    c64 = lax.broadcasted_iota(jnp.int32, (c, c), 1)
    cum = (sign * (r64 - c64) >= 0).astype(BF16)

    js = range(cpb)
    sls = [pl.ds(j * c, c) for j in js]
    bdg = lambda x: _block_diag(x, first)
    r = [r_ref[0, sl, :] for sl in sls]
    v = [v_ref[0, sl, :] for sl in sls]
    kk = [kk_ref[0, sl, :] for sl in sls]
    lw = [lw_ref[0, sl, :] for sl in sls]
    kd = [kd_ref[0, sl, :] for sl in sls]
    be = [be_ref[0, sl, :] for sl in sls]

    one = lambda x, y, dims=NN: _mm(x, y, dims, 1)
    rows2 = lambda x, y: jnp.concatenate([x, y], axis=0)
    cols2 = lambda x, y: jnp.concatenate([x, y], axis=1)

    def cumsum(x):
        l1 = x.astype(BF16)
        rem = x - l1.astype(F32)
        l2 = rem.astype(BF16)
        l3 = (rem - l2.astype(F32)).astype(BF16)
        y = _dot(cum, jnp.concatenate([l1, l2, l3], axis=1))
        return y[:, :LANES] + (y[:, LANES:2 * LANES] + y[:, 2 * LANES:])

    cs = [cumsum(lw[j]) for j in js]
    g_inv = [jnp.exp(-cs[j]) for j in js]
    g_tot = [jnp.exp(jnp.sum(lw[j], axis=0, keepdims=True)) for j in js]
    ab = [-kk[j] * jnp.exp(cs[j] - lw[j]) for j in js]
    rb = [r[j] * jnp.exp(cs[j]) for j in js]
    bt = [be[j] * g_inv[j] for j in js]
    kt = [kd[j] * g_inv[j] for j in js]

    sc = [one(rows2(ab[j], rb[j]), rows2(bdg(bt[j]), bdg(kt[j])), NT) for j in js]
    m_ab = [jnp.where(strict2, sc[j][:c, :LANES], 0.0) for j in js]
    n_rb = [jnp.where(incl2, sc[j][c:, :LANES], 0.0) for j in js]
    m_ak = [jnp.where(strict2, sc[j][:c, LANES:], 0.0) for j in js]
    n_rk = [jnp.where(incl2, sc[j][c:, LANES:], 0.0) for j in js]

    levels = int(math.log2(c)) - 1
    t = [eye2 + m_ab[j] for j in js]
    p = [one(m_ab[j], bdg(m_ab[j])) for j in js]
    for _ in range(1, levels - 1):
        y = [one(rows2(p[j], t[j]), bdg(p[j])) for j in js]
        p = [y[j][:c] for j in js]
        t = [t[j] + y[j][c:] for j in js]
    t = [t[j] + one(t[j], bdg(p[j])) for j in js]

    def residual(a, tt):
        ah, al = _split(a)
        th, tl = _split(bdg(tt))
        y = _dot(rows2(ah, al), th)
        return eye2 - (y[:c] + (y[c:] + _dot(ah, tl)))

    res = [residual(eye2 - m_ab[j], t[j]) for j in js]
    t = [t[j] + one(t[j], bdg(res[j])) for j in js]

    vbd = [bdg(v[j]) for j in js]
    mv = [one(rows2(m_ak[j], n_rk[j]), vbd[j]) for j in js]
    tp = [one(t[j], cols2(bdg(ab[j]), bdg(mv[j][:c]))) for j in js]
    ap = [tp[j][:, :LANES] for j in js]
    pp = [tp[j][:, LANES:] for j in js]
    nr = [one(n_rb[j], cols2(bdg(ap[j]), bdg(pp[j]))) for j in js]
    for j in js:
        rp_ref[0, 0, 0, sls[j], :] = rb[j] + nr[j][:, :LANES]
        qp_ref[0, 0, 0, sls[j], :] = nr[j][:, LANES:] + mv[j][c:]
    heads = lambda x: jnp.where(first, x[:HEAD_DIM], x[HEAD_DIM:])
    for j in js:
        st = one(cols2(ap[j], pp[j]), bt[j], TN)
        g_ref[0, 0, 0, j] = (eye2 + heads(st[:LANES])) * g_tot[j]
        h_ref[0, 0, 0, j] = heads(st[LANES:] + one(v[j], kt[j], TN)) * g_tot[j]


def _rwkv_p1(pk, cpb=16):
    bsz, s, _ = pk.shape
    nc = s // CHUNK
    npair = RWKV_WIDTH // LANES
    rows = cpb * CHUNK

    def slab(sidx):
        return pl.BlockSpec((1, rows, LANES), lambda b, hp, d, ci: (b, ci, sidx * npair + hp))

    def dslab(off):
        return pl.BlockSpec((1, rows, LANES),
                            lambda b, hp, d, ci: (b, ci, (SLAB_DIR0 + 3 * d + off) * npair + hp))

    seq = pl.BlockSpec((1, 1, 1, rows, LANES), lambda b, hp, d, ci: (b, hp, d, ci, 0))
    mat = pl.BlockSpec((1, 1, 1, cpb, HEAD_DIM, LANES), lambda b, hp, d, ci: (b, hp, d, ci, 0, 0))
    return pl.pallas_call(
        functools.partial(_p1_kernel, cpb=cpb),
        out_shape=(jax.ShapeDtypeStruct((bsz, npair, 2, s, LANES), F32),
                   jax.ShapeDtypeStruct((bsz, npair, 2, s, LANES), F32),
                   jax.ShapeDtypeStruct((bsz, npair, 2, nc, HEAD_DIM, LANES), F32),
                   jax.ShapeDtypeStruct((bsz, npair, 2, nc, HEAD_DIM, LANES), F32)),
        grid=(bsz, npair, 2, nc // cpb),
        in_specs=[slab(SLAB_R), slab(SLAB_V), slab(SLAB_KK), dslab(0), dslab(1), dslab(2)],
        out_specs=(seq, seq, mat, mat),
        compiler_params=_params("arbitrary", "arbitrary", "arbitrary", "arbitrary"),
        name="rwkv_p1",
    )(pk, pk, pk, pk, pk, pk)


def _p2_kernel(rp_ref, qp_ref, g_ref, h_ref, gate_ref, bonus_ref, lng_ref, lnb_ref, bd_ref, o_ref,
               y_ref, s_ref, *, nc, te, hpb):
    c = CHUNK
    s_ref[...] = jnp.zeros_like(s_ref)
    first = lax.broadcasted_iota(jnp.int32, (HEAD_DIM, LANES), 1) < HEAD_DIM
    chains = [(hq, d) for hq in range(hpb) for d in range(2)]

    def step(j, _):
        for hq, d in chains:
            jc = j if d == 0 else nc - 1 - j
            sl = pl.ds(pl.multiple_of(jc * c, c), c)
            st = s_ref[hq, d]
            y_ref[hq, d, sl, :] = (_mm(rp_ref[0, hq, d, sl, :], st, NT, WKV_PASSES_CHAIN)
                                   + qp_ref[0, hq, d, sl, :])
            gmat = _block_diag(g_ref[0, hq, d, jc], first)
            hmat = _block_diag(h_ref[0, hq, d, jc], first)
            s_ref[hq, d] = _mm(st, gmat, NN, WKV_PASSES_CHAIN) + hmat
        return 0

    lax.fori_loop(0, nc, step, 0)

    bd = bd_ref[...]
    inv = 1.0 / HEAD_DIM

    def epi(i, _):
        sl = pl.ds(pl.multiple_of(i * te, te), te)
        for hq in range(hpb):
            lanes = slice(hq * LANES, (hq + 1) * LANES)
            y = y_ref[hq, 0, sl, :] + y_ref[hq, 1, sl, :]
            mean = _mm_exact_rhs(y, bd) * inv
            yc = y - mean
            var = _mm_exact_rhs(yc * yc, bd) * inv
            yn = yc * lax.rsqrt(var + LNX_EPS) * lng_ref[:, lanes] + lnb_ref[:, lanes]
            o_ref[0, sl, lanes] = ((yn + bonus_ref[0, sl, lanes]) * gate_ref[0, sl, lanes]).astype(o_ref.dtype)
        return 0

    lax.fori_loop(0, (nc * c) // te, epi, 0)


def _rwkv_p2(rp, qp, gm, hm, pk, lnx_g, lnx_b, bd128, te=1024, hpb=2):
    bsz, npair, _, s, _ = rp.shape
    nc = s // CHUNK
    width = hpb * LANES
    nq = npair // hpb
    seq = pl.BlockSpec((1, hpb, 2, s, LANES), lambda b, q: (b, q, 0, 0, 0))
    mat = pl.BlockSpec((1, hpb, 2, nc, HEAD_DIM, LANES), lambda b, q: (b, q, 0, 0, 0, 0))
    return pl.pallas_call(
        functools.partial(_p2_kernel, nc=nc, te=min(te, s), hpb=hpb),
        out_shape=jax.ShapeDtypeStruct((bsz, s, RWKV_WIDTH), BF16),
        grid=(bsz, nq),
        in_specs=[seq, seq, mat, mat,
                  pl.BlockSpec((1, s, width), lambda b, q: (b, 0, SLAB_G * nq + q)),
                  pl.BlockSpec((1, s, width), lambda b, q: (b, 0, SLAB_BONUS * nq + q)),
                  pl.BlockSpec((1, width), lambda b, q: (0, q)),
                  pl.BlockSpec((1, width), lambda b, q: (0, q)),
                  pl.BlockSpec((LANES, LANES), lambda b, q: (0, 0))],
        out_specs=pl.BlockSpec((1, s, width), lambda b, q: (b, 0, q)),
        scratch_shapes=[pltpu.VMEM((hpb, 2, s, LANES), F32), pltpu.VMEM((hpb, 2, LANES, LANES), F32)],
        compiler_params=_params("arbitrary", "arbitrary"),
        name="rwkv_p2",
    )(rp, qp, gm, hm, pk, pk, lnx_g, lnx_b, bd128)


def _outproj_kernel(att_ref, rw_ref, x_ref, mod_ref, g2_ref, wo_ref, wq_ref, k1_ref, k2_ref,
                    x1_ref, h2_ref, s1_ref, s2_ref):
    mod = mod_ref[0]
    a = ATT_WIDTH
    mixed = _dot(att_ref[0], wo_ref[0:a, :]) + _dot(rw_ref[0], wo_ref[a:, :])
    x1 = x_ref[0] + mod[2:3] * mixed
    x1_ref[0] = x1
    ms = jnp.mean(x1 * x1, axis=-1, keepdims=True)
    h2 = x1 * lax.rsqrt(ms + NORM_EPS) * g2_ref[...]
    h2 = (h2 * (1.0 + mod[4:5]) + mod[3:4]).astype(BF16)
    h2_ref[0] = h2
    q = _dot(h2, wq_ref[...])
    k1 = k1_ref[...]
    k2 = k2_ref[...]
    for h in range(PEER_HEADS):
        base = h * 2 * LANES
        s1 = _dot(k1, q[:, base:base + LANES].astype(BF16), NT)
        s2 = _dot(k2, q[:, base + LANES:base + 2 * LANES].astype(BF16), NT)
        for g in range(s1.shape[1] // LANES):
            s1_ref[h, g] = s1[:, g * LANES:(g + 1) * LANES]
            s2_ref[h, g] = s2[:, g * LANES:(g + 1) * LANES]


def _outproj(att, rw, x, mod6, norm2_g, w_out_bf, wq_bf, k1_bf, k2_bf, tm=512):
    bsz, s, d = x.shape
    nt = s // tm
    t = bsz * s
    nq = wq_bf.shape[1]
    const = lambda b, i: (0, 0)
    gpt = tm // LANES
    tok = pl.BlockSpec((PEER_HEADS, gpt, PEER_N_KEYS, LANES), lambda b, i: (0, b * nt + i, 0, 0))
    stat = jax.ShapeDtypeStruct((PEER_HEADS, t // LANES, PEER_N_KEYS, LANES), F32)
    return pl.pallas_call(
        _outproj_kernel,
        out_shape=(jax.ShapeDtypeStruct((bsz, s, d), F32),
                   jax.ShapeDtypeStruct((bsz, s, d), BF16),
                   stat, stat),
        grid=(bsz, nt),
        in_specs=[pl.BlockSpec((1, tm, ATT_WIDTH), lambda b, i: (b, i, 0)),
                  pl.BlockSpec((1, tm, RWKV_WIDTH), lambda b, i: (b, i, 0)),
                  pl.BlockSpec((1, tm, d), lambda b, i: (b, i, 0)),
                  pl.BlockSpec((1, 6, d), lambda b, i: (b, 0, 0)),
                  pl.BlockSpec((1, d), const),
                  pl.BlockSpec((d, d), const),
                  pl.BlockSpec((d, nq), const),
                  pl.BlockSpec((PEER_N_KEYS, LANES), const),
                  pl.BlockSpec((PEER_N_KEYS, LANES), const)],
        out_specs=(pl.BlockSpec((1, tm, d), lambda b, i: (b, i, 0)),
                   pl.BlockSpec((1, tm, d), lambda b, i: (b, i, 0)),
                   tok, tok),
        compiler_params=_params("arbitrary", "arbitrary"),
        name="outproj",
    )(att, rw, x, mod6, norm2_g, w_out_bf, wq_bf, k1_bf, k2_bf)


def _oddeven_sort_pairs(n):
    pairs = []
    p = 1
    while p < n:
        k = p
        while k >= 1:
            for j in range(k % p, n - k, 2 * k):
                for i in range(min(k, n - j - k)):
                    if (i + j) // (2 * p) == (i + j + k) // (2 * p):
                        pairs.append((i + j, i + j + k))
            k //= 2
        p *= 2
    return pairs


def _bitonic_merge_pairs(n):
    pairs = []
    stride = n // 2
    while stride >= 1:
        pairs += [(i, i + stride) for i in range(n) if not i & stride]
        stride //= 2
    return pairs


def _compare_exchange(vals, pairs):
    vals = list(vals)
    for i, j in pairs:
        a, b = vals[i], vals[j]
        if b is None:
            continue
        if a is None:
            vals[i], vals[j] = b, None
        else:
            vals[i], vals[j] = jnp.maximum(a, b), jnp.minimum(a, b)
    return vals


def _top16_sorted(vregs):
    kk = PEER_TOPK
    vals = list(vregs) + [None] * (kk - len(vregs))
    vals = _compare_exchange(vals, _oddeven_sort_pairs(kk))
    for shift in (4, 2, 1):
        other = [None if v is None else pltpu.roll(v, shift, axis=0) for v in vals]
        merged = []
        for k in range(kk):
            a, b = vals[k], other[kk - 1 - k]
            merged.append(b if a is None else a if b is None else jnp.maximum(a, b))
        vals = _compare_exchange(merged, _bitonic_merge_pairs(kk))
    return vals


def _topk_kernel(s1_ref, s2_ref, c1_ref, e1_ref, r2_ref, e2_ref, *, groups):
    kk = PEER_TOPK
    sub = 8
    neg = -jnp.inf
    rows8 = lax.broadcasted_iota(jnp.int32, (sub, LANES), 0)

    def rows_of(reps):
        out = reps[0]
        for r in range(1, sub):
            out = jnp.where(rows8 == r, reps[r], out)
        return out

    def all_sublanes_sum(x):
        for shift in (4, 2, 1):
            x = x + pltpu.roll(x, shift, axis=0)
        return x

    def per_tile(it_idx, _):
        g = it_idx // PEER_HEADS
        h = it_idx % PEER_HEADS
        s1 = [s1_ref[h, g, pl.ds(v * sub, sub), :] for v in range(PEER_N_KEYS // sub)]
        s2 = [s2_ref[h, g, pl.ds(v * sub, sub), :] for v in range(PEER_N_KEYS // sub)]
        a = _top16_sorted(s1)
        b = _top16_sorted(s2)
        a_lo, a_hi = rows_of(a[:sub]), rows_of(a[sub:])
        b_lo, b_hi = rows_of(b[:sub]), rows_of(b[sub:])

        cand = [a[0] + b_lo, a[0] + b_hi, a[1] + b_lo]
        for i in range(2, sub):
            cand.append(jnp.where(rows8 < kk // (i + 1), a[i] + b_lo, neg))
        cand.append(a_hi + b[0])
        top = _top16_sorted(cand)
        tau = top[kk - 1]
        z = None
        for t in top:
            ez = jnp.exp(t - top[0])
            z = ez if z is None else z + ez
        inv_z = 1.0 / z

        counts = []
        for r in range(kk):
            hits = (jnp.where(a[r] + b_lo >= tau, 1.0, 0.0) + jnp.where(a[r] + b_hi >= tau, 1.0, 0.0))
            counts.append(all_sublanes_sum(hits))
        for v in range(PEER_N_KEYS // sub):
            rows = pl.ds(v * sub, sub)
            cnt = jnp.zeros((sub, LANES), F32)
            rank = jnp.full((sub, LANES), float(kk), F32)
            for r in range(kk):
                cnt = jnp.where(s1[v] == a[r], counts[r], cnt)
                rank = jnp.where(s2[v] == b[r], float(r), rank)
            c1_ref[h, g, rows, :] = cnt
            e1_ref[h, g, rows, :] = jnp.exp(s1[v] - a[0]) * inv_z
            s2[v] = (rank, jnp.exp(s2[v] - b[0]))
        rank2 = jnp.concatenate([rv for rv, _ in s2], axis=0).astype(BF16)
        e2 = jnp.concatenate([ev for _, ev in s2], axis=0).astype(BF16)
        r2_ref[h, g] = pltpu.bitcast(rank2, jnp.uint32)
        e2_ref[h, g] = pltpu.bitcast(e2, jnp.uint32)
        return 0

    lax.fori_loop(0, groups * PEER_HEADS, per_tile, 0)


def _peer_topk(s1t, s2t, tn=512):
    nh, ng, nk, _ = s1t.shape
    groups = tn // LANES
    blk = pl.BlockSpec((nh, groups, nk, LANES), lambda i: (0, i, 0, 0))
    f32 = jax.ShapeDtypeStruct(s1t.shape, F32)
    b16 = jax.ShapeDtypeStruct((nh, ng, nk // 2, LANES), jnp.uint32)
    pblk = pl.BlockSpec((nh, groups, nk // 2, LANES), lambda i: (0, i, 0, 0))
    return pl.pallas_call(
        functools.partial(_topk_kernel, groups=groups),
        out_shape=(f32, f32, b16, b16),
        grid=(ng // groups,),
        in_specs=[blk, blk],
        out_specs=(blk, blk, pblk, pblk),
        compiler_params=_params("arbitrary"),
        name="peer_topk",
    )(s1t, s2t)


class _Pieces(list):
    every = 1
    phase = 0


def _peer_kernel(h2_ref, u_ref, vt_ref, c1_ref, e1_ref, r2_ref, e2_ref, x1_ref, mod_ref, o_ref,
                 acc_ref, act0_ref, act1_ref, p0_ref, p1_ref, *, tn, te, n_tiles):
    k = pl.program_id(0)
    n_items = pl.num_programs(0) - 2
    nk = PEER_N_KEYS
    ngroups = tn // LANES

    @pl.when(k == 0)
    def _():
        acc_ref[...] = jnp.zeros_like(acc_ref)
        for ref in (act0_ref, act1_ref, p0_ref, p1_ref):
            ref[...] = jnp.zeros_like(ref)

    ni = te // nk
    i0 = pl.multiple_of((jnp.clip(k - 1, 0, n_items - 1) % n_tiles) * ni, ni)
    tile_c = jnp.clip(k - 2, 0, n_items - 1) % n_tiles
    first_c = tile_c == 0

    d_model = acc_ref.shape[0]
    ksplit = 4
    halves = [(mh, nh, kh) for mh in range(2) for nh in range(ngroups // 2) for kh in range(ksplit)]

    def stage_c(p_a, mh, nh, kh):
        rows = slice(mh * (d_model // 2), (mh + 1) * (d_model // 2))
        cols = slice(nh * 2 * LANES, (nh + 1) * 2 * LANES)
        ks = slice(kh * (te // ksplit), (kh + 1) * (te // ksplit))
        p_prev = jnp.concatenate([p_a[2 * nh, ks, :], p_a[2 * nh + 1, ks, :]], axis=1)
        prev = acc_ref[rows, cols]
        if kh == 0:
            prev = jnp.where(first_c, jnp.zeros_like(prev), prev)
        acc_ref[rows, cols] = prev + _dot(vt_ref[rows, ks], p_prev)

    def stage_a(act_a, mh, nh, kh):
        rows = slice(mh * (te // 2), (mh + 1) * (te // 2))
        ks = slice(kh * (d_model // ksplit), (kh + 1) * (d_model // ksplit))
        act = _dot(u_ref[rows, ks], h2_ref[nh * 2 * LANES:(nh + 1) * 2 * LANES, ks], NT)
        if kh == 0:
            act_a[2 * nh, rows, :] = act[:, :LANES]
            act_a[2 * nh + 1, rows, :] = act[:, LANES:]
        else:
            act_a[2 * nh, rows, :] += act[:, :LANES]
            act_a[2 * nh + 1, rows, :] += act[:, LANES:]

    def rows_bf16(row):
        tile = jnp.broadcast_to(row, (BF16_ROWS, LANES)).astype(BF16)
        return jnp.concatenate([tile] * (nk // BF16_ROWS), axis=0)

    quad = 4

    def stage_b(act_b, p_b, tg, iq, pieces):
        gates = [jnp.zeros((nk, LANES), BF16) for _ in range(quad)]
        for h in range(PEER_HEADS):
            if h % pieces.every == pieces.phase and pieces:
                pieces.pop(0)()
            c1 = c1_ref[h, tg, pl.ds(i0, ni), :]
            e1 = e1_ref[h, tg, pl.ds(i0, ni), :]
            rank2 = pltpu.bitcast(r2_ref[h, tg], BF16)
            e2 = pltpu.bitcast(e2_ref[h, tg], BF16)
            for q in range(quad):
                il = iq * quad + q
                sel = rank2 < rows_bf16(c1[il:il + 1, :])
                gates[q] = gates[q] + jnp.where(sel, e2 * rows_bf16(e1[il:il + 1, :]), jnp.zeros_like(e2))
        for q in range(quad):
            il = iq * quad + q
            a = act_b[tg, il * nk:(il + 1) * nk, :]
            gelu = 0.5 * a * (1.0 + lax.erf(a * (2.0 ** -0.5)))
            p_b[tg, il * nk:(il + 1) * nk, :] = gates[q] * gelu.astype(BF16)

    def stages(act_a, act_b, p_a, p_b):
        pieces = _Pieces([functools.partial(stage_c, p_a, *hv) for hv in halves]
                         + [functools.partial(stage_a, act_a, *hv) for hv in halves])
        n_blocks = ngroups * (ni // quad)
        pieces.every = max(1, PEER_HEADS * n_blocks // len(pieces))
        pieces.phase = 0
        for tg in range(ngroups):
            for iq in range(ni // quad):
                stage_b(act_b, p_b, tg, iq, pieces)
        while pieces:
            pieces.pop(0)()

    @pl.when(k % 2 == 0)
    def _():
        stages(act0_ref, act1_ref, p0_ref, p1_ref)

    @pl.when(k % 2 == 1)
    def _():
        stages(act1_ref, act0_ref, p1_ref, p0_ref)

    @pl.when((tile_c == n_tiles - 1) & (k >= 2))
    def _():
        o_ref[...] = x1_ref[...] + mod_ref[0, 5:6, :] * acc_ref[...].T


def _peer_ffn(h2, u_bf, vt_bf, c1, e1, r2, e2, x1, mod6, seq, tn=512, te=1024):
    t, d = h2.shape
    ne = u_bf.shape[0]
    ngroups = tn // LANES
    n_tiles = ne // te
    n_items = (t // tn) * n_tiles

    def item(k, lag):
        w = jnp.clip(k - lag, 0, n_items - 1)
        return w // n_tiles, w % n_tiles

    stat = pl.BlockSpec((PEER_HEADS, ngroups, PEER_N_KEYS, LANES), lambda k: (0, item(k, 1)[0], 0, 0))
    pstat = pl.BlockSpec((PEER_HEADS, ngroups, PEER_N_KEYS // 2, LANES), lambda k: (0, item(k, 1)[0], 0, 0))
    return pl.pallas_call(
        functools.partial(_peer_kernel, tn=tn, te=te, n_tiles=n_tiles),
        out_shape=jax.ShapeDtypeStruct((t, d), F32),
        grid=(n_items + 2,),
        in_specs=[pl.BlockSpec((tn, d), lambda k: (item(k, 0)[0], 0)),
                  pl.BlockSpec((te, d), lambda k: (item(k, 0)[1], 0)),
                  pl.BlockSpec((d, te), lambda k: (0, item(k, 2)[1])),
                  stat, stat, pstat, pstat,
                  pl.BlockSpec((tn, d), lambda k: (item(k, 2)[0], 0)),
                  pl.BlockSpec((1, 6, d), lambda k: ((item(k, 2)[0] * tn) // seq, 0, 0))],
        out_specs=pl.BlockSpec((tn, d), lambda k: (item(k, 2)[0], 0)),
        scratch_shapes=[pltpu.VMEM((d, tn), F32),
                        pltpu.VMEM((ngroups, te, LANES), F32), pltpu.VMEM((ngroups, te, LANES), F32),
                        pltpu.VMEM((ngroups, te, LANES), BF16), pltpu.VMEM((ngroups, te, LANES), BF16)],
        compiler_params=_params("arbitrary"),
        name="peer_ffn",
    )(h2, u_bf, vt_bf, c1, e1, r2, e2, x1, mod6)


def _head_block_diag(n):
    idx = jnp.arange(n, dtype=jnp.int32) // HEAD_DIM
    return (idx[:, None] == idx[None, :]).astype(BF16)


def _layer(x, c, ada_w, ada_b, norm1_g, w_in, mu_prev, mu_next, q_norm_g, k_norm_g, w_decay0, w_decay_up,
           a_gate0, a_gate_up, g_up, k_k, k_a, r_k, lnx_g, lnx_b, w_out, norm2_g, peer_w_query,
           peer_sub_keys1, peer_sub_keys2, peer_u, peer_v):
    bsz, s, d = x.shape
    n_att_heads = ATT_WIDTH // HEAD_DIM
    bd512 = _head_block_diag(ATT_WIDTH)
    bd128 = _head_block_diag(LANES)

    mod6 = _ada(c, ada_w, ada_b).reshape(bsz, 6, d)
    gq = jnp.tile(q_norm_g, n_att_heads).reshape(1, ATT_WIDTH)
    gk = jnp.tile(k_norm_g, n_att_heads).reshape(1, ATT_WIDTH)
    qkv, zrw = _inproj(x, mod6, norm1_g.reshape(1, d), w_in.astype(BF16), gq, gk, bd512)

    att = _attention(qkv, n_att_heads)

    zero_w = jnp.zeros((2, LANES - DECAY_LORA, RWKV_WIDTH), F32)
    wup_pad = jnp.concatenate([w_decay_up, zero_w], axis=1).astype(BF16)
    aup_pad = jnp.concatenate([zero_w, a_gate_up], axis=1).astype(BF16)
    pk = _rwkv_prep(zrw, mu_prev.reshape(1, -1), mu_next.reshape(1, -1), wup_pad, aup_pad, g_up.astype(BF16),
                    w_decay0, a_gate0, k_k.reshape(1, -1), k_a.reshape(1, -1), r_k.reshape(1, -1), bd512)
    rp, qp, gm, hm = _rwkv_p1(pk)
    rw = _rwkv_p2(rp, qp, gm, hm, pk, lnx_g.reshape(1, -1), lnx_b.reshape(1, -1), bd128)

    x1, h2, s1t, s2t = _outproj(att, rw, x, mod6, norm2_g.reshape(1, d), w_out.astype(BF16),
                                peer_w_query.astype(BF16), peer_sub_keys1.astype(BF16),
                                peer_sub_keys2.astype(BF16))
    c1, e1, r2, e2 = _peer_topk(s1t, s2t)
    out = _peer_ffn(h2.reshape(bsz * s, d), peer_u.astype(BF16), peer_v.T.astype(BF16), c1, e1, r2, e2,
                    x1.reshape(bsz * s, d), mod6, s)
    return out.reshape(bsz, s, d)


def kernel(x, c, ada_w, ada_b, norm1_g, w_in, mu_prev, mu_next, q_norm_g, k_norm_g, w_decay0, w_decay_up,
           a_gate0, a_gate_up, g_up, k_k, k_a, r_k, lnx_g, lnx_b, w_out, norm2_g, peer_w_query,
           peer_sub_keys1, peer_sub_keys2, peer_u, peer_v):
    depth = ada_w.shape[0]
    for l in range(depth):
        x = _layer(x, c, ada_w[l], ada_b[l], norm1_g[l], w_in[l], mu_prev[l], mu_next[l], q_norm_g[l],
                   k_norm_g[l], w_decay0[l], w_decay_up[l], a_gate0[l], a_gate_up[l], g_up[l], k_k[l], k_a[l],
                   r_k[l], lnx_g[l], lnx_b[l], w_out[l], norm2_g[l], peer_w_query[l], peer_sub_keys1[l],
                   peer_sub_keys2[l], peer_u[l], peer_v[l])
    return x
```

```python
import functools
import math

import jax
import jax.numpy as jnp
from jax import lax
from jax.experimental import pallas as pl
from jax.experimental.pallas import tpu as pltpu

F32 = jnp.float32
BF16 = jnp.bfloat16

HEAD_DIM = 64
ATT_WIDTH = 512
RWKV_WIDTH = 512
DECAY_LORA = 64
AAA_LORA = 64
GATE_LORA = 128
SHIFT_WIDTH = 3 * RWKV_WIDTH + DECAY_LORA + AAA_LORA + GATE_LORA
PEER_N_KEYS = 128
PEER_HEADS = 8
PEER_TOPK = 16
NORM_EPS = 1e-6
LNX_EPS = 64e-5
MASK_VALUE = -1e30
LOG2E = 1.4426950408889634
ATT_HALF_WINDOWS = ((1, 64), (4, 256), (16, 1024))

LANES = 128
BF16_ROWS = 16
CHUNK = 64
ATT_BLOCK = 128
ATT_REACH = 1024 // ATT_BLOCK
VMEM_LIMIT_BYTES = 56 * 1024 * 1024
WKV_PASSES_CHAIN = 1

NN = (((1,), (0,)), ((), ()))
NT = (((1,), (1,)), ((), ()))
TN = (((0,), (0,)), ((), ()))

SLAB_R, SLAB_V, SLAB_KK, SLAB_DIR0, SLAB_G, SLAB_BONUS, NUM_SLABS = 0, 1, 2, 3, 7, 8, 9


def _params(*sem):
    return pltpu.CompilerParams(dimension_semantics=sem, vmem_limit_bytes=VMEM_LIMIT_BYTES)


def _dot(a, b, dims=NN):
    return lax.dot_general(a, b, dims, preferred_element_type=F32)


def _split(a):
    hi = a.astype(BF16)
    lo = (a - hi.astype(F32)).astype(BF16)
    return hi, lo


def _mm(a, b, dims=NN, passes=3):
    if passes == 1:
        return _dot(a.astype(BF16), b.astype(BF16), dims)
    ah, al = _split(a)
    bh, bl = _split(b)
    return _dot(ah, bh, dims) + (_dot(ah, bl, dims) + _dot(al, bh, dims))


def _mm_exact_rhs(a, b_exact):
    ah, al = _split(a)
    return _dot(ah, b_exact) + _dot(al, b_exact)


def _sigmoid(x):
    return 1.0 / (1.0 + jnp.exp(-x))


def _ada_kernel(c_ref, w_ref, b_ref, o_ref):
    c = c_ref[...]
    o_ref[...] = _mm(c * _sigmoid(c), w_ref[...]) + b_ref[...]


def _ada(c, ada_w, ada_b):
    bsz, d = c.shape
    n = ada_w.shape[1]
    tn = 1024
    return pl.pallas_call(
        _ada_kernel,
        out_shape=jax.ShapeDtypeStruct((bsz, n), F32),
        grid=(n // tn,),
        in_specs=[pl.BlockSpec((bsz, d), lambda j: (0, 0)),
                  pl.BlockSpec((d, tn), lambda j: (0, j)),
                  pl.BlockSpec((1, tn), lambda j: (0, j))],
        out_specs=pl.BlockSpec((bsz, tn), lambda j: (0, j)),
        compiler_params=_params("arbitrary"),
        name="ada",
    )(c, ada_w, ada_b.reshape(1, n))


def _inproj_kernel(x_ref, mod_ref, g1_ref, w_ref, gq_ref, gk_ref, bd_ref, qkv_ref, zrw_ref):
    x = x_ref[0]
    mod = mod_ref[0]
    ms = jnp.mean(x * x, axis=-1, keepdims=True)
    h = x * lax.rsqrt(ms + NORM_EPS) * g1_ref[...]
    h = (h * (1.0 + mod[1:2]) + mod[0:1]).astype(BF16)
    bd = bd_ref[...]

    def head_norm(z, g):
        ss = _mm_exact_rhs(z * z, bd) * (1.0 / HEAD_DIM)
        return z * lax.rsqrt(ss + NORM_EPS) * g

    a = ATT_WIDTH
    zq = _dot(h, w_ref[:, 0:a])
    qkv_ref[0, :, 0:a] = (head_norm(zq, gq_ref[...]) * (HEAD_DIM ** -0.5 * LOG2E)).astype(BF16)
    zk = _dot(h, w_ref[:, a:2 * a])
    qkv_ref[0, :, a:2 * a] = head_norm(zk, gk_ref[...]).astype(BF16)
    qkv_ref[0, :, 2 * a:3 * a] = _dot(h, w_ref[:, 2 * a:3 * a]).astype(BF16)
    zrw_ref[0] = _dot(h, w_ref[:, 3 * a:])


def _inproj(x, mod6, norm1_g, w_in_bf, gq, gk, bd512, tm=512):
    bsz, s, d = x.shape
    nin = w_in_bf.shape[1]
    return pl.pallas_call(
        _inproj_kernel,
        out_shape=(jax.ShapeDtypeStruct((bsz, s, 3 * ATT_WIDTH), BF16),
                   jax.ShapeDtypeStruct((bsz, s, SHIFT_WIDTH), F32)),
        grid=(bsz, s // tm),
        in_specs=[pl.BlockSpec((1, tm, d), lambda b, i: (b, i, 0)),
                  pl.BlockSpec((1, 6, d), lambda b, i: (b, 0, 0)),
                  pl.BlockSpec((1, d), lambda b, i: (0, 0)),
                  pl.BlockSpec((d, nin), lambda b, i: (0, 0)),
                  pl.BlockSpec((1, ATT_WIDTH), lambda b, i: (0, 0)),
                  pl.BlockSpec((1, ATT_WIDTH), lambda b, i: (0, 0)),
                  pl.BlockSpec((ATT_WIDTH, ATT_WIDTH), lambda b, i: (0, 0))],
        out_specs=(pl.BlockSpec((1, tm, 3 * ATT_WIDTH), lambda b, i: (b, i, 0)),
                   pl.BlockSpec((1, tm, SHIFT_WIDTH), lambda b, i: (b, i, 0))),
        compiler_params=_params("arbitrary", "arbitrary"),
        name="inproj",
    )(x, mod6, norm1_g, w_in_bf, gq, gk, bd512)


ATT_FAR_DILATION, ATT_FAR_HALF = ATT_HALF_WINDOWS[-1]
ATT_NEAR_BLOCKS = 6


def _attn_near_window(nkb):
    return min(nkb, ATT_NEAR_BLOCKS)


def _attn_near_start(qb, nkb):
    return jnp.clip(((qb - 2) // 2) * 2, 0, nkb - _attn_near_window(nkb))


def _attn_far_bias(n_heads, nkb):
    per_block = ATT_BLOCK // ATT_FAR_DILATION
    i = jnp.arange(ATT_BLOCK, dtype=jnp.int32)[:, None]
    ip = jnp.arange(ATT_BLOCK, dtype=jnp.int32)[None, :]
    start = _attn_near_start(i // per_block, nkb)
    kb = ip // per_block
    dist = ATT_FAR_DILATION * jnp.abs(i - ip)
    far = (dist <= ATT_FAR_HALF) & ~((kb >= start) & (kb < start + _attn_near_window(nkb)))
    slopes = jnp.exp2(-8.0 * (jnp.arange(n_heads, dtype=F32) + 1.0) / n_heads) * LOG2E
    return jnp.where(far[None], -slopes[:, None, None] * dist.astype(F32)[None], MASK_VALUE)


def _attn_bias_table(n_heads, nkb):
    reach = _attn_near_window(nkb) - 1
    nd = 2 * reach + 1
    dd = jnp.arange(nd, dtype=jnp.int32)[None, :, None]
    r = jnp.arange(ATT_BLOCK, dtype=jnp.int32)[:, None, None]
    c = jnp.arange(ATT_BLOCK, dtype=jnp.int32)[None, None, :]
    dt = r - c - (dd - reach) * ATT_BLOCK
    adt = jnp.abs(dt)
    mult = jnp.zeros(dt.shape, F32)
    for dil, half in ATT_HALF_WINDOWS:
        mult = mult + ((adt <= half) & (dt % dil == 0)).astype(F32)
    logm = jnp.where(mult > 0, jnp.log2(jnp.maximum(mult, 1.0)), MASK_VALUE)
    slopes = jnp.exp2(-8.0 * (jnp.arange(n_heads, dtype=F32) + 1.0) / n_heads) * LOG2E
    bias = logm[None] - slopes[:, None, None, None] * adt.astype(F32)[None]
    return bias.reshape(n_heads, ATT_BLOCK, nd * ATT_BLOCK)


def _attn_kernel(q_ref, k_ref, v_ref, bias_ref, far_ref, o_ref, s_ref, v1_ref, g_ref, accf_ref, mf_ref,
                 *, nkb, win):
    blk = ATT_BLOCK
    reach = win - 1
    has_far = win < nkb
    lane = lax.broadcasted_iota(jnp.int32, (blk, LANES), 1)
    first = lane < HEAD_DIM

    def two_heads(q):
        zero = jnp.zeros_like(q)
        return jnp.concatenate([jnp.where(first, q, zero), jnp.where(first, zero, q)], axis=0)

    def with_ones(v, hh):
        is_own = (lax.broadcasted_iota(jnp.int32, v.shape, 1) < HEAD_DIM) == (hh == 0)
        return jnp.where(is_own, v, jnp.ones_like(v))

    v = v_ref[0]
    for hh in range(2):
        v1_ref[hh] = with_ones(v, hh)

    if has_far:
        g_ref[0] = q_ref[0].astype(F32)
        g_ref[1] = k_ref[0].astype(F32)
        g_ref[2] = v.astype(F32)
        n_far = nkb * blk // ATT_FAR_DILATION

        def residue(r, _):
            rows = pl.ds(r, n_far, stride=ATT_FAR_DILATION)
            qr = g_ref[0, rows, :].astype(BF16)
            kr = g_ref[1, rows, :].astype(BF16)
            vr = g_ref[2, rows, :].astype(BF16)
            s = _dot(two_heads(qr), kr, NT)
            for hh in range(2):
                sb = s[hh * blk:(hh + 1) * blk] + far_ref[hh]
                mx = jnp.max(sb, axis=-1, keepdims=True)
                p = jnp.exp2(sb - mx).astype(BF16)
                accf_ref[hh, rows, :] = _dot(p, with_ones(vr, hh))
                mf_ref[hh, rows, :] = jnp.broadcast_to(mx, (blk, LANES))
            return 0

        lax.fori_loop(0, ATT_FAR_DILATION, residue, 0, unroll=4)

    group = s_ref.shape[0]
    slabs = range(0, win, 2)

    def query_block_group(i, _):
        us = range(group)
        qis = [group * i + u for u in us]
        rows = [pl.ds(pl.multiple_of(qi * blk, blk), blk) for qi in qis]
        qab = [two_heads(q_ref[0, rows[u], :]) for u in us]
        ks = [_attn_near_start(qi, nkb) if has_far else 0 for qi in qis]

        m = [[jnp.full((blk, LANES), -jnp.inf, F32) for _ in range(2)] for _ in us]
        for w0 in slabs:
            for u in us:
                kj0 = ks[u] + w0
                kslab = k_ref[0, pl.ds(pl.multiple_of(kj0 * blk, blk), 2 * blk), :]
                s = _dot(qab[u], kslab, NT)
                x0 = pl.multiple_of((kj0 - qis[u] + reach) * blk, blk)
                for hh in range(2):
                    sb = s[hh * blk:(hh + 1) * blk] + bias_ref[hh, :, pl.ds(x0, 2 * blk)]
                    s_ref[u, hh, :, w0 * blk:(w0 + 2) * blk] = sb
                    m[u][hh] = jnp.maximum(m[u][hh], jnp.maximum(sb[:, :blk], sb[:, blk:]))
        mrow = [[jnp.max(m[u][hh], axis=-1, keepdims=True) for hh in range(2)] for u in us]

        if has_far:
            acc = [[None, None] for _ in us]
            for u in us:
                for hh in range(2):
                    m_far = mf_ref[hh, rows[u], :][:, 0:1]
                    m_all = jnp.maximum(mrow[u][hh], m_far)
                    acc[u][hh] = accf_ref[hh, rows[u], :] * jnp.exp2(m_far - m_all)
                    mrow[u][hh] = m_all
        else:
            acc = [[jnp.zeros((blk, LANES), F32) for _ in range(2)] for _ in us]
        for w0 in slabs:
            for u in us:
                keys = pl.ds(pl.multiple_of((ks[u] + w0) * blk, blk), 2 * blk)
                for hh in range(2):
                    p = jnp.exp2(s_ref[u, hh, :, w0 * blk:(w0 + 2) * blk] - mrow[u][hh]).astype(BF16)
                    acc[u][hh] = acc[u][hh] + _dot(p, v1_ref[hh, keys, :])
        for u in us:
            num = jnp.where(first, acc[u][0], acc[u][1])
            den = jnp.where(first, pltpu.roll(acc[u][0], HEAD_DIM, axis=1), pltpu.roll(acc[u][1], HEAD_DIM, axis=1))
            o_ref[0, rows[u], :] = (num / den).astype(o_ref.dtype)
        return 0

    lax.fori_loop(0, nkb // group, query_block_group, 0)


def _attention(qkv, n_heads):
    bsz, s, _ = qkv.shape
    nkb = s // ATT_BLOCK
    win = _attn_near_window(nkb)
    assert nkb % 2 == 0 and (win == nkb or s // ATT_FAR_DILATION == ATT_BLOCK)
    bias = _attn_bias_table(n_heads, nkb)
    far = _attn_far_bias(n_heads, nkb)
    npair = ATT_WIDTH // LANES
    return pl.pallas_call(
        functools.partial(_attn_kernel, nkb=nkb, win=win),
        out_shape=jax.ShapeDtypeStruct((bsz, s, ATT_WIDTH), BF16),
        grid=(npair, bsz),
        in_specs=[pl.BlockSpec((1, s, LANES), lambda hp, b: (b, 0, hp)),
                  pl.BlockSpec((1, s, LANES), lambda hp, b: (b, 0, npair + hp)),
                  pl.BlockSpec((1, s, LANES), lambda hp, b: (b, 0, 2 * npair + hp)),
                  pl.BlockSpec((2, ATT_BLOCK, bias.shape[2]), lambda hp, b: (hp, 0, 0)),
                  pl.BlockSpec((2, ATT_BLOCK, ATT_BLOCK), lambda hp, b: (hp, 0, 0))],
        out_specs=pl.BlockSpec((1, s, LANES), lambda hp, b: (b, 0, hp)),
        scratch_shapes=[pltpu.VMEM((math.gcd(nkb, 4), 2, ATT_BLOCK, win * ATT_BLOCK), F32), pltpu.VMEM((2, s, LANES), BF16),
                        pltpu.VMEM((3, s, LANES), F32), pltpu.VMEM((2, s, LANES), F32),
                        pltpu.VMEM((2, s, LANES), F32)],
        compiler_params=_params("arbitrary", "arbitrary"),
        name="attn",
    )(qkv, qkv, qkv, bias, far)


def _prep_kernel(z_ref, zp_ref, zn_ref, mup_ref, mun_ref, wup_ref, aup_ref, gup_ref, w0_ref, a0_ref,
                 kk_ref, ka_ref, rk_ref, bd_ref, o_ref, lw_ref, *, tq):
    i = pl.program_id(1)
    last = pl.num_programs(1) - 1
    z = z_ref[0]
    row = lax.broadcasted_iota(jnp.int32, (tq, 1), 0)
    prev_row = zp_ref[0, 7:8, :] * (i > 0).astype(F32)
    next_row = zn_ref[0, 0:1, :] * (i < last).astype(F32)
    zp = jnp.where(row == 0, prev_row, pltpu.roll(z, 1, axis=0))
    zn = jnp.where(row == tq - 1, next_row, pltpu.roll(z, tq - 1, axis=0))
    zs = z + mup_ref[...] * (zp - z) + mun_ref[...] * (zn - z)

    w = RWKV_WIDTH
    r = zs[:, 0:w]
    k = zs[:, w:2 * w]
    v = zs[:, 2 * w:3 * w]
    xwa = zs[:, 3 * w:3 * w + LANES]
    xg = zs[:, 3 * w + LANES:]
    bd = bd_ref[...]

    g = _dot(_sigmoid(xg).astype(BF16), gup_ref[...])
    kk = k * kk_ref[...]
    ss = _mm_exact_rhs(kk * kk, bd)
    kk = kk * lax.rsqrt(jnp.maximum(ss, 1e-12))
    ka = ka_ref[...]
    txw = jnp.tanh(xwa).astype(BF16)
    xab = xwa.astype(BF16)

    def put(slab, val):
        o_ref[0, :, slab * w:(slab + 1) * w] = val.astype(o_ref.dtype)

    put(SLAB_R, r)
    put(SLAB_V, v)
    put(SLAB_KK, kk)
    put(SLAB_G, g)
    a_sum = jnp.zeros_like(k)
    for d in range(2):
        y = w0_ref[d:d + 1, :] + _dot(txw, wup_ref[d])
        wlog = -(jnp.maximum(-y, 0.0) + jnp.log(1.0 + jnp.exp(-jnp.abs(y)))) - 0.5
        a = _sigmoid(a0_ref[d:d + 1, :] + _dot(xab, aup_ref[d]))
        a_sum = a_sum + a
        lw_ref[0, :, d * w:(d + 1) * w] = -jnp.exp(wlog)
        put(SLAB_DIR0 + 2 * d, k * (1.0 + (a - 1.0) * ka))
        put(SLAB_DIR0 + 2 * d + 1, kk * a)
    k_bonus = k * (1.0 + (0.5 * a_sum - 1.0) * ka)
    bsum = _mm_exact_rhs(r * k_bonus * rk_ref[...], bd)
    put(SLAB_BONUS, bsum * v)


def _rwkv_prep(zrw, mu_prev, mu_next, wup_pad, aup_pad, g_up_bf, w0, a0, k_k, k_a, r_k, bd512, tq=512):
    bsz, s, sw = zrw.shape
    w = RWKV_WIDTH
    nt = s // tq
    row = lambda b, i: (0, 0)
    return pl.pallas_call(
        functools.partial(_prep_kernel, tq=tq),
        out_shape=(jax.ShapeDtypeStruct((bsz, s, NUM_SLABS * w), BF16),
                   jax.ShapeDtypeStruct((bsz, s, 2 * w), F32)),
        grid=(bsz, nt),
        in_specs=[pl.BlockSpec((1, tq, sw), lambda b, i: (b, i, 0)),
                  pl.BlockSpec((1, 8, sw), lambda b, i: (b, jnp.maximum(i * (tq // 8) - 1, 0), 0)),
                  pl.BlockSpec((1, 8, sw), lambda b, i: (b, jnp.minimum((i + 1) * (tq // 8), s // 8 - 1), 0)),
                  pl.BlockSpec((1, sw), row),
                  pl.BlockSpec((1, sw), row),
                  pl.BlockSpec((2, LANES, w), lambda b, i: (0, 0, 0)),
                  pl.BlockSpec((2, LANES, w), lambda b, i: (0, 0, 0)),
                  pl.BlockSpec((GATE_LORA, w), row),
                  pl.BlockSpec((2, w), row),
                  pl.BlockSpec((2, w), row),
                  pl.BlockSpec((1, w), row),
                  pl.BlockSpec((1, w), row),
                  pl.BlockSpec((1, w), row),
                  pl.BlockSpec((w, w), row)],
        out_specs=(pl.BlockSpec((1, tq, NUM_SLABS * w), lambda b, i: (b, i, 0)),
                   pl.BlockSpec((1, tq, 2 * w), lambda b, i: (b, i, 0))),
        compiler_params=_params("arbitrary", "arbitrary"),
        name="rwkv_prep",
    )(zrw, zrw, zrw, mu_prev, mu_next, wup_pad, aup_pad, g_up_bf, w0, a0, k_k, k_a, r_k, bd512)


def _block_diag(x, first):
    zero = jnp.zeros_like(x)
    return jnp.concatenate([jnp.where(first, x, zero), jnp.where(first, zero, x)], axis=0)


def _p1_kernel(r_ref, v_ref, kk_ref, lw_ref, kd_ref, be_ref, rp_ref, qp_ref, g_ref, h_ref, *, cpb):
    c = CHUNK
    sign = 1 - 2 * pl.program_id(2)
    rowi = lax.broadcasted_iota(jnp.int32, (c, LANES), 0)
    lane = lax.broadcasted_iota(jnp.int32, (c, LANES), 1)
    coli = lane & (c - 1)
    first = lane < HEAD_DIM
    before = sign * (rowi - coli)
    strict2 = before > 0
    incl2 = before >= 0
    eye2 = (coli == rowi).astype(F32)
    r64 = lax.broadcasted_iota(jnp.int32, (c, c), 0)
    c64 = lax.broadcasted_iota(jnp.int32, (c, c), 1)
    cum = (sign * (r64 - c64) >= 0).astype(BF16)

    js = range(cpb)
    sls = [pl.ds(j * c, c) for j in js]
    bdg = lambda x: _block_diag(x, first)
    r = [r_ref[0, sl, :].astype(F32) for sl in sls]
    v = [v_ref[0, sl, :].astype(F32) for sl in sls]
    kk = [kk_ref[0, sl, :].astype(F32) for sl in sls]
    lw = [lw_ref[0, sl, :] for sl in sls]
    kd = [kd_ref[0, sl, :].astype(F32) for sl in sls]
    be = [be_ref[0, sl, :].astype(F32) for sl in sls]

    one = lambda x, y, dims=NN: _mm(x, y, dims, 1)
    rows2 = lambda x, y: jnp.concatenate([x, y], axis=0)
    cols2 = lambda x, y: jnp.concatenate([x, y], axis=1)

    def cumsum(x):
        l1 = x.astype(BF16)
        rem = x - l1.astype(F32)
        l2 = rem.astype(BF16)
        l3 = (rem - l2.astype(F32)).astype(BF16)
        y = _dot(cum, jnp.concatenate([l1, l2, l3], axis=1))
        return y[:, :LANES] + (y[:, LANES:2 * LANES] + y[:, 2 * LANES:])

    cs = [cumsum(lw[j]) for j in js]
    g_inv = [jnp.exp(-cs[j]) for j in js]
    g_tot = [jnp.exp(jnp.sum(lw[j], axis=0, keepdims=True)) for j in js]
    ab = [-kk[j] * jnp.exp(cs[j] - lw[j]) for j in js]
    rb = [r[j] * jnp.exp(cs[j]) for j in js]
    bt = [be[j] * g_inv[j] for j in js]
    kt = [kd[j] * g_inv[j] for j in js]

    sc = [one(rows2(ab[j], rb[j]), rows2(bdg(bt[j]), bdg(kt[j])), NT) for j in js]
    m_ab = [jnp.where(strict2, sc[j][:c, :LANES], 0.0) for j in js]
    n_rb = [jnp.where(incl2, sc[j][c:, :LANES], 0.0) for j in js]
    m_ak = [jnp.where(strict2, sc[j][:c, LANES:], 0.0) for j in js]
    n_rk = [jnp.where(incl2, sc[j][c:, LANES:], 0.0) for j in js]

    levels = int(math.log2(c)) - 1
    t = [eye2 + m_ab[j] for j in js]
    p = [one(m_ab[j], bdg(m_ab[j])) for j in js]
    for _ in range(1, levels - 1):
        y = [one(rows2(p[j], t[j]), bdg(p[j])) for j in js]
        p = [y[j][:c] for j in js]
        t = [t[j] + y[j][c:] for j in js]
    t = [t[j] + one(t[j], bdg(p[j])) for j in js]

    def residual(a, tt):
        ah, al = _split(a)
        th, tl = _split(bdg(tt))
        y = _dot(rows2(ah, al), th)
        return eye2 - (y[:c] + (y[c:] + _dot(ah, tl)))

    res = [residual(eye2 - m_ab[j], t[j]) for j in js]
    t = [t[j] + one(t[j], bdg(res[j])) for j in js]

    vbd = [bdg(v[j]) for j in js]
    mv = [one(rows2(m_ak[j], n_rk[j]), vbd[j]) for j in js]
    tp = [one(t[j], cols2(bdg(ab[j]), bdg(mv[j][:c]))) for j in js]
    ap = [tp[j][:, :LANES] for j in js]
    pp = [tp[j][:, LANES:] for j in js]
    nr = [one(n_rb[j], cols2(bdg(ap[j]), bdg(pp[j]))) for j in js]
    for j in js:
        rp_ref[0, 0, 0, sls[j], :] = rb[j] + nr[j][:, :LANES]
        qp_ref[0, 0, 0, sls[j], :] = nr[j][:, LANES:] + mv[j][c:]
    heads = lambda x: jnp.where(first, x[:HEAD_DIM], x[HEAD_DIM:])
    for j in js:
        st = one(cols2(ap[j], pp[j]), bt[j], TN)
        g_ref[0, 0, 0, j] = (eye2 + heads(st[:LANES])) * g_tot[j]
        h_ref[0, 0, 0, j] = heads(st[LANES:] + one(v[j], kt[j], TN)) * g_tot[j]


def _rwkv_p1(pk, plw, cpb=16):
    bsz, s, _ = pk.shape
    nc = s // CHUNK
    npair = RWKV_WIDTH // LANES
    rows = cpb * CHUNK

    def slab(sidx):
        return pl.BlockSpec((1, rows, LANES), lambda b, hp, d, ci: (b, ci, sidx * npair + hp))

    def dslab(off):
        return pl.BlockSpec((1, rows, LANES),
                            lambda b, hp, d, ci: (b, ci, (SLAB_DIR0 + 2 * d + off) * npair + hp))

    lw_spec = pl.BlockSpec((1, rows, LANES), lambda b, hp, d, ci: (b, ci, d * npair + hp))

    seq = pl.BlockSpec((1, 1, 1, rows, LANES), lambda b, hp, d, ci: (b, hp, d, ci, 0))
    mat = pl.BlockSpec((1, 1, 1, cpb, HEAD_DIM, LANES), lambda b, hp, d, ci: (b, hp, d, ci, 0, 0))
    return pl.pallas_call(
        functools.partial(_p1_kernel, cpb=cpb),
        out_shape=(jax.ShapeDtypeStruct((bsz, npair, 2, s, LANES), F32),
                   jax.ShapeDtypeStruct((bsz, npair, 2, s, LANES), F32),
                   jax.ShapeDtypeStruct((bsz, npair, 2, nc, HEAD_DIM, LANES), F32),
                   jax.ShapeDtypeStruct((bsz, npair, 2, nc, HEAD_DIM, LANES), F32)),
        grid=(bsz, npair, 2, nc // cpb),
        in_specs=[slab(SLAB_R), slab(SLAB_V), slab(SLAB_KK), lw_spec, dslab(0), dslab(1)],
        out_specs=(seq, seq, mat, mat),
        compiler_params=_params("arbitrary", "arbitrary", "arbitrary", "arbitrary"),
        name="rwkv_p1",
    )(pk, pk, pk, plw, pk, pk)


def _p2_kernel(rp_ref, qp_ref, g_ref, h_ref, gate_ref, bonus_ref, lng_ref, lnb_ref, bd_ref, o_ref,
               y_ref, s_ref, *, nc, te, hpb):
    c = CHUNK
    s_ref[...] = jnp.zeros_like(s_ref)
    first = lax.broadcasted_iota(jnp.int32, (HEAD_DIM, LANES), 1) < HEAD_DIM
    chains = [(hq, d) for hq in range(hpb) for d in range(2)]

    def step(j, _):
        for hq, d in chains:
            jc = j if d == 0 else nc - 1 - j
            sl = pl.ds(pl.multiple_of(jc * c, c), c)
            st = s_ref[hq, d]
            y_ref[hq, d, sl, :] = (_mm(rp_ref[0, hq, d, sl, :], st, NT, WKV_PASSES_CHAIN)
                                   + qp_ref[0, hq, d, sl, :])
            gmat = _block_diag(g_ref[0, hq, d, jc], first)
            hmat = _block_diag(h_ref[0, hq, d, jc], first)
            s_ref[hq, d] = _mm(st, gmat, NN, WKV_PASSES_CHAIN) + hmat
        return 0

    lax.fori_loop(0, nc, step, 0)

    bd = bd_ref[...]
    inv = 1.0 / HEAD_DIM

    def epi(i, _):
        sl = pl.ds(pl.multiple_of(i * te, te), te)
        for hq in range(hpb):
            lanes = slice(hq * LANES, (hq + 1) * LANES)
            y = y_ref[hq, 0, sl, :] + y_ref[hq, 1, sl, :]
            mean = _mm_exact_rhs(y, bd) * inv
            yc = y - mean
            var = _mm_exact_rhs(yc * yc, bd) * inv
            yn = yc * lax.rsqrt(var + LNX_EPS) * lng_ref[:, lanes] + lnb_ref[:, lanes]
            o_ref[0, sl, lanes] = ((yn + bonus_ref[0, sl, lanes].astype(F32))
                                   * gate_ref[0, sl, lanes].astype(F32)).astype(o_ref.dtype)
        return 0

    lax.fori_loop(0, (nc * c) // te, epi, 0)


def _rwkv_p2(rp, qp, gm, hm, pk, lnx_g, lnx_b, bd128, te=1024, hpb=2):
    bsz, npair, _, s, _ = rp.shape
    nc = s // CHUNK
    width = hpb * LANES
    nq = npair // hpb
    seq = pl.BlockSpec((1, hpb, 2, s, LANES), lambda b, q: (b, q, 0, 0, 0))
    mat = pl.BlockSpec((1, hpb, 2, nc, HEAD_DIM, LANES), lambda b, q: (b, q, 0, 0, 0, 0))
    return pl.pallas_call(
        functools.partial(_p2_kernel, nc=nc, te=min(te, s), hpb=hpb),
        out_shape=jax.ShapeDtypeStruct((bsz, s, RWKV_WIDTH), BF16),
        grid=(bsz, nq),
        in_specs=[seq, seq, mat, mat,
                  pl.BlockSpec((1, s, width), lambda b, q: (b, 0, SLAB_G * nq + q)),
                  pl.BlockSpec((1, s, width), lambda b, q: (b, 0, SLAB_BONUS * nq + q)),
                  pl.BlockSpec((1, width), lambda b, q: (0, q)),
                  pl.BlockSpec((1, width), lambda b, q: (0, q)),
                  pl.BlockSpec((LANES, LANES), lambda b, q: (0, 0))],
        out_specs=pl.BlockSpec((1, s, width), lambda b, q: (b, 0, q)),
        scratch_shapes=[pltpu.VMEM((hpb, 2, s, LANES), F32), pltpu.VMEM((hpb, 2, LANES, LANES), F32)],
        compiler_params=_params("arbitrary", "arbitrary"),
        name="rwkv_p2",
    )(rp, qp, gm, hm, pk, pk, lnx_g, lnx_b, bd128)


def _outproj_kernel(att_ref, rw_ref, x_ref, mod_ref, g2_ref, wo_ref, wq_ref, k1_ref, k2_ref,
                    x1_ref, h2_ref, s1_ref, s2_ref):
    mod = mod_ref[0]
    a = ATT_WIDTH
    mixed = _dot(att_ref[0], wo_ref[0:a, :]) + _dot(rw_ref[0], wo_ref[a:, :])
    x1 = x_ref[0] + mod[2:3] * mixed
    x1_ref[0] = x1
    ms = jnp.mean(x1 * x1, axis=-1, keepdims=True)
    h2 = x1 * lax.rsqrt(ms + NORM_EPS) * g2_ref[...]
    h2 = (h2 * (1.0 + mod[4:5]) + mod[3:4]).astype(BF16)
    h2_ref[0] = h2
    q = _dot(h2, wq_ref[...])
    k1 = k1_ref[...]
    k2 = k2_ref[...]
    for h in range(PEER_HEADS):
        base = h * 2 * LANES
        s1 = _dot(k1, q[:, base:base + LANES].astype(BF16), NT)
        s2 = _dot(k2, q[:, base + LANES:base + 2 * LANES].astype(BF16), NT)
        for g in range(s1.shape[1] // LANES):
            s1_ref[h, g] = s1[:, g * LANES:(g + 1) * LANES]
            s2_ref[h, g] = s2[:, g * LANES:(g + 1) * LANES]


def _outproj(att, rw, x, mod6, norm2_g, w_out_bf, wq_bf, k1_bf, k2_bf, tm=512):
    bsz, s, d = x.shape
    nt = s // tm
    t = bsz * s
    nq = wq_bf.shape[1]
    const = lambda b, i: (0, 0)
    gpt = tm // LANES
    tok = pl.BlockSpec((PEER_HEADS, gpt, PEER_N_KEYS, LANES), lambda b, i: (0, b * nt + i, 0, 0))
    stat = jax.ShapeDtypeStruct((PEER_HEADS, t // LANES, PEER_N_KEYS, LANES), F32)
    return pl.pallas_call(
        _outproj_kernel,
        out_shape=(jax.ShapeDtypeStruct((bsz, s, d), F32),
                   jax.ShapeDtypeStruct((bsz, s, d), BF16),
                   stat, stat),
        grid=(bsz, nt),
        in_specs=[pl.BlockSpec((1, tm, ATT_WIDTH), lambda b, i: (b, i, 0)),
                  pl.BlockSpec((1, tm, RWKV_WIDTH), lambda b, i: (b, i, 0)),
                  pl.BlockSpec((1, tm, d), lambda b, i: (b, i, 0)),
                  pl.BlockSpec((1, 6, d), lambda b, i: (b, 0, 0)),
                  pl.BlockSpec((1, d), const),
                  pl.BlockSpec((d, d), const),
                  pl.BlockSpec((d, nq), const),
                  pl.BlockSpec((PEER_N_KEYS, LANES), const),
                  pl.BlockSpec((PEER_N_KEYS, LANES), const)],
        out_specs=(pl.BlockSpec((1, tm, d), lambda b, i: (b, i, 0)),
                   pl.BlockSpec((1, tm, d), lambda b, i: (b, i, 0)),
                   tok, tok),
        compiler_params=_params("arbitrary", "arbitrary"),
        name="outproj",
    )(att, rw, x, mod6, norm2_g, w_out_bf, wq_bf, k1_bf, k2_bf)


def _oddeven_sort_pairs(n):
    pairs = []
    p = 1
    while p < n:
        k = p
        while k >= 1:
            for j in range(k % p, n - k, 2 * k):
                for i in range(min(k, n - j - k)):
                    if (i + j) // (2 * p) == (i + j + k) // (2 * p):
                        pairs.append((i + j, i + j + k))
            k //= 2
        p *= 2
    return pairs


def _bitonic_merge_pairs(n):
    pairs = []
    stride = n // 2
    while stride >= 1:
        pairs += [(i, i + stride) for i in range(n) if not i & stride]
        stride //= 2
    return pairs


def _compare_exchange(vals, pairs):
    vals = list(vals)
    for i, j in pairs:
        a, b = vals[i], vals[j]
        if b is None:
            continue
        if a is None:
            vals[i], vals[j] = b, None
        else:
            vals[i], vals[j] = jnp.maximum(a, b), jnp.minimum(a, b)
    return vals


def _top16_sorted(vregs):
    kk = PEER_TOPK
    vals = list(vregs) + [None] * (kk - len(vregs))
    vals = _compare_exchange(vals, _oddeven_sort_pairs(kk))
    for shift in (4, 2, 1):
        other = [None if v is None else pltpu.roll(v, shift, axis=0) for v in vals]
        merged = []
        for k in range(kk):
            a, b = vals[k], other[kk - 1 - k]
            merged.append(b if a is None else a if b is None else jnp.maximum(a, b))
        vals = _compare_exchange(merged, _bitonic_merge_pairs(kk))
    return vals


def _topk_kernel(s1_ref, s2_ref, c1_ref, e1_ref, r2_ref, e2_ref, *, groups):
    kk = PEER_TOPK
    sub = 8
    neg = -jnp.inf
    rows8 = lax.broadcasted_iota(jnp.int32, (sub, LANES), 0)

    def rows_of(reps):
        out = reps[0]
        for r in range(1, sub):
            out = jnp.where(rows8 == r, reps[r], out)
        return out

    def all_sublanes_sum(x):
        for shift in (4, 2, 1):
            x = x + pltpu.roll(x, shift, axis=0)
        return x

    def per_tile(it_idx, _):
        g = it_idx // PEER_HEADS
        h = it_idx % PEER_HEADS
        s1 = [s1_ref[h, g, pl.ds(v * sub, sub), :] for v in range(PEER_N_KEYS // sub)]
        s2 = [s2_ref[h, g, pl.ds(v * sub, sub), :] for v in range(PEER_N_KEYS // sub)]
        a = _top16_sorted(s1)
        b = _top16_sorted(s2)
        a_lo, a_hi = rows_of(a[:sub]), rows_of(a[sub:])
        b_lo, b_hi = rows_of(b[:sub]), rows_of(b[sub:])

        cand = [a[0] + b_lo, a[0] + b_hi, a[1] + b_lo]
        for i in range(2, sub):
            cand.append(jnp.where(rows8 < kk // (i + 1), a[i] + b_lo, neg))
        cand.append(a_hi + b[0])
        top = _top16_sorted(cand)
        tau = top[kk - 1]
        z = None
        for t in top:
            ez = jnp.exp(t - top[0])
            z = ez if z is None else z + ez
        inv_z = 1.0 / z

        counts = []
        for r in range(kk):
            hits = (jnp.where(a[r] + b_lo >= tau, 1.0, 0.0) + jnp.where(a[r] + b_hi >= tau, 1.0, 0.0))
            counts.append(all_sublanes_sum(hits))
        for v in range(PEER_N_KEYS // sub):
            rows = pl.ds(v * sub, sub)
            cnt = jnp.zeros((sub, LANES), F32)
            rank = jnp.full((sub, LANES), float(kk), F32)
            for r in range(kk):
                cnt = jnp.where(s1[v] == a[r], counts[r], cnt)
                rank = jnp.where(s2[v] == b[r], float(r), rank)
            c1_ref[h, g, rows, :] = cnt
            e1_ref[h, g, rows, :] = jnp.exp(s1[v] - a[0]) * inv_z
            s2[v] = (rank, jnp.exp(s2[v] - b[0]))
        rank2 = jnp.concatenate([rv for rv, _ in s2], axis=0).astype(BF16)
        e2 = jnp.concatenate([ev for _, ev in s2], axis=0).astype(BF16)
        r2_ref[h, g] = pltpu.bitcast(rank2, jnp.uint32)
        e2_ref[h, g] = pltpu.bitcast(e2, jnp.uint32)
        return 0

    lax.fori_loop(0, groups * PEER_HEADS, per_tile, 0)


def _peer_topk(s1t, s2t, tn=512):
    nh, ng, nk, _ = s1t.shape
    groups = tn // LANES
    blk = pl.BlockSpec((nh, groups, nk, LANES), lambda i: (0, i, 0, 0))
    f32 = jax.ShapeDtypeStruct(s1t.shape, F32)
    b16 = jax.ShapeDtypeStruct((nh, ng, nk // 2, LANES), jnp.uint32)
    pblk = pl.BlockSpec((nh, groups, nk // 2, LANES), lambda i: (0, i, 0, 0))
    return pl.pallas_call(
        functools.partial(_topk_kernel, groups=groups),
        out_shape=(f32, f32, b16, b16),
        grid=(ng // groups,),
        in_specs=[blk, blk],
        out_specs=(blk, blk, pblk, pblk),
        compiler_params=_params("arbitrary"),
        name="peer_topk",
    )(s1t, s2t)


class _Pieces(list):
    every = 1
    phase = 0


def _peer_kernel(h2_ref, u_ref, vt_ref, c1_ref, e1_ref, r2_ref, e2_ref, x1_ref, mod_ref, o_ref,
                 acc_ref, act0_ref, act1_ref, p0_ref, p1_ref, *, tn, te, n_tiles):
    k = pl.program_id(0)
    n_items = pl.num_programs(0) - 2
    nk = PEER_N_KEYS
    ngroups = tn // LANES

    @pl.when(k == 0)
    def _():
        acc_ref[...] = jnp.zeros_like(acc_ref)
        for ref in (act0_ref, act1_ref, p0_ref, p1_ref):
            ref[...] = jnp.zeros_like(ref)

    ni = te // nk
    i0 = pl.multiple_of((jnp.clip(k - 1, 0, n_items - 1) % n_tiles) * ni, ni)
    tile_c = jnp.clip(k - 2, 0, n_items - 1) % n_tiles
    first_c = tile_c == 0

    d_model = acc_ref.shape[0]
    ksplit = 4
    msplit = 2
    halves = [(mh, nh, kh) for mh in range(msplit) for nh in range(ngroups // 2) for kh in range(ksplit)]

    def stage_c(p_a, mh, nh, kh):
        rows = slice(mh * (d_model // msplit), (mh + 1) * (d_model // msplit))
        cols = slice(nh * 2 * LANES, (nh + 1) * 2 * LANES)
        ks = slice(kh * (te // ksplit), (kh + 1) * (te // ksplit))
        p_prev = jnp.concatenate([p_a[2 * nh, ks, :], p_a[2 * nh + 1, ks, :]], axis=1)
        prev = acc_ref[rows, cols]
        if kh == 0:
            prev = jnp.where(first_c, jnp.zeros_like(prev), prev)
        acc_ref[rows, cols] = prev + _dot(vt_ref[rows, ks], p_prev)

    def stage_a(act_a, mh, nh, kh):
        rows = slice(mh * (te // msplit), (mh + 1) * (te // msplit))
        ks = slice(kh * (d_model // ksplit), (kh + 1) * (d_model // ksplit))
        act = _dot(u_ref[rows, ks], h2_ref[nh * 2 * LANES:(nh + 1) * 2 * LANES, ks], NT)
        if kh == 0:
            act_a[2 * nh, rows, :] = act[:, :LANES]
            act_a[2 * nh + 1, rows, :] = act[:, LANES:]
        else:
            act_a[2 * nh, rows, :] += act[:, :LANES]
            act_a[2 * nh + 1, rows, :] += act[:, LANES:]

    def rows_bf16(row):
        tile = jnp.broadcast_to(row, (BF16_ROWS, LANES)).astype(BF16)
        return jnp.concatenate([tile] * (nk // BF16_ROWS), axis=0)

    quad = 4

    def stage_b(act_b, p_b, tg, iq, pieces):
        gates = [jnp.zeros((nk, LANES), BF16) for _ in range(quad)]
        for h in range(PEER_HEADS):
            if h % pieces.every == pieces.phase and pieces:
                pieces.pop(0)()
            c1 = c1_ref[h, tg, pl.ds(i0, ni), :]
            e1 = e1_ref[h, tg, pl.ds(i0, ni), :]
            rank2 = pltpu.bitcast(r2_ref[h, tg], BF16)
            e2 = pltpu.bitcast(e2_ref[h, tg], BF16)
            for q in range(quad):
                il = iq * quad + q
                sel = rank2 < rows_bf16(c1[il:il + 1, :])
                gates[q] = gates[q] + jnp.where(sel, e2 * rows_bf16(e1[il:il + 1, :]), jnp.zeros_like(e2))
        for q in range(quad):
            il = iq * quad + q
            a = act_b[tg, il * nk:(il + 1) * nk, :]
            gelu = 0.5 * a * (1.0 + lax.erf(a * (2.0 ** -0.5)))
            p_b[tg, il * nk:(il + 1) * nk, :] = gates[q] * gelu.astype(BF16)

    def stages(act_a, act_b, p_a, p_b):
        pieces = _Pieces([functools.partial(stage_c, p_a, *hv) for hv in halves]
                         + [functools.partial(stage_a, act_a, *hv) for hv in halves])
        n_blocks = ngroups * (ni // quad)
        pieces.every = max(1, PEER_HEADS * n_blocks // len(pieces))
        pieces.phase = 0
        for tg in range(ngroups):
            for iq in range(ni // quad):
                stage_b(act_b, p_b, tg, iq, pieces)
        while pieces:
            pieces.pop(0)()

    @pl.when(k % 2 == 0)
    def _():
        stages(act0_ref, act1_ref, p0_ref, p1_ref)

    @pl.when(k % 2 == 1)
    def _():
        stages(act1_ref, act0_ref, p1_ref, p0_ref)

    @pl.when((tile_c == n_tiles - 1) & (k >= 2))
    def _():
        o_ref[...] = x1_ref[...] + mod_ref[0, 5:6, :] * acc_ref[...].T


def _peer_ffn(h2, u_bf, vt_bf, c1, e1, r2, e2, x1, mod6, seq, tn=512, te=1024):
    t, d = h2.shape
    ne = u_bf.shape[0]
    ngroups = tn // LANES
    n_tiles = ne // te
    n_items = (t // tn) * n_tiles

    def item(k, lag):
        w = jnp.clip(k - lag, 0, n_items - 1)
        return w // n_tiles, w % n_tiles

    stat = pl.BlockSpec((PEER_HEADS, ngroups, PEER_N_KEYS, LANES), lambda k: (0, item(k, 1)[0], 0, 0))
    pstat = pl.BlockSpec((PEER_HEADS, ngroups, PEER_N_KEYS // 2, LANES), lambda k: (0, item(k, 1)[0], 0, 0))
    return pl.pallas_call(
        functools.partial(_peer_kernel, tn=tn, te=te, n_tiles=n_tiles),
        out_shape=jax.ShapeDtypeStruct((t, d), F32),
        grid=(n_items + 2,),
        in_specs=[pl.BlockSpec((tn, d), lambda k: (item(k, 0)[0], 0)),
                  pl.BlockSpec((te, d), lambda k: (item(k, 0)[1], 0)),
                  pl.BlockSpec((d, te), lambda k: (0, item(k, 2)[1])),
                  stat, stat, pstat, pstat,
                  pl.BlockSpec((tn, d), lambda k: (item(k, 2)[0], 0)),
                  pl.BlockSpec((1, 6, d), lambda k: ((item(k, 2)[0] * tn) // seq, 0, 0))],
        out_specs=pl.BlockSpec((tn, d), lambda k: (item(k, 2)[0], 0)),
        scratch_shapes=[pltpu.VMEM((d, tn), F32),
                        pltpu.VMEM((ngroups, te, LANES), F32), pltpu.VMEM((ngroups, te, LANES), F32),
                        pltpu.VMEM((ngroups, te, LANES), BF16), pltpu.VMEM((ngroups, te, LANES), BF16)],
        compiler_params=_params("arbitrary"),
        name="peer_ffn",
    )(h2, u_bf, vt_bf, c1, e1, r2, e2, x1, mod6)


def _head_block_diag(n):
    idx = jnp.arange(n, dtype=jnp.int32) // HEAD_DIM
    return (idx[:, None] == idx[None, :]).astype(BF16)


def _layer(x, c, ada_w, ada_b, norm1_g, w_in, mu_prev, mu_next, q_norm_g, k_norm_g, w_decay0, w_decay_up,
           a_gate0, a_gate_up, g_up, k_k, k_a, r_k, lnx_g, lnx_b, w_out, norm2_g, peer_w_query,
           peer_sub_keys1, peer_sub_keys2, peer_u, peer_v):
    bsz, s, d = x.shape
    n_att_heads = ATT_WIDTH // HEAD_DIM
    bd512 = _head_block_diag(ATT_WIDTH)
    bd128 = _head_block_diag(LANES)

    mod6 = _ada(c, ada_w, ada_b).reshape(bsz, 6, d)
    gq = jnp.tile(q_norm_g, n_att_heads).reshape(1, ATT_WIDTH)
    gk = jnp.tile(k_norm_g, n_att_heads).reshape(1, ATT_WIDTH)
    qkv, zrw = _inproj(x, mod6, norm1_g.reshape(1, d), w_in.astype(BF16), gq, gk, bd512)

    att = _attention(qkv, n_att_heads)

    zero_w = jnp.zeros((2, LANES - DECAY_LORA, RWKV_WIDTH), F32)
    wup_pad = jnp.concatenate([w_decay_up, zero_w], axis=1).astype(BF16)
    aup_pad = jnp.concatenate([zero_w, a_gate_up], axis=1).astype(BF16)
    pk, plw = _rwkv_prep(zrw, mu_prev.reshape(1, -1), mu_next.reshape(1, -1), wup_pad, aup_pad, g_up.astype(BF16),
                         w_decay0, a_gate0, k_k.reshape(1, -1), k_a.reshape(1, -1), r_k.reshape(1, -1), bd512)
    rp, qp, gm, hm = _rwkv_p1(pk, plw)
    rw = _rwkv_p2(rp, qp, gm, hm, pk, lnx_g.reshape(1, -1), lnx_b.reshape(1, -1), bd128)

    x1, h2, s1t, s2t = _outproj(att, rw, x, mod6, norm2_g.reshape(1, d), w_out.astype(BF16),
                                peer_w_query.astype(BF16), peer_sub_keys1.astype(BF16),
                                peer_sub_keys2.astype(BF16))
    c1, e1, r2, e2 = _peer_topk(s1t, s2t)
    out = _peer_ffn(h2.reshape(bsz * s, d), peer_u.astype(BF16), peer_v.T.astype(BF16), c1, e1, r2, e2,
                    x1.reshape(bsz * s, d), mod6, s)
    return out.reshape(bsz, s, d)


def kernel(x, c, ada_w, ada_b, norm1_g, w_in, mu_prev, mu_next, q_norm_g, k_norm_g, w_decay0, w_decay_up,
           a_gate0, a_gate_up, g_up, k_k, k_a, r_k, lnx_g, lnx_b, w_out, norm2_g, peer_w_query,
           peer_sub_keys1, peer_sub_keys2, peer_u, peer_v):
    depth = ada_w.shape[0]
    for l in range(depth):
        x = _layer(x, c, ada_w[l], ada_b[l], norm1_g[l], w_in[l], mu_prev[l], mu_next[l], q_norm_g[l],
                   k_norm_g[l], w_decay0[l], w_decay_up[l], a_gate0[l], a_gate_up[l], g_up[l], k_k[l], k_a[l],
                   r_k[l], lnx_g[l], lnx_b[l], w_out[l], norm2_g[l], peer_w_query[l], peer_sub_keys1[l],
                   peer_sub_keys2[l], peer_u[l], peer_v[l])
    return x
```

```python
import functools
import math

import jax
import jax.numpy as jnp
from jax import lax
from jax.experimental import pallas as pl
from jax.experimental.pallas import tpu as pltpu

F32 = jnp.float32
BF16 = jnp.bfloat16

HEAD_DIM = 64
ATT_WIDTH = 512
RWKV_WIDTH = 512
DECAY_LORA = 64
AAA_LORA = 64
GATE_LORA = 128
SHIFT_WIDTH = 3 * RWKV_WIDTH + DECAY_LORA + AAA_LORA + GATE_LORA
PEER_N_KEYS = 128
PEER_HEADS = 8
PEER_TOPK = 16
NORM_EPS = 1e-6
LNX_EPS = 64e-5
MASK_VALUE = -1e30
LOG2E = 1.4426950408889634
ATT_HALF_WINDOWS = ((1, 64), (4, 256), (16, 1024))

LANES = 128
BF16_ROWS = 16
CHUNK = 64
ATT_BLOCK = 128
ATT_REACH = 1024 // ATT_BLOCK
VMEM_LIMIT_BYTES = 56 * 1024 * 1024
WKV_PASSES_CHAIN = 1

NN = (((1,), (0,)), ((), ()))
NT = (((1,), (1,)), ((), ()))
TN = (((0,), (0,)), ((), ()))

SLAB_R, SLAB_V, SLAB_KK, SLAB_DIR0, SLAB_G, SLAB_BONUS, NUM_SLABS = 0, 1, 2, 3, 7, 8, 9


def _params(*sem):
    return pltpu.CompilerParams(dimension_semantics=sem, vmem_limit_bytes=VMEM_LIMIT_BYTES)


def _dot(a, b, dims=NN):
    return lax.dot_general(a, b, dims, preferred_element_type=F32)


def _split(a):
    hi = a.astype(BF16)
    lo = (a - hi.astype(F32)).astype(BF16)
    return hi, lo


def _mm(a, b, dims=NN, passes=3):
    if passes == 1:
        return _dot(a.astype(BF16), b.astype(BF16), dims)
    ah, al = _split(a)
    bh, bl = _split(b)
    return _dot(ah, bh, dims) + (_dot(ah, bl, dims) + _dot(al, bh, dims))


def _mm_exact_rhs(a, b_exact):
    ah, al = _split(a)
    return _dot(ah, b_exact) + _dot(al, b_exact)


def _sigmoid(x):
    return 1.0 / (1.0 + jnp.exp(-x))


def _ada_kernel(c_ref, w_ref, b_ref, o_ref):
    c = c_ref[...]
    o_ref[...] = _mm(c * _sigmoid(c), w_ref[...]) + b_ref[...]


def _ada(c, ada_w, ada_b):
    bsz, d = c.shape
    n = ada_w.shape[1]
    tn = 1024
    return pl.pallas_call(
        _ada_kernel,
        out_shape=jax.ShapeDtypeStruct((bsz, n), F32),
        grid=(n // tn,),
        in_specs=[pl.BlockSpec((bsz, d), lambda j: (0, 0)),
                  pl.BlockSpec((d, tn), lambda j: (0, j)),
                  pl.BlockSpec((1, tn), lambda j: (0, j))],
        out_specs=pl.BlockSpec((bsz, tn), lambda j: (0, j)),
        compiler_params=_params("arbitrary"),
        name="ada",
    )(c, ada_w, ada_b.reshape(1, n))


def _inproj_kernel(x_ref, mod_ref, g1_ref, w_ref, gq_ref, gk_ref, bd_ref, qkv_ref, zrw_ref):
    x = x_ref[0]
    mod = mod_ref[0]
    ms = jnp.mean(x * x, axis=-1, keepdims=True)
    h = x * lax.rsqrt(ms + NORM_EPS) * g1_ref[...]
    h = (h * (1.0 + mod[1:2]) + mod[0:1]).astype(BF16)
    bd = bd_ref[...]

    def head_norm(z, g):
        ss = _mm_exact_rhs(z * z, bd) * (1.0 / HEAD_DIM)
        return z * lax.rsqrt(ss + NORM_EPS) * g

    a = ATT_WIDTH
    zq = _dot(h, w_ref[:, 0:a])
    qkv_ref[0, :, 0:a] = (head_norm(zq, gq_ref[...]) * (HEAD_DIM ** -0.5 * LOG2E)).astype(BF16)
    zk = _dot(h, w_ref[:, a:2 * a])
    qkv_ref[0, :, a:2 * a] = head_norm(zk, gk_ref[...]).astype(BF16)
    qkv_ref[0, :, 2 * a:3 * a] = _dot(h, w_ref[:, 2 * a:3 * a]).astype(BF16)
    zrw_ref[0] = _dot(h, w_ref[:, 3 * a:])


def _inproj(x, mod6, norm1_g, w_in_bf, gq, gk, bd512, tm=512):
    bsz, s, d = x.shape
    nin = w_in_bf.shape[1]
    return pl.pallas_call(
        _inproj_kernel,
        out_shape=(jax.ShapeDtypeStruct((bsz, s, 3 * ATT_WIDTH), BF16),
                   jax.ShapeDtypeStruct((bsz, s, SHIFT_WIDTH), F32)),
        grid=(bsz, s // tm),
        in_specs=[pl.BlockSpec((1, tm, d), lambda b, i: (b, i, 0)),
                  pl.BlockSpec((1, 6, d), lambda b, i: (b, 0, 0)),
                  pl.BlockSpec((1, d), lambda b, i: (0, 0)),
                  pl.BlockSpec((d, nin), lambda b, i: (0, 0)),
                  pl.BlockSpec((1, ATT_WIDTH), lambda b, i: (0, 0)),
                  pl.BlockSpec((1, ATT_WIDTH), lambda b, i: (0, 0)),
                  pl.BlockSpec((ATT_WIDTH, ATT_WIDTH), lambda b, i: (0, 0))],
        out_specs=(pl.BlockSpec((1, tm, 3 * ATT_WIDTH), lambda b, i: (b, i, 0)),
                   pl.BlockSpec((1, tm, SHIFT_WIDTH), lambda b, i: (b, i, 0))),
        compiler_params=_params("arbitrary", "arbitrary"),
        name="inproj",
    )(x, mod6, norm1_g, w_in_bf, gq, gk, bd512)


ATT_FAR_DILATION, ATT_FAR_HALF = ATT_HALF_WINDOWS[-1]
ATT_NEAR_BLOCKS = 6


def _attn_near_window(nkb):
    return min(nkb, ATT_NEAR_BLOCKS)


def _attn_near_start(qb, nkb):
    return jnp.clip(((qb - 2) // 2) * 2, 0, nkb - _attn_near_window(nkb))


def _attn_far_bias(n_heads, nkb):
    per_block = ATT_BLOCK // ATT_FAR_DILATION
    i = jnp.arange(ATT_BLOCK, dtype=jnp.int32)[:, None]
    ip = jnp.arange(ATT_BLOCK, dtype=jnp.int32)[None, :]
    start = _attn_near_start(i // per_block, nkb)
    kb = ip // per_block
    dist = ATT_FAR_DILATION * jnp.abs(i - ip)
    far = (dist <= ATT_FAR_HALF) & ~((kb >= start) & (kb < start + _attn_near_window(nkb)))
    slopes = jnp.exp2(-8.0 * (jnp.arange(n_heads, dtype=F32) + 1.0) / n_heads) * LOG2E
    return jnp.where(far[None], -slopes[:, None, None] * dist.astype(F32)[None], MASK_VALUE)


def _attn_bias_table(n_heads, nkb):
    reach = _attn_near_window(nkb) - 1
    nd = 2 * reach + 1
    dd = jnp.arange(nd, dtype=jnp.int32)[None, :, None]
    r = jnp.arange(ATT_BLOCK, dtype=jnp.int32)[:, None, None]
    c = jnp.arange(ATT_BLOCK, dtype=jnp.int32)[None, None, :]
    dt = r - c - (dd - reach) * ATT_BLOCK
    adt = jnp.abs(dt)
    mult = jnp.zeros(dt.shape, F32)
    for dil, half in ATT_HALF_WINDOWS:
        mult = mult + ((adt <= half) & (dt % dil == 0)).astype(F32)
    logm = jnp.where(mult > 0, jnp.log2(jnp.maximum(mult, 1.0)), MASK_VALUE)
    slopes = jnp.exp2(-8.0 * (jnp.arange(n_heads, dtype=F32) + 1.0) / n_heads) * LOG2E
    bias = logm[None] - slopes[:, None, None, None] * adt.astype(F32)[None]
    return bias.reshape(n_heads, ATT_BLOCK, nd * ATT_BLOCK)


def _attn_kernel(q_ref, k_ref, v_ref, bias_ref, far_ref, o_ref, s_ref, v1_ref, g_ref, accf_ref, mf_ref,
                 *, nkb, win):
    blk = ATT_BLOCK
    reach = win - 1
    has_far = win < nkb
    lane = lax.broadcasted_iota(jnp.int32, (blk, LANES), 1)
    first = lane < HEAD_DIM

    def two_heads(q):
        zero = jnp.zeros_like(q)
        return jnp.concatenate([jnp.where(first, q, zero), jnp.where(first, zero, q)], axis=0)

    def with_ones(v, hh):
        is_own = (lax.broadcasted_iota(jnp.int32, v.shape, 1) < HEAD_DIM) == (hh == 0)
        return jnp.where(is_own, v, jnp.ones_like(v))

    v = v_ref[0]
    for hh in range(2):
        v1_ref[hh] = with_ones(v, hh)

    if has_far:
        g_ref[0] = q_ref[0].astype(F32)
        g_ref[1] = k_ref[0].astype(F32)
        g_ref[2] = v.astype(F32)
        n_far = nkb * blk // ATT_FAR_DILATION

        def residue(r, _):
            rows = pl.ds(r, n_far, stride=ATT_FAR_DILATION)
            qr = g_ref[0, rows, :].astype(BF16)
            kr = g_ref[1, rows, :].astype(BF16)
            vr = g_ref[2, rows, :].astype(BF16)
            s = _dot(two_heads(qr), kr, NT)
            for hh in range(2):
                sb = s[hh * blk:(hh + 1) * blk] + far_ref[hh]
                mx = jnp.max(sb, axis=-1, keepdims=True)
                p = jnp.exp2(sb - mx).astype(BF16)
                accf_ref[hh, rows, :] = _dot(p, with_ones(vr, hh))
                mf_ref[hh, rows, :] = jnp.broadcast_to(mx, (blk, LANES))
            return 0

        lax.fori_loop(0, ATT_FAR_DILATION, residue, 0, unroll=4)

    group = s_ref.shape[0]
    slabs = range(0, win, 2)

    def query_block_group(i, _):
        us = range(group)
        qis = [group * i + u for u in us]
        rows = [pl.ds(pl.multiple_of(qi * blk, blk), blk) for qi in qis]
        qab = [two_heads(q_ref[0, rows[u], :]) for u in us]
        ks = [_attn_near_start(qi, nkb) if has_far else 0 for qi in qis]

        m = [[jnp.full((blk, LANES), -jnp.inf, F32) for _ in range(2)] for _ in us]
        for w0 in slabs:
            for u in us:
                kj0 = ks[u] + w0
                kslab = k_ref[0, pl.ds(pl.multiple_of(kj0 * blk, blk), 2 * blk), :]
                s = _dot(qab[u], kslab, NT)
                x0 = pl.multiple_of((kj0 - qis[u] + reach) * blk, blk)
                for hh in range(2):
                    sb = s[hh * blk:(hh + 1) * blk] + bias_ref[hh, :, pl.ds(x0, 2 * blk)]
                    s_ref[u, hh, :, w0 * blk:(w0 + 2) * blk] = sb
                    m[u][hh] = jnp.maximum(m[u][hh], jnp.maximum(sb[:, :blk], sb[:, blk:]))
        mrow = [[jnp.max(m[u][hh], axis=-1, keepdims=True) for hh in range(2)] for u in us]

        if has_far:
            acc = [[None, None] for _ in us]
            for u in us:
                for hh in range(2):
                    m_far = mf_ref[hh, rows[u], :][:, 0:1]
                    m_all = jnp.maximum(mrow[u][hh], m_far)
                    acc[u][hh] = accf_ref[hh, rows[u], :] * jnp.exp2(m_far - m_all)
                    mrow[u][hh] = m_all
        else:
            acc = [[jnp.zeros((blk, LANES), F32) for _ in range(2)] for _ in us]
        for w0 in slabs:
            for u in us:
                keys = pl.ds(pl.multiple_of((ks[u] + w0) * blk, blk), 2 * blk)
                for hh in range(2):
                    p = jnp.exp2(s_ref[u, hh, :, w0 * blk:(w0 + 2) * blk] - mrow[u][hh]).astype(BF16)
                    acc[u][hh] = acc[u][hh] + _dot(p, v1_ref[hh, keys, :])
        for u in us:
            num = jnp.where(first, acc[u][0], acc[u][1])
            den = jnp.where(first, pltpu.roll(acc[u][0], HEAD_DIM, axis=1), pltpu.roll(acc[u][1], HEAD_DIM, axis=1))
            o_ref[0, rows[u], :] = (num / den).astype(o_ref.dtype)
        return 0

    lax.fori_loop(0, nkb // group, query_block_group, 0)


def _attention(qkv, n_heads):
    bsz, s, _ = qkv.shape
    nkb = s // ATT_BLOCK
    win = _attn_near_window(nkb)
    assert nkb % 2 == 0 and (win == nkb or s // ATT_FAR_DILATION == ATT_BLOCK)
    bias = _attn_bias_table(n_heads, nkb)
    far = _attn_far_bias(n_heads, nkb)
    npair = ATT_WIDTH // LANES
    return pl.pallas_call(
        functools.partial(_attn_kernel, nkb=nkb, win=win),
        out_shape=jax.ShapeDtypeStruct((bsz, s, ATT_WIDTH), BF16),
        grid=(npair, bsz),
        in_specs=[pl.BlockSpec((1, s, LANES), lambda hp, b: (b, 0, hp)),
                  pl.BlockSpec((1, s, LANES), lambda hp, b: (b, 0, npair + hp)),
                  pl.BlockSpec((1, s, LANES), lambda hp, b: (b, 0, 2 * npair + hp)),
                  pl.BlockSpec((2, ATT_BLOCK, bias.shape[2]), lambda hp, b: (hp, 0, 0)),
                  pl.BlockSpec((2, ATT_BLOCK, ATT_BLOCK), lambda hp, b: (hp, 0, 0))],
        out_specs=pl.BlockSpec((1, s, LANES), lambda hp, b: (b, 0, hp)),
        scratch_shapes=[pltpu.VMEM((math.gcd(nkb, 4), 2, ATT_BLOCK, win * ATT_BLOCK), F32), pltpu.VMEM((2, s, LANES), BF16),
                        pltpu.VMEM((3, s, LANES), F32), pltpu.VMEM((2, s, LANES), F32),
                        pltpu.VMEM((2, s, LANES), F32)],
        compiler_params=_params("arbitrary", "arbitrary"),
        name="attn",
    )(qkv, qkv, qkv, bias, far)


def _prep_kernel(z_ref, zp_ref, zn_ref, mup_ref, mun_ref, wup_ref, aup_ref, gup_ref, w0_ref, a0_ref,
                 kk_ref, ka_ref, rk_ref, bd_ref, o_ref, lw_ref, *, tq):
    i = pl.program_id(1)
    last = pl.num_programs(1) - 1
    z = z_ref[0]
    row = lax.broadcasted_iota(jnp.int32, (tq, 1), 0)
    prev_row = zp_ref[0, 7:8, :] * (i > 0).astype(F32)
    next_row = zn_ref[0, 0:1, :] * (i < last).astype(F32)
    zp = jnp.where(row == 0, prev_row, pltpu.roll(z, 1, axis=0))
    zn = jnp.where(row == tq - 1, next_row, pltpu.roll(z, tq - 1, axis=0))
    zs = z + mup_ref[...] * (zp - z) + mun_ref[...] * (zn - z)

    w = RWKV_WIDTH
    r = zs[:, 0:w]
    k = zs[:, w:2 * w]
    v = zs[:, 2 * w:3 * w]
    xwa = zs[:, 3 * w:3 * w + LANES]
    xg = zs[:, 3 * w + LANES:]
    bd = bd_ref[...]

    g = _dot(_sigmoid(xg).astype(BF16), gup_ref[...])
    kk = k * kk_ref[...]
    ss = _mm_exact_rhs(kk * kk, bd)
    kk = kk * lax.rsqrt(jnp.maximum(ss, 1e-12))
    ka = ka_ref[...]
    txw = jnp.tanh(xwa).astype(BF16)
    xab = xwa.astype(BF16)

    def put(slab, val):
        o_ref[0, :, slab * w:(slab + 1) * w] = val.astype(o_ref.dtype)

    put(SLAB_R, r)
    put(SLAB_V, v)
    put(SLAB_KK, kk)
    put(SLAB_G, g)
    a_sum = jnp.zeros_like(k)
    for d in range(2):
        y = w0_ref[d:d + 1, :] + _dot(txw, wup_ref[d])
        wlog = -(jnp.maximum(-y, 0.0) + jnp.log(1.0 + jnp.exp(-jnp.abs(y)))) - 0.5
        a = _sigmoid(a0_ref[d:d + 1, :] + _dot(xab, aup_ref[d]))
        a_sum = a_sum + a
        lw_ref[0, :, d * w:(d + 1) * w] = -jnp.exp(wlog)
        put(SLAB_DIR0 + 2 * d, k * (1.0 + (a - 1.0) * ka))
        put(SLAB_DIR0 + 2 * d + 1, kk * a)
    k_bonus = k * (1.0 + (0.5 * a_sum - 1.0) * ka)
    bsum = _mm_exact_rhs(r * k_bonus * rk_ref[...], bd)
    put(SLAB_BONUS, bsum * v)


def _rwkv_prep(zrw, mu_prev, mu_next, wup_pad, aup_pad, g_up_bf, w0, a0, k_k, k_a, r_k, bd512, tq=512):
    bsz, s, sw = zrw.shape
    w = RWKV_WIDTH
    nt = s // tq
    row = lambda b, i: (0, 0)
    return pl.pallas_call(
        functools.partial(_prep_kernel, tq=tq),
        out_shape=(jax.ShapeDtypeStruct((bsz, s, NUM_SLABS * w), BF16),
                   jax.ShapeDtypeStruct((bsz, s, 2 * w), F32)),
        grid=(bsz, nt),
        in_specs=[pl.BlockSpec((1, tq, sw), lambda b, i: (b, i, 0)),
                  pl.BlockSpec((1, 8, sw), lambda b, i: (b, jnp.maximum(i * (tq // 8) - 1, 0), 0)),
                  pl.BlockSpec((1, 8, sw), lambda b, i: (b, jnp.minimum((i + 1) * (tq // 8), s // 8 - 1), 0)),
                  pl.BlockSpec((1, sw), row),
                  pl.BlockSpec((1, sw), row),
                  pl.BlockSpec((2, LANES, w), lambda b, i: (0, 0, 0)),
                  pl.BlockSpec((2, LANES, w), lambda b, i: (0, 0, 0)),
                  pl.BlockSpec((GATE_LORA, w), row),
                  pl.BlockSpec((2, w), row),
                  pl.BlockSpec((2, w), row),
                  pl.BlockSpec((1, w), row),
                  pl.BlockSpec((1, w), row),
                  pl.BlockSpec((1, w), row),
                  pl.BlockSpec((w, w), row)],
        out_specs=(pl.BlockSpec((1, tq, NUM_SLABS * w), lambda b, i: (b, i, 0)),
                   pl.BlockSpec((1, tq, 2 * w), lambda b, i: (b, i, 0))),
        compiler_params=_params("arbitrary", "arbitrary"),
        name="rwkv_prep",
    )(zrw, zrw, zrw, mu_prev, mu_next, wup_pad, aup_pad, g_up_bf, w0, a0, k_k, k_a, r_k, bd512)


def _block_diag(x, first):
    zero = jnp.zeros_like(x)
    return jnp.concatenate([jnp.where(first, x, zero), jnp.where(first, zero, x)], axis=0)


def _p1_kernel(r_ref, v_ref, kk_ref, lw_ref, kd_ref, be_ref, rp_ref, qp_ref, g_ref, h_ref, *, cpb):
    c = CHUNK
    sign = 1 - 2 * pl.program_id(2)
    rowi = lax.broadcasted_iota(jnp.int32, (c, LANES), 0)
    lane = lax.broadcasted_iota(jnp.int32, (c, LANES), 1)
    coli = lane & (c - 1)
    first = lane < HEAD_DIM
    before = sign * (rowi - coli)
    strict2 = before > 0
    incl2 = before >= 0
    eye2 = (coli == rowi).astype(F32)
    r64 = lax.broadcasted_iota(jnp.int32, (c, c), 0)
    c64 = lax.broadcasted_iota(jnp.int32, (c, c), 1)
    cum = (sign * (r64 - c64) >= 0).astype(BF16)

    js = range(cpb)
    sls = [pl.ds(j * c, c) for j in js]
    bdg = lambda x: _block_diag(x, first)
    r = [r_ref[0, sl, :].astype(F32) for sl in sls]
    v = [v_ref[0, sl, :].astype(F32) for sl in sls]
    kk = [kk_ref[0, sl, :].astype(F32) for sl in sls]
    lw = [lw_ref[0, sl, :] for sl in sls]
    kd = [kd_ref[0, sl, :].astype(F32) for sl in sls]
    be = [be_ref[0, sl, :].astype(F32) for sl in sls]

    one = lambda x, y, dims=NN: _mm(x, y, dims, 1)
    rows2 = lambda x, y: jnp.concatenate([x, y], axis=0)
    cols2 = lambda x, y: jnp.concatenate([x, y], axis=1)

    def cumsum(x):
        l1 = x.astype(BF16)
        rem = x - l1.astype(F32)
        l2 = rem.astype(BF16)
        l3 = (rem - l2.astype(F32)).astype(BF16)
        y = _dot(cum, jnp.concatenate([l1, l2, l3], axis=1))
        return y[:, :LANES] + (y[:, LANES:2 * LANES] + y[:, 2 * LANES:])

    cs = [cumsum(lw[j]) for j in js]
    g_inv = [jnp.exp(-cs[j]) for j in js]
    g_tot = [jnp.exp(jnp.sum(lw[j], axis=0, keepdims=True)) for j in js]
    ab = [-kk[j] * jnp.exp(cs[j] - lw[j]) for j in js]
    rb = [r[j] * jnp.exp(cs[j]) for j in js]
    bt = [be[j] * g_inv[j] for j in js]
    kt = [kd[j] * g_inv[j] for j in js]

    sc = [one(rows2(ab[j], rb[j]), rows2(bdg(bt[j]), bdg(kt[j])), NT) for j in js]
    m_ab = [jnp.where(strict2, sc[j][:c, :LANES], 0.0) for j in js]
    n_rb = [jnp.where(incl2, sc[j][c:, :LANES], 0.0) for j in js]
    m_ak = [jnp.where(strict2, sc[j][:c, LANES:], 0.0) for j in js]
    n_rk = [jnp.where(incl2, sc[j][c:, LANES:], 0.0) for j in js]

    levels = int(math.log2(c)) - 1
    t = [eye2 + m_ab[j] for j in js]
    p = [one(m_ab[j], bdg(m_ab[j])) for j in js]
    for _ in range(1, levels - 1):
        y = [one(rows2(p[j], t[j]), bdg(p[j])) for j in js]
        p = [y[j][:c] for j in js]
        t = [t[j] + y[j][c:] for j in js]
    t = [t[j] + one(t[j], bdg(p[j])) for j in js]

    def residual(a, tt):
        ah, al = _split(a)
        th, tl = _split(bdg(tt))
        y = _dot(rows2(ah, al), th)
        return eye2 - (y[:c] + (y[c:] + _dot(ah, tl)))

    res = [residual(eye2 - m_ab[j], t[j]) for j in js]
    t = [t[j] + one(t[j], bdg(res[j])) for j in js]

    vbd = [bdg(v[j]) for j in js]
    mv = [one(rows2(m_ak[j], n_rk[j]), vbd[j]) for j in js]
    tp = [one(t[j], cols2(bdg(ab[j]), bdg(mv[j][:c]))) for j in js]
    ap = [tp[j][:, :LANES] for j in js]
    pp = [tp[j][:, LANES:] for j in js]
    nr = [one(n_rb[j], cols2(bdg(ap[j]), bdg(pp[j]))) for j in js]
    for j in js:
        rp_ref[0, 0, 0, sls[j], :] = rb[j] + nr[j][:, :LANES]
        qp_ref[0, 0, 0, sls[j], :] = nr[j][:, LANES:] + mv[j][c:]
    heads = lambda x: jnp.where(first, x[:HEAD_DIM], x[HEAD_DIM:])
    for j in js:
        st = one(cols2(ap[j], pp[j]), bt[j], TN)
        g_ref[0, 0, 0, j] = (eye2 + heads(st[:LANES])) * g_tot[j]
        h_ref[0, 0, 0, j] = heads(st[LANES:] + one(v[j], kt[j], TN)) * g_tot[j]


def _rwkv_p1(pk, plw, cpb=16):
    bsz, s, _ = pk.shape
    nc = s // CHUNK
    npair = RWKV_WIDTH // LANES
    rows = cpb * CHUNK

    def slab(sidx):
        return pl.BlockSpec((1, rows, LANES), lambda b, hp, d, ci: (b, ci, sidx * npair + hp))

    def dslab(off):
        return pl.BlockSpec((1, rows, LANES),
                            lambda b, hp, d, ci: (b, ci, (SLAB_DIR0 + 2 * d + off) * npair + hp))

    lw_spec = pl.BlockSpec((1, rows, LANES), lambda b, hp, d, ci: (b, ci, d * npair + hp))

    seq = pl.BlockSpec((1, 1, 1, rows, LANES), lambda b, hp, d, ci: (b, hp, d, ci, 0))
    mat = pl.BlockSpec((1, 1, 1, cpb, HEAD_DIM, LANES), lambda b, hp, d, ci: (b, hp, d, ci, 0, 0))
    return pl.pallas_call(
        functools.partial(_p1_kernel, cpb=cpb),
        out_shape=(jax.ShapeDtypeStruct((bsz, npair, 2, s, LANES), F32),
                   jax.ShapeDtypeStruct((bsz, npair, 2, s, LANES), F32),
                   jax.ShapeDtypeStruct((bsz, npair, 2, nc, HEAD_DIM, LANES), F32),
                   jax.ShapeDtypeStruct((bsz, npair, 2, nc, HEAD_DIM, LANES), F32)),
        grid=(bsz, npair, 2, nc // cpb),
        in_specs=[slab(SLAB_R), slab(SLAB_V), slab(SLAB_KK), lw_spec, dslab(0), dslab(1)],
        out_specs=(seq, seq, mat, mat),
        compiler_params=_params("arbitrary", "arbitrary", "arbitrary", "arbitrary"),
        name="rwkv_p1",
    )(pk, pk, pk, plw, pk, pk)


def _p2_kernel(rp_ref, qp_ref, g_ref, h_ref, gate_ref, bonus_ref, lng_ref, lnb_ref, bd_ref, o_ref,
               y_ref, s_ref, *, nc, te, hpb):
    c = CHUNK
    s_ref[...] = jnp.zeros_like(s_ref)
    first = lax.broadcasted_iota(jnp.int32, (HEAD_DIM, LANES), 1) < HEAD_DIM
    chains = [(hq, d) for hq in range(hpb) for d in range(2)]

    def step(j, _):
        for hq, d in chains:
            jc = j if d == 0 else nc - 1 - j
            sl = pl.ds(pl.multiple_of(jc * c, c), c)
            st = s_ref[hq, d]
            y_ref[hq, d, sl, :] = (_mm(rp_ref[0, hq, d, sl, :], st, NT, WKV_PASSES_CHAIN)
                                   + qp_ref[0, hq, d, sl, :])
            gmat = _block_diag(g_ref[0, hq, d, jc], first)
            hmat = _block_diag(h_ref[0, hq, d, jc], first)
            s_ref[hq, d] = _mm(st, gmat, NN, WKV_PASSES_CHAIN) + hmat
        return 0

    lax.fori_loop(0, nc, step, 0)

    bd = bd_ref[...]
    inv = 1.0 / HEAD_DIM

    def epi(i, _):
        sl = pl.ds(pl.multiple_of(i * te, te), te)
        for hq in range(hpb):
            lanes = slice(hq * LANES, (hq + 1) * LANES)
            y = y_ref[hq, 0, sl, :] + y_ref[hq, 1, sl, :]
            mean = _mm_exact_rhs(y, bd) * inv
            yc = y - mean
            var = _mm_exact_rhs(yc * yc, bd) * inv
            yn = yc * lax.rsqrt(var + LNX_EPS) * lng_ref[:, lanes] + lnb_ref[:, lanes]
            o_ref[0, sl, lanes] = ((yn + bonus_ref[0, sl, lanes].astype(F32))
                                   * gate_ref[0, sl, lanes].astype(F32)).astype(o_ref.dtype)
        return 0

    lax.fori_loop(0, (nc * c) // te, epi, 0)


def _rwkv_p2(rp, qp, gm, hm, pk, lnx_g, lnx_b, bd128, te=1024, hpb=2):
    bsz, npair, _, s, _ = rp.shape
    nc = s // CHUNK
    width = hpb * LANES
    nq = npair // hpb
    seq = pl.BlockSpec((1, hpb, 2, s, LANES), lambda b, q: (b, q, 0, 0, 0))
    mat = pl.BlockSpec((1, hpb, 2, nc, HEAD_DIM, LANES), lambda b, q: (b, q, 0, 0, 0, 0))
    return pl.pallas_call(
        functools.partial(_p2_kernel, nc=nc, te=min(te, s), hpb=hpb),
        out_shape=jax.ShapeDtypeStruct((bsz, s, RWKV_WIDTH), BF16),
        grid=(bsz, nq),
        in_specs=[seq, seq, mat, mat,
                  pl.BlockSpec((1, s, width), lambda b, q: (b, 0, SLAB_G * nq + q)),
                  pl.BlockSpec((1, s, width), lambda b, q: (b, 0, SLAB_BONUS * nq + q)),
                  pl.BlockSpec((1, width), lambda b, q: (0, q)),
                  pl.BlockSpec((1, width), lambda b, q: (0, q)),
                  pl.BlockSpec((LANES, LANES), lambda b, q: (0, 0))],
        out_specs=pl.BlockSpec((1, s, width), lambda b, q: (b, 0, q)),
        scratch_shapes=[pltpu.VMEM((hpb, 2, s, LANES), F32), pltpu.VMEM((hpb, 2, LANES, LANES), F32)],
        compiler_params=_params("arbitrary", "arbitrary"),
        name="rwkv_p2",
    )(rp, qp, gm, hm, pk, pk, lnx_g, lnx_b, bd128)


def _outproj_kernel(att_ref, rw_ref, x_ref, mod_ref, g2_ref, wo_ref, wq_ref, k1_ref, k2_ref,
                    x1_ref, h2_ref, s1_ref, s2_ref):
    mod = mod_ref[0]
    a = ATT_WIDTH
    mixed = _dot(att_ref[0], wo_ref[0:a, :]) + _dot(rw_ref[0], wo_ref[a:, :])
    x1 = x_ref[0] + mod[2:3] * mixed
    x1_ref[0] = x1
    ms = jnp.mean(x1 * x1, axis=-1, keepdims=True)
    h2 = x1 * lax.rsqrt(ms + NORM_EPS) * g2_ref[...]
    h2 = (h2 * (1.0 + mod[4:5]) + mod[3:4]).astype(BF16)
    h2_ref[0] = h2
    q = _dot(h2, wq_ref[...])
    k1 = k1_ref[...]
    k2 = k2_ref[...]
    for h in range(PEER_HEADS):
        base = h * 2 * LANES
        s1 = _dot(k1, q[:, base:base + LANES].astype(BF16), NT)
        s2 = _dot(k2, q[:, base + LANES:base + 2 * LANES].astype(BF16), NT)
        for g in range(s1.shape[1] // LANES):
            s1_ref[h, g] = s1[:, g * LANES:(g + 1) * LANES]
            s2_ref[h, g] = s2[:, g * LANES:(g + 1) * LANES]


def _outproj(att, rw, x, mod6, norm2_g, w_out_bf, wq_bf, k1_bf, k2_bf, tm=512):
    bsz, s, d = x.shape
    nt = s // tm
    t = bsz * s
    nq = wq_bf.shape[1]
    const = lambda b, i: (0, 0)
    gpt = tm // LANES
    tok = pl.BlockSpec((PEER_HEADS, gpt, PEER_N_KEYS, LANES), lambda b, i: (0, b * nt + i, 0, 0))
    stat = jax.ShapeDtypeStruct((PEER_HEADS, t // LANES, PEER_N_KEYS, LANES), F32)
    return pl.pallas_call(
        _outproj_kernel,
        out_shape=(jax.ShapeDtypeStruct((bsz, s, d), F32),
                   jax.ShapeDtypeStruct((bsz, s, d), BF16),
                   stat, stat),
        grid=(bsz, nt),
        in_specs=[pl.BlockSpec((1, tm, ATT_WIDTH), lambda b, i: (b, i, 0)),
                  pl.BlockSpec((1, tm, RWKV_WIDTH), lambda b, i: (b, i, 0)),
                  pl.BlockSpec((1, tm, d), lambda b, i: (b, i, 0)),
                  pl.BlockSpec((1, 6, d), lambda b, i: (b, 0, 0)),
                  pl.BlockSpec((1, d), const),
                  pl.BlockSpec((d, d), const),
                  pl.BlockSpec((d, nq), const),
                  pl.BlockSpec((PEER_N_KEYS, LANES), const),
                  pl.BlockSpec((PEER_N_KEYS, LANES), const)],
        out_specs=(pl.BlockSpec((1, tm, d), lambda b, i: (b, i, 0)),
                   pl.BlockSpec((1, tm, d), lambda b, i: (b, i, 0)),
                   tok, tok),
        compiler_params=_params("arbitrary", "arbitrary"),
        name="outproj",
    )(att, rw, x, mod6, norm2_g, w_out_bf, wq_bf, k1_bf, k2_bf)


def _oddeven_sort_pairs(n):
    pairs = []
    p = 1
    while p < n:
        k = p
        while k >= 1:
            for j in range(k % p, n - k, 2 * k):
                for i in range(min(k, n - j - k)):
                    if (i + j) // (2 * p) == (i + j + k) // (2 * p):
                        pairs.append((i + j, i + j + k))
            k //= 2
        p *= 2
    return pairs


def _bitonic_merge_pairs(n):
    pairs = []
    stride = n // 2
    while stride >= 1:
        pairs += [(i, i + stride) for i in range(n) if not i & stride]
        stride //= 2
    return pairs


def _compare_exchange(vals, pairs):
    vals = list(vals)
    for i, j in pairs:
        a, b = vals[i], vals[j]
        if b is None:
            continue
        if a is None:
            vals[i], vals[j] = b, None
        else:
            vals[i], vals[j] = jnp.maximum(a, b), jnp.minimum(a, b)
    return vals


def _top16_sorted(vregs):
    kk = PEER_TOPK
    vals = list(vregs) + [None] * (kk - len(vregs))
    vals = _compare_exchange(vals, _oddeven_sort_pairs(kk))
    for shift in (4, 2, 1):
        other = [None if v is None else pltpu.roll(v, shift, axis=0) for v in vals]
        merged = []
        for k in range(kk):
            a, b = vals[k], other[kk - 1 - k]
            merged.append(b if a is None else a if b is None else jnp.maximum(a, b))
        vals = _compare_exchange(merged, _bitonic_merge_pairs(kk))
    return vals


def _topk_kernel(s1_ref, s2_ref, c1_ref, e1_ref, r2_ref, e2_ref, *, groups):
    kk = PEER_TOPK
    sub = 8
    neg = -jnp.inf
    rows8 = lax.broadcasted_iota(jnp.int32, (sub, LANES), 0)

    def rows_of(reps):
        out = reps[0]
        for r in range(1, sub):
            out = jnp.where(rows8 == r, reps[r], out)
        return out

    def all_sublanes_sum(x):
        for shift in (4, 2, 1):
            x = x + pltpu.roll(x, shift, axis=0)
        return x

    def per_tile(it_idx, _):
        g = it_idx // PEER_HEADS
        h = it_idx % PEER_HEADS
        s1 = [s1_ref[h, g, pl.ds(v * sub, sub), :] for v in range(PEER_N_KEYS // sub)]
        s2 = [s2_ref[h, g, pl.ds(v * sub, sub), :] for v in range(PEER_N_KEYS // sub)]
        a = _top16_sorted(s1)
        b = _top16_sorted(s2)
        a_lo, a_hi = rows_of(a[:sub]), rows_of(a[sub:])
        b_lo, b_hi = rows_of(b[:sub]), rows_of(b[sub:])

        cand = [a[0] + b_lo, a[0] + b_hi, a[1] + b_lo]
        for i in range(2, sub):
            cand.append(jnp.where(rows8 < kk // (i + 1), a[i] + b_lo, neg))
        cand.append(a_hi + b[0])
        top = _top16_sorted(cand)
        tau = top[kk - 1]
        z = None
        for t in top:
            ez = jnp.exp(t - top[0])
            z = ez if z is None else z + ez
        inv_z = 1.0 / z

        counts = []
        for r in range(kk):
            hits = (jnp.where(a[r] + b_lo >= tau, 1.0, 0.0) + jnp.where(a[r] + b_hi >= tau, 1.0, 0.0))
            counts.append(all_sublanes_sum(hits))
        for v in range(PEER_N_KEYS // sub):
            rows = pl.ds(v * sub, sub)
            cnt = jnp.zeros((sub, LANES), F32)
            rank = jnp.full((sub, LANES), float(kk), F32)
            for r in range(kk):
                cnt = jnp.where(s1[v] == a[r], counts[r], cnt)
                rank = jnp.where(s2[v] == b[r], float(r), rank)
            c1_ref[h, g, rows, :] = cnt
            e1_ref[h, g, rows, :] = jnp.exp(s1[v] - a[0]) * inv_z
            s2[v] = (rank, jnp.exp(s2[v] - b[0]))
        rank2 = jnp.concatenate([rv for rv, _ in s2], axis=0).astype(BF16)
        e2 = jnp.concatenate([ev for _, ev in s2], axis=0).astype(BF16)
        r2_ref[h, g] = pltpu.bitcast(rank2, jnp.uint32)
        e2_ref[h, g] = pltpu.bitcast(e2, jnp.uint32)
        return 0

    lax.fori_loop(0, groups * PEER_HEADS, per_tile, 0)


def _peer_topk(s1t, s2t, tn=512):
    nh, ng, nk, _ = s1t.shape
    groups = tn // LANES
    blk = pl.BlockSpec((nh, groups, nk, LANES), lambda i: (0, i, 0, 0))
    f32 = jax.ShapeDtypeStruct(s1t.shape, F32)
    b16 = jax.ShapeDtypeStruct((nh, ng, nk // 2, LANES), jnp.uint32)
    pblk = pl.BlockSpec((nh, groups, nk // 2, LANES), lambda i: (0, i, 0, 0))
    return pl.pallas_call(
        functools.partial(_topk_kernel, groups=groups),
        out_shape=(f32, f32, b16, b16),
        grid=(ng // groups,),
        in_specs=[blk, blk],
        out_specs=(blk, blk, pblk, pblk),
        compiler_params=_params("arbitrary"),
        name="peer_topk",
    )(s1t, s2t)


class _Pieces(list):
    every = 1
    phase = 0


def _peer_kernel(h2_ref, u_ref, vt_ref, c1_ref, e1_ref, r2_ref, e2_ref, x1_ref, mod_ref, o_ref,
                 acc_ref, act0_ref, act1_ref, p0_ref, p1_ref, *, tn, te, n_tiles):
    k = pl.program_id(0)
    n_items = pl.num_programs(0) - 2
    nk = PEER_N_KEYS
    ngroups = tn // LANES

    @pl.when(k == 0)
    def _():
        acc_ref[...] = jnp.zeros_like(acc_ref)
        for ref in (act0_ref, act1_ref, p0_ref, p1_ref):
            ref[...] = jnp.zeros_like(ref)

    ni = te // nk
    i0 = pl.multiple_of((jnp.clip(k - 1, 0, n_items - 1) % n_tiles) * ni, ni)
    tile_c = jnp.clip(k - 2, 0, n_items - 1) % n_tiles
    first_c = tile_c == 0

    d_model = acc_ref.shape[0]
    ksplit = 4
    msplit = 2
    halves = [(mh, nh, kh) for mh in range(msplit) for nh in range(ngroups // 2) for kh in range(ksplit)]

    def stage_c(p_a, mh, nh, kh):
        rows = slice(mh * (d_model // msplit), (mh + 1) * (d_model // msplit))
        cols = slice(nh * 2 * LANES, (nh + 1) * 2 * LANES)
        ks = slice(kh * (te // ksplit), (kh + 1) * (te // ksplit))
        p_prev = jnp.concatenate([p_a[2 * nh, ks, :], p_a[2 * nh + 1, ks, :]], axis=1)
        prev = acc_ref[rows, cols]
        if kh == 0:
            prev = jnp.where(first_c, jnp.zeros_like(prev), prev)
        acc_ref[rows, cols] = prev + _dot(vt_ref[rows, ks], p_prev)

    def stage_a(act_a, mh, nh, kh):
        rows = slice(mh * (te // msplit), (mh + 1) * (te // msplit))
        ks = slice(kh * (d_model // ksplit), (kh + 1) * (d_model // ksplit))
        act = _dot(u_ref[rows, ks], h2_ref[nh * 2 * LANES:(nh + 1) * 2 * LANES, ks], NT)
        if kh == 0:
            act_a[2 * nh, rows, :] = act[:, :LANES]
            act_a[2 * nh + 1, rows, :] = act[:, LANES:]
        else:
            act_a[2 * nh, rows, :] += act[:, :LANES]
            act_a[2 * nh + 1, rows, :] += act[:, LANES:]

    def rows_bf16(row):
        tile = jnp.broadcast_to(row, (BF16_ROWS, LANES)).astype(BF16)
        return jnp.concatenate([tile] * (nk // BF16_ROWS), axis=0)

    quad = 4

    def stage_b(act_b, p_b, tg, iq, pieces):
        gates = [jnp.zeros((nk, LANES), BF16) for _ in range(quad)]
        for h in range(PEER_HEADS):
            if h % pieces.every == pieces.phase and pieces:
                pieces.pop(0)()
            c1 = c1_ref[h, tg, pl.ds(i0, ni), :]
            e1 = e1_ref[h, tg, pl.ds(i0, ni), :]
            rank2 = pltpu.bitcast(r2_ref[h, tg], BF16)
            e2 = pltpu.bitcast(e2_ref[h, tg], BF16)
            for q in range(quad):
                il = iq * quad + q
                sel = rank2 < rows_bf16(c1[il:il + 1, :])
                gates[q] = gates[q] + jnp.where(sel, e2 * rows_bf16(e1[il:il + 1, :]), jnp.zeros_like(e2))
        for q in range(quad):
            il = iq * quad + q
            a = act_b[tg, il * nk:(il + 1) * nk, :]
            gelu = 0.5 * a * (1.0 + lax.erf(a * (2.0 ** -0.5)))
            p_b[tg, il * nk:(il + 1) * nk, :] = gates[q] * gelu.astype(BF16)

    def stages(act_a, act_b, p_a, p_b):
        pieces = _Pieces([functools.partial(stage_c, p_a, *hv) for hv in halves]
                         + [functools.partial(stage_a, act_a, *hv) for hv in halves])
        n_blocks = ngroups * (ni // quad)
        pieces.every = max(1, PEER_HEADS * n_blocks // len(pieces))
        pieces.phase = 0
        for tg in range(ngroups):
            for iq in range(ni // quad):
                stage_b(act_b, p_b, tg, iq, pieces)
        while pieces:
            pieces.pop(0)()

    @pl.when(k % 2 == 0)
    def _():
        stages(act0_ref, act1_ref, p0_ref, p1_ref)

    @pl.when(k % 2 == 1)
    def _():
        stages(act1_ref, act0_ref, p1_ref, p0_ref)

    @pl.when((tile_c == n_tiles - 1) & (k >= 2))
    def _():
        o_ref[...] = x1_ref[...] + mod_ref[0, 5:6, :] * acc_ref[...].T


def _peer_ffn(h2, u_bf, vt_bf, c1, e1, r2, e2, x1, mod6, seq, tn=512, te=2048):
    t, d = h2.shape
    ne = u_bf.shape[0]
    ngroups = tn // LANES
    n_tiles = ne // te
    n_items = (t // tn) * n_tiles

    def item(k, lag):
        w = jnp.clip(k - lag, 0, n_items - 1)
        return w // n_tiles, w % n_tiles

    stat = pl.BlockSpec((PEER_HEADS, ngroups, PEER_N_KEYS, LANES), lambda k: (0, item(k, 1)[0], 0, 0))
    pstat = pl.BlockSpec((PEER_HEADS, ngroups, PEER_N_KEYS // 2, LANES), lambda k: (0, item(k, 1)[0], 0, 0))
    return pl.pallas_call(
        functools.partial(_peer_kernel, tn=tn, te=te, n_tiles=n_tiles),
        out_shape=jax.ShapeDtypeStruct((t, d), F32),
        grid=(n_items + 2,),
        in_specs=[pl.BlockSpec((tn, d), lambda k: (item(k, 0)[0], 0)),
                  pl.BlockSpec((te, d), lambda k: (item(k, 0)[1], 0)),
                  pl.BlockSpec((d, te), lambda k: (0, item(k, 2)[1])),
                  stat, stat, pstat, pstat,
                  pl.BlockSpec((tn, d), lambda k: (item(k, 2)[0], 0)),
                  pl.BlockSpec((1, 6, d), lambda k: ((item(k, 2)[0] * tn) // seq, 0, 0))],
        out_specs=pl.BlockSpec((tn, d), lambda k: (item(k, 2)[0], 0)),
        scratch_shapes=[pltpu.VMEM((d, tn), F32),
                        pltpu.VMEM((ngroups, te, LANES), F32), pltpu.VMEM((ngroups, te, LANES), F32),
                        pltpu.VMEM((ngroups, te, LANES), BF16), pltpu.VMEM((ngroups, te, LANES), BF16)],
        compiler_params=_params("arbitrary"),
        name="peer_ffn",
    )(h2, u_bf, vt_bf, c1, e1, r2, e2, x1, mod6)


def _head_block_diag(n):
    idx = jnp.arange(n, dtype=jnp.int32) // HEAD_DIM
    return (idx[:, None] == idx[None, :]).astype(BF16)


def _layer(x, c, ada_w, ada_b, norm1_g, w_in, mu_prev, mu_next, q_norm_g, k_norm_g, w_decay0, w_decay_up,
           a_gate0, a_gate_up, g_up, k_k, k_a, r_k, lnx_g, lnx_b, w_out, norm2_g, peer_w_query,
           peer_sub_keys1, peer_sub_keys2, peer_u, peer_v):
    bsz, s, d = x.shape
    n_att_heads = ATT_WIDTH // HEAD_DIM
    bd512 = _head_block_diag(ATT_WIDTH)
    bd128 = _head_block_diag(LANES)

    mod6 = _ada(c, ada_w, ada_b).reshape(bsz, 6, d)
    gq = jnp.tile(q_norm_g, n_att_heads).reshape(1, ATT_WIDTH)
    gk = jnp.tile(k_norm_g, n_att_heads).reshape(1, ATT_WIDTH)
    qkv, zrw = _inproj(x, mod6, norm1_g.reshape(1, d), w_in.astype(BF16), gq, gk, bd512)

    att = _attention(qkv, n_att_heads)

    zero_w = jnp.zeros((2, LANES - DECAY_LORA, RWKV_WIDTH), F32)
    wup_pad = jnp.concatenate([w_decay_up, zero_w], axis=1).astype(BF16)
    aup_pad = jnp.concatenate([zero_w, a_gate_up], axis=1).astype(BF16)
    pk, plw = _rwkv_prep(zrw, mu_prev.reshape(1, -1), mu_next.reshape(1, -1), wup_pad, aup_pad, g_up.astype(BF16),
                         w_decay0, a_gate0, k_k.reshape(1, -1), k_a.reshape(1, -1), r_k.reshape(1, -1), bd512)
    rp, qp, gm, hm = _rwkv_p1(pk, plw)
    rw = _rwkv_p2(rp, qp, gm, hm, pk, lnx_g.reshape(1, -1), lnx_b.reshape(1, -1), bd128)

    x1, h2, s1t, s2t = _outproj(att, rw, x, mod6, norm2_g.reshape(1, d), w_out.astype(BF16),
                                peer_w_query.astype(BF16), peer_sub_keys1.astype(BF16),
                                peer_sub_keys2.astype(BF16))
    c1, e1, r2, e2 = _peer_topk(s1t, s2t)
    out = _peer_ffn(h2.reshape(bsz * s, d), peer_u.astype(BF16), peer_v.T.astype(BF16), c1, e1, r2, e2,
                    x1.reshape(bsz * s, d), mod6, s)
    return out.reshape(bsz, s, d)


def kernel(x, c, ada_w, ada_b, norm1_g, w_in, mu_prev, mu_next, q_norm_g, k_norm_g, w_decay0, w_decay_up,
           a_gate0, a_gate_up, g_up, k_k, k_a, r_k, lnx_g, lnx_b, w_out, norm2_g, peer_w_query,
           peer_sub_keys1, peer_sub_keys2, peer_u, peer_v):
    depth = ada_w.shape[0]
    for l in range(depth):
        x = _layer(x, c, ada_w[l], ada_b[l], norm1_g[l], w_in[l], mu_prev[l], mu_next[l], q_norm_g[l],
                   k_norm_g[l], w_decay0[l], w_decay_up[l], a_gate0[l], a_gate_up[l], g_up[l], k_k[l], k_a[l],
                   r_k[l], lnx_g[l], lnx_b[l], w_out[l], norm2_g[l], peer_w_query[l], peer_sub_keys1[l],
                   peer_sub_keys2[l], peer_u[l], peer_v[l])
    return x
```

```python
import functools
import math

import jax
import jax.numpy as jnp
from jax import lax
from jax.experimental import pallas as pl
from jax.experimental.pallas import tpu as pltpu

F32 = jnp.float32
BF16 = jnp.bfloat16

HEAD_DIM = 64
ATT_WIDTH = 512
RWKV_WIDTH = 512
DECAY_LORA = 64
AAA_LORA = 64
GATE_LORA = 128
SHIFT_WIDTH = 3 * RWKV_WIDTH + DECAY_LORA + AAA_LORA + GATE_LORA
PEER_N_KEYS = 128
PEER_HEADS = 8
PEER_TOPK = 16
NORM_EPS = 1e-6
LNX_EPS = 64e-5
MASK_VALUE = -1e30
LOG2E = 1.4426950408889634
ATT_HALF_WINDOWS = ((1, 64), (4, 256), (16, 1024))

LANES = 128
BF16_ROWS = 16
CHUNK = 64
ATT_BLOCK = 128
ATT_REACH = 1024 // ATT_BLOCK
VMEM_LIMIT_BYTES = 56 * 1024 * 1024
WKV_PASSES_CHAIN = 1

NN = (((1,), (0,)), ((), ()))
NT = (((1,), (1,)), ((), ()))
TN = (((0,), (0,)), ((), ()))

SLAB_R, SLAB_V, SLAB_KK, SLAB_DIR0, SLAB_G, SLAB_BONUS, NUM_SLABS = 0, 1, 2, 3, 7, 8, 9


def _params(*sem):
    return pltpu.CompilerParams(dimension_semantics=sem, vmem_limit_bytes=VMEM_LIMIT_BYTES)


def _dot(a, b, dims=NN):
    return lax.dot_general(a, b, dims, preferred_element_type=F32)


def _split(a):
    hi = a.astype(BF16)
    lo = (a - hi.astype(F32)).astype(BF16)
    return hi, lo


def _mm(a, b, dims=NN, passes=3):
    if passes == 1:
        return _dot(a.astype(BF16), b.astype(BF16), dims)
    ah, al = _split(a)
    bh, bl = _split(b)
    return _dot(ah, bh, dims) + (_dot(ah, bl, dims) + _dot(al, bh, dims))


def _mm_exact_rhs(a, b_exact):
    ah, al = _split(a)
    return _dot(ah, b_exact) + _dot(al, b_exact)


def _sigmoid(x):
    return 1.0 / (1.0 + jnp.exp(-x))


def _ada_kernel(c_ref, w_ref, b_ref, o_ref):
    c = c_ref[...]
    o_ref[...] = _mm(c * _sigmoid(c), w_ref[...]) + b_ref[...]


def _ada(c, ada_w, ada_b):
    bsz, d = c.shape
    n = ada_w.shape[1]
    tn = 1024
    return pl.pallas_call(
        _ada_kernel,
        out_shape=jax.ShapeDtypeStruct((bsz, n), F32),
        grid=(n // tn,),
        in_specs=[pl.BlockSpec((bsz, d), lambda j: (0, 0)),
                  pl.BlockSpec((d, tn), lambda j: (0, j)),
                  pl.BlockSpec((1, tn), lambda j: (0, j))],
        out_specs=pl.BlockSpec((bsz, tn), lambda j: (0, j)),
        compiler_params=_params("arbitrary"),
        name="ada",
    )(c, ada_w, ada_b.reshape(1, n))


def _inproj_kernel(x_ref, mod_ref, g1_ref, w_ref, gq_ref, gk_ref, bd_ref, qkv_ref, zrw_ref):
    x = x_ref[0]
    mod = mod_ref[0]
    ms = jnp.mean(x * x, axis=-1, keepdims=True)
    h = x * lax.rsqrt(ms + NORM_EPS) * g1_ref[...]
    h = (h * (1.0 + mod[1:2]) + mod[0:1]).astype(BF16)
    bd = bd_ref[...]

    def head_norm(z, g):
        ss = _mm_exact_rhs(z * z, bd) * (1.0 / HEAD_DIM)
        return z * lax.rsqrt(ss + NORM_EPS) * g

    a = ATT_WIDTH
    zq = _dot(h, w_ref[:, 0:a])
    qkv_ref[0, :, 0:a] = (head_norm(zq, gq_ref[...]) * (HEAD_DIM ** -0.5 * LOG2E)).astype(BF16)
    zk = _dot(h, w_ref[:, a:2 * a])
    qkv_ref[0, :, a:2 * a] = head_norm(zk, gk_ref[...]).astype(BF16)
    qkv_ref[0, :, 2 * a:3 * a] = _dot(h, w_ref[:, 2 * a:3 * a]).astype(BF16)
    zrw_ref[0] = _dot(h, w_ref[:, 3 * a:])


def _inproj(x, mod6, norm1_g, w_in_bf, gq, gk, bd512, tm=512):
    bsz, s, d = x.shape
    nin = w_in_bf.shape[1]
    return pl.pallas_call(
        _inproj_kernel,
        out_shape=(jax.ShapeDtypeStruct((bsz, s, 3 * ATT_WIDTH), BF16),
                   jax.ShapeDtypeStruct((bsz, s, SHIFT_WIDTH), F32)),
        grid=(bsz, s // tm),
        in_specs=[pl.BlockSpec((1, tm, d), lambda b, i: (b, i, 0)),
                  pl.BlockSpec((1, 6, d), lambda b, i: (b, 0, 0)),
                  pl.BlockSpec((1, d), lambda b, i: (0, 0)),
                  pl.BlockSpec((d, nin), lambda b, i: (0, 0)),
                  pl.BlockSpec((1, ATT_WIDTH), lambda b, i: (0, 0)),
                  pl.BlockSpec((1, ATT_WIDTH), lambda b, i: (0, 0)),
                  pl.BlockSpec((ATT_WIDTH, ATT_WIDTH), lambda b, i: (0, 0))],
        out_specs=(pl.BlockSpec((1, tm, 3 * ATT_WIDTH), lambda b, i: (b, i, 0)),
                   pl.BlockSpec((1, tm, SHIFT_WIDTH), lambda b, i: (b, i, 0))),
        compiler_params=_params("arbitrary", "arbitrary"),
        name="inproj",
    )(x, mod6, norm1_g, w_in_bf, gq, gk, bd512)


ATT_FAR_DILATION, ATT_FAR_HALF = ATT_HALF_WINDOWS[-1]
ATT_NEAR_BLOCKS = 6


def _attn_near_window(nkb):
    return min(nkb, ATT_NEAR_BLOCKS)


def _attn_near_start(qb, nkb):
    return jnp.clip(((qb - 2) // 2) * 2, 0, nkb - _attn_near_window(nkb))


def _attn_far_bias(n_heads, nkb):
    per_block = ATT_BLOCK // ATT_FAR_DILATION
    i = jnp.arange(ATT_BLOCK, dtype=jnp.int32)[:, None]
    ip = jnp.arange(ATT_BLOCK, dtype=jnp.int32)[None, :]
    start = _attn_near_start(i // per_block, nkb)
    kb = ip // per_block
    dist = ATT_FAR_DILATION * jnp.abs(i - ip)
    far = (dist <= ATT_FAR_HALF) & ~((kb >= start) & (kb < start + _attn_near_window(nkb)))
    slopes = jnp.exp2(-8.0 * (jnp.arange(n_heads, dtype=F32) + 1.0) / n_heads) * LOG2E
    return jnp.where(far[None], -slopes[:, None, None] * dist.astype(F32)[None], MASK_VALUE)


def _attn_bias_table(n_heads, nkb):
    reach = _attn_near_window(nkb) - 1
    nd = 2 * reach + 1
    dd = jnp.arange(nd, dtype=jnp.int32)[None, :, None]
    r = jnp.arange(ATT_BLOCK, dtype=jnp.int32)[:, None, None]
    c = jnp.arange(ATT_BLOCK, dtype=jnp.int32)[None, None, :]
    dt = r - c - (dd - reach) * ATT_BLOCK
    adt = jnp.abs(dt)
    mult = jnp.zeros(dt.shape, F32)
    for dil, half in ATT_HALF_WINDOWS:
        mult = mult + ((adt <= half) & (dt % dil == 0)).astype(F32)
    logm = jnp.where(mult > 0, jnp.log2(jnp.maximum(mult, 1.0)), MASK_VALUE)
    slopes = jnp.exp2(-8.0 * (jnp.arange(n_heads, dtype=F32) + 1.0) / n_heads) * LOG2E
    bias = logm[None] - slopes[:, None, None, None] * adt.astype(F32)[None]
    return bias.reshape(n_heads, ATT_BLOCK, nd * ATT_BLOCK)


def _attn_kernel(q_ref, k_ref, v_ref, bias_ref, far_ref, o_ref, s_ref, v1_ref, g_ref, accf_ref, mf_ref,
                 *, nkb, win):
    blk = ATT_BLOCK
    reach = win - 1
    has_far = win < nkb
    lane = lax.broadcasted_iota(jnp.int32, (blk, LANES), 1)
    first = lane < HEAD_DIM

    def two_heads(q):
        zero = jnp.zeros_like(q)
        return jnp.concatenate([jnp.where(first, q, zero), jnp.where(first, zero, q)], axis=0)

    def with_ones(v, hh):
        is_own = (lax.broadcasted_iota(jnp.int32, v.shape, 1) < HEAD_DIM) == (hh == 0)
        return jnp.where(is_own, v, jnp.ones_like(v))

    v = v_ref[0]
    for hh in range(2):
        v1_ref[hh] = with_ones(v, hh)

    if has_far:
        g_ref[0] = q_ref[0].astype(F32)
        g_ref[1] = k_ref[0].astype(F32)
        g_ref[2] = v.astype(F32)
        n_far = nkb * blk // ATT_FAR_DILATION

        def residue(r, _):
            rows = pl.ds(r, n_far, stride=ATT_FAR_DILATION)
            qr = g_ref[0, rows, :].astype(BF16)
            kr = g_ref[1, rows, :].astype(BF16)
            vr = g_ref[2, rows, :].astype(BF16)
            s = _dot(two_heads(qr), kr, NT)
            for hh in range(2):
                sb = s[hh * blk:(hh + 1) * blk] + far_ref[hh]
                mx = jnp.max(sb, axis=-1, keepdims=True)
                p = jnp.exp2(sb - mx).astype(BF16)
                accf_ref[hh, rows, :] = _dot(p, with_ones(vr, hh))
                mf_ref[hh, rows, :] = jnp.broadcast_to(mx, (blk, LANES))
            return 0

        lax.fori_loop(0, ATT_FAR_DILATION, residue, 0, unroll=4)

    group = s_ref.shape[0]
    slabs = range(0, win, 2)

    def query_block_group(i, _):
        us = range(group)
        qis = [group * i + u for u in us]
        rows = [pl.ds(pl.multiple_of(qi * blk, blk), blk) for qi in qis]
        qab = [two_heads(q_ref[0, rows[u], :]) for u in us]
        ks = [_attn_near_start(qi, nkb) if has_far else 0 for qi in qis]

        m = [[jnp.full((blk, LANES), -jnp.inf, F32) for _ in range(2)] for _ in us]
        for w0 in slabs:
            for u in us:
                kj0 = ks[u] + w0
                kslab = k_ref[0, pl.ds(pl.multiple_of(kj0 * blk, blk), 2 * blk), :]
                s = _dot(qab[u], kslab, NT)
                x0 = pl.multiple_of((kj0 - qis[u] + reach) * blk, blk)
                for hh in range(2):
                    sb = s[hh * blk:(hh + 1) * blk] + bias_ref[hh, :, pl.ds(x0, 2 * blk)]
                    s_ref[u, hh, :, w0 * blk:(w0 + 2) * blk] = sb
                    m[u][hh] = jnp.maximum(m[u][hh], jnp.maximum(sb[:, :blk], sb[:, blk:]))
        mrow = [[jnp.max(m[u][hh], axis=-1, keepdims=True) for hh in range(2)] for u in us]

        if has_far:
            acc = [[None, None] for _ in us]
            for u in us:
                for hh in range(2):
                    m_far = mf_ref[hh, rows[u], :][:, 0:1]
                    m_all = jnp.maximum(mrow[u][hh], m_far)
                    acc[u][hh] = accf_ref[hh, rows[u], :] * jnp.exp2(m_far - m_all)
                    mrow[u][hh] = m_all
        else:
            acc = [[jnp.zeros((blk, LANES), F32) for _ in range(2)] for _ in us]
        for w0 in slabs:
            for u in us:
                keys = pl.ds(pl.multiple_of((ks[u] + w0) * blk, blk), 2 * blk)
                for hh in range(2):
                    p = jnp.exp2(s_ref[u, hh, :, w0 * blk:(w0 + 2) * blk] - mrow[u][hh]).astype(BF16)
                    acc[u][hh] = acc[u][hh] + _dot(p, v1_ref[hh, keys, :])
        for u in us:
            num = jnp.where(first, acc[u][0], acc[u][1])
            den = jnp.where(first, pltpu.roll(acc[u][0], HEAD_DIM, axis=1), pltpu.roll(acc[u][1], HEAD_DIM, axis=1))
            o_ref[0, rows[u], :] = (num / den).astype(o_ref.dtype)
        return 0

    lax.fori_loop(0, nkb // group, query_block_group, 0)


def _attention(qkv, n_heads):
    bsz, s, _ = qkv.shape
    nkb = s // ATT_BLOCK
    win = _attn_near_window(nkb)
    assert nkb % 2 == 0 and (win == nkb or s // ATT_FAR_DILATION == ATT_BLOCK)
    bias = _attn_bias_table(n_heads, nkb)
    far = _attn_far_bias(n_heads, nkb)
    npair = ATT_WIDTH // LANES
    return pl.pallas_call(
        functools.partial(_attn_kernel, nkb=nkb, win=win),
        out_shape=jax.ShapeDtypeStruct((bsz, s, ATT_WIDTH), BF16),
        grid=(npair, bsz),
        in_specs=[pl.BlockSpec((1, s, LANES), lambda hp, b: (b, 0, hp)),
                  pl.BlockSpec((1, s, LANES), lambda hp, b: (b, 0, npair + hp)),
                  pl.BlockSpec((1, s, LANES), lambda hp, b: (b, 0, 2 * npair + hp)),
                  pl.BlockSpec((2, ATT_BLOCK, bias.shape[2]), lambda hp, b: (hp, 0, 0)),
                  pl.BlockSpec((2, ATT_BLOCK, ATT_BLOCK), lambda hp, b: (hp, 0, 0))],
        out_specs=pl.BlockSpec((1, s, LANES), lambda hp, b: (b, 0, hp)),
        scratch_shapes=[pltpu.VMEM((math.gcd(nkb, 4), 2, ATT_BLOCK, win * ATT_BLOCK), F32), pltpu.VMEM((2, s, LANES), BF16),
                        pltpu.VMEM((3, s, LANES), F32), pltpu.VMEM((2, s, LANES), F32),
                        pltpu.VMEM((2, s, LANES), F32)],
        compiler_params=_params("arbitrary", "arbitrary"),
        name="attn",
    )(qkv, qkv, qkv, bias, far)


def _prep_kernel(z_ref, zp_ref, zn_ref, mup_ref, mun_ref, wup_ref, aup_ref, gup_ref, w0_ref, a0_ref,
                 kk_ref, ka_ref, rk_ref, bd_ref, o_ref, lw_ref, *, tq):
    i = pl.program_id(1)
    last = pl.num_programs(1) - 1
    z = z_ref[0]
    row = lax.broadcasted_iota(jnp.int32, (tq, 1), 0)
    prev_row = zp_ref[0, 7:8, :] * (i > 0).astype(F32)
    next_row = zn_ref[0, 0:1, :] * (i < last).astype(F32)
    zp = jnp.where(row == 0, prev_row, pltpu.roll(z, 1, axis=0))
    zn = jnp.where(row == tq - 1, next_row, pltpu.roll(z, tq - 1, axis=0))
    zs = z + mup_ref[...] * (zp - z) + mun_ref[...] * (zn - z)

    w = RWKV_WIDTH
    r = zs[:, 0:w]
    k = zs[:, w:2 * w]
    v = zs[:, 2 * w:3 * w]
    xwa = zs[:, 3 * w:3 * w + LANES]
    xg = zs[:, 3 * w + LANES:]
    bd = bd_ref[...]

    g = _dot(_sigmoid(xg).astype(BF16), gup_ref[...])
    kk = k * kk_ref[...]
    ss = _mm_exact_rhs(kk * kk, bd)
    kk = kk * lax.rsqrt(jnp.maximum(ss, 1e-12))
    ka = ka_ref[...]
    txw = jnp.tanh(xwa).astype(BF16)
    xab = xwa.astype(BF16)

    def put(slab, val):
        o_ref[0, :, slab * w:(slab + 1) * w] = val.astype(o_ref.dtype)

    put(SLAB_R, r)
    put(SLAB_V, v)
    put(SLAB_KK, kk)
    put(SLAB_G, g)
    a_sum = jnp.zeros_like(k)
    for d in range(2):
        y = w0_ref[d:d + 1, :] + _dot(txw, wup_ref[d])
        wlog = -(jnp.maximum(-y, 0.0) + jnp.log(1.0 + jnp.exp(-jnp.abs(y)))) - 0.5
        a = _sigmoid(a0_ref[d:d + 1, :] + _dot(xab, aup_ref[d]))
        a_sum = a_sum + a
        lw_ref[0, :, d * w:(d + 1) * w] = -jnp.exp(wlog)
        put(SLAB_DIR0 + 2 * d, k * (1.0 + (a - 1.0) * ka))
        put(SLAB_DIR0 + 2 * d + 1, kk * a)
    k_bonus = k * (1.0 + (0.5 * a_sum - 1.0) * ka)
    bsum = _mm_exact_rhs(r * k_bonus * rk_ref[...], bd)
    put(SLAB_BONUS, bsum * v)


def _rwkv_prep(zrw, mu_prev, mu_next, wup_pad, aup_pad, g_up_bf, w0, a0, k_k, k_a, r_k, bd512, tq=512):
    bsz, s, sw = zrw.shape
    w = RWKV_WIDTH
    nt = s // tq
    row = lambda b, i: (0, 0)
    return pl.pallas_call(
        functools.partial(_prep_kernel, tq=tq),
        out_shape=(jax.ShapeDtypeStruct((bsz, s, NUM_SLABS * w), BF16),
                   jax.ShapeDtypeStruct((bsz, s, 2 * w), F32)),
        grid=(bsz, nt),
        in_specs=[pl.BlockSpec((1, tq, sw), lambda b, i: (b, i, 0)),
                  pl.BlockSpec((1, 8, sw), lambda b, i: (b, jnp.maximum(i * (tq // 8) - 1, 0), 0)),
                  pl.BlockSpec((1, 8, sw), lambda b, i: (b, jnp.minimum((i + 1) * (tq // 8), s // 8 - 1), 0)),
                  pl.BlockSpec((1, sw), row),
                  pl.BlockSpec((1, sw), row),
                  pl.BlockSpec((2, LANES, w), lambda b, i: (0, 0, 0)),
                  pl.BlockSpec((2, LANES, w), lambda b, i: (0, 0, 0)),
                  pl.BlockSpec((GATE_LORA, w), row),
                  pl.BlockSpec((2, w), row),
                  pl.BlockSpec((2, w), row),
                  pl.BlockSpec((1, w), row),
                  pl.BlockSpec((1, w), row),
                  pl.BlockSpec((1, w), row),
                  pl.BlockSpec((w, w), row)],
        out_specs=(pl.BlockSpec((1, tq, NUM_SLABS * w), lambda b, i: (b, i, 0)),
                   pl.BlockSpec((1, tq, 2 * w), lambda b, i: (b, i, 0))),
        compiler_params=_params("arbitrary", "arbitrary"),
        name="rwkv_prep",
    )(zrw, zrw, zrw, mu_prev, mu_next, wup_pad, aup_pad, g_up_bf, w0, a0, k_k, k_a, r_k, bd512)


def _block_diag(x, first):
    zero = jnp.zeros_like(x)
    return jnp.concatenate([jnp.where(first, x, zero), jnp.where(first, zero, x)], axis=0)


def _p1_kernel(r_ref, v_ref, kk_ref, lw_ref, kd_ref, be_ref, rp_ref, qp_ref, g_ref, h_ref, *, cpb):
    c = CHUNK
    sign = 1 - 2 * pl.program_id(2)
    rowi = lax.broadcasted_iota(jnp.int32, (c, LANES), 0)
    lane = lax.broadcasted_iota(jnp.int32, (c, LANES), 1)
    coli = lane & (c - 1)
    first = lane < HEAD_DIM
    before = sign * (rowi - coli)
    strict2 = before > 0
    incl2 = before >= 0
    eye2 = (coli == rowi).astype(F32)
    r64 = lax.broadcasted_iota(jnp.int32, (c, c), 0)
    c64 = lax.broadcasted_iota(jnp.int32, (c, c), 1)
    cum = (sign * (r64 - c64) >= 0).astype(BF16)

    js = range(cpb)
    sls = [pl.ds(j * c, c) for j in js]
    bdg = lambda x: _block_diag(x, first)
    r = [r_ref[0, sl, :].astype(F32) for sl in sls]
    v = [v_ref[0, sl, :].astype(F32) for sl in sls]
    kk = [kk_ref[0, sl, :].astype(F32) for sl in sls]
    lw = [lw_ref[0, sl, :] for sl in sls]
    kd = [kd_ref[0, sl, :].astype(F32) for sl in sls]
    be = [be_ref[0, sl, :].astype(F32) for sl in sls]

    one = lambda x, y, dims=NN: _mm(x, y, dims, 1)
    rows2 = lambda x, y: jnp.concatenate([x, y], axis=0)
    cols2 = lambda x, y: jnp.concatenate([x, y], axis=1)

    def cumsum(x):
        l1 = x.astype(BF16)
        rem = x - l1.astype(F32)
        l2 = rem.astype(BF16)
        l3 = (rem - l2.astype(F32)).astype(BF16)
        y = _dot(cum, jnp.concatenate([l1, l2, l3], axis=1))
        return y[:, :LANES] + (y[:, LANES:2 * LANES] + y[:, 2 * LANES:])

    cs = [cumsum(lw[j]) for j in js]
    g_inv = [jnp.exp(-cs[j]) for j in js]
    g_tot = [jnp.exp(jnp.sum(lw[j], axis=0, keepdims=True)) for j in js]
    ab = [-kk[j] * jnp.exp(cs[j] - lw[j]) for j in js]
    rb = [r[j] * jnp.exp(cs[j]) for j in js]
    bt = [be[j] * g_inv[j] for j in js]
    kt = [kd[j] * g_inv[j] for j in js]

    sc = [one(rows2(ab[j], rb[j]), rows2(bdg(bt[j]), bdg(kt[j])), NT) for j in js]
    m_ab = [jnp.where(strict2, sc[j][:c, :LANES], 0.0) for j in js]
    n_rb = [jnp.where(incl2, sc[j][c:, :LANES], 0.0) for j in js]
    m_ak = [jnp.where(strict2, sc[j][:c, LANES:], 0.0) for j in js]
    n_rk = [jnp.where(incl2, sc[j][c:, LANES:], 0.0) for j in js]

    levels = int(math.log2(c)) - 1
    t = [eye2 + m_ab[j] for j in js]
    p = [one(m_ab[j], bdg(m_ab[j])) for j in js]
    for _ in range(1, levels - 1):
        y = [one(rows2(p[j], t[j]), bdg(p[j])) for j in js]
        p = [y[j][:c] for j in js]
        t = [t[j] + y[j][c:] for j in js]
    t = [t[j] + one(t[j], bdg(p[j])) for j in js]

    def residual(a, tt):
        ah, al = _split(a)
        th, tl = _split(bdg(tt))
        y = _dot(rows2(ah, al), th)
        return eye2 - (y[:c] + (y[c:] + _dot(ah, tl)))

    res = [residual(eye2 - m_ab[j], t[j]) for j in js]
    t = [t[j] + one(t[j], bdg(res[j])) for j in js]

    vbd = [bdg(v[j]) for j in js]
    mv = [one(rows2(m_ak[j], n_rk[j]), vbd[j]) for j in js]
    tp = [one(t[j], cols2(bdg(ab[j]), bdg(mv[j][:c]))) for j in js]
    ap = [tp[j][:, :LANES] for j in js]
    pp = [tp[j][:, LANES:] for j in js]
    nr = [one(n_rb[j], cols2(bdg(ap[j]), bdg(pp[j]))) for j in js]
    for j in js:
        rp_ref[0, 0, 0, sls[j], :] = rb[j] + nr[j][:, :LANES]
        qp_ref[0, 0, 0, sls[j], :] = nr[j][:, LANES:] + mv[j][c:]
    heads = lambda x: jnp.where(first, x[:HEAD_DIM], x[HEAD_DIM:])
    for j in js:
        st = one(cols2(ap[j], pp[j]), bt[j], TN)
        g_ref[0, 0, 0, j] = (eye2 + heads(st[:LANES])) * g_tot[j]
        h_ref[0, 0, 0, j] = heads(st[LANES:] + one(v[j], kt[j], TN)) * g_tot[j]


def _rwkv_p1(pk, plw, cpb=16):
    bsz, s, _ = pk.shape
    nc = s // CHUNK
    npair = RWKV_WIDTH // LANES
    rows = cpb * CHUNK

    def slab(sidx):
        return pl.BlockSpec((1, rows, LANES), lambda b, hp, d, ci: (b, ci, sidx * npair + hp))

    def dslab(off):
        return pl.BlockSpec((1, rows, LANES),
                            lambda b, hp, d, ci: (b, ci, (SLAB_DIR0 + 2 * d + off) * npair + hp))

    lw_spec = pl.BlockSpec((1, rows, LANES), lambda b, hp, d, ci: (b, ci, d * npair + hp))

    seq = pl.BlockSpec((1, 1, 1, rows, LANES), lambda b, hp, d, ci: (b, hp, d, ci, 0))
    mat = pl.BlockSpec((1, 1, 1, cpb, HEAD_DIM, LANES), lambda b, hp, d, ci: (b, hp, d, ci, 0, 0))
    return pl.pallas_call(
        functools.partial(_p1_kernel, cpb=cpb),
        out_shape=(jax.ShapeDtypeStruct((bsz, npair, 2, s, LANES), F32),
                   jax.ShapeDtypeStruct((bsz, npair, 2, s, LANES), F32),
                   jax.ShapeDtypeStruct((bsz, npair, 2, nc, HEAD_DIM, LANES), F32),
                   jax.ShapeDtypeStruct((bsz, npair, 2, nc, HEAD_DIM, LANES), F32)),
        grid=(bsz, npair, 2, nc // cpb),
        in_specs=[slab(SLAB_R), slab(SLAB_V), slab(SLAB_KK), lw_spec, dslab(0), dslab(1)],
        out_specs=(seq, seq, mat, mat),
        compiler_params=_params("arbitrary", "arbitrary", "arbitrary", "arbitrary"),
        name="rwkv_p1",
    )(pk, pk, pk, plw, pk, pk)


def _p2_kernel(rp_ref, qp_ref, g_ref, h_ref, gate_ref, bonus_ref, lng_ref, lnb_ref, bd_ref, o_ref,
               y_ref, s_ref, *, nc, te, hpb):
    c = CHUNK
    s_ref[...] = jnp.zeros_like(s_ref)
    first = lax.broadcasted_iota(jnp.int32, (HEAD_DIM, LANES), 1) < HEAD_DIM
    chains = [(hq, d) for hq in range(hpb) for d in range(2)]

    def step(j, _):
        for hq, d in chains:
            jc = j if d == 0 else nc - 1 - j
            sl = pl.ds(pl.multiple_of(jc * c, c), c)
            st = s_ref[hq, d]
            y_ref[hq, d, sl, :] = (_mm(rp_ref[0, hq, d, sl, :], st, NT, WKV_PASSES_CHAIN)
                                   + qp_ref[0, hq, d, sl, :])
            gmat = _block_diag(g_ref[0, hq, d, jc], first)
            hmat = _block_diag(h_ref[0, hq, d, jc], first)
            s_ref[hq, d] = _mm(st, gmat, NN, WKV_PASSES_CHAIN) + hmat
        return 0

    lax.fori_loop(0, nc, step, 0)

    bd = bd_ref[...]
    inv = 1.0 / HEAD_DIM

    def epi(i, _):
        sl = pl.ds(pl.multiple_of(i * te, te), te)
        for hq in range(hpb):
            lanes = slice(hq * LANES, (hq + 1) * LANES)
            y = y_ref[hq, 0, sl, :] + y_ref[hq, 1, sl, :]
            mean = _mm_exact_rhs(y, bd) * inv
            yc = y - mean
            var = _mm_exact_rhs(yc * yc, bd) * inv
            yn = yc * lax.rsqrt(var + LNX_EPS) * lng_ref[:, lanes] + lnb_ref[:, lanes]
            o_ref[0, sl, lanes] = ((yn + bonus_ref[0, sl, lanes].astype(F32))
                                   * gate_ref[0, sl, lanes].astype(F32)).astype(o_ref.dtype)
        return 0

    lax.fori_loop(0, (nc * c) // te, epi, 0)


def _rwkv_p2(rp, qp, gm, hm, pk, lnx_g, lnx_b, bd128, te=1024, hpb=2):
    bsz, npair, _, s, _ = rp.shape
    nc = s // CHUNK
    width = hpb * LANES
    nq = npair // hpb
    seq = pl.BlockSpec((1, hpb, 2, s, LANES), lambda b, q: (b, q, 0, 0, 0))
    mat = pl.BlockSpec((1, hpb, 2, nc, HEAD_DIM, LANES), lambda b, q: (b, q, 0, 0, 0, 0))
    return pl.pallas_call(
        functools.partial(_p2_kernel, nc=nc, te=min(te, s), hpb=hpb),
        out_shape=jax.ShapeDtypeStruct((bsz, s, RWKV_WIDTH), BF16),
        grid=(bsz, nq),
        in_specs=[seq, seq, mat, mat,
                  pl.BlockSpec((1, s, width), lambda b, q: (b, 0, SLAB_G * nq + q)),
                  pl.BlockSpec((1, s, width), lambda b, q: (b, 0, SLAB_BONUS * nq + q)),
                  pl.BlockSpec((1, width), lambda b, q: (0, q)),
                  pl.BlockSpec((1, width), lambda b, q: (0, q)),
                  pl.BlockSpec((LANES, LANES), lambda b, q: (0, 0))],
        out_specs=pl.BlockSpec((1, s, width), lambda b, q: (b, 0, q)),
        scratch_shapes=[pltpu.VMEM((hpb, 2, s, LANES), F32), pltpu.VMEM((hpb, 2, LANES, LANES), F32)],
        compiler_params=_params("arbitrary", "arbitrary"),
        name="rwkv_p2",
    )(rp, qp, gm, hm, pk, pk, lnx_g, lnx_b, bd128)


def _outproj_kernel(att_ref, rw_ref, x_ref, mod_ref, g2_ref, wo_ref, wq_ref, k1_ref, k2_ref,
                    x1_ref, h2_ref, s1_ref, s2_ref):
    mod = mod_ref[0]
    a = ATT_WIDTH
    mixed = _dot(att_ref[0], wo_ref[0:a, :]) + _dot(rw_ref[0], wo_ref[a:, :])
    x1 = x_ref[0] + mod[2:3] * mixed
    x1_ref[0] = x1
    ms = jnp.mean(x1 * x1, axis=-1, keepdims=True)
    h2 = x1 * lax.rsqrt(ms + NORM_EPS) * g2_ref[...]
    h2 = (h2 * (1.0 + mod[4:5]) + mod[3:4]).astype(BF16)
    h2_ref[0] = h2
    q = _dot(h2, wq_ref[...])
    k1 = k1_ref[...]
    k2 = k2_ref[...]
    for h in range(PEER_HEADS):
        base = h * 2 * LANES
        s1 = _dot(k1, q[:, base:base + LANES].astype(BF16), NT)
        s2 = _dot(k2, q[:, base + LANES:base + 2 * LANES].astype(BF16), NT)
        for g in range(s1.shape[1] // LANES):
            s1_ref[h, g] = s1[:, g * LANES:(g + 1) * LANES]
            s2_ref[h, g] = s2[:, g * LANES:(g + 1) * LANES]


def _outproj(att, rw, x, mod6, norm2_g, w_out_bf, wq_bf, k1_bf, k2_bf, tm=512):
    bsz, s, d = x.shape
    nt = s // tm
    t = bsz * s
    nq = wq_bf.shape[1]
    const = lambda b, i: (0, 0)
    gpt = tm // LANES
    tok = pl.BlockSpec((PEER_HEADS, gpt, PEER_N_KEYS, LANES), lambda b, i: (0, b * nt + i, 0, 0))
    stat = jax.ShapeDtypeStruct((PEER_HEADS, t // LANES, PEER_N_KEYS, LANES), F32)
    return pl.pallas_call(
        _outproj_kernel,
        out_shape=(jax.ShapeDtypeStruct((bsz, s, d), F32),
                   jax.ShapeDtypeStruct((bsz, s, d), BF16),
                   stat, stat),
        grid=(bsz, nt),
        in_specs=[pl.BlockSpec((1, tm, ATT_WIDTH), lambda b, i: (b, i, 0)),
                  pl.BlockSpec((1, tm, RWKV_WIDTH), lambda b, i: (b, i, 0)),
                  pl.BlockSpec((1, tm, d), lambda b, i: (b, i, 0)),
                  pl.BlockSpec((1, 6, d), lambda b, i: (b, 0, 0)),
                  pl.BlockSpec((1, d), const),
                  pl.BlockSpec((d, d), const),
                  pl.BlockSpec((d, nq), const),
                  pl.BlockSpec((PEER_N_KEYS, LANES), const),
                  pl.BlockSpec((PEER_N_KEYS, LANES), const)],
        out_specs=(pl.BlockSpec((1, tm, d), lambda b, i: (b, i, 0)),
                   pl.BlockSpec((1, tm, d), lambda b, i: (b, i, 0)),
                   tok, tok),
        compiler_params=_params("arbitrary", "arbitrary"),
        name="outproj",
    )(att, rw, x, mod6, norm2_g, w_out_bf, wq_bf, k1_bf, k2_bf)


def _oddeven_sort_pairs(n):
    pairs = []
    p = 1
    while p < n:
        k = p
        while k >= 1:
            for j in range(k % p, n - k, 2 * k):
                for i in range(min(k, n - j - k)):
                    if (i + j) // (2 * p) == (i + j + k) // (2 * p):
                        pairs.append((i + j, i + j + k))
            k //= 2
        p *= 2
    return pairs


def _bitonic_merge_pairs(n):
    pairs = []
    stride = n // 2
    while stride >= 1:
        pairs += [(i, i + stride) for i in range(n) if not i & stride]
        stride //= 2
    return pairs


def _compare_exchange(vals, pairs):
    vals = list(vals)
    for i, j in pairs:
        a, b = vals[i], vals[j]
        if b is None:
            continue
        if a is None:
            vals[i], vals[j] = b, None
        else:
            vals[i], vals[j] = jnp.maximum(a, b), jnp.minimum(a, b)
    return vals


def _top16_sorted(vregs):
    kk = PEER_TOPK
    vals = list(vregs) + [None] * (kk - len(vregs))
    vals = _compare_exchange(vals, _oddeven_sort_pairs(kk))
    for shift in (4, 2, 1):
        other = [None if v is None else pltpu.roll(v, shift, axis=0) for v in vals]
        merged = []
        for k in range(kk):
            a, b = vals[k], other[kk - 1 - k]
            merged.append(b if a is None else a if b is None else jnp.maximum(a, b))
        vals = _compare_exchange(merged, _bitonic_merge_pairs(kk))
    return vals


def _topk_kernel(s1_ref, s2_ref, c1_ref, e1_ref, r2_ref, e2_ref, *, groups):
    kk = PEER_TOPK
    sub = 8
    neg = -jnp.inf
    rows8 = lax.broadcasted_iota(jnp.int32, (sub, LANES), 0)

    def rows_of(reps):
        out = reps[0]
        for r in range(1, sub):
            out = jnp.where(rows8 == r, reps[r], out)
        return out

    def all_sublanes_sum(x):
        for shift in (4, 2, 1):
            x = x + pltpu.roll(x, shift, axis=0)
        return x

    def per_tile(it_idx, _):
        g = it_idx // PEER_HEADS
        h = it_idx % PEER_HEADS
        s1 = [s1_ref[h, g, pl.ds(v * sub, sub), :] for v in range(PEER_N_KEYS // sub)]
        s2 = [s2_ref[h, g, pl.ds(v * sub, sub), :] for v in range(PEER_N_KEYS // sub)]
        a = _top16_sorted(s1)
        b = _top16_sorted(s2)
        a_lo, a_hi = rows_of(a[:sub]), rows_of(a[sub:])
        b_lo, b_hi = rows_of(b[:sub]), rows_of(b[sub:])

        cand = [a[0] + b_lo, a[0] + b_hi, a[1] + b_lo]
        for i in range(2, sub):
            cand.append(jnp.where(rows8 < kk // (i + 1), a[i] + b_lo, neg))
        cand.append(a_hi + b[0])
        top = _top16_sorted(cand)
        tau = top[kk - 1]
        z = None
        for t in top:
            ez = jnp.exp(t - top[0])
            z = ez if z is None else z + ez
        inv_z = 1.0 / z

        counts = []
        for r in range(kk):
            hits = (jnp.where(a[r] + b_lo >= tau, 1.0, 0.0) + jnp.where(a[r] + b_hi >= tau, 1.0, 0.0))
            counts.append(all_sublanes_sum(hits))
        for v in range(PEER_N_KEYS // sub):
            rows = pl.ds(v * sub, sub)
            cnt = jnp.zeros((sub, LANES), F32)
            rank = jnp.full((sub, LANES), float(kk), F32)
            for r in range(kk):
                cnt = jnp.where(s1[v] == a[r], counts[r], cnt)
                rank = jnp.where(s2[v] == b[r], float(r), rank)
            c1_ref[h, g, rows, :] = cnt
            e1_ref[h, g, rows, :] = jnp.exp(s1[v] - a[0]) * inv_z
            s2[v] = (rank, jnp.exp(s2[v] - b[0]))
        rank2 = jnp.concatenate([rv for rv, _ in s2], axis=0).astype(BF16)
        e2 = jnp.concatenate([ev for _, ev in s2], axis=0).astype(BF16)
        r2_ref[h, g] = pltpu.bitcast(rank2, jnp.uint32)
        e2_ref[h, g] = pltpu.bitcast(e2, jnp.uint32)
        return 0

    lax.fori_loop(0, groups * PEER_HEADS, per_tile, 0)


def _peer_topk(s1t, s2t, tn=512):
    nh, ng, nk, _ = s1t.shape
    groups = tn // LANES
    blk = pl.BlockSpec((nh, groups, nk, LANES), lambda i: (0, i, 0, 0))
    f32 = jax.ShapeDtypeStruct(s1t.shape, F32)
    b16 = jax.ShapeDtypeStruct((nh, ng, nk // 2, LANES), jnp.uint32)
    pblk = pl.BlockSpec((nh, groups, nk // 2, LANES), lambda i: (0, i, 0, 0))
    return pl.pallas_call(
        functools.partial(_topk_kernel, groups=groups),
        out_shape=(f32, f32, b16, b16),
        grid=(ng // groups,),
        in_specs=[blk, blk],
        out_specs=(blk, blk, pblk, pblk),
        compiler_params=_params("arbitrary"),
        name="peer_topk",
    )(s1t, s2t)


class _Pieces(list):
    every = 1
    phase = 0


def _peer_kernel(h2_ref, u_ref, vt_ref, c1_ref, e1_ref, r2_ref, e2_ref, x1_ref, mod_ref, o_ref,
                 acc_ref, act0_ref, act1_ref, p0_ref, p1_ref, *, tn, te, n_tiles):
    k = pl.program_id(0)
    n_items = pl.num_programs(0) - 2
    nk = PEER_N_KEYS
    ngroups = tn // LANES

    @pl.when(k == 0)
    def _():
        acc_ref[...] = jnp.zeros_like(acc_ref)
        for ref in (act0_ref, act1_ref, p0_ref, p1_ref):
            ref[...] = jnp.zeros_like(ref)

    ni = te // nk
    i0 = pl.multiple_of((jnp.clip(k - 1, 0, n_items - 1) % n_tiles) * ni, ni)
    tile_c = jnp.clip(k - 2, 0, n_items - 1) % n_tiles
    first_c = tile_c == 0

    d_model = acc_ref.shape[0]
    ksplit = 4
    msplit = 2
    halves = [(mh, nh, kh) for mh in range(msplit) for nh in range(ngroups // 2) for kh in range(ksplit)]

    def stage_c(p_a, mh, nh, kh):
        rows = slice(mh * (d_model // msplit), (mh + 1) * (d_model // msplit))
        cols = slice(nh * 2 * LANES, (nh + 1) * 2 * LANES)
        ks = slice(kh * (te // ksplit), (kh + 1) * (te // ksplit))
        p_prev = jnp.concatenate([p_a[2 * nh, ks, :], p_a[2 * nh + 1, ks, :]], axis=1)
        prev = acc_ref[rows, cols]
        if kh == 0:
            prev = jnp.where(first_c, jnp.zeros_like(prev), prev)
        acc_ref[rows, cols] = prev + _dot(vt_ref[rows, ks], p_prev)

    def stage_a(act_a, mh, nh, kh):
        rows = slice(mh * (te // msplit), (mh + 1) * (te // msplit))
        ks = slice(kh * (d_model // ksplit), (kh + 1) * (d_model // ksplit))
        act = _dot(u_ref[rows, ks], h2_ref[nh * 2 * LANES:(nh + 1) * 2 * LANES, ks], NT)
        if kh == 0:
            act_a[2 * nh, rows, :] = act[:, :LANES]
            act_a[2 * nh + 1, rows, :] = act[:, LANES:]
        else:
            act_a[2 * nh, rows, :] += act[:, :LANES]
            act_a[2 * nh + 1, rows, :] += act[:, LANES:]

    def rows_bf16(row):
        tile = jnp.broadcast_to(row, (BF16_ROWS, LANES)).astype(BF16)
        return jnp.concatenate([tile] * (nk // BF16_ROWS), axis=0)

    quad = 4

    def stage_b(act_b, p_b, tg, iq, pieces):
        gates = [jnp.zeros((nk, LANES), BF16) for _ in range(quad)]
        for h in range(PEER_HEADS):
            if h % pieces.every == pieces.phase and pieces:
                pieces.pop(0)()
            c1 = c1_ref[h, tg, pl.ds(i0, ni), :]
            e1 = e1_ref[h, tg, pl.ds(i0, ni), :]
            rank2 = pltpu.bitcast(r2_ref[h, tg], BF16)
            e2 = pltpu.bitcast(e2_ref[h, tg], BF16)
            for q in range(quad):
                il = iq * quad + q
                sel = rank2 < rows_bf16(c1[il:il + 1, :])
                gates[q] = gates[q] + jnp.where(sel, e2 * rows_bf16(e1[il:il + 1, :]), jnp.zeros_like(e2))
        for q in range(quad):
            il = iq * quad + q
            a = act_b[tg, il * nk:(il + 1) * nk, :]
            gelu = 0.5 * a * (1.0 + lax.erf(a * (2.0 ** -0.5)))
            p_b[tg, il * nk:(il + 1) * nk, :] = gates[q] * gelu.astype(BF16)

    def stages(act_a, act_b, p_a, p_b):
        pieces = _Pieces(piece for hv in halves for piece in
                         (functools.partial(stage_c, p_a, *hv), functools.partial(stage_a, act_a, *hv)))
        n_blocks = ngroups * (ni // quad)
        pieces.every = max(1, PEER_HEADS * n_blocks // len(pieces))
        pieces.phase = 0
        for tg in range(ngroups):
            for iq in range(ni // quad):
                stage_b(act_b, p_b, tg, iq, pieces)
        while pieces:
            pieces.pop(0)()

    @pl.when(k % 2 == 0)
    def _():
        stages(act0_ref, act1_ref, p0_ref, p1_ref)

    @pl.when(k % 2 == 1)
    def _():
        stages(act1_ref, act0_ref, p1_ref, p0_ref)

    @pl.when((tile_c == n_tiles - 1) & (k >= 2))
    def _():
        o_ref[...] = x1_ref[...] + mod_ref[0, 5:6, :] * acc_ref[...].T


def _peer_ffn(h2, u_bf, vt_bf, c1, e1, r2, e2, x1, mod6, seq, tn=512, te=1024):
    t, d = h2.shape
    ne = u_bf.shape[0]
    ngroups = tn // LANES
    n_tiles = ne // te
    n_items = (t // tn) * n_tiles

    def item(k, lag):
        w = jnp.clip(k - lag, 0, n_items - 1)
        return w // n_tiles, w % n_tiles

    stat = pl.BlockSpec((PEER_HEADS, ngroups, PEER_N_KEYS, LANES), lambda k: (0, item(k, 1)[0], 0, 0))
    pstat = pl.BlockSpec((PEER_HEADS, ngroups, PEER_N_KEYS // 2, LANES), lambda k: (0, item(k, 1)[0], 0, 0))
    return pl.pallas_call(
        functools.partial(_peer_kernel, tn=tn, te=te, n_tiles=n_tiles),
        out_shape=jax.ShapeDtypeStruct((t, d), F32),
        grid=(n_items + 2,),
        in_specs=[pl.BlockSpec((tn, d), lambda k: (item(k, 0)[0], 0)),
                  pl.BlockSpec((te, d), lambda k: (item(k, 0)[1], 0)),
                  pl.BlockSpec((d, te), lambda k: (0, item(k, 2)[1])),
                  stat, stat, pstat, pstat,
                  pl.BlockSpec((tn, d), lambda k: (item(k, 2)[0], 0)),
                  pl.BlockSpec((1, 6, d), lambda k: ((item(k, 2)[0] * tn) // seq, 0, 0))],
        out_specs=pl.BlockSpec((tn, d), lambda k: (item(k, 2)[0], 0)),
        scratch_shapes=[pltpu.VMEM((d, tn), F32),
                        pltpu.VMEM((ngroups, te, LANES), F32), pltpu.VMEM((ngroups, te, LANES), F32),
                        pltpu.VMEM((ngroups, te, LANES), BF16), pltpu.VMEM((ngroups, te, LANES), BF16)],
        compiler_params=_params("arbitrary"),
        name="peer_ffn",
    )(h2, u_bf, vt_bf, c1, e1, r2, e2, x1, mod6)


def _head_block_diag(n):
    idx = jnp.arange(n, dtype=jnp.int32) // HEAD_DIM
    return (idx[:, None] == idx[None, :]).astype(BF16)


def _layer(x, c, ada_w, ada_b, norm1_g, w_in, mu_prev, mu_next, q_norm_g, k_norm_g, w_decay0, w_decay_up,
           a_gate0, a_gate_up, g_up, k_k, k_a, r_k, lnx_g, lnx_b, w_out, norm2_g, peer_w_query,
           peer_sub_keys1, peer_sub_keys2, peer_u, peer_v):
    bsz, s, d = x.shape
    n_att_heads = ATT_WIDTH // HEAD_DIM
    bd512 = _head_block_diag(ATT_WIDTH)
    bd128 = _head_block_diag(LANES)

    mod6 = _ada(c, ada_w, ada_b).reshape(bsz, 6, d)
    gq = jnp.tile(q_norm_g, n_att_heads).reshape(1, ATT_WIDTH)
    gk = jnp.tile(k_norm_g, n_att_heads).reshape(1, ATT_WIDTH)
    qkv, zrw = _inproj(x, mod6, norm1_g.reshape(1, d), w_in.astype(BF16), gq, gk, bd512)

    att = _attention(qkv, n_att_heads)

    zero_w = jnp.zeros((2, LANES - DECAY_LORA, RWKV_WIDTH), F32)
    wup_pad = jnp.concatenate([w_decay_up, zero_w], axis=1).astype(BF16)
    aup_pad = jnp.concatenate([zero_w, a_gate_up], axis=1).astype(BF16)
    pk, plw = _rwkv_prep(zrw, mu_prev.reshape(1, -1), mu_next.reshape(1, -1), wup_pad, aup_pad, g_up.astype(BF16),
                         w_decay0, a_gate0, k_k.reshape(1, -1), k_a.reshape(1, -1), r_k.reshape(1, -1), bd512)
    rp, qp, gm, hm = _rwkv_p1(pk, plw)
    rw = _rwkv_p2(rp, qp, gm, hm, pk, lnx_g.reshape(1, -1), lnx_b.reshape(1, -1), bd128)

    x1, h2, s1t, s2t = _outproj(att, rw, x, mod6, norm2_g.reshape(1, d), w_out.astype(BF16),
                                peer_w_query.astype(BF16), peer_sub_keys1.astype(BF16),
                                peer_sub_keys2.astype(BF16))
    c1, e1, r2, e2 = _peer_topk(s1t, s2t)
    out = _peer_ffn(h2.reshape(bsz * s, d), peer_u.astype(BF16), peer_v.T.astype(BF16), c1, e1, r2, e2,
                    x1.reshape(bsz * s, d), mod6, s)
    return out.reshape(bsz, s, d)


def kernel(x, c, ada_w, ada_b, norm1_g, w_in, mu_prev, mu_next, q_norm_g, k_norm_g, w_decay0, w_decay_up,
           a_gate0, a_gate_up, g_up, k_k, k_a, r_k, lnx_g, lnx_b, w_out, norm2_g, peer_w_query,
           peer_sub_keys1, peer_sub_keys2, peer_u, peer_v):
    depth = ada_w.shape[0]
    for l in range(depth):
        x = _layer(x, c, ada_w[l], ada_b[l], norm1_g[l], w_in[l], mu_prev[l], mu_next[l], q_norm_g[l],
                   k_norm_g[l], w_decay0[l], w_decay_up[l], a_gate0[l], a_gate_up[l], g_up[l], k_k[l], k_a[l],
                   r_k[l], lnx_g[l], lnx_b[l], w_out[l], norm2_g[l], peer_w_query[l], peer_sub_keys1[l],
                   peer_sub_keys2[l], peer_u[l], peer_v[l])
    return x
```

```python
import functools
import math

import jax
import jax.numpy as jnp
from jax import lax
from jax.experimental import pallas as pl
from jax.experimental.pallas import tpu as pltpu

F32 = jnp.float32
BF16 = jnp.bfloat16

HEAD_DIM = 64
ATT_WIDTH = 512
RWKV_WIDTH = 512
DECAY_LORA = 64
AAA_LORA = 64
GATE_LORA = 128
SHIFT_WIDTH = 3 * RWKV_WIDTH + DECAY_LORA + AAA_LORA + GATE_LORA
PEER_N_KEYS = 128
PEER_HEADS = 8
PEER_TOPK = 16
NORM_EPS = 1e-6
LNX_EPS = 64e-5
MASK_VALUE = -1e30
LOG2E = 1.4426950408889634
ATT_HALF_WINDOWS = ((1, 64), (4, 256), (16, 1024))

LANES = 128
BF16_ROWS = 16
CHUNK = 64
ATT_BLOCK = 128
ATT_REACH = 1024 // ATT_BLOCK
VMEM_LIMIT_BYTES = 56 * 1024 * 1024
WKV_PASSES_CHAIN = 1

NN = (((1,), (0,)), ((), ()))
NT = (((1,), (1,)), ((), ()))
TN = (((0,), (0,)), ((), ()))

SLAB_R, SLAB_V, SLAB_KK, SLAB_DIR0, SLAB_G, SLAB_BONUS, NUM_SLABS = 0, 1, 2, 3, 7, 8, 9


def _params(*sem):
    return pltpu.CompilerParams(dimension_semantics=sem, vmem_limit_bytes=VMEM_LIMIT_BYTES)


def _dot(a, b, dims=NN):
    return lax.dot_general(a, b, dims, preferred_element_type=F32)


def _split(a):
    hi = a.astype(BF16)
    lo = (a - hi.astype(F32)).astype(BF16)
    return hi, lo


def _mm(a, b, dims=NN, passes=3):
    if passes == 1:
        return _dot(a.astype(BF16), b.astype(BF16), dims)
    ah, al = _split(a)
    bh, bl = _split(b)
    return _dot(ah, bh, dims) + (_dot(ah, bl, dims) + _dot(al, bh, dims))


def _mm_exact_rhs(a, b_exact):
    ah, al = _split(a)
    return _dot(ah, b_exact) + _dot(al, b_exact)


def _sigmoid(x):
    return 1.0 / (1.0 + jnp.exp(-x))


def _ada_kernel(c_ref, w_ref, b_ref, o_ref):
    c = c_ref[...]
    o_ref[...] = _mm(c * _sigmoid(c), w_ref[...]) + b_ref[...]


def _ada(c, ada_w, ada_b):
    bsz, d = c.shape
    n = ada_w.shape[1]
    tn = 1024
    return pl.pallas_call(
        _ada_kernel,
        out_shape=jax.ShapeDtypeStruct((bsz, n), F32),
        grid=(n // tn,),
        in_specs=[pl.BlockSpec((bsz, d), lambda j: (0, 0)),
                  pl.BlockSpec((d, tn), lambda j: (0, j)),
                  pl.BlockSpec((1, tn), lambda j: (0, j))],
        out_specs=pl.BlockSpec((bsz, tn), lambda j: (0, j)),
        compiler_params=_params("arbitrary"),
        name="ada",
    )(c, ada_w, ada_b.reshape(1, n))


def _inproj_kernel(x_ref, mod_ref, g1_ref, w_ref, gq_ref, gk_ref, bd_ref, qkv_ref, zrw_ref):
    x = x_ref[0]
    mod = mod_ref[0]
    ms = jnp.mean(x * x, axis=-1, keepdims=True)
    h = x * lax.rsqrt(ms + NORM_EPS) * g1_ref[...]
    h = (h * (1.0 + mod[1:2]) + mod[0:1]).astype(BF16)
    bd = bd_ref[...]

    def head_norm(z, g):
        ss = _mm_exact_rhs(z * z, bd) * (1.0 / HEAD_DIM)
        return z * lax.rsqrt(ss + NORM_EPS) * g

    a = ATT_WIDTH
    zq = _dot(h, w_ref[:, 0:a])
    qkv_ref[0, :, 0:a] = (head_norm(zq, gq_ref[...]) * (HEAD_DIM ** -0.5 * LOG2E)).astype(BF16)
    zk = _dot(h, w_ref[:, a:2 * a])
    qkv_ref[0, :, a:2 * a] = head_norm(zk, gk_ref[...]).astype(BF16)
    qkv_ref[0, :, 2 * a:3 * a] = _dot(h, w_ref[:, 2 * a:3 * a]).astype(BF16)
    zrw_ref[0] = _dot(h, w_ref[:, 3 * a:])


def _inproj(x, mod6, norm1_g, w_in_bf, gq, gk, bd512, tm=512):
    bsz, s, d = x.shape
    nin = w_in_bf.shape[1]
    return pl.pallas_call(
        _inproj_kernel,
        out_shape=(jax.ShapeDtypeStruct((bsz, s, 3 * ATT_WIDTH), BF16),
                   jax.ShapeDtypeStruct((bsz, s, SHIFT_WIDTH), F32)),
        grid=(bsz, s // tm),
        in_specs=[pl.BlockSpec((1, tm, d), lambda b, i: (b, i, 0)),
                  pl.BlockSpec((1, 6, d), lambda b, i: (b, 0, 0)),
                  pl.BlockSpec((1, d), lambda b, i: (0, 0)),
                  pl.BlockSpec((d, nin), lambda b, i: (0, 0)),
                  pl.BlockSpec((1, ATT_WIDTH), lambda b, i: (0, 0)),
                  pl.BlockSpec((1, ATT_WIDTH), lambda b, i: (0, 0)),
                  pl.BlockSpec((ATT_WIDTH, ATT_WIDTH), lambda b, i: (0, 0))],
        out_specs=(pl.BlockSpec((1, tm, 3 * ATT_WIDTH), lambda b, i: (b, i, 0)),
                   pl.BlockSpec((1, tm, SHIFT_WIDTH), lambda b, i: (b, i, 0))),
        compiler_params=_params("arbitrary", "arbitrary"),
        name="inproj",
    )(x, mod6, norm1_g, w_in_bf, gq, gk, bd512)


ATT_FAR_DILATION, ATT_FAR_HALF = ATT_HALF_WINDOWS[-1]
ATT_NEAR_BLOCKS = 6


def _attn_near_window(nkb):
    return min(nkb, ATT_NEAR_BLOCKS)


def _attn_near_start(qb, nkb):
    return jnp.clip(((qb - 2) // 2) * 2, 0, nkb - _attn_near_window(nkb))


def _attn_far_bias(n_heads, nkb):
    per_block = ATT_BLOCK // ATT_FAR_DILATION
    i = jnp.arange(ATT_BLOCK, dtype=jnp.int32)[:, None]
    ip = jnp.arange(ATT_BLOCK, dtype=jnp.int32)[None, :]
    start = _attn_near_start(i // per_block, nkb)
    kb = ip // per_block
    dist = ATT_FAR_DILATION * jnp.abs(i - ip)
    far = (dist <= ATT_FAR_HALF) & ~((kb >= start) & (kb < start + _attn_near_window(nkb)))
    slopes = jnp.exp2(-8.0 * (jnp.arange(n_heads, dtype=F32) + 1.0) / n_heads) * LOG2E
    return jnp.where(far[None], -slopes[:, None, None] * dist.astype(F32)[None], MASK_VALUE)


def _attn_bias_table(n_heads, nkb):
    reach = _attn_near_window(nkb) - 1
    nd = 2 * reach + 1
    dd = jnp.arange(nd, dtype=jnp.int32)[None, :, None]
    r = jnp.arange(ATT_BLOCK, dtype=jnp.int32)[:, None, None]
    c = jnp.arange(ATT_BLOCK, dtype=jnp.int32)[None, None, :]
    dt = r - c - (dd - reach) * ATT_BLOCK
    adt = jnp.abs(dt)
    mult = jnp.zeros(dt.shape, F32)
    for dil, half in ATT_HALF_WINDOWS:
        mult = mult + ((adt <= half) & (dt % dil == 0)).astype(F32)
    logm = jnp.where(mult > 0, jnp.log2(jnp.maximum(mult, 1.0)), MASK_VALUE)
    slopes = jnp.exp2(-8.0 * (jnp.arange(n_heads, dtype=F32) + 1.0) / n_heads) * LOG2E
    bias = logm[None] - slopes[:, None, None, None] * adt.astype(F32)[None]
    return bias.reshape(n_heads, ATT_BLOCK, nd * ATT_BLOCK)


def _attn_kernel(q_ref, k_ref, v_ref, bias_ref, far_ref, o_ref, s_ref, v1_ref, g_ref, accf_ref, mf_ref,
                 *, nkb, win):
    blk = ATT_BLOCK
    reach = win - 1
    has_far = win < nkb
    lane = lax.broadcasted_iota(jnp.int32, (blk, LANES), 1)
    first = lane < HEAD_DIM

    def two_heads(q):
        zero = jnp.zeros_like(q)
        return jnp.concatenate([jnp.where(first, q, zero), jnp.where(first, zero, q)], axis=0)

    def with_ones(v, hh):
        is_own = (lax.broadcasted_iota(jnp.int32, v.shape, 1) < HEAD_DIM) == (hh == 0)
        return jnp.where(is_own, v, jnp.ones_like(v))

    v = v_ref[0]
    for hh in range(2):
        v1_ref[hh] = with_ones(v, hh)

    if has_far:
        g_ref[0] = q_ref[0].astype(F32)
        g_ref[1] = k_ref[0].astype(F32)
        g_ref[2] = v.astype(F32)
        n_far = nkb * blk // ATT_FAR_DILATION

        def residue(r, _):
            rows = pl.ds(r, n_far, stride=ATT_FAR_DILATION)
            qr = g_ref[0, rows, :].astype(BF16)
            kr = g_ref[1, rows, :].astype(BF16)
            vr = g_ref[2, rows, :].astype(BF16)
            s = _dot(two_heads(qr), kr, NT)
            for hh in range(2):
                sb = s[hh * blk:(hh + 1) * blk] + far_ref[hh]
                mx = jnp.max(sb, axis=-1, keepdims=True)
                p = jnp.exp2(sb - mx).astype(BF16)
                accf_ref[hh, rows, :] = _dot(p, with_ones(vr, hh))
                mf_ref[hh, rows, :] = jnp.broadcast_to(mx, (blk, LANES))
            return 0

        lax.fori_loop(0, ATT_FAR_DILATION, residue, 0, unroll=4)

    group = s_ref.shape[0]
    slabs = range(0, win, 2)

    def query_block_group(i, _):
        us = range(group)
        qis = [group * i + u for u in us]
        rows = [pl.ds(pl.multiple_of(qi * blk, blk), blk) for qi in qis]
        qab = [two_heads(q_ref[0, rows[u], :]) for u in us]
        ks = [_attn_near_start(qi, nkb) if has_far else 0 for qi in qis]

        m = [[jnp.full((blk, LANES), -jnp.inf, F32) for _ in range(2)] for _ in us]
        for w0 in slabs:
            for u in us:
                kj0 = ks[u] + w0
                kslab = k_ref[0, pl.ds(pl.multiple_of(kj0 * blk, blk), 2 * blk), :]
                s = _dot(qab[u], kslab, NT)
                x0 = pl.multiple_of((kj0 - qis[u] + reach) * blk, blk)
                for hh in range(2):
                    sb = s[hh * blk:(hh + 1) * blk] + bias_ref[hh, :, pl.ds(x0, 2 * blk)]
                    s_ref[u, hh, :, w0 * blk:(w0 + 2) * blk] = sb
                    m[u][hh] = jnp.maximum(m[u][hh], jnp.maximum(sb[:, :blk], sb[:, blk:]))
        mrow = [[jnp.max(m[u][hh], axis=-1, keepdims=True) for hh in range(2)] for u in us]

        if has_far:
            acc = [[None, None] for _ in us]
            for u in us:
                for hh in range(2):
                    m_far = mf_ref[hh, rows[u], :][:, 0:1]
                    m_all = jnp.maximum(mrow[u][hh], m_far)
                    acc[u][hh] = accf_ref[hh, rows[u], :] * jnp.exp2(m_far - m_all)
                    mrow[u][hh] = m_all
        else:
            acc = [[jnp.zeros((blk, LANES), F32) for _ in range(2)] for _ in us]
        for w0 in slabs:
            for u in us:
                keys = pl.ds(pl.multiple_of((ks[u] + w0) * blk, blk), 2 * blk)
                for hh in range(2):
                    p = jnp.exp2(s_ref[u, hh, :, w0 * blk:(w0 + 2) * blk] - mrow[u][hh]).astype(BF16)
                    acc[u][hh] = acc[u][hh] + _dot(p, v1_ref[hh, keys, :])
        for u in us:
            num = jnp.where(first, acc[u][0], acc[u][1])
            den = jnp.where(first, pltpu.roll(acc[u][0], HEAD_DIM, axis=1), pltpu.roll(acc[u][1], HEAD_DIM, axis=1))
            o_ref[0, rows[u], :] = (num / den).astype(o_ref.dtype)
        return 0

    lax.fori_loop(0, nkb // group, query_block_group, 0)


def _attention(qkv, n_heads):
    bsz, s, _ = qkv.shape
    nkb = s // ATT_BLOCK
    win = _attn_near_window(nkb)
    assert nkb % 2 == 0 and (win == nkb or s // ATT_FAR_DILATION == ATT_BLOCK)
    bias = _attn_bias_table(n_heads, nkb)
    far = _attn_far_bias(n_heads, nkb)
    npair = ATT_WIDTH // LANES
    return pl.pallas_call(
        functools.partial(_attn_kernel, nkb=nkb, win=win),
        out_shape=jax.ShapeDtypeStruct((bsz, s, ATT_WIDTH), BF16),
        grid=(npair, bsz),
        in_specs=[pl.BlockSpec((1, s, LANES), lambda hp, b: (b, 0, hp)),
                  pl.BlockSpec((1, s, LANES), lambda hp, b: (b, 0, npair + hp)),
                  pl.BlockSpec((1, s, LANES), lambda hp, b: (b, 0, 2 * npair + hp)),
                  pl.BlockSpec((2, ATT_BLOCK, bias.shape[2]), lambda hp, b: (hp, 0, 0)),
                  pl.BlockSpec((2, ATT_BLOCK, ATT_BLOCK), lambda hp, b: (hp, 0, 0))],
        out_specs=pl.BlockSpec((1, s, LANES), lambda hp, b: (b, 0, hp)),
        scratch_shapes=[pltpu.VMEM((math.gcd(nkb, 4), 2, ATT_BLOCK, win * ATT_BLOCK), F32), pltpu.VMEM((2, s, LANES), BF16),
                        pltpu.VMEM((3, s, LANES), F32), pltpu.VMEM((2, s, LANES), F32),
                        pltpu.VMEM((2, s, LANES), F32)],
        compiler_params=_params("arbitrary", "arbitrary"),
        name="attn",
    )(qkv, qkv, qkv, bias, far)


def _prep_kernel(z_ref, zp_ref, zn_ref, mup_ref, mun_ref, wup_ref, aup_ref, gup_ref, w0_ref, a0_ref,
                 kk_ref, ka_ref, rk_ref, bd_ref, o_ref, lw_ref, *, tq):
    i = pl.program_id(1)
    last = pl.num_programs(1) - 1
    z = z_ref[0]
    row = lax.broadcasted_iota(jnp.int32, (tq, 1), 0)
    prev_row = zp_ref[0, 7:8, :] * (i > 0).astype(F32)
    next_row = zn_ref[0, 0:1, :] * (i < last).astype(F32)
    zp = jnp.where(row == 0, prev_row, pltpu.roll(z, 1, axis=0))
    zn = jnp.where(row == tq - 1, next_row, pltpu.roll(z, tq - 1, axis=0))
    zs = z + mup_ref[...] * (zp - z) + mun_ref[...] * (zn - z)

    w = RWKV_WIDTH
    r = zs[:, 0:w]
    k = zs[:, w:2 * w]
    v = zs[:, 2 * w:3 * w]
    xwa = zs[:, 3 * w:3 * w + LANES]
    xg = zs[:, 3 * w + LANES:]
    bd = bd_ref[...]

    g = _dot(_sigmoid(xg).astype(BF16), gup_ref[...])
    kk = k * kk_ref[...]
    ss = _mm_exact_rhs(kk * kk, bd)
    kk = kk * lax.rsqrt(jnp.maximum(ss, 1e-12))
    ka = ka_ref[...]
    txw = jnp.tanh(xwa).astype(BF16)
    xab = xwa.astype(BF16)

    def put(slab, val):
        o_ref[0, :, slab * w:(slab + 1) * w] = val.astype(o_ref.dtype)

    put(SLAB_R, r)
    put(SLAB_V, v)
    put(SLAB_KK, kk)
    put(SLAB_G, g)
    a_sum = jnp.zeros_like(k)
    for d in range(2):
        y = w0_ref[d:d + 1, :] + _dot(txw, wup_ref[d])
        wlog = -(jnp.maximum(-y, 0.0) + jnp.log(1.0 + jnp.exp(-jnp.abs(y)))) - 0.5
        a = _sigmoid(a0_ref[d:d + 1, :] + _dot(xab, aup_ref[d]))
        a_sum = a_sum + a
        lw_ref[0, :, d * w:(d + 1) * w] = -jnp.exp(wlog)
        put(SLAB_DIR0 + 2 * d, k * (1.0 + (a - 1.0) * ka))
        put(SLAB_DIR0 + 2 * d + 1, kk * a)
    k_bonus = k * (1.0 + (0.5 * a_sum - 1.0) * ka)
    bsum = _mm_exact_rhs(r * k_bonus * rk_ref[...], bd)
    put(SLAB_BONUS, bsum * v)


def _rwkv_prep(zrw, mu_prev, mu_next, wup_pad, aup_pad, g_up_bf, w0, a0, k_k, k_a, r_k, bd512, tq=512):
    bsz, s, sw = zrw.shape
    w = RWKV_WIDTH
    nt = s // tq
    row = lambda b, i: (0, 0)
    return pl.pallas_call(
        functools.partial(_prep_kernel, tq=tq),
        out_shape=(jax.ShapeDtypeStruct((bsz, s, NUM_SLABS * w), BF16),
                   jax.ShapeDtypeStruct((bsz, s, 2 * w), F32)),
        grid=(bsz, nt),
        in_specs=[pl.BlockSpec((1, tq, sw), lambda b, i: (b, i, 0)),
                  pl.BlockSpec((1, 8, sw), lambda b, i: (b, jnp.maximum(i * (tq // 8) - 1, 0), 0)),
                  pl.BlockSpec((1, 8, sw), lambda b, i: (b, jnp.minimum((i + 1) * (tq // 8), s // 8 - 1), 0)),
                  pl.BlockSpec((1, sw), row),
                  pl.BlockSpec((1, sw), row),
                  pl.BlockSpec((2, LANES, w), lambda b, i: (0, 0, 0)),
                  pl.BlockSpec((2, LANES, w), lambda b, i: (0, 0, 0)),
                  pl.BlockSpec((GATE_LORA, w), row),
                  pl.BlockSpec((2, w), row),
                  pl.BlockSpec((2, w), row),
                  pl.BlockSpec((1, w), row),
                  pl.BlockSpec((1, w), row),
                  pl.BlockSpec((1, w), row),
                  pl.BlockSpec((w, w), row)],
        out_specs=(pl.BlockSpec((1, tq, NUM_SLABS * w), lambda b, i: (b, i, 0)),
                   pl.BlockSpec((1, tq, 2 * w), lambda b, i: (b, i, 0))),
        compiler_params=_params("arbitrary", "arbitrary"),
        name="rwkv_prep",
    )(zrw, zrw, zrw, mu_prev, mu_next, wup_pad, aup_pad, g_up_bf, w0, a0, k_k, k_a, r_k, bd512)


def _block_diag(x, first):
    zero = jnp.zeros_like(x)
    return jnp.concatenate([jnp.where(first, x, zero), jnp.where(first, zero, x)], axis=0)


def _p1_kernel(r_ref, v_ref, kk_ref, lw_ref, kd_ref, be_ref, rp_ref, qp_ref, g_ref, h_ref, *, cpb):
    c = CHUNK
    sign = 1 - 2 * pl.program_id(2)
    rowi = lax.broadcasted_iota(jnp.int32, (c, LANES), 0)
    lane = lax.broadcasted_iota(jnp.int32, (c, LANES), 1)
    coli = lane & (c - 1)
    first = lane < HEAD_DIM
    before = sign * (rowi - coli)
    strict2 = before > 0
    incl2 = before >= 0
    eye2 = (coli == rowi).astype(F32)
    r64 = lax.broadcasted_iota(jnp.int32, (c, c), 0)
    c64 = lax.broadcasted_iota(jnp.int32, (c, c), 1)
    cum = (sign * (r64 - c64) >= 0).astype(BF16)

    js = range(cpb)
    sls = [pl.ds(j * c, c) for j in js]
    bdg = lambda x: _block_diag(x, first)
    r = [r_ref[0, sl, :].astype(F32) for sl in sls]
    v = [v_ref[0, sl, :].astype(F32) for sl in sls]
    kk = [kk_ref[0, sl, :].astype(F32) for sl in sls]
    lw = [lw_ref[0, sl, :] for sl in sls]
    kd = [kd_ref[0, sl, :].astype(F32) for sl in sls]
    be = [be_ref[0, sl, :].astype(F32) for sl in sls]

    one = lambda x, y, dims=NN: _mm(x, y, dims, 1)
    rows2 = lambda x, y: jnp.concatenate([x, y], axis=0)
    cols2 = lambda x, y: jnp.concatenate([x, y], axis=1)

    def cumsum(x):
        l1 = x.astype(BF16)
        rem = x - l1.astype(F32)
        l2 = rem.astype(BF16)
        l3 = (rem - l2.astype(F32)).astype(BF16)
        y = _dot(cum, jnp.concatenate([l1, l2, l3], axis=1))
        return y[:, :LANES] + (y[:, LANES:2 * LANES] + y[:, 2 * LANES:])

    cs = [cumsum(lw[j]) for j in js]
    g_inv = [jnp.exp(-cs[j]) for j in js]
    g_tot = [jnp.exp(jnp.sum(lw[j], axis=0, keepdims=True)) for j in js]
    ab = [-kk[j] * jnp.exp(cs[j] - lw[j]) for j in js]
    rb = [r[j] * jnp.exp(cs[j]) for j in js]
    bt = [be[j] * g_inv[j] for j in js]
    kt = [kd[j] * g_inv[j] for j in js]

    sc = [one(rows2(ab[j], rb[j]), rows2(bdg(bt[j]), bdg(kt[j])), NT) for j in js]
    m_ab = [jnp.where(strict2, sc[j][:c, :LANES], 0.0) for j in js]
    n_rb = [jnp.where(incl2, sc[j][c:, :LANES], 0.0) for j in js]
    m_ak = [jnp.where(strict2, sc[j][:c, LANES:], 0.0) for j in js]
    n_rk = [jnp.where(incl2, sc[j][c:, LANES:], 0.0) for j in js]

    levels = int(math.log2(c)) - 1
    t = [eye2 + m_ab[j] for j in js]
    p = [one(m_ab[j], bdg(m_ab[j])) for j in js]
    for _ in range(1, levels - 1):
        y = [one(rows2(p[j], t[j]), bdg(p[j])) for j in js]
        p = [y[j][:c] for j in js]
        t = [t[j] + y[j][c:] for j in js]
    t = [t[j] + one(t[j], bdg(p[j])) for j in js]

    def residual(a, tt):
        ah, al = _split(a)
        th, tl = _split(bdg(tt))
        y = _dot(rows2(ah, al), th)
        return eye2 - (y[:c] + (y[c:] + _dot(ah, tl)))

    res = [residual(eye2 - m_ab[j], t[j]) for j in js]
    t = [t[j] + one(t[j], bdg(res[j])) for j in js]

    vbd = [bdg(v[j]) for j in js]
    mv = [one(rows2(m_ak[j], n_rk[j]), vbd[j]) for j in js]
    tp = [one(t[j], cols2(bdg(ab[j]), bdg(mv[j][:c]))) for j in js]
    ap = [tp[j][:, :LANES] for j in js]
    pp = [tp[j][:, LANES:] for j in js]
    nr = [one(n_rb[j], cols2(bdg(ap[j]), bdg(pp[j]))) for j in js]
    for j in js:
        rp_ref[0, 0, 0, sls[j], :] = rb[j] + nr[j][:, :LANES]
        qp_ref[0, 0, 0, sls[j], :] = nr[j][:, LANES:] + mv[j][c:]
    heads = lambda x: jnp.where(first, x[:HEAD_DIM], x[HEAD_DIM:])
    for j in js:
        st = one(cols2(ap[j], pp[j]), bt[j], TN)
        g_ref[0, 0, 0, j] = (eye2 + heads(st[:LANES])) * g_tot[j]
        h_ref[0, 0, 0, j] = heads(st[LANES:] + one(v[j], kt[j], TN)) * g_tot[j]


def _rwkv_p1(pk, plw, cpb=32):
    bsz, s, _ = pk.shape
    nc = s // CHUNK
    npair = RWKV_WIDTH // LANES
    rows = cpb * CHUNK

    def slab(sidx):
        return pl.BlockSpec((1, rows, LANES), lambda b, hp, d, ci: (b, ci, sidx * npair + hp))

    def dslab(off):
        return pl.BlockSpec((1, rows, LANES),
                            lambda b, hp, d, ci: (b, ci, (SLAB_DIR0 + 2 * d + off) * npair + hp))

    lw_spec = pl.BlockSpec((1, rows, LANES), lambda b, hp, d, ci: (b, ci, d * npair + hp))

    seq = pl.BlockSpec((1, 1, 1, rows, LANES), lambda b, hp, d, ci: (b, hp, d, ci, 0))
    mat = pl.BlockSpec((1, 1, 1, cpb, HEAD_DIM, LANES), lambda b, hp, d, ci: (b, hp, d, ci, 0, 0))
    return pl.pallas_call(
        functools.partial(_p1_kernel, cpb=cpb),
        out_shape=(jax.ShapeDtypeStruct((bsz, npair, 2, s, LANES), F32),
                   jax.ShapeDtypeStruct((bsz, npair, 2, s, LANES), F32),
                   jax.ShapeDtypeStruct((bsz, npair, 2, nc, HEAD_DIM, LANES), F32),
                   jax.ShapeDtypeStruct((bsz, npair, 2, nc, HEAD_DIM, LANES), F32)),
        grid=(bsz, npair, 2, nc // cpb),
        in_specs=[slab(SLAB_R), slab(SLAB_V), slab(SLAB_KK), lw_spec, dslab(0), dslab(1)],
        out_specs=(seq, seq, mat, mat),
        compiler_params=_params("arbitrary", "arbitrary", "arbitrary", "arbitrary"),
        name="rwkv_p1",
    )(pk, pk, pk, plw, pk, pk)


def _p2_kernel(rp_ref, qp_ref, g_ref, h_ref, gate_ref, bonus_ref, lng_ref, lnb_ref, bd_ref, o_ref,
               y_ref, s_ref, *, nc, te, hpb):
    c = CHUNK
    s_ref[...] = jnp.zeros_like(s_ref)
    first = lax.broadcasted_iota(jnp.int32, (HEAD_DIM, LANES), 1) < HEAD_DIM
    chains = [(hq, d) for hq in range(hpb) for d in range(2)]

    def step(j, _):
        for hq, d in chains:
            jc = j if d == 0 else nc - 1 - j
            sl = pl.ds(pl.multiple_of(jc * c, c), c)
            st = s_ref[hq, d]
            y_ref[hq, d, sl, :] = (_mm(rp_ref[0, hq, d, sl, :], st, NT, WKV_PASSES_CHAIN)
                                   + qp_ref[0, hq, d, sl, :])
            gmat = _block_diag(g_ref[0, hq, d, jc], first)
            hmat = _block_diag(h_ref[0, hq, d, jc], first)
            s_ref[hq, d] = _mm(st, gmat, NN, WKV_PASSES_CHAIN) + hmat
        return 0

    lax.fori_loop(0, nc, step, 0)

    bd = bd_ref[...]
    inv = 1.0 / HEAD_DIM

    def epi(i, _):
        sl = pl.ds(pl.multiple_of(i * te, te), te)
        for hq in range(hpb):
            lanes = slice(hq * LANES, (hq + 1) * LANES)
            y = y_ref[hq, 0, sl, :] + y_ref[hq, 1, sl, :]
            mean = _mm_exact_rhs(y, bd) * inv
            yc = y - mean
            var = _mm_exact_rhs(yc * yc, bd) * inv
            yn = yc * lax.rsqrt(var + LNX_EPS) * lng_ref[:, lanes] + lnb_ref[:, lanes]
            o_ref[0, sl, lanes] = ((yn + bonus_ref[0, sl, lanes].astype(F32))
                                   * gate_ref[0, sl, lanes].astype(F32)).astype(o_ref.dtype)
        return 0

    lax.fori_loop(0, (nc * c) // te, epi, 0)


def _rwkv_p2(rp, qp, gm, hm, pk, lnx_g, lnx_b, bd128, te=1024, hpb=2):
    bsz, npair, _, s, _ = rp.shape
    nc = s // CHUNK
    width = hpb * LANES
    nq = npair // hpb
    seq = pl.BlockSpec((1, hpb, 2, s, LANES), lambda b, q: (b, q, 0, 0, 0))
    mat = pl.BlockSpec((1, hpb, 2, nc, HEAD_DIM, LANES), lambda b, q: (b, q, 0, 0, 0, 0))
    return pl.pallas_call(
        functools.partial(_p2_kernel, nc=nc, te=min(te, s), hpb=hpb),
        out_shape=jax.ShapeDtypeStruct((bsz, s, RWKV_WIDTH), BF16),
        grid=(bsz, nq),
        in_specs=[seq, seq, mat, mat,
                  pl.BlockSpec((1, s, width), lambda b, q: (b, 0, SLAB_G * nq + q)),
                  pl.BlockSpec((1, s, width), lambda b, q: (b, 0, SLAB_BONUS * nq + q)),
                  pl.BlockSpec((1, width), lambda b, q: (0, q)),
                  pl.BlockSpec((1, width), lambda b, q: (0, q)),
                  pl.BlockSpec((LANES, LANES), lambda b, q: (0, 0))],
        out_specs=pl.BlockSpec((1, s, width), lambda b, q: (b, 0, q)),
        scratch_shapes=[pltpu.VMEM((hpb, 2, s, LANES), F32), pltpu.VMEM((hpb, 2, LANES, LANES), F32)],
        compiler_params=_params("arbitrary", "arbitrary"),
        name="rwkv_p2",
    )(rp, qp, gm, hm, pk, pk, lnx_g, lnx_b, bd128)


def _outproj_kernel(att_ref, rw_ref, x_ref, mod_ref, g2_ref, wo_ref, wq_ref, k1_ref, k2_ref,
                    x1_ref, h2_ref, s1_ref, s2_ref):
    mod = mod_ref[0]
    a = ATT_WIDTH
    mixed = _dot(att_ref[0], wo_ref[0:a, :]) + _dot(rw_ref[0], wo_ref[a:, :])
    x1 = x_ref[0] + mod[2:3] * mixed
    x1_ref[0] = x1
    ms = jnp.mean(x1 * x1, axis=-1, keepdims=True)
    h2 = x1 * lax.rsqrt(ms + NORM_EPS) * g2_ref[...]
    h2 = (h2 * (1.0 + mod[4:5]) + mod[3:4]).astype(BF16)
    h2_ref[0] = h2
    q = _dot(h2, wq_ref[...])
    k1 = k1_ref[...]
    k2 = k2_ref[...]
    for h in range(PEER_HEADS):
        base = h * 2 * LANES
        s1 = _dot(k1, q[:, base:base + LANES].astype(BF16), NT)
        s2 = _dot(k2, q[:, base + LANES:base + 2 * LANES].astype(BF16), NT)
        for g in range(s1.shape[1] // LANES):
            s1_ref[h, g] = s1[:, g * LANES:(g + 1) * LANES]
            s2_ref[h, g] = s2[:, g * LANES:(g + 1) * LANES]


def _outproj(att, rw, x, mod6, norm2_g, w_out_bf, wq_bf, k1_bf, k2_bf, tm=512):
    bsz, s, d = x.shape
    nt = s // tm
    t = bsz * s
    nq = wq_bf.shape[1]
    const = lambda b, i: (0, 0)
    gpt = tm // LANES
    tok = pl.BlockSpec((PEER_HEADS, gpt, PEER_N_KEYS, LANES), lambda b, i: (0, b * nt + i, 0, 0))
    stat = jax.ShapeDtypeStruct((PEER_HEADS, t // LANES, PEER_N_KEYS, LANES), F32)
    return pl.pallas_call(
        _outproj_kernel,
        out_shape=(jax.ShapeDtypeStruct((bsz, s, d), F32),
                   jax.ShapeDtypeStruct((bsz, s, d), BF16),
                   stat, stat),
        grid=(bsz, nt),
        in_specs=[pl.BlockSpec((1, tm, ATT_WIDTH), lambda b, i: (b, i, 0)),
                  pl.BlockSpec((1, tm, RWKV_WIDTH), lambda b, i: (b, i, 0)),
                  pl.BlockSpec((1, tm, d), lambda b, i: (b, i, 0)),
                  pl.BlockSpec((1, 6, d), lambda b, i: (b, 0, 0)),
                  pl.BlockSpec((1, d), const),
                  pl.BlockSpec((d, d), const),
                  pl.BlockSpec((d, nq), const),
                  pl.BlockSpec((PEER_N_KEYS, LANES), const),
                  pl.BlockSpec((PEER_N_KEYS, LANES), const)],
        out_specs=(pl.BlockSpec((1, tm, d), lambda b, i: (b, i, 0)),
                   pl.BlockSpec((1, tm, d), lambda b, i: (b, i, 0)),
                   tok, tok),
        compiler_params=_params("arbitrary", "arbitrary"),
        name="outproj",
    )(att, rw, x, mod6, norm2_g, w_out_bf, wq_bf, k1_bf, k2_bf)


def _oddeven_sort_pairs(n):
    pairs = []
    p = 1
    while p < n:
        k = p
        while k >= 1:
            for j in range(k % p, n - k, 2 * k):
                for i in range(min(k, n - j - k)):
                    if (i + j) // (2 * p) == (i + j + k) // (2 * p):
                        pairs.append((i + j, i + j + k))
            k //= 2
        p *= 2
    return pairs


def _bitonic_merge_pairs(n):
    pairs = []
    stride = n // 2
    while stride >= 1:
        pairs += [(i, i + stride) for i in range(n) if not i & stride]
        stride //= 2
    return pairs


def _compare_exchange(vals, pairs):
    vals = list(vals)
    for i, j in pairs:
        a, b = vals[i], vals[j]
        if b is None:
            continue
        if a is None:
            vals[i], vals[j] = b, None
        else:
            vals[i], vals[j] = jnp.maximum(a, b), jnp.minimum(a, b)
    return vals


def _top16_sorted(vregs):
    kk = PEER_TOPK
    vals = list(vregs) + [None] * (kk - len(vregs))
    vals = _compare_exchange(vals, _oddeven_sort_pairs(kk))
    for shift in (4, 2, 1):
        other = [None if v is None else pltpu.roll(v, shift, axis=0) for v in vals]
        merged = []
        for k in range(kk):
            a, b = vals[k], other[kk - 1 - k]
            merged.append(b if a is None else a if b is None else jnp.maximum(a, b))
        vals = _compare_exchange(merged, _bitonic_merge_pairs(kk))
    return vals


def _topk_kernel(s1_ref, s2_ref, c1_ref, e1_ref, r2_ref, e2_ref, *, groups):
    kk = PEER_TOPK
    sub = 8
    neg = -jnp.inf
    rows8 = lax.broadcasted_iota(jnp.int32, (sub, LANES), 0)

    def rows_of(reps):
        out = reps[0]
        for r in range(1, sub):
            out = jnp.where(rows8 == r, reps[r], out)
        return out

    def all_sublanes_sum(x):
        for shift in (4, 2, 1):
            x = x + pltpu.roll(x, shift, axis=0)
        return x

    def per_tile(it_idx, _):
        g = it_idx // PEER_HEADS
        h = it_idx % PEER_HEADS
        s1 = [s1_ref[h, g, pl.ds(v * sub, sub), :] for v in range(PEER_N_KEYS // sub)]
        s2 = [s2_ref[h, g, pl.ds(v * sub, sub), :] for v in range(PEER_N_KEYS // sub)]
        a = _top16_sorted(s1)
        b = _top16_sorted(s2)
        a_lo, a_hi = rows_of(a[:sub]), rows_of(a[sub:])
        b_lo, b_hi = rows_of(b[:sub]), rows_of(b[sub:])

        cand = [a[0] + b_lo, a[0] + b_hi, a[1] + b_lo]
        for i in range(2, sub):
            cand.append(jnp.where(rows8 < kk // (i + 1), a[i] + b_lo, neg))
        cand.append(a_hi + b[0])
        top = _top16_sorted(cand)
        tau = top[kk - 1]
        z = None
        for t in top:
            ez = jnp.exp(t - top[0])
            z = ez if z is None else z + ez
        inv_z = 1.0 / z

        counts = []
        for r in range(kk):
            hits = (jnp.where(a[r] + b_lo >= tau, 1.0, 0.0) + jnp.where(a[r] + b_hi >= tau, 1.0, 0.0))
            counts.append(all_sublanes_sum(hits))
        for v in range(PEER_N_KEYS // sub):
            rows = pl.ds(v * sub, sub)
            cnt = jnp.zeros((sub, LANES), F32)
            rank = jnp.full((sub, LANES), float(kk), F32)
            for r in range(kk):
                cnt = jnp.where(s1[v] == a[r], counts[r], cnt)
                rank = jnp.where(s2[v] == b[r], float(r), rank)
            c1_ref[h, g, rows, :] = cnt
            e1_ref[h, g, rows, :] = jnp.exp(s1[v] - a[0]) * inv_z
            s2[v] = (rank, jnp.exp(s2[v] - b[0]))
        rank2 = jnp.concatenate([rv for rv, _ in s2], axis=0).astype(BF16)
        e2 = jnp.concatenate([ev for _, ev in s2], axis=0).astype(BF16)
        r2_ref[h, g] = pltpu.bitcast(rank2, jnp.uint32)
        e2_ref[h, g] = pltpu.bitcast(e2, jnp.uint32)
        return 0

    lax.fori_loop(0, groups * PEER_HEADS, per_tile, 0)


def _peer_topk(s1t, s2t, tn=512):
    nh, ng, nk, _ = s1t.shape
    groups = tn // LANES
    blk = pl.BlockSpec((nh, groups, nk, LANES), lambda i: (0, i, 0, 0))
    f32 = jax.ShapeDtypeStruct(s1t.shape, F32)
    b16 = jax.ShapeDtypeStruct((nh, ng, nk // 2, LANES), jnp.uint32)
    pblk = pl.BlockSpec((nh, groups, nk // 2, LANES), lambda i: (0, i, 0, 0))
    return pl.pallas_call(
        functools.partial(_topk_kernel, groups=groups),
        out_shape=(f32, f32, b16, b16),
        grid=(ng // groups,),
        in_specs=[blk, blk],
        out_specs=(blk, blk, pblk, pblk),
        compiler_params=_params("arbitrary"),
        name="peer_topk",
    )(s1t, s2t)


class _Pieces(list):
    every = 1
    phase = 0


def _peer_kernel(h2_ref, u_ref, vt_ref, c1_ref, e1_ref, r2_ref, e2_ref, x1_ref, mod_ref, o_ref,
                 acc_ref, act0_ref, act1_ref, p0_ref, p1_ref, *, tn, te, n_tiles):
    k = pl.program_id(0)
    n_items = pl.num_programs(0) - 2
    nk = PEER_N_KEYS
    ngroups = tn // LANES

    @pl.when(k == 0)
    def _():
        acc_ref[...] = jnp.zeros_like(acc_ref)
        for ref in (act0_ref, act1_ref, p0_ref, p1_ref):
            ref[...] = jnp.zeros_like(ref)

    ni = te // nk
    i0 = pl.multiple_of((jnp.clip(k - 1, 0, n_items - 1) % n_tiles) * ni, ni)
    tile_c = jnp.clip(k - 2, 0, n_items - 1) % n_tiles

    d_model = acc_ref.shape[0]
    ksplit = 4
    msplit = 2
    halves = [(mh, nh, kh) for mh in range(msplit) for nh in range(ngroups // 2) for kh in range(ksplit)]

    def stage_c(p_a, mh, nh, kh):
        rows = slice(mh * (d_model // msplit), (mh + 1) * (d_model // msplit))
        cols = slice(nh * 2 * LANES, (nh + 1) * 2 * LANES)
        ks = slice(kh * (te // ksplit), (kh + 1) * (te // ksplit))
        p_prev = jnp.concatenate([p_a[2 * nh, ks, :], p_a[2 * nh + 1, ks, :]], axis=1)
        acc_ref[rows, cols] += _dot(vt_ref[rows, ks], p_prev)

    def stage_a(act_a, mh, nh, kh):
        rows = slice(mh * (te // msplit), (mh + 1) * (te // msplit))
        ks = slice(kh * (d_model // ksplit), (kh + 1) * (d_model // ksplit))
        act = _dot(u_ref[rows, ks], h2_ref[nh * 2 * LANES:(nh + 1) * 2 * LANES, ks], NT)
        if kh == 0:
            act_a[2 * nh, rows, :] = act[:, :LANES]
            act_a[2 * nh + 1, rows, :] = act[:, LANES:]
        else:
            act_a[2 * nh, rows, :] += act[:, :LANES]
            act_a[2 * nh + 1, rows, :] += act[:, LANES:]

    def rows_bf16(row):
        tile = jnp.broadcast_to(row, (BF16_ROWS, LANES)).astype(BF16)
        return jnp.concatenate([tile] * (nk // BF16_ROWS), axis=0)

    quad = 4

    def stage_b(act_b, p_b, tg, iq, pieces):
        gates = [jnp.zeros((nk, LANES), BF16) for _ in range(quad)]
        for h in range(PEER_HEADS):
            if h % pieces.every == pieces.phase and pieces:
                pieces.pop(0)()
            c1 = c1_ref[h, tg, pl.ds(i0, ni), :]
            e1 = e1_ref[h, tg, pl.ds(i0, ni), :]
            rank2 = pltpu.bitcast(r2_ref[h, tg], BF16)
            e2 = pltpu.bitcast(e2_ref[h, tg], BF16)
            for q in range(quad):
                il = iq * quad + q
                sel = rank2 < rows_bf16(c1[il:il + 1, :])
                gates[q] = gates[q] + jnp.where(sel, e2 * rows_bf16(e1[il:il + 1, :]), jnp.zeros_like(e2))
        for q in range(quad):
            il = iq * quad + q
            a = act_b[tg, il * nk:(il + 1) * nk, :]
            gelu = 0.5 * a * (1.0 + lax.erf(a * (2.0 ** -0.5)))
            p_b[tg, il * nk:(il + 1) * nk, :] = gates[q] * gelu.astype(BF16)

    def stages(act_a, act_b, p_a, p_b):
        pieces = _Pieces([functools.partial(stage_c, p_a, *hv) for hv in halves]
                         + [functools.partial(stage_a, act_a, *hv) for hv in halves])
        n_blocks = ngroups * (ni // quad)
        pieces.every = max(1, PEER_HEADS * n_blocks // len(pieces))
        pieces.phase = 0
        for tg in range(ngroups):
            for iq in range(ni // quad):
                stage_b(act_b, p_b, tg, iq, pieces)
        while pieces:
            pieces.pop(0)()

    @pl.when(k % 2 == 0)
    def _():
        stages(act0_ref, act1_ref, p0_ref, p1_ref)

    @pl.when(k % 2 == 1)
    def _():
        stages(act1_ref, act0_ref, p1_ref, p0_ref)

    @pl.when((tile_c == n_tiles - 1) & (k >= 2))
    def _():
        o_ref[...] = x1_ref[...] + mod_ref[0, 5:6, :] * acc_ref[...].T
        acc_ref[...] = jnp.zeros_like(acc_ref)


def _peer_ffn(h2, u_bf, vt_bf, c1, e1, r2, e2, x1, mod6, seq, tn=512, te=1024):
    t, d = h2.shape
    ne = u_bf.shape[0]
    ngroups = tn // LANES
    n_tiles = ne // te
    n_items = (t // tn) * n_tiles

    def item(k, lag):
        w = jnp.clip(k - lag, 0, n_items - 1)
        return w // n_tiles, w % n_tiles

    stat = pl.BlockSpec((PEER_HEADS, ngroups, PEER_N_KEYS, LANES), lambda k: (0, item(k, 1)[0], 0, 0))
    pstat = pl.BlockSpec((PEER_HEADS, ngroups, PEER_N_KEYS // 2, LANES), lambda k: (0, item(k, 1)[0], 0, 0))
    return pl.pallas_call(
        functools.partial(_peer_kernel, tn=tn, te=te, n_tiles=n_tiles),
        out_shape=jax.ShapeDtypeStruct((t, d), F32),
        grid=(n_items + 2,),
        in_specs=[pl.BlockSpec((tn, d), lambda k: (item(k, 0)[0], 0)),
                  pl.BlockSpec((te, d), lambda k: (item(k, 0)[1], 0)),
                  pl.BlockSpec((d, te), lambda k: (0, item(k, 2)[1])),
                  stat, stat, pstat, pstat,
                  pl.BlockSpec((tn, d), lambda k: (item(k, 2)[0], 0)),
                  pl.BlockSpec((1, 6, d), lambda k: ((item(k, 2)[0] * tn) // seq, 0, 0))],
        out_specs=pl.BlockSpec((tn, d), lambda k: (item(k, 2)[0], 0)),
        scratch_shapes=[pltpu.VMEM((d, tn), F32),
                        pltpu.VMEM((ngroups, te, LANES), F32), pltpu.VMEM((ngroups, te, LANES), F32),
                        pltpu.VMEM((ngroups, te, LANES), BF16), pltpu.VMEM((ngroups, te, LANES), BF16)],
        compiler_params=_params("arbitrary"),
        name="peer_ffn",
    )(h2, u_bf, vt_bf, c1, e1, r2, e2, x1, mod6)


def _head_block_diag(n):
    idx = jnp.arange(n, dtype=jnp.int32) // HEAD_DIM
    return (idx[:, None] == idx[None, :]).astype(BF16)


def _layer(x, c, ada_w, ada_b, norm1_g, w_in, mu_prev, mu_next, q_norm_g, k_norm_g, w_decay0, w_decay_up,
           a_gate0, a_gate_up, g_up, k_k, k_a, r_k, lnx_g, lnx_b, w_out, norm2_g, peer_w_query,
           peer_sub_keys1, peer_sub_keys2, peer_u, peer_v):
    bsz, s, d = x.shape
    n_att_heads = ATT_WIDTH // HEAD_DIM
    bd512 = _head_block_diag(ATT_WIDTH)
    bd128 = _head_block_diag(LANES)

    mod6 = _ada(c, ada_w, ada_b).reshape(bsz, 6, d)
    gq = jnp.tile(q_norm_g, n_att_heads).reshape(1, ATT_WIDTH)
    gk = jnp.tile(k_norm_g, n_att_heads).reshape(1, ATT_WIDTH)
    qkv, zrw = _inproj(x, mod6, norm1_g.reshape(1, d), w_in.astype(BF16), gq, gk, bd512)

    att = _attention(qkv, n_att_heads)

    zero_w = jnp.zeros((2, LANES - DECAY_LORA, RWKV_WIDTH), F32)
    wup_pad = jnp.concatenate([w_decay_up, zero_w], axis=1).astype(BF16)
    aup_pad = jnp.concatenate([zero_w, a_gate_up], axis=1).astype(BF16)
    pk, plw = _rwkv_prep(zrw, mu_prev.reshape(1, -1), mu_next.reshape(1, -1), wup_pad, aup_pad, g_up.astype(BF16),
                         w_decay0, a_gate0, k_k.reshape(1, -1), k_a.reshape(1, -1), r_k.reshape(1, -1), bd512)
    rp, qp, gm, hm = _rwkv_p1(pk, plw)
    rw = _rwkv_p2(rp, qp, gm, hm, pk, lnx_g.reshape(1, -1), lnx_b.reshape(1, -1), bd128)

    x1, h2, s1t, s2t = _outproj(att, rw, x, mod6, norm2_g.reshape(1, d), w_out.astype(BF16),
                                peer_w_query.astype(BF16), peer_sub_keys1.astype(BF16),
                                peer_sub_keys2.astype(BF16))
    c1, e1, r2, e2 = _peer_topk(s1t, s2t)
    out = _peer_ffn(h2.reshape(bsz * s, d), peer_u.astype(BF16), peer_v.T.astype(BF16), c1, e1, r2, e2,
                    x1.reshape(bsz * s, d), mod6, s)
    return out.reshape(bsz, s, d)


def kernel(x, c, ada_w, ada_b, norm1_g, w_in, mu_prev, mu_next, q_norm_g, k_norm_g, w_decay0, w_decay_up,
           a_gate0, a_gate_up, g_up, k_k, k_a, r_k, lnx_g, lnx_b, w_out, norm2_g, peer_w_query,
           peer_sub_keys1, peer_sub_keys2, peer_u, peer_v):
    depth = ada_w.shape[0]
    for l in range(depth):
        x = _layer(x, c, ada_w[l], ada_b[l], norm1_g[l], w_in[l], mu_prev[l], mu_next[l], q_norm_g[l],
                   k_norm_g[l], w_decay0[l], w_decay_up[l], a_gate0[l], a_gate_up[l], g_up[l], k_k[l], k_a[l],
                   r_k[l], lnx_g[l], lnx_b[l], w_out[l], norm2_g[l], peer_w_query[l], peer_sub_keys1[l],
                   peer_sub_keys2[l], peer_u[l], peer_v[l])
    return x
```

```python
import functools
import math

import jax
import jax.numpy as jnp
from jax import lax
from jax.experimental import pallas as pl
from jax.experimental.pallas import tpu as pltpu

F32 = jnp.float32
BF16 = jnp.bfloat16

HEAD_DIM = 64
ATT_WIDTH = 512
RWKV_WIDTH = 512
DECAY_LORA = 64
AAA_LORA = 64
GATE_LORA = 128
SHIFT_WIDTH = 3 * RWKV_WIDTH + DECAY_LORA + AAA_LORA + GATE_LORA
PEER_N_KEYS = 128
PEER_HEADS = 8
PEER_TOPK = 16
NORM_EPS = 1e-6
LNX_EPS = 64e-5
MASK_VALUE = -1e30
LOG2E = 1.4426950408889634
ATT_HALF_WINDOWS = ((1, 64), (4, 256), (16, 1024))

LANES = 128
BF16_ROWS = 16
CHUNK = 64
ATT_BLOCK = 128
VMEM_LIMIT_BYTES = 56 * 1024 * 1024
WKV_PASSES_CHAIN = 1

NN = (((1,), (0,)), ((), ()))
NT = (((1,), (1,)), ((), ()))
TN = (((0,), (0,)), ((), ()))

SLAB_R, SLAB_V, SLAB_KK, SLAB_DIR0, SLAB_G, SLAB_BONUS, NUM_SLABS = 0, 1, 2, 3, 7, 8, 9


def _params(*sem):
    return pltpu.CompilerParams(dimension_semantics=sem, vmem_limit_bytes=VMEM_LIMIT_BYTES)


def _dot(a, b, dims=NN):
    return lax.dot_general(a, b, dims, preferred_element_type=F32)


def _split(a):
    hi = a.astype(BF16)
    lo = (a - hi.astype(F32)).astype(BF16)
    return hi, lo


def _mm(a, b, dims=NN, passes=3):
    if passes == 1:
        return _dot(a.astype(BF16), b.astype(BF16), dims)
    ah, al = _split(a)
    bh, bl = _split(b)
    return _dot(ah, bh, dims) + (_dot(ah, bl, dims) + _dot(al, bh, dims))


def _mm_exact_rhs(a, b_exact):
    ah, al = _split(a)
    return _dot(ah, b_exact) + _dot(al, b_exact)


def _sigmoid(x):
    return 1.0 / (1.0 + jnp.exp(-x))


def _ada_kernel(c_ref, w_ref, b_ref, o_ref):
    c = c_ref[...]
    o_ref[...] = _mm(c * _sigmoid(c), w_ref[...]) + b_ref[...]


def _ada(c, ada_w, ada_b):
    bsz, d = c.shape
    n = ada_w.shape[1]
    tn = 1024
    return pl.pallas_call(
        _ada_kernel,
        out_shape=jax.ShapeDtypeStruct((bsz, n), F32),
        grid=(n // tn,),
        in_specs=[pl.BlockSpec((bsz, d), lambda j: (0, 0)),
                  pl.BlockSpec((d, tn), lambda j: (0, j)),
                  pl.BlockSpec((1, tn), lambda j: (0, j))],
        out_specs=pl.BlockSpec((bsz, tn), lambda j: (0, j)),
        compiler_params=_params("arbitrary"),
        name="ada",
    )(c, ada_w, ada_b.reshape(1, n))


def _inproj_kernel(x_ref, mod_ref, g1_ref, w_ref, gq_ref, gk_ref, bd_ref, qkv_ref, zrw_ref):
    x = x_ref[0]
    mod = mod_ref[0]
    ms = jnp.mean(x * x, axis=-1, keepdims=True)
    h = x * lax.rsqrt(ms + NORM_EPS) * g1_ref[...]
    h = (h * (1.0 + mod[1:2]) + mod[0:1]).astype(BF16)
    bd = bd_ref[...]

    def head_norm(z, g):
        ss = _mm_exact_rhs(z * z, bd) * (1.0 / HEAD_DIM)
        return z * lax.rsqrt(ss + NORM_EPS) * g

    a = ATT_WIDTH
    zq = _dot(h, w_ref[:, 0:a])
    qkv_ref[0, :, 0:a] = (head_norm(zq, gq_ref[...]) * (HEAD_DIM ** -0.5 * LOG2E)).astype(BF16)
    zk = _dot(h, w_ref[:, a:2 * a])
    qkv_ref[0, :, a:2 * a] = head_norm(zk, gk_ref[...]).astype(BF16)
    qkv_ref[0, :, 2 * a:3 * a] = _dot(h, w_ref[:, 2 * a:3 * a]).astype(BF16)
    zrw_ref[0] = _dot(h, w_ref[:, 3 * a:])


def _inproj(x, mod6, norm1_g, w_in_bf, gq, gk, bd512, tm=512):
    bsz, s, d = x.shape
    nin = w_in_bf.shape[1]
    return pl.pallas_call(
        _inproj_kernel,
        out_shape=(jax.ShapeDtypeStruct((bsz, s, 3 * ATT_WIDTH), BF16),
                   jax.ShapeDtypeStruct((bsz, s, SHIFT_WIDTH), F32)),
        grid=(bsz, s // tm),
        in_specs=[pl.BlockSpec((1, tm, d), lambda b, i: (b, i, 0)),
                  pl.BlockSpec((1, 6, d), lambda b, i: (b, 0, 0)),
                  pl.BlockSpec((1, d), lambda b, i: (0, 0)),
                  pl.BlockSpec((d, nin), lambda b, i: (0, 0)),
                  pl.BlockSpec((1, ATT_WIDTH), lambda b, i: (0, 0)),
                  pl.BlockSpec((1, ATT_WIDTH), lambda b, i: (0, 0)),
                  pl.BlockSpec((ATT_WIDTH, ATT_WIDTH), lambda b, i: (0, 0))],
        out_specs=(pl.BlockSpec((1, tm, 3 * ATT_WIDTH), lambda b, i: (b, i, 0)),
                   pl.BlockSpec((1, tm, SHIFT_WIDTH), lambda b, i: (b, i, 0))),
        compiler_params=_params("arbitrary", "arbitrary"),
        name="inproj",
    )(x, mod6, norm1_g, w_in_bf, gq, gk, bd512)


ATT_FAR_DILATION, ATT_FAR_HALF = ATT_HALF_WINDOWS[-1]
ATT_NEAR_BLOCKS = 6


def _attn_near_window(nkb):
    return min(nkb, ATT_NEAR_BLOCKS)


def _attn_near_start(qb, nkb):
    return jnp.clip(((qb - 2) // 2) * 2, 0, nkb - _attn_near_window(nkb))


def _attn_far_bias(n_heads, nkb):
    per_block = ATT_BLOCK // ATT_FAR_DILATION
    i = jnp.arange(ATT_BLOCK, dtype=jnp.int32)[:, None]
    ip = jnp.arange(ATT_BLOCK, dtype=jnp.int32)[None, :]
    start = _attn_near_start(i // per_block, nkb)
    kb = ip // per_block
    dist = ATT_FAR_DILATION * jnp.abs(i - ip)
    far = (dist <= ATT_FAR_HALF) & ~((kb >= start) & (kb < start + _attn_near_window(nkb)))
    slopes = jnp.exp2(-8.0 * (jnp.arange(n_heads, dtype=F32) + 1.0) / n_heads) * LOG2E
    return jnp.where(far[None], -slopes[:, None, None] * dist.astype(F32)[None], MASK_VALUE)


def _attn_bias_table(n_heads, nkb):
    reach = _attn_near_window(nkb) - 1
    nd = 2 * reach + 1
    dd = jnp.arange(nd, dtype=jnp.int32)[None, :, None]
    r = jnp.arange(ATT_BLOCK, dtype=jnp.int32)[:, None, None]
    c = jnp.arange(ATT_BLOCK, dtype=jnp.int32)[None, None, :]
    dt = r - c - (dd - reach) * ATT_BLOCK
    adt = jnp.abs(dt)
    mult = jnp.zeros(dt.shape, F32)
    for dil, half in ATT_HALF_WINDOWS:
        mult = mult + ((adt <= half) & (dt % dil == 0)).astype(F32)
    logm = jnp.where(mult > 0, jnp.log2(jnp.maximum(mult, 1.0)), MASK_VALUE)
    slopes = jnp.exp2(-8.0 * (jnp.arange(n_heads, dtype=F32) + 1.0) / n_heads) * LOG2E
    bias = logm[None] - slopes[:, None, None, None] * adt.astype(F32)[None]
    return bias.reshape(n_heads, ATT_BLOCK, nd * ATT_BLOCK)


def _attn_kernel(q_ref, k_ref, v_ref, bias_ref, far_ref, o_ref, s_ref, v1_ref, g_ref, accf_ref, mf_ref,
                 *, nkb, win):
    blk = ATT_BLOCK
    reach = win - 1
    has_far = win < nkb
    lane = lax.broadcasted_iota(jnp.int32, (blk, LANES), 1)
    first = lane < HEAD_DIM

    def two_heads(q):
        zero = jnp.zeros_like(q)
        return jnp.concatenate([jnp.where(first, q, zero), jnp.where(first, zero, q)], axis=0)

    def with_ones(v, hh):
        is_own = (lax.broadcasted_iota(jnp.int32, v.shape, 1) < HEAD_DIM) == (hh == 0)
        return jnp.where(is_own, v, jnp.ones_like(v))

    v = v_ref[0]
    for hh in range(2):
        v1_ref[hh] = with_ones(v, hh)

    if has_far:
        g_ref[0] = q_ref[0].astype(F32)
        g_ref[1] = k_ref[0].astype(F32)
        g_ref[2] = v.astype(F32)
        n_far = nkb * blk // ATT_FAR_DILATION

        def residue(r, _):
            rows = pl.ds(r, n_far, stride=ATT_FAR_DILATION)
            qr = g_ref[0, rows, :].astype(BF16)
            kr = g_ref[1, rows, :].astype(BF16)
            vr = g_ref[2, rows, :].astype(BF16)
            s = _dot(two_heads(qr), kr, NT)
            for hh in range(2):
                sb = s[hh * blk:(hh + 1) * blk] + far_ref[hh]
                mx = jnp.max(sb, axis=-1, keepdims=True)
                p = jnp.exp2(sb - mx).astype(BF16)
                accf_ref[hh, rows, :] = _dot(p, with_ones(vr, hh))
                mf_ref[hh, rows, :] = jnp.broadcast_to(mx, (blk, LANES))
            return 0

        lax.fori_loop(0, ATT_FAR_DILATION, residue, 0, unroll=4)

    group = s_ref.shape[0]
    slabs = range(0, win, 2)

    def query_block_group(i, _):
        us = range(group)
        qis = [group * i + u for u in us]
        rows = [pl.ds(pl.multiple_of(qi * blk, blk), blk) for qi in qis]
        qab = [two_heads(q_ref[0, rows[u], :]) for u in us]
        ks = [_attn_near_start(qi, nkb) if has_far else 0 for qi in qis]

        m = [[jnp.full((blk, LANES), -jnp.inf, F32) for _ in range(2)] for _ in us]
        for w0 in slabs:
            for u in us:
                kj0 = ks[u] + w0
                kslab = k_ref[0, pl.ds(pl.multiple_of(kj0 * blk, blk), 2 * blk), :]
                s = _dot(qab[u], kslab, NT)
                x0 = pl.multiple_of((kj0 - qis[u] + reach) * blk, blk)
                for hh in range(2):
                    sb = s[hh * blk:(hh + 1) * blk] + bias_ref[hh, :, pl.ds(x0, 2 * blk)]
                    s_ref[u, hh, :, w0 * blk:(w0 + 2) * blk] = sb
                    m[u][hh] = jnp.maximum(m[u][hh], jnp.maximum(sb[:, :blk], sb[:, blk:]))
        mrow = [[jnp.max(m[u][hh], axis=-1, keepdims=True) for hh in range(2)] for u in us]

        if has_far:
            acc = [[None, None] for _ in us]
            for u in us:
                for hh in range(2):
                    m_far = mf_ref[hh, rows[u], :][:, 0:1]
                    m_all = jnp.maximum(mrow[u][hh], m_far)
                    acc[u][hh] = accf_ref[hh, rows[u], :] * jnp.exp2(m_far - m_all)
                    mrow[u][hh] = m_all
        else:
            acc = [[jnp.zeros((blk, LANES), F32) for _ in range(2)] for _ in us]
        for w0 in slabs:
            for u in us:
                keys = pl.ds(pl.multiple_of((ks[u] + w0) * blk, blk), 2 * blk)
                for hh in range(2):
                    p = jnp.exp2(s_ref[u, hh, :, w0 * blk:(w0 + 2) * blk] - mrow[u][hh]).astype(BF16)
                    acc[u][hh] = acc[u][hh] + _dot(p, v1_ref[hh, keys, :])
        for u in us:
            num = jnp.where(first, acc[u][0], acc[u][1])
            den = jnp.where(first, pltpu.roll(acc[u][0], HEAD_DIM, axis=1), pltpu.roll(acc[u][1], HEAD_DIM, axis=1))
            o_ref[0, rows[u], :] = (num / den).astype(o_ref.dtype)
        return 0

    lax.fori_loop(0, nkb // group, query_block_group, 0)


def _attention(qkv, n_heads):
    bsz, s, _ = qkv.shape
    nkb = s // ATT_BLOCK
    win = _attn_near_window(nkb)
    assert nkb % 2 == 0 and (win == nkb or s // ATT_FAR_DILATION == ATT_BLOCK)
    bias = _attn_bias_table(n_heads, nkb)
    far = _attn_far_bias(n_heads, nkb)
    npair = ATT_WIDTH // LANES
    return pl.pallas_call(
        functools.partial(_attn_kernel, nkb=nkb, win=win),
        out_shape=jax.ShapeDtypeStruct((bsz, s, ATT_WIDTH), BF16),
        grid=(npair, bsz),
        in_specs=[pl.BlockSpec((1, s, LANES), lambda hp, b: (b, 0, hp)),
                  pl.BlockSpec((1, s, LANES), lambda hp, b: (b, 0, npair + hp)),
                  pl.BlockSpec((1, s, LANES), lambda hp, b: (b, 0, 2 * npair + hp)),
                  pl.BlockSpec((2, ATT_BLOCK, bias.shape[2]), lambda hp, b: (hp, 0, 0)),
                  pl.BlockSpec((2, ATT_BLOCK, ATT_BLOCK), lambda hp, b: (hp, 0, 0))],
        out_specs=pl.BlockSpec((1, s, LANES), lambda hp, b: (b, 0, hp)),
        scratch_shapes=[pltpu.VMEM((math.gcd(nkb, 4), 2, ATT_BLOCK, win * ATT_BLOCK), F32), pltpu.VMEM((2, s, LANES), BF16),
                        pltpu.VMEM((3, s, LANES), F32), pltpu.VMEM((2, s, LANES), F32),
                        pltpu.VMEM((2, s, LANES), F32)],
        compiler_params=_params("arbitrary", "arbitrary"),
        name="attn",
    )(qkv, qkv, qkv, bias, far)


def _prep_kernel(z_ref, zp_ref, zn_ref, mup_ref, mun_ref, wup_ref, aup_ref, gup_ref, w0_ref, a0_ref,
                 kk_ref, ka_ref, rk_ref, bd_ref, o_ref, lw_ref, *, tq):
    i = pl.program_id(1)
    last = pl.num_programs(1) - 1
    z = z_ref[0]
    row = lax.broadcasted_iota(jnp.int32, (tq, 1), 0)
    prev_row = zp_ref[0, 7:8, :] * (i > 0).astype(F32)
    next_row = zn_ref[0, 0:1, :] * (i < last).astype(F32)
    zp = jnp.where(row == 0, prev_row, pltpu.roll(z, 1, axis=0))
    zn = jnp.where(row == tq - 1, next_row, pltpu.roll(z, tq - 1, axis=0))
    zs = z + mup_ref[...] * (zp - z) + mun_ref[...] * (zn - z)

    w = RWKV_WIDTH
    r = zs[:, 0:w]
    k = zs[:, w:2 * w]
    v = zs[:, 2 * w:3 * w]
    xwa = zs[:, 3 * w:3 * w + LANES]
    xg = zs[:, 3 * w + LANES:]
    bd = bd_ref[...]

    g = _dot(_sigmoid(xg).astype(BF16), gup_ref[...])
    kk = k * kk_ref[...]
    ss = _mm_exact_rhs(kk * kk, bd)
    kk = kk * lax.rsqrt(jnp.maximum(ss, 1e-12))
    ka = ka_ref[...]
    txw = jnp.tanh(xwa).astype(BF16)
    xab = xwa.astype(BF16)

    def put(slab, val):
        o_ref[0, :, slab * w:(slab + 1) * w] = val.astype(o_ref.dtype)

    put(SLAB_R, r)
    put(SLAB_V, v)
    put(SLAB_KK, kk)
    put(SLAB_G, g)
    a_sum = jnp.zeros_like(k)
    for d in range(2):
        y = w0_ref[d:d + 1, :] + _dot(txw, wup_ref[d])
        wlog = -(jnp.maximum(-y, 0.0) + jnp.log(1.0 + jnp.exp(-jnp.abs(y)))) - 0.5
        a = _sigmoid(a0_ref[d:d + 1, :] + _dot(xab, aup_ref[d]))
        a_sum = a_sum + a
        lw_ref[0, :, d * w:(d + 1) * w] = -jnp.exp(wlog)
        put(SLAB_DIR0 + 2 * d, k * (1.0 + (a - 1.0) * ka))
        put(SLAB_DIR0 + 2 * d + 1, kk * a)
    k_bonus = k * (1.0 + (0.5 * a_sum - 1.0) * ka)
    bsum = _mm_exact_rhs(r * k_bonus * rk_ref[...], bd)
    put(SLAB_BONUS, bsum * v)


def _rwkv_prep(zrw, mu_prev, mu_next, wup_pad, aup_pad, g_up_bf, w0, a0, k_k, k_a, r_k, bd512, tq=512):
    bsz, s, sw = zrw.shape
    w = RWKV_WIDTH
    nt = s // tq
    row = lambda b, i: (0, 0)
    return pl.pallas_call(
        functools.partial(_prep_kernel, tq=tq),
        out_shape=(jax.ShapeDtypeStruct((bsz, s, NUM_SLABS * w), BF16),
                   jax.ShapeDtypeStruct((bsz, s, 2 * w), F32)),
        grid=(bsz, nt),
        in_specs=[pl.BlockSpec((1, tq, sw), lambda b, i: (b, i, 0)),
                  pl.BlockSpec((1, 8, sw), lambda b, i: (b, jnp.maximum(i * (tq // 8) - 1, 0), 0)),
                  pl.BlockSpec((1, 8, sw), lambda b, i: (b, jnp.minimum((i + 1) * (tq // 8), s // 8 - 1), 0)),
                  pl.BlockSpec((1, sw), row),
                  pl.BlockSpec((1, sw), row),
                  pl.BlockSpec((2, LANES, w), lambda b, i: (0, 0, 0)),
                  pl.BlockSpec((2, LANES, w), lambda b, i: (0, 0, 0)),
                  pl.BlockSpec((GATE_LORA, w), row),
                  pl.BlockSpec((2, w), row),
                  pl.BlockSpec((2, w), row),
                  pl.BlockSpec((1, w), row),
                  pl.BlockSpec((1, w), row),
                  pl.BlockSpec((1, w), row),
                  pl.BlockSpec((w, w), row)],
        out_specs=(pl.BlockSpec((1, tq, NUM_SLABS * w), lambda b, i: (b, i, 0)),
                   pl.BlockSpec((1, tq, 2 * w), lambda b, i: (b, i, 0))),
        compiler_params=_params("arbitrary", "arbitrary"),
        name="rwkv_prep",
    )(zrw, zrw, zrw, mu_prev, mu_next, wup_pad, aup_pad, g_up_bf, w0, a0, k_k, k_a, r_k, bd512)


def _block_diag(x, first):
    zero = jnp.zeros_like(x)
    return jnp.concatenate([jnp.where(first, x, zero), jnp.where(first, zero, x)], axis=0)


def _p1_kernel(r_ref, v_ref, kk_ref, lw_ref, kd_ref, be_ref, rp_ref, qp_ref, g_ref, h_ref, *, cpb):
    c = CHUNK
    sign = 1 - 2 * pl.program_id(2)
    rowi = lax.broadcasted_iota(jnp.int32, (c, LANES), 0)
    lane = lax.broadcasted_iota(jnp.int32, (c, LANES), 1)
    coli = lane & (c - 1)
    first = lane < HEAD_DIM
    before = sign * (rowi - coli)
    strict2 = before > 0
    incl2 = before >= 0
    eye2 = (coli == rowi).astype(F32)
    r64 = lax.broadcasted_iota(jnp.int32, (c, c), 0)
    c64 = lax.broadcasted_iota(jnp.int32, (c, c), 1)
    cum = (sign * (r64 - c64) >= 0).astype(BF16)

    js = range(cpb)
    sls = [pl.ds(j * c, c) for j in js]
    bdg = lambda x: _block_diag(x, first)
    r = [r_ref[0, sl, :].astype(F32) for sl in sls]
    v = [v_ref[0, sl, :].astype(F32) for sl in sls]
    kk = [kk_ref[0, sl, :].astype(F32) for sl in sls]
    lw = [lw_ref[0, sl, :] for sl in sls]
    kd = [kd_ref[0, sl, :].astype(F32) for sl in sls]
    be = [be_ref[0, sl, :].astype(F32) for sl in sls]

    one = lambda x, y, dims=NN: _mm(x, y, dims, 1)
    rows2 = lambda x, y: jnp.concatenate([x, y], axis=0)
    cols2 = lambda x, y: jnp.concatenate([x, y], axis=1)

    def cumsum(x):
        l1 = x.astype(BF16)
        rem = x - l1.astype(F32)
        l2 = rem.astype(BF16)
        l3 = (rem - l2.astype(F32)).astype(BF16)
        y = _dot(cum, jnp.concatenate([l1, l2, l3], axis=1))
        return y[:, :LANES] + (y[:, LANES:2 * LANES] + y[:, 2 * LANES:])

    cs = [cumsum(lw[j]) for j in js]
    g_inv = [jnp.exp(-cs[j]) for j in js]
    g_tot = [jnp.exp(jnp.sum(lw[j], axis=0, keepdims=True)) for j in js]
    ab = [-kk[j] * jnp.exp(cs[j] - lw[j]) for j in js]
    rb = [r[j] * jnp.exp(cs[j]) for j in js]
    bt = [be[j] * g_inv[j] for j in js]
    kt = [kd[j] * g_inv[j] for j in js]

    sc = [one(rows2(ab[j], rb[j]), rows2(bdg(bt[j]), bdg(kt[j])), NT) for j in js]
    m_ab = [jnp.where(strict2, sc[j][:c, :LANES], 0.0) for j in js]
    n_rb = [jnp.where(incl2, sc[j][c:, :LANES], 0.0) for j in js]
    m_ak = [jnp.where(strict2, sc[j][:c, LANES:], 0.0) for j in js]
    n_rk = [jnp.where(incl2, sc[j][c:, LANES:], 0.0) for j in js]

    levels = int(math.log2(c)) - 1
    t = [eye2 + m_ab[j] for j in js]
    p = [one(m_ab[j], bdg(m_ab[j])) for j in js]
    for _ in range(1, levels - 1):
        y = [one(rows2(p[j], t[j]), bdg(p[j])) for j in js]
        p = [y[j][:c] for j in js]
        t = [t[j] + y[j][c:] for j in js]
    t = [t[j] + one(t[j], bdg(p[j])) for j in js]

    def residual(a, tt):
        ah, al = _split(a)
        th, tl = _split(bdg(tt))
        y = _dot(rows2(ah, al), th)
        return eye2 - (y[:c] + (y[c:] + _dot(ah, tl)))

    res = [residual(eye2 - m_ab[j], t[j]) for j in js]
    t = [t[j] + one(t[j], bdg(res[j])) for j in js]

    vbd = [bdg(v[j]) for j in js]
    mv = [one(rows2(m_ak[j], n_rk[j]), vbd[j]) for j in js]
    tp = [one(t[j], cols2(bdg(ab[j]), bdg(mv[j][:c]))) for j in js]
    ap = [tp[j][:, :LANES] for j in js]
    pp = [tp[j][:, LANES:] for j in js]
    nr = [one(n_rb[j], cols2(bdg(ap[j]), bdg(pp[j]))) for j in js]
    for j in js:
        rp_ref[0, 0, 0, sls[j], :] = rb[j] + nr[j][:, :LANES]
        qp_ref[0, 0, 0, sls[j], :] = nr[j][:, LANES:] + mv[j][c:]
    heads = lambda x: jnp.where(first, x[:HEAD_DIM], x[HEAD_DIM:])
    for j in js:
        st = one(cols2(ap[j], pp[j]), bt[j], TN)
        g_ref[0, 0, 0, j] = (eye2 + heads(st[:LANES])) * g_tot[j]
        h_ref[0, 0, 0, j] = heads(st[LANES:] + one(v[j], kt[j], TN)) * g_tot[j]


def _rwkv_p1(pk, plw, cpb=32):
    bsz, s, _ = pk.shape
    nc = s // CHUNK
    npair = RWKV_WIDTH // LANES
    rows = cpb * CHUNK

    def slab(sidx):
        return pl.BlockSpec((1, rows, LANES), lambda b, hp, d, ci: (b, ci, sidx * npair + hp))

    def dslab(off):
        return pl.BlockSpec((1, rows, LANES),
                            lambda b, hp, d, ci: (b, ci, (SLAB_DIR0 + 2 * d + off) * npair + hp))

    lw_spec = pl.BlockSpec((1, rows, LANES), lambda b, hp, d, ci: (b, ci, d * npair + hp))

    seq = pl.BlockSpec((1, 1, 1, rows, LANES), lambda b, hp, d, ci: (b, hp, d, ci, 0))
    mat = pl.BlockSpec((1, 1, 1, cpb, HEAD_DIM, LANES), lambda b, hp, d, ci: (b, hp, d, ci, 0, 0))
    return pl.pallas_call(
        functools.partial(_p1_kernel, cpb=cpb),
        out_shape=(jax.ShapeDtypeStruct((bsz, npair, 2, s, LANES), F32),
                   jax.ShapeDtypeStruct((bsz, npair, 2, s, LANES), F32),
                   jax.ShapeDtypeStruct((bsz, npair, 2, nc, HEAD_DIM, LANES), F32),
                   jax.ShapeDtypeStruct((bsz, npair, 2, nc, HEAD_DIM, LANES), F32)),
        grid=(bsz, npair, 2, nc // cpb),
        in_specs=[slab(SLAB_R), slab(SLAB_V), slab(SLAB_KK), lw_spec, dslab(0), dslab(1)],
        out_specs=(seq, seq, mat, mat),
        compiler_params=_params("arbitrary", "arbitrary", "arbitrary", "arbitrary"),
        name="rwkv_p1",
    )(pk, pk, pk, plw, pk, pk)


def _p2_kernel(rp_ref, qp_ref, g_ref, h_ref, gate_ref, bonus_ref, lng_ref, lnb_ref, bd_ref, o_ref,
               y_ref, s_ref, *, nc, te, hpb):
    c = CHUNK
    s_ref[...] = jnp.zeros_like(s_ref)
    first = lax.broadcasted_iota(jnp.int32, (HEAD_DIM, LANES), 1) < HEAD_DIM
    chains = [(hq, d) for hq in range(hpb) for d in range(2)]

    def step(j, _):
        for hq, d in chains:
            jc = j if d == 0 else nc - 1 - j
            sl = pl.ds(pl.multiple_of(jc * c, c), c)
            st = s_ref[hq, d]
            y_ref[hq, d, sl, :] = (_mm(rp_ref[0, hq, d, sl, :], st, NT, WKV_PASSES_CHAIN)
                                   + qp_ref[0, hq, d, sl, :])
            gmat = _block_diag(g_ref[0, hq, d, jc], first)
            hmat = _block_diag(h_ref[0, hq, d, jc], first)
            s_ref[hq, d] = _mm(st, gmat, NN, WKV_PASSES_CHAIN) + hmat
        return 0

    lax.fori_loop(0, nc, step, 0)

    bd = bd_ref[...]
    inv = 1.0 / HEAD_DIM

    def epi(i, _):
        sl = pl.ds(pl.multiple_of(i * te, te), te)
        for hq in range(hpb):
            lanes = slice(hq * LANES, (hq + 1) * LANES)
            y = y_ref[hq, 0, sl, :] + y_ref[hq, 1, sl, :]
            mean = _mm_exact_rhs(y, bd) * inv
            yc = y - mean
            var = _mm_exact_rhs(yc * yc, bd) * inv
            yn = yc * lax.rsqrt(var + LNX_EPS) * lng_ref[:, lanes] + lnb_ref[:, lanes]
            o_ref[0, sl, lanes] = ((yn + bonus_ref[0, sl, lanes].astype(F32))
                                   * gate_ref[0, sl, lanes].astype(F32)).astype(o_ref.dtype)
        return 0

    lax.fori_loop(0, (nc * c) // te, epi, 0)


def _rwkv_p2(rp, qp, gm, hm, pk, lnx_g, lnx_b, bd128, te=1024, hpb=2):
    bsz, npair, _, s, _ = rp.shape
    nc = s // CHUNK
    width = hpb * LANES
    nq = npair // hpb
    seq = pl.BlockSpec((1, hpb, 2, s, LANES), lambda b, q: (b, q, 0, 0, 0))
    mat = pl.BlockSpec((1, hpb, 2, nc, HEAD_DIM, LANES), lambda b, q: (b, q, 0, 0, 0, 0))
    return pl.pallas_call(
        functools.partial(_p2_kernel, nc=nc, te=min(te, s), hpb=hpb),
        out_shape=jax.ShapeDtypeStruct((bsz, s, RWKV_WIDTH), BF16),
        grid=(bsz, nq),
        in_specs=[seq, seq, mat, mat,
                  pl.BlockSpec((1, s, width), lambda b, q: (b, 0, SLAB_G * nq + q)),
                  pl.BlockSpec((1, s, width), lambda b, q: (b, 0, SLAB_BONUS * nq + q)),
                  pl.BlockSpec((1, width), lambda b, q: (0, q)),
                  pl.BlockSpec((1, width), lambda b, q: (0, q)),
                  pl.BlockSpec((LANES, LANES), lambda b, q: (0, 0))],
        out_specs=pl.BlockSpec((1, s, width), lambda b, q: (b, 0, q)),
        scratch_shapes=[pltpu.VMEM((hpb, 2, s, LANES), F32), pltpu.VMEM((hpb, 2, LANES, LANES), F32)],
        compiler_params=_params("arbitrary", "arbitrary"),
        name="rwkv_p2",
    )(rp, qp, gm, hm, pk, pk, lnx_g, lnx_b, bd128)


def _outproj_kernel(att_ref, rw_ref, x_ref, mod_ref, g2_ref, wo_ref, wq_ref, k1_ref, k2_ref,
                    x1_ref, h2_ref, s1_ref, s2_ref):
    mod = mod_ref[0]
    a = ATT_WIDTH
    mixed = _dot(att_ref[0], wo_ref[0:a, :]) + _dot(rw_ref[0], wo_ref[a:, :])
    x1 = x_ref[0] + mod[2:3] * mixed
    x1_ref[0] = x1
    ms = jnp.mean(x1 * x1, axis=-1, keepdims=True)
    h2 = x1 * lax.rsqrt(ms + NORM_EPS) * g2_ref[...]
    h2 = (h2 * (1.0 + mod[4:5]) + mod[3:4]).astype(BF16)
    h2_ref[0] = h2
    q = _dot(h2, wq_ref[...])
    k1 = k1_ref[...]
    k2 = k2_ref[...]
    for h in range(PEER_HEADS):
        base = h * 2 * LANES
        s1 = _dot(k1, q[:, base:base + LANES].astype(BF16), NT)
        s2 = _dot(k2, q[:, base + LANES:base + 2 * LANES].astype(BF16), NT)
        for g in range(s1.shape[1] // LANES):
            s1_ref[h, g] = s1[:, g * LANES:(g + 1) * LANES]
            s2_ref[h, g] = s2[:, g * LANES:(g + 1) * LANES]


def _outproj(att, rw, x, mod6, norm2_g, w_out_bf, wq_bf, k1_bf, k2_bf, tm=512):
    bsz, s, d = x.shape
    nt = s // tm
    t = bsz * s
    nq = wq_bf.shape[1]
    const = lambda b, i: (0, 0)
    gpt = tm // LANES
    tok = pl.BlockSpec((PEER_HEADS, gpt, PEER_N_KEYS, LANES), lambda b, i: (0, b * nt + i, 0, 0))
    stat = jax.ShapeDtypeStruct((PEER_HEADS, t // LANES, PEER_N_KEYS, LANES), F32)
    return pl.pallas_call(
        _outproj_kernel,
        out_shape=(jax.ShapeDtypeStruct((bsz, s, d), F32),
                   jax.ShapeDtypeStruct((bsz, s, d), BF16),
                   stat, stat),
        grid=(bsz, nt),
        in_specs=[pl.BlockSpec((1, tm, ATT_WIDTH), lambda b, i: (b, i, 0)),
                  pl.BlockSpec((1, tm, RWKV_WIDTH), lambda b, i: (b, i, 0)),
                  pl.BlockSpec((1, tm, d), lambda b, i: (b, i, 0)),
                  pl.BlockSpec((1, 6, d), lambda b, i: (b, 0, 0)),
                  pl.BlockSpec((1, d), const),
                  pl.BlockSpec((d, d), const),
                  pl.BlockSpec((d, nq), const),
                  pl.BlockSpec((PEER_N_KEYS, LANES), const),
                  pl.BlockSpec((PEER_N_KEYS, LANES), const)],
        out_specs=(pl.BlockSpec((1, tm, d), lambda b, i: (b, i, 0)),
                   pl.BlockSpec((1, tm, d), lambda b, i: (b, i, 0)),
                   tok, tok),
        compiler_params=_params("arbitrary", "arbitrary"),
        name="outproj",
    )(att, rw, x, mod6, norm2_g, w_out_bf, wq_bf, k1_bf, k2_bf)


def _oddeven_sort_pairs(n):
    pairs = []
    p = 1
    while p < n:
        k = p
        while k >= 1:
            for j in range(k % p, n - k, 2 * k):
                for i in range(min(k, n - j - k)):
                    if (i + j) // (2 * p) == (i + j + k) // (2 * p):
                        pairs.append((i + j, i + j + k))
            k //= 2
        p *= 2
    return pairs


def _bitonic_merge_pairs(n):
    pairs = []
    stride = n // 2
    while stride >= 1:
        pairs += [(i, i + stride) for i in range(n) if not i & stride]
        stride //= 2
    return pairs


def _compare_exchange(vals, pairs):
    vals = list(vals)
    for i, j in pairs:
        a, b = vals[i], vals[j]
        if b is None:
            continue
        if a is None:
            vals[i], vals[j] = b, None
        else:
            vals[i], vals[j] = jnp.maximum(a, b), jnp.minimum(a, b)
    return vals


def _top16_sorted(vregs):
    kk = PEER_TOPK
    vals = list(vregs) + [None] * (kk - len(vregs))
    vals = _compare_exchange(vals, _oddeven_sort_pairs(kk))
    for shift in (4, 2, 1):
        other = [None if v is None else pltpu.roll(v, shift, axis=0) for v in vals]
        merged = []
        for k in range(kk):
            a, b = vals[k], other[kk - 1 - k]
            merged.append(b if a is None else a if b is None else jnp.maximum(a, b))
        vals = _compare_exchange(merged, _bitonic_merge_pairs(kk))
    return vals


def _topk_kernel(s1_ref, s2_ref, c1_ref, e1_ref, r2_ref, e2_ref, *, groups):
    kk = PEER_TOPK
    sub = 8
    neg = -jnp.inf
    rows8 = lax.broadcasted_iota(jnp.int32, (sub, LANES), 0)

    def rows_of(reps):
        out = reps[0]
        for r in range(1, sub):
            out = jnp.where(rows8 == r, reps[r], out)
        return out

    def all_sublanes_sum(x):
        for shift in (4, 2, 1):
            x = x + pltpu.roll(x, shift, axis=0)
        return x

    def per_tile(it_idx, _):
        g = it_idx // PEER_HEADS
        h = it_idx % PEER_HEADS
        s1 = [s1_ref[h, g, pl.ds(v * sub, sub), :] for v in range(PEER_N_KEYS // sub)]
        s2 = [s2_ref[h, g, pl.ds(v * sub, sub), :] for v in range(PEER_N_KEYS // sub)]
        a = _top16_sorted(s1)
        b = _top16_sorted(s2)
        a_lo, a_hi = rows_of(a[:sub]), rows_of(a[sub:])
        b_lo, b_hi = rows_of(b[:sub]), rows_of(b[sub:])

        cand = [a[0] + b_lo, a[0] + b_hi, a[1] + b_lo]
        for i in range(2, sub):
            cand.append(jnp.where(rows8 < kk // (i + 1), a[i] + b_lo, neg))
        cand.append(a_hi + b[0])
        top = _top16_sorted(cand)
        tau = top[kk - 1]
        z = None
        for t in top:
            ez = jnp.exp(t - top[0])
            z = ez if z is None else z + ez
        inv_z = 1.0 / z

        counts = []
        for r in range(kk):
            hits = (jnp.where(a[r] + b_lo >= tau, 1.0, 0.0) + jnp.where(a[r] + b_hi >= tau, 1.0, 0.0))
            counts.append(all_sublanes_sum(hits))
        for v in range(PEER_N_KEYS // sub):
            rows = pl.ds(v * sub, sub)
            cnt = jnp.zeros((sub, LANES), F32)
            rank = jnp.full((sub, LANES), float(kk), F32)
            for r in range(kk):
                cnt = jnp.where(s1[v] == a[r], counts[r], cnt)
                rank = jnp.where(s2[v] == b[r], float(r), rank)
            c1_ref[h, g, rows, :] = cnt
            e1_ref[h, g, rows, :] = jnp.exp(s1[v] - a[0]) * inv_z
            s2[v] = (rank, jnp.exp(s2[v] - b[0]))
        rank2 = jnp.concatenate([rv for rv, _ in s2], axis=0).astype(BF16)
        e2 = jnp.concatenate([ev for _, ev in s2], axis=0).astype(BF16)
        r2_ref[h, g] = pltpu.bitcast(rank2, jnp.uint32)
        e2_ref[h, g] = pltpu.bitcast(e2, jnp.uint32)
        return 0

    lax.fori_loop(0, groups * PEER_HEADS, per_tile, 0)


def _peer_topk(s1t, s2t, tn=512):
    nh, ng, nk, _ = s1t.shape
    groups = tn // LANES
    blk = pl.BlockSpec((nh, groups, nk, LANES), lambda i: (0, i, 0, 0))
    f32 = jax.ShapeDtypeStruct(s1t.shape, F32)
    b16 = jax.ShapeDtypeStruct((nh, ng, nk // 2, LANES), jnp.uint32)
    pblk = pl.BlockSpec((nh, groups, nk // 2, LANES), lambda i: (0, i, 0, 0))
    return pl.pallas_call(
        functools.partial(_topk_kernel, groups=groups),
        out_shape=(f32, f32, b16, b16),
        grid=(ng // groups,),
        in_specs=[blk, blk],
        out_specs=(blk, blk, pblk, pblk),
        compiler_params=_params("arbitrary"),
        name="peer_topk",
    )(s1t, s2t)


class _Pieces(list):
    every = 1
    phase = 0


def _peer_kernel(h2_ref, u_ref, vt_ref, c1_ref, e1_ref, r2_ref, e2_ref, x1_ref, mod_ref, o_ref,
                 acc_ref, act0_ref, act1_ref, p0_ref, p1_ref, *, tn, te, n_tiles):
    k = pl.program_id(0)
    n_items = pl.num_programs(0) - 2
    nk = PEER_N_KEYS
    ngroups = tn // LANES

    @pl.when(k == 0)
    def _():
        acc_ref[...] = jnp.zeros_like(acc_ref)
        for ref in (act0_ref, act1_ref, p0_ref, p1_ref):
            ref[...] = jnp.zeros_like(ref)

    ni = te // nk
    i0 = pl.multiple_of((jnp.clip(k - 1, 0, n_items - 1) % n_tiles) * ni, ni)
    tile_c = jnp.clip(k - 2, 0, n_items - 1) % n_tiles

    d_model = acc_ref.shape[0]
    ksplit = 4
    msplit = 2
    halves = [(mh, nh, kh) for mh in range(msplit) for nh in range(ngroups // 2) for kh in range(ksplit)]

    def stage_c(p_a, mh, nh, kh):
        rows = slice(mh * (d_model // msplit), (mh + 1) * (d_model // msplit))
        cols = slice(nh * 2 * LANES, (nh + 1) * 2 * LANES)
        ks = slice(kh * (te // ksplit), (kh + 1) * (te // ksplit))
        p_prev = jnp.concatenate([p_a[2 * nh, ks, :], p_a[2 * nh + 1, ks, :]], axis=1)
        acc_ref[rows, cols] += _dot(vt_ref[rows, ks], p_prev)

    def stage_a(act_a, mh, nh, kh):
        rows = slice(mh * (te // msplit), (mh + 1) * (te // msplit))
        ks = slice(kh * (d_model // ksplit), (kh + 1) * (d_model // ksplit))
        act = _dot(u_ref[rows, ks], h2_ref[nh * 2 * LANES:(nh + 1) * 2 * LANES, ks], NT)
        if kh == 0:
            act_a[2 * nh, rows, :] = act[:, :LANES]
            act_a[2 * nh + 1, rows, :] = act[:, LANES:]
        else:
            act_a[2 * nh, rows, :] += act[:, :LANES]
            act_a[2 * nh + 1, rows, :] += act[:, LANES:]

    def rows_bf16(row):
        tile = jnp.broadcast_to(row, (BF16_ROWS, LANES)).astype(BF16)
        return jnp.concatenate([tile] * (nk // BF16_ROWS), axis=0)

    quad = 4

    def stage_b(act_b, p_b, tg, iq, pieces):
        gates = [jnp.zeros((nk, LANES), BF16) for _ in range(quad)]
        for h in range(PEER_HEADS):
            if h % pieces.every == pieces.phase and pieces:
                pieces.pop(0)()
            c1 = c1_ref[h, tg, pl.ds(i0, ni), :]
            e1 = e1_ref[h, tg, pl.ds(i0, ni), :]
            rank2 = pltpu.bitcast(r2_ref[h, tg], BF16)
            e2 = pltpu.bitcast(e2_ref[h, tg], BF16)
            for q in range(quad):
                il = iq * quad + q
                sel = rank2 < rows_bf16(c1[il:il + 1, :])
                gates[q] = gates[q] + jnp.where(sel, e2 * rows_bf16(e1[il:il + 1, :]), jnp.zeros_like(e2))
        for q in range(quad):
            il = iq * quad + q
            a = act_b[tg, il * nk:(il + 1) * nk, :]
            gelu = 0.5 * a * (1.0 + lax.erf(a * (2.0 ** -0.5)))
            p_b[tg, il * nk:(il + 1) * nk, :] = gates[q] * gelu.astype(BF16)

    def stages(act_a, act_b, p_a, p_b):
        pieces = _Pieces([functools.partial(stage_c, p_a, *hv) for hv in halves]
                         + [functools.partial(stage_a, act_a, *hv) for hv in halves])
        n_blocks = ngroups * (ni // quad)
        pieces.every = max(1, PEER_HEADS * n_blocks // len(pieces))
        pieces.phase = 0
        for tg in range(ngroups):
            for iq in range(ni // quad):
                stage_b(act_b, p_b, tg, iq, pieces)
        while pieces:
            pieces.pop(0)()

    @pl.when(k % 2 == 0)
    def _():
        stages(act0_ref, act1_ref, p0_ref, p1_ref)

    @pl.when(k % 2 == 1)
    def _():
        stages(act1_ref, act0_ref, p1_ref, p0_ref)

    @pl.when((tile_c == n_tiles - 1) & (k >= 2))
    def _():
        o_ref[...] = x1_ref[...] + mod_ref[0, 5:6, :] * acc_ref[...].T
        acc_ref[...] = jnp.zeros_like(acc_ref)


def _peer_ffn(h2, u_bf, vt_bf, c1, e1, r2, e2, x1, mod6, seq, tn=512, te=1024):
    t, d = h2.shape
    ne = u_bf.shape[0]
    ngroups = tn // LANES
    n_tiles = ne // te
    n_items = (t // tn) * n_tiles

    def item(k, lag):
        w = jnp.clip(k - lag, 0, n_items - 1)
        return w // n_tiles, w % n_tiles

    stat = pl.BlockSpec((PEER_HEADS, ngroups, PEER_N_KEYS, LANES), lambda k: (0, item(k, 1)[0], 0, 0))
    pstat = pl.BlockSpec((PEER_HEADS, ngroups, PEER_N_KEYS // 2, LANES), lambda k: (0, item(k, 1)[0], 0, 0))
    return pl.pallas_call(
        functools.partial(_peer_kernel, tn=tn, te=te, n_tiles=n_tiles),
        out_shape=jax.ShapeDtypeStruct((t, d), F32),
        grid=(n_items + 2,),
        in_specs=[pl.BlockSpec((tn, d), lambda k: (item(k, 0)[0], 0)),
                  pl.BlockSpec((te, d), lambda k: (item(k, 0)[1], 0)),
                  pl.BlockSpec((d, te), lambda k: (0, item(k, 2)[1])),
                  stat, stat, pstat, pstat,
                  pl.BlockSpec((tn, d), lambda k: (item(k, 2)[0], 0)),
                  pl.BlockSpec((1, 6, d), lambda k: ((item(k, 2)[0] * tn) // seq, 0, 0))],
        out_specs=pl.BlockSpec((tn, d), lambda k: (item(k, 2)[0], 0)),
        scratch_shapes=[pltpu.VMEM((d, tn), F32),
                        pltpu.VMEM((ngroups, te, LANES), F32), pltpu.VMEM((ngroups, te, LANES), F32),
                        pltpu.VMEM((ngroups, te, LANES), BF16), pltpu.VMEM((ngroups, te, LANES), BF16)],
        compiler_params=_params("arbitrary"),
        name="peer_ffn",
    )(h2, u_bf, vt_bf, c1, e1, r2, e2, x1, mod6)


def _head_block_diag(n):
    idx = jnp.arange(n, dtype=jnp.int32) // HEAD_DIM
    return (idx[:, None] == idx[None, :]).astype(BF16)


def _layer(x, c, ada_w, ada_b, norm1_g, w_in, mu_prev, mu_next, q_norm_g, k_norm_g, w_decay0, w_decay_up,
           a_gate0, a_gate_up, g_up, k_k, k_a, r_k, lnx_g, lnx_b, w_out, norm2_g, peer_w_query,
           peer_sub_keys1, peer_sub_keys2, peer_u, peer_v):
    bsz, s, d = x.shape
    n_att_heads = ATT_WIDTH // HEAD_DIM
    bd512 = _head_block_diag(ATT_WIDTH)
    bd128 = _head_block_diag(LANES)

    mod6 = _ada(c, ada_w, ada_b).reshape(bsz, 6, d)
    gq = jnp.tile(q_norm_g, n_att_heads).reshape(1, ATT_WIDTH)
    gk = jnp.tile(k_norm_g, n_att_heads).reshape(1, ATT_WIDTH)
    qkv, zrw = _inproj(x, mod6, norm1_g.reshape(1, d), w_in.astype(BF16), gq, gk, bd512)

    att = _attention(qkv, n_att_heads)

    zero_w = jnp.zeros((2, LANES - DECAY_LORA, RWKV_WIDTH), F32)
    wup_pad = jnp.concatenate([w_decay_up, zero_w], axis=1).astype(BF16)
    aup_pad = jnp.concatenate([zero_w, a_gate_up], axis=1).astype(BF16)
    pk, plw = _rwkv_prep(zrw, mu_prev.reshape(1, -1), mu_next.reshape(1, -1), wup_pad, aup_pad, g_up.astype(BF16),
                         w_decay0, a_gate0, k_k.reshape(1, -1), k_a.reshape(1, -1), r_k.reshape(1, -1), bd512)
    rp, qp, gm, hm = _rwkv_p1(pk, plw)
    rw = _rwkv_p2(rp, qp, gm, hm, pk, lnx_g.reshape(1, -1), lnx_b.reshape(1, -1), bd128)

    x1, h2, s1t, s2t = _outproj(att, rw, x, mod6, norm2_g.reshape(1, d), w_out.astype(BF16),
                                peer_w_query.astype(BF16), peer_sub_keys1.astype(BF16),
                                peer_sub_keys2.astype(BF16))
    c1, e1, r2, e2 = _peer_topk(s1t, s2t)
    out = _peer_ffn(h2.reshape(bsz * s, d), peer_u.astype(BF16), peer_v.T.astype(BF16), c1, e1, r2, e2,
                    x1.reshape(bsz * s, d), mod6, s)
    return out.reshape(bsz, s, d)


def kernel(x, c, ada_w, ada_b, norm1_g, w_in, mu_prev, mu_next, q_norm_g, k_norm_g, w_decay0, w_decay_up,
           a_gate0, a_gate_up, g_up, k_k, k_a, r_k, lnx_g, lnx_b, w_out, norm2_g, peer_w_query,
           peer_sub_keys1, peer_sub_keys2, peer_u, peer_v):
    depth = ada_w.shape[0]
    for l in range(depth):
        x = _layer(x, c, ada_w[l], ada_b[l], norm1_g[l], w_in[l], mu_prev[l], mu_next[l], q_norm_g[l],
                   k_norm_g[l], w_decay0[l], w_decay_up[l], a_gate0[l], a_gate_up[l], g_up[l], k_k[l], k_a[l],
                   r_k[l], lnx_g[l], lnx_b[l], w_out[l], norm2_g[l], peer_w_query[l], peer_sub_keys1[l],
                   peer_sub_keys2[l], peer_u[l], peer_v[l])
    return x
```

```python
import functools
import math

import jax
import jax.numpy as jnp
from jax import lax
from jax.experimental import pallas as pl
from jax.experimental.pallas import tpu as pltpu

F32 = jnp.float32
BF16 = jnp.bfloat16

HEAD_DIM = 64
ATT_WIDTH = 512
RWKV_WIDTH = 512
DECAY_LORA = 64
AAA_LORA = 64
GATE_LORA = 128
SHIFT_WIDTH = 3 * RWKV_WIDTH + DECAY_LORA + AAA_LORA + GATE_LORA
PEER_N_KEYS = 128
PEER_HEADS = 8
PEER_TOPK = 16
NORM_EPS = 1e-6
LNX_EPS = 64e-5
MASK_VALUE = -1e30
LOG2E = 1.4426950408889634
ATT_HALF_WINDOWS = ((1, 64), (4, 256), (16, 1024))

LANES = 128
BF16_ROWS = 16
CHUNK = 64
ATT_BLOCK = 128
VMEM_LIMIT_BYTES = 56 * 1024 * 1024
WKV_PASSES_CHAIN = 1

NN = (((1,), (0,)), ((), ()))
NT = (((1,), (1,)), ((), ()))
TN = (((0,), (0,)), ((), ()))

SLAB_R, SLAB_V, SLAB_KK, SLAB_DIR0, SLAB_G, SLAB_BONUS, NUM_SLABS = 0, 1, 2, 3, 7, 8, 9


def _params(*sem):
    return pltpu.CompilerParams(dimension_semantics=sem, vmem_limit_bytes=VMEM_LIMIT_BYTES)


def _dot(a, b, dims=NN):
    return lax.dot_general(a, b, dims, preferred_element_type=F32)


def _split(a):
    hi = a.astype(BF16)
    lo = (a - hi.astype(F32)).astype(BF16)
    return hi, lo


def _mm(a, b, dims=NN, passes=3):
    if passes == 1:
        return _dot(a.astype(BF16), b.astype(BF16), dims)
    ah, al = _split(a)
    bh, bl = _split(b)
    return _dot(ah, bh, dims) + (_dot(ah, bl, dims) + _dot(al, bh, dims))


def _mm_exact_rhs(a, b_exact):
    ah, al = _split(a)
    return _dot(ah, b_exact) + _dot(al, b_exact)


def _sigmoid(x):
    return 1.0 / (1.0 + jnp.exp(-x))


def _ada_kernel(c_ref, w_ref, b_ref, o_ref):
    c = c_ref[...]
    o_ref[...] = _mm(c * _sigmoid(c), w_ref[...]) + b_ref[...]


def _ada(c, ada_w, ada_b):
    bsz, d = c.shape
    n = ada_w.shape[1]
    tn = 1024
    return pl.pallas_call(
        _ada_kernel,
        out_shape=jax.ShapeDtypeStruct((bsz, n), F32),
        grid=(n // tn,),
        in_specs=[pl.BlockSpec((bsz, d), lambda j: (0, 0)),
                  pl.BlockSpec((d, tn), lambda j: (0, j)),
                  pl.BlockSpec((1, tn), lambda j: (0, j))],
        out_specs=pl.BlockSpec((bsz, tn), lambda j: (0, j)),
        compiler_params=_params("arbitrary"),
        name="ada",
    )(c, ada_w, ada_b.reshape(1, n))


def _inproj_kernel(x_ref, mod_ref, g1_ref, w_ref, gq_ref, gk_ref, bd_ref, qkv_ref, zrw_ref):
    x = x_ref[0]
    mod = mod_ref[0]
    ms = jnp.mean(x * x, axis=-1, keepdims=True)
    h = x * lax.rsqrt(ms + NORM_EPS) * g1_ref[...]
    h = (h * (1.0 + mod[1:2]) + mod[0:1]).astype(BF16)
    bd = bd_ref[...]

    def head_norm(z, g):
        ss = _mm_exact_rhs(z * z, bd) * (1.0 / HEAD_DIM)
        return z * lax.rsqrt(ss + NORM_EPS) * g

    a = ATT_WIDTH
    zq = _dot(h, w_ref[:, 0:a])
    qkv_ref[0, :, 0:a] = (head_norm(zq, gq_ref[...]) * (HEAD_DIM ** -0.5 * LOG2E)).astype(BF16)
    zk = _dot(h, w_ref[:, a:2 * a])
    qkv_ref[0, :, a:2 * a] = head_norm(zk, gk_ref[...]).astype(BF16)
    qkv_ref[0, :, 2 * a:3 * a] = _dot(h, w_ref[:, 2 * a:3 * a]).astype(BF16)
    zrw_ref[0] = _dot(h, w_ref[:, 3 * a:])


def _inproj(x, mod6, norm1_g, w_in_bf, gq, gk, bd512, tm=1024):
    bsz, s, d = x.shape
    nin = w_in_bf.shape[1]
    return pl.pallas_call(
        _inproj_kernel,
        out_shape=(jax.ShapeDtypeStruct((bsz, s, 3 * ATT_WIDTH), BF16),
                   jax.ShapeDtypeStruct((bsz, s, SHIFT_WIDTH), F32)),
        grid=(bsz, s // tm),
        in_specs=[pl.BlockSpec((1, tm, d), lambda b, i: (b, i, 0)),
                  pl.BlockSpec((1, 6, d), lambda b, i: (b, 0, 0)),
                  pl.BlockSpec((1, d), lambda b, i: (0, 0)),
                  pl.BlockSpec((d, nin), lambda b, i: (0, 0)),
                  pl.BlockSpec((1, ATT_WIDTH), lambda b, i: (0, 0)),
                  pl.BlockSpec((1, ATT_WIDTH), lambda b, i: (0, 0)),
                  pl.BlockSpec((ATT_WIDTH, ATT_WIDTH), lambda b, i: (0, 0))],
        out_specs=(pl.BlockSpec((1, tm, 3 * ATT_WIDTH), lambda b, i: (b, i, 0)),
                   pl.BlockSpec((1, tm, SHIFT_WIDTH), lambda b, i: (b, i, 0))),
        compiler_params=_params("arbitrary", "arbitrary"),
        name="inproj",
    )(x, mod6, norm1_g, w_in_bf, gq, gk, bd512)


ATT_FAR_DILATION, ATT_FAR_HALF = ATT_HALF_WINDOWS[-1]
ATT_NEAR_BLOCKS = 6


def _attn_near_window(nkb):
    return min(nkb, ATT_NEAR_BLOCKS)


def _attn_near_start(qb, nkb):
    return jnp.clip(((qb - 2) // 2) * 2, 0, nkb - _attn_near_window(nkb))


def _attn_far_bias(n_heads, nkb):
    per_block = ATT_BLOCK // ATT_FAR_DILATION
    i = jnp.arange(ATT_BLOCK, dtype=jnp.int32)[:, None]
    ip = jnp.arange(ATT_BLOCK, dtype=jnp.int32)[None, :]
    start = _attn_near_start(i // per_block, nkb)
    kb = ip // per_block
    dist = ATT_FAR_DILATION * jnp.abs(i - ip)
    far = (dist <= ATT_FAR_HALF) & ~((kb >= start) & (kb < start + _attn_near_window(nkb)))
    slopes = jnp.exp2(-8.0 * (jnp.arange(n_heads, dtype=F32) + 1.0) / n_heads) * LOG2E
    return jnp.where(far[None], -slopes[:, None, None] * dist.astype(F32)[None], MASK_VALUE)


def _attn_bias_table(n_heads, nkb):
    reach = _attn_near_window(nkb) - 1
    nd = 2 * reach + 1
    dd = jnp.arange(nd, dtype=jnp.int32)[None, :, None]
    r = jnp.arange(ATT_BLOCK, dtype=jnp.int32)[:, None, None]
    c = jnp.arange(ATT_BLOCK, dtype=jnp.int32)[None, None, :]
    dt = r - c - (dd - reach) * ATT_BLOCK
    adt = jnp.abs(dt)
    mult = jnp.zeros(dt.shape, F32)
    for dil, half in ATT_HALF_WINDOWS:
        mult = mult + ((adt <= half) & (dt % dil == 0)).astype(F32)
    logm = jnp.where(mult > 0, jnp.log2(jnp.maximum(mult, 1.0)), MASK_VALUE)
    slopes = jnp.exp2(-8.0 * (jnp.arange(n_heads, dtype=F32) + 1.0) / n_heads) * LOG2E
    bias = logm[None] - slopes[:, None, None, None] * adt.astype(F32)[None]
    return bias.reshape(n_heads, ATT_BLOCK, nd * ATT_BLOCK)


def _attn_kernel(q_ref, k_ref, v_ref, bias_ref, far_ref, o_ref, s_ref, v1_ref, g_ref, accf_ref, mf_ref,
                 *, nkb, win):
    blk = ATT_BLOCK
    reach = win - 1
    has_far = win < nkb
    lane = lax.broadcasted_iota(jnp.int32, (blk, LANES), 1)
    first = lane < HEAD_DIM

    def two_heads(q):
        zero = jnp.zeros_like(q)
        return jnp.concatenate([jnp.where(first, q, zero), jnp.where(first, zero, q)], axis=0)

    def with_ones(v, hh):
        is_own = (lax.broadcasted_iota(jnp.int32, v.shape, 1) < HEAD_DIM) == (hh == 0)
        return jnp.where(is_own, v, jnp.ones_like(v))

    v = v_ref[0]
    for hh in range(2):
        v1_ref[hh] = with_ones(v, hh)

    if has_far:
        g_ref[0] = q_ref[0].astype(F32)
        g_ref[1] = k_ref[0].astype(F32)
        g_ref[2] = v.astype(F32)
        n_far = nkb * blk // ATT_FAR_DILATION

        def residue(r, _):
            rows = pl.ds(r, n_far, stride=ATT_FAR_DILATION)
            qr = g_ref[0, rows, :].astype(BF16)
            kr = g_ref[1, rows, :].astype(BF16)
            vr = g_ref[2, rows, :].astype(BF16)
            s = _dot(two_heads(qr), kr, NT)
            for hh in range(2):
                sb = s[hh * blk:(hh + 1) * blk] + far_ref[hh]
                mx = jnp.max(sb, axis=-1, keepdims=True)
                p = jnp.exp2(sb - mx).astype(BF16)
                accf_ref[hh, rows, :] = _dot(p, with_ones(vr, hh))
                mf_ref[hh, rows, :] = jnp.broadcast_to(mx, (blk, LANES))
            return 0

        lax.fori_loop(0, ATT_FAR_DILATION, residue, 0, unroll=8)

    group = s_ref.shape[0]
    slabs = range(0, win, 2)

    def query_block_group(i, _):
        us = range(group)
        qis = [group * i + u for u in us]
        rows = [pl.ds(pl.multiple_of(qi * blk, blk), blk) for qi in qis]
        qab = [two_heads(q_ref[0, rows[u], :]) for u in us]
        ks = [_attn_near_start(qi, nkb) if has_far else 0 for qi in qis]

        m = [[jnp.full((blk, LANES), -jnp.inf, F32) for _ in range(2)] for _ in us]
        for w0 in slabs:
            for u in us:
                kj0 = ks[u] + w0
                kslab = k_ref[0, pl.ds(pl.multiple_of(kj0 * blk, blk), 2 * blk), :]
                s = _dot(qab[u], kslab, NT)
                x0 = pl.multiple_of((kj0 - qis[u] + reach) * blk, blk)
                for hh in range(2):
                    sb = s[hh * blk:(hh + 1) * blk] + bias_ref[hh, :, pl.ds(x0, 2 * blk)]
                    s_ref[u, hh, :, w0 * blk:(w0 + 2) * blk] = sb
                    m[u][hh] = jnp.maximum(m[u][hh], jnp.maximum(sb[:, :blk], sb[:, blk:]))
        mrow = [[jnp.max(m[u][hh], axis=-1, keepdims=True) for hh in range(2)] for u in us]

        if has_far:
            acc = [[None, None] for _ in us]
            for u in us:
                for hh in range(2):
                    m_far = mf_ref[hh, rows[u], :][:, 0:1]
                    m_all = jnp.maximum(mrow[u][hh], m_far)
                    acc[u][hh] = accf_ref[hh, rows[u], :] * jnp.exp2(m_far - m_all)
                    mrow[u][hh] = m_all
        else:
            acc = [[jnp.zeros((blk, LANES), F32) for _ in range(2)] for _ in us]
        for w0 in slabs:
            for u in us:
                keys = pl.ds(pl.multiple_of((ks[u] + w0) * blk, blk), 2 * blk)
                for hh in range(2):
                    p = jnp.exp2(s_ref[u, hh, :, w0 * blk:(w0 + 2) * blk] - mrow[u][hh]).astype(BF16)
                    acc[u][hh] = acc[u][hh] + _dot(p, v1_ref[hh, keys, :])
        for u in us:
            num = jnp.where(first, acc[u][0], acc[u][1])
            den = jnp.where(first, pltpu.roll(acc[u][0], HEAD_DIM, axis=1), pltpu.roll(acc[u][1], HEAD_DIM, axis=1))
            o_ref[0, rows[u], :] = (num / den).astype(o_ref.dtype)
        return 0

    lax.fori_loop(0, nkb // group, query_block_group, 0)


def _attention(qkv, n_heads):
    bsz, s, _ = qkv.shape
    nkb = s // ATT_BLOCK
    win = _attn_near_window(nkb)
    assert nkb % 2 == 0 and (win == nkb or s // ATT_FAR_DILATION == ATT_BLOCK)
    bias = _attn_bias_table(n_heads, nkb)
    far = _attn_far_bias(n_heads, nkb)
    npair = ATT_WIDTH // LANES
    return pl.pallas_call(
        functools.partial(_attn_kernel, nkb=nkb, win=win),
        out_shape=jax.ShapeDtypeStruct((bsz, s, ATT_WIDTH), BF16),
        grid=(npair, bsz),
        in_specs=[pl.BlockSpec((1, s, LANES), lambda hp, b: (b, 0, hp)),
                  pl.BlockSpec((1, s, LANES), lambda hp, b: (b, 0, npair + hp)),
                  pl.BlockSpec((1, s, LANES), lambda hp, b: (b, 0, 2 * npair + hp)),
                  pl.BlockSpec((2, ATT_BLOCK, bias.shape[2]), lambda hp, b: (hp, 0, 0)),
                  pl.BlockSpec((2, ATT_BLOCK, ATT_BLOCK), lambda hp, b: (hp, 0, 0))],
        out_specs=pl.BlockSpec((1, s, LANES), lambda hp, b: (b, 0, hp)),
        scratch_shapes=[pltpu.VMEM((math.gcd(nkb, 4), 2, ATT_BLOCK, win * ATT_BLOCK), F32), pltpu.VMEM((2, s, LANES), BF16),
                        pltpu.VMEM((3, s, LANES), F32), pltpu.VMEM((2, s, LANES), F32),
                        pltpu.VMEM((2, s, LANES), F32)],
        compiler_params=_params("arbitrary", "arbitrary"),
        name="attn",
    )(qkv, qkv, qkv, bias, far)


def _prep_kernel(z_ref, zp_ref, zn_ref, mup_ref, mun_ref, wup_ref, aup_ref, gup_ref, w0_ref, a0_ref,
                 kk_ref, ka_ref, rk_ref, bd_ref, o_ref, lw_ref, *, tq):
    i = pl.program_id(1)
    last = pl.num_programs(1) - 1
    z = z_ref[0]
    row = lax.broadcasted_iota(jnp.int32, (tq, 1), 0)
    prev_row = zp_ref[0, 7:8, :] * (i > 0).astype(F32)
    next_row = zn_ref[0, 0:1, :] * (i < last).astype(F32)
    zp = jnp.where(row == 0, prev_row, pltpu.roll(z, 1, axis=0))
    zn = jnp.where(row == tq - 1, next_row, pltpu.roll(z, tq - 1, axis=0))
    zs = z + mup_ref[...] * (zp - z) + mun_ref[...] * (zn - z)

    w = RWKV_WIDTH
    r = zs[:, 0:w]
    k = zs[:, w:2 * w]
    v = zs[:, 2 * w:3 * w]
    xwa = zs[:, 3 * w:3 * w + LANES]
    xg = zs[:, 3 * w + LANES:]
    bd = bd_ref[...]

    g = _dot(_sigmoid(xg).astype(BF16), gup_ref[...])
    kk = k * kk_ref[...]
    ss = _mm_exact_rhs(kk * kk, bd)
    kk = kk * lax.rsqrt(jnp.maximum(ss, 1e-12))
    ka = ka_ref[...]
    txw = jnp.tanh(xwa).astype(BF16)
    xab = xwa.astype(BF16)

    def put(slab, val):
        o_ref[0, :, slab * w:(slab + 1) * w] = val.astype(o_ref.dtype)

    put(SLAB_R, r)
    put(SLAB_V, v)
    put(SLAB_KK, kk)
    put(SLAB_G, g)
    a_sum = jnp.zeros_like(k)
    for d in range(2):
        y = w0_ref[d:d + 1, :] + _dot(txw, wup_ref[d])
        wlog = -(jnp.maximum(-y, 0.0) + jnp.log(1.0 + jnp.exp(-jnp.abs(y)))) - 0.5
        a = _sigmoid(a0_ref[d:d + 1, :] + _dot(xab, aup_ref[d]))
        a_sum = a_sum + a
        lw_ref[0, :, d * w:(d + 1) * w] = -jnp.exp(wlog)
        put(SLAB_DIR0 + 2 * d, k * (1.0 + (a - 1.0) * ka))
        put(SLAB_DIR0 + 2 * d + 1, kk * a)
    k_bonus = k * (1.0 + (0.5 * a_sum - 1.0) * ka)
    bsum = _mm_exact_rhs(r * k_bonus * rk_ref[...], bd)
    put(SLAB_BONUS, bsum * v)


def _rwkv_prep(zrw, mu_prev, mu_next, wup_pad, aup_pad, g_up_bf, w0, a0, k_k, k_a, r_k, bd512, tq=512):
    bsz, s, sw = zrw.shape
    w = RWKV_WIDTH
    nt = s // tq
    row = lambda b, i: (0, 0)
    return pl.pallas_call(
        functools.partial(_prep_kernel, tq=tq),
        out_shape=(jax.ShapeDtypeStruct((bsz, s, NUM_SLABS * w), BF16),
                   jax.ShapeDtypeStruct((bsz, s, 2 * w), F32)),
        grid=(bsz, nt),
        in_specs=[pl.BlockSpec((1, tq, sw), lambda b, i: (b, i, 0)),
                  pl.BlockSpec((1, 8, sw), lambda b, i: (b, jnp.maximum(i * (tq // 8) - 1, 0), 0)),
                  pl.BlockSpec((1, 8, sw), lambda b, i: (b, jnp.minimum((i + 1) * (tq // 8), s // 8 - 1), 0)),
                  pl.BlockSpec((1, sw), row),
                  pl.BlockSpec((1, sw), row),
                  pl.BlockSpec((2, LANES, w), lambda b, i: (0, 0, 0)),
                  pl.BlockSpec((2, LANES, w), lambda b, i: (0, 0, 0)),
                  pl.BlockSpec((GATE_LORA, w), row),
                  pl.BlockSpec((2, w), row),
                  pl.BlockSpec((2, w), row),
                  pl.BlockSpec((1, w), row),
                  pl.BlockSpec((1, w), row),
                  pl.BlockSpec((1, w), row),
                  pl.BlockSpec((w, w), row)],
        out_specs=(pl.BlockSpec((1, tq, NUM_SLABS * w), lambda b, i: (b, i, 0)),
                   pl.BlockSpec((1, tq, 2 * w), lambda b, i: (b, i, 0))),
        compiler_params=_params("arbitrary", "arbitrary"),
        name="rwkv_prep",
    )(zrw, zrw, zrw, mu_prev, mu_next, wup_pad, aup_pad, g_up_bf, w0, a0, k_k, k_a, r_k, bd512)


def _block_diag(x, first):
    zero = jnp.zeros_like(x)
    return jnp.concatenate([jnp.where(first, x, zero), jnp.where(first, zero, x)], axis=0)


def _p1_kernel(r_ref, v_ref, kk_ref, lw_ref, kd_ref, be_ref, rp_ref, qp_ref, g_ref, h_ref, *, cpb):
    c = CHUNK
    sign = 1 - 2 * pl.program_id(2)
    rowi = lax.broadcasted_iota(jnp.int32, (c, LANES), 0)
    lane = lax.broadcasted_iota(jnp.int32, (c, LANES), 1)
    coli = lane & (c - 1)
    first = lane < HEAD_DIM
    before = sign * (rowi - coli)
    strict2 = before > 0
    incl2 = before >= 0
    eye2 = (coli == rowi).astype(F32)
    r64 = lax.broadcasted_iota(jnp.int32, (c, c), 0)
    c64 = lax.broadcasted_iota(jnp.int32, (c, c), 1)
    cum = (sign * (r64 - c64) >= 0).astype(BF16)

    js = range(cpb)
    sls = [pl.ds(j * c, c) for j in js]
    bdg = lambda x: _block_diag(x, first)
    r = [r_ref[0, sl, :].astype(F32) for sl in sls]
    v = [v_ref[0, sl, :].astype(F32) for sl in sls]
    kk = [kk_ref[0, sl, :].astype(F32) for sl in sls]
    lw = [lw_ref[0, sl, :] for sl in sls]
    kd = [kd_ref[0, sl, :].astype(F32) for sl in sls]
    be = [be_ref[0, sl, :].astype(F32) for sl in sls]

    one = lambda x, y, dims=NN: _mm(x, y, dims, 1)
    rows2 = lambda x, y: jnp.concatenate([x, y], axis=0)
    cols2 = lambda x, y: jnp.concatenate([x, y], axis=1)

    def cumsum(x):
        l1 = x.astype(BF16)
        rem = x - l1.astype(F32)
        l2 = rem.astype(BF16)
        l3 = (rem - l2.astype(F32)).astype(BF16)
        y = _dot(cum, jnp.concatenate([l1, l2, l3], axis=1))
        return y[:, :LANES] + (y[:, LANES:2 * LANES] + y[:, 2 * LANES:])

    cs = [cumsum(lw[j]) for j in js]
    g_inv = [jnp.exp(-cs[j]) for j in js]
    g_tot = [jnp.exp(jnp.sum(lw[j], axis=0, keepdims=True)) for j in js]
    ab = [-kk[j] * jnp.exp(cs[j] - lw[j]) for j in js]
    rb = [r[j] * jnp.exp(cs[j]) for j in js]
    bt = [be[j] * g_inv[j] for j in js]
    kt = [kd[j] * g_inv[j] for j in js]

    sc = [one(rows2(ab[j], rb[j]), rows2(bdg(bt[j]), bdg(kt[j])), NT) for j in js]
    m_ab = [jnp.where(strict2, sc[j][:c, :LANES], 0.0) for j in js]
    n_rb = [jnp.where(incl2, sc[j][c:, :LANES], 0.0) for j in js]
    m_ak = [jnp.where(strict2, sc[j][:c, LANES:], 0.0) for j in js]
    n_rk = [jnp.where(incl2, sc[j][c:, LANES:], 0.0) for j in js]

    levels = int(math.log2(c)) - 1
    t = [eye2 + m_ab[j] for j in js]
    p = [one(m_ab[j], bdg(m_ab[j])) for j in js]
    for _ in range(1, levels - 1):
        y = [one(rows2(p[j], t[j]), bdg(p[j])) for j in js]
        p = [y[j][:c] for j in js]
        t = [t[j] + y[j][c:] for j in js]
    t = [t[j] + one(t[j], bdg(p[j])) for j in js]

    def residual(a, tt):
        ah, al = _split(a)
        th, tl = _split(bdg(tt))
        y = _dot(rows2(ah, al), th)
        return eye2 - (y[:c] + (y[c:] + _dot(ah, tl)))

    res = [residual(eye2 - m_ab[j], t[j]) for j in js]
    t = [t[j] + one(t[j], bdg(res[j])) for j in js]

    vbd = [bdg(v[j]) for j in js]
    mv = [one(rows2(m_ak[j], n_rk[j]), vbd[j]) for j in js]
    tp = [one(t[j], cols2(bdg(ab[j]), bdg(mv[j][:c]))) for j in js]
    ap = [tp[j][:, :LANES] for j in js]
    pp = [tp[j][:, LANES:] for j in js]
    nr = [one(n_rb[j], cols2(bdg(ap[j]), bdg(pp[j]))) for j in js]
    for j in js:
        rp_ref[0, 0, 0, sls[j], :] = rb[j] + nr[j][:, :LANES]
        qp_ref[0, 0, 0, sls[j], :] = nr[j][:, LANES:] + mv[j][c:]
    heads = lambda x: jnp.where(first, x[:HEAD_DIM], x[HEAD_DIM:])
    for j in js:
        st = one(cols2(ap[j], pp[j]), bt[j], TN)
        g_ref[0, 0, 0, j] = (eye2 + heads(st[:LANES])) * g_tot[j]
        h_ref[0, 0, 0, j] = heads(st[LANES:] + one(v[j], kt[j], TN)) * g_tot[j]


def _rwkv_p1(pk, plw, cpb=32):
    bsz, s, _ = pk.shape
    nc = s // CHUNK
    npair = RWKV_WIDTH // LANES
    rows = cpb * CHUNK

    def slab(sidx):
        return pl.BlockSpec((1, rows, LANES), lambda b, hp, d, ci: (b, ci, sidx * npair + hp))

    def dslab(off):
        return pl.BlockSpec((1, rows, LANES),
                            lambda b, hp, d, ci: (b, ci, (SLAB_DIR0 + 2 * d + off) * npair + hp))

    lw_spec = pl.BlockSpec((1, rows, LANES), lambda b, hp, d, ci: (b, ci, d * npair + hp))

    seq = pl.BlockSpec((1, 1, 1, rows, LANES), lambda b, hp, d, ci: (b, hp, d, ci, 0))
    mat = pl.BlockSpec((1, 1, 1, cpb, HEAD_DIM, LANES), lambda b, hp, d, ci: (b, hp, d, ci, 0, 0))
    return pl.pallas_call(
        functools.partial(_p1_kernel, cpb=cpb),
        out_shape=(jax.ShapeDtypeStruct((bsz, npair, 2, s, LANES), F32),
                   jax.ShapeDtypeStruct((bsz, npair, 2, s, LANES), F32),
                   jax.ShapeDtypeStruct((bsz, npair, 2, nc, HEAD_DIM, LANES), F32),
                   jax.ShapeDtypeStruct((bsz, npair, 2, nc, HEAD_DIM, LANES), F32)),
        grid=(bsz, npair, 2, nc // cpb),
        in_specs=[slab(SLAB_R), slab(SLAB_V), slab(SLAB_KK), lw_spec, dslab(0), dslab(1)],
        out_specs=(seq, seq, mat, mat),
        compiler_params=_params("arbitrary", "arbitrary", "arbitrary", "arbitrary"),
        name="rwkv_p1",
    )(pk, pk, pk, plw, pk, pk)


def _p2_kernel(rp_ref, qp_ref, g_ref, h_ref, gate_ref, bonus_ref, lng_ref, lnb_ref, bd_ref, o_ref,
               y_ref, s_ref, *, nc, te, hpb):
    c = CHUNK
    s_ref[...] = jnp.zeros_like(s_ref)
    first = lax.broadcasted_iota(jnp.int32, (HEAD_DIM, LANES), 1) < HEAD_DIM
    chains = [(hq, d) for hq in range(hpb) for d in range(2)]

    def step(j, _):
        for hq, d in chains:
            jc = j if d == 0 else nc - 1 - j
            sl = pl.ds(pl.multiple_of(jc * c, c), c)
            st = s_ref[hq, d]
            y_ref[hq, d, sl, :] = (_mm(rp_ref[0, hq, d, sl, :], st, NT, WKV_PASSES_CHAIN)
                                   + qp_ref[0, hq, d, sl, :])
            gmat = _block_diag(g_ref[0, hq, d, jc], first)
            hmat = _block_diag(h_ref[0, hq, d, jc], first)
            s_ref[hq, d] = _mm(st, gmat, NN, WKV_PASSES_CHAIN) + hmat
        return 0

    lax.fori_loop(0, nc, step, 0)

    bd = bd_ref[...]
    inv = 1.0 / HEAD_DIM

    def epi(i, _):
        sl = pl.ds(pl.multiple_of(i * te, te), te)
        for hq in range(hpb):
            lanes = slice(hq * LANES, (hq + 1) * LANES)
            y = y_ref[hq, 0, sl, :] + y_ref[hq, 1, sl, :]
            mean = _mm_exact_rhs(y, bd) * inv
            yc = y - mean
            var = _mm_exact_rhs(yc * yc, bd) * inv
            yn = yc * lax.rsqrt(var + LNX_EPS) * lng_ref[:, lanes] + lnb_ref[:, lanes]
            o_ref[0, sl, lanes] = ((yn + bonus_ref[0, sl, lanes].astype(F32))
                                   * gate_ref[0, sl, lanes].astype(F32)).astype(o_ref.dtype)
        return 0

    lax.fori_loop(0, (nc * c) // te, epi, 0)


def _rwkv_p2(rp, qp, gm, hm, pk, lnx_g, lnx_b, bd128, te=1024, hpb=2):
    bsz, npair, _, s, _ = rp.shape
    nc = s // CHUNK
    width = hpb * LANES
    nq = npair // hpb
    seq = pl.BlockSpec((1, hpb, 2, s, LANES), lambda b, q: (b, q, 0, 0, 0))
    mat = pl.BlockSpec((1, hpb, 2, nc, HEAD_DIM, LANES), lambda b, q: (b, q, 0, 0, 0, 0))
    return pl.pallas_call(
        functools.partial(_p2_kernel, nc=nc, te=min(te, s), hpb=hpb),
        out_shape=jax.ShapeDtypeStruct((bsz, s, RWKV_WIDTH), BF16),
        grid=(bsz, nq),
        in_specs=[seq, seq, mat, mat,
                  pl.BlockSpec((1, s, width), lambda b, q: (b, 0, SLAB_G * nq + q)),
                  pl.BlockSpec((1, s, width), lambda b, q: (b, 0, SLAB_BONUS * nq + q)),
                  pl.BlockSpec((1, width), lambda b, q: (0, q)),
                  pl.BlockSpec((1, width), lambda b, q: (0, q)),
                  pl.BlockSpec((LANES, LANES), lambda b, q: (0, 0))],
        out_specs=pl.BlockSpec((1, s, width), lambda b, q: (b, 0, q)),
        scratch_shapes=[pltpu.VMEM((hpb, 2, s, LANES), F32), pltpu.VMEM((hpb, 2, LANES, LANES), F32)],
        compiler_params=_params("arbitrary", "arbitrary"),
        name="rwkv_p2",
    )(rp, qp, gm, hm, pk, pk, lnx_g, lnx_b, bd128)


def _outproj_kernel(att_ref, rw_ref, x_ref, mod_ref, g2_ref, wo_ref, wq_ref, k1_ref, k2_ref,
                    x1_ref, h2_ref, s1_ref, s2_ref):
    mod = mod_ref[0]
    a = ATT_WIDTH
    mixed = _dot(att_ref[0], wo_ref[0:a, :]) + _dot(rw_ref[0], wo_ref[a:, :])
    x1 = x_ref[0] + mod[2:3] * mixed
    x1_ref[0] = x1
    ms = jnp.mean(x1 * x1, axis=-1, keepdims=True)
    h2 = x1 * lax.rsqrt(ms + NORM_EPS) * g2_ref[...]
    h2 = (h2 * (1.0 + mod[4:5]) + mod[3:4]).astype(BF16)
    h2_ref[0] = h2
    q = _dot(h2, wq_ref[...])
    k1 = k1_ref[...]
    k2 = k2_ref[...]
    for h in range(PEER_HEADS):
        base = h * 2 * LANES
        s1 = _dot(k1, q[:, base:base + LANES].astype(BF16), NT)
        s2 = _dot(k2, q[:, base + LANES:base + 2 * LANES].astype(BF16), NT)
        for g in range(s1.shape[1] // LANES):
            s1_ref[h, g] = s1[:, g * LANES:(g + 1) * LANES]
            s2_ref[h, g] = s2[:, g * LANES:(g + 1) * LANES]


def _outproj(att, rw, x, mod6, norm2_g, w_out_bf, wq_bf, k1_bf, k2_bf, tm=512):
    bsz, s, d = x.shape
    nt = s // tm
    t = bsz * s
    nq = wq_bf.shape[1]
    const = lambda b, i: (0, 0)
    gpt = tm // LANES
    tok = pl.BlockSpec((PEER_HEADS, gpt, PEER_N_KEYS, LANES), lambda b, i: (0, b * nt + i, 0, 0))
    stat = jax.ShapeDtypeStruct((PEER_HEADS, t // LANES, PEER_N_KEYS, LANES), F32)
    return pl.pallas_call(
        _outproj_kernel,
        out_shape=(jax.ShapeDtypeStruct((bsz, s, d), F32),
                   jax.ShapeDtypeStruct((bsz, s, d), BF16),
                   stat, stat),
        grid=(bsz, nt),
        in_specs=[pl.BlockSpec((1, tm, ATT_WIDTH), lambda b, i: (b, i, 0)),
                  pl.BlockSpec((1, tm, RWKV_WIDTH), lambda b, i: (b, i, 0)),
                  pl.BlockSpec((1, tm, d), lambda b, i: (b, i, 0)),
                  pl.BlockSpec((1, 6, d), lambda b, i: (b, 0, 0)),
                  pl.BlockSpec((1, d), const),
                  pl.BlockSpec((d, d), const),
                  pl.BlockSpec((d, nq), const),
                  pl.BlockSpec((PEER_N_KEYS, LANES), const),
                  pl.BlockSpec((PEER_N_KEYS, LANES), const)],
        out_specs=(pl.BlockSpec((1, tm, d), lambda b, i: (b, i, 0)),
                   pl.BlockSpec((1, tm, d), lambda b, i: (b, i, 0)),
                   tok, tok),
        compiler_params=_params("arbitrary", "arbitrary"),
        name="outproj",
    )(att, rw, x, mod6, norm2_g, w_out_bf, wq_bf, k1_bf, k2_bf)


def _oddeven_sort_pairs(n):
    pairs = []
    p = 1
    while p < n:
        k = p
        while k >= 1:
            for j in range(k % p, n - k, 2 * k):
                for i in range(min(k, n - j - k)):
                    if (i + j) // (2 * p) == (i + j + k) // (2 * p):
                        pairs.append((i + j, i + j + k))
            k //= 2
        p *= 2
    return pairs


def _bitonic_merge_pairs(n):
    pairs = []
    stride = n // 2
    while stride >= 1:
        pairs += [(i, i + stride) for i in range(n) if not i & stride]
        stride //= 2
    return pairs


def _compare_exchange(vals, pairs):
    vals = list(vals)
    for i, j in pairs:
        a, b = vals[i], vals[j]
        if b is None:
            continue
        if a is None:
            vals[i], vals[j] = b, None
        else:
            vals[i], vals[j] = jnp.maximum(a, b), jnp.minimum(a, b)
    return vals


def _top16_sorted(vregs):
    kk = PEER_TOPK
    vals = list(vregs) + [None] * (kk - len(vregs))
    vals = _compare_exchange(vals, _oddeven_sort_pairs(kk))
    for shift in (4, 2, 1):
        other = [None if v is None else pltpu.roll(v, shift, axis=0) for v in vals]
        merged = []
        for k in range(kk):
            a, b = vals[k], other[kk - 1 - k]
            merged.append(b if a is None else a if b is None else jnp.maximum(a, b))
        vals = _compare_exchange(merged, _bitonic_merge_pairs(kk))
    return vals


def _topk_kernel(s1_ref, s2_ref, c1_ref, e1_ref, r2_ref, e2_ref, *, groups):
    kk = PEER_TOPK
    sub = 8
    neg = -jnp.inf
    rows8 = lax.broadcasted_iota(jnp.int32, (sub, LANES), 0)

    def rows_of(reps):
        out = reps[0]
        for r in range(1, sub):
            out = jnp.where(rows8 == r, reps[r], out)
        return out

    def all_sublanes_sum(x):
        for shift in (4, 2, 1):
            x = x + pltpu.roll(x, shift, axis=0)
        return x

    def per_tile(it_idx, _):
        g = it_idx // PEER_HEADS
        h = it_idx % PEER_HEADS
        s1 = [s1_ref[h, g, pl.ds(v * sub, sub), :] for v in range(PEER_N_KEYS // sub)]
        s2 = [s2_ref[h, g, pl.ds(v * sub, sub), :] for v in range(PEER_N_KEYS // sub)]
        a = _top16_sorted(s1)
        b = _top16_sorted(s2)
        a_lo, a_hi = rows_of(a[:sub]), rows_of(a[sub:])
        b_lo, b_hi = rows_of(b[:sub]), rows_of(b[sub:])

        cand = [a[0] + b_lo, a[0] + b_hi, a[1] + b_lo]
        for i in range(2, sub):
            cand.append(jnp.where(rows8 < kk // (i + 1), a[i] + b_lo, neg))
        cand.append(a_hi + b[0])
        top = _top16_sorted(cand)
        tau = top[kk - 1]
        z = None
        for t in top:
            ez = jnp.exp(t - top[0])
            z = ez if z is None else z + ez
        inv_z = 1.0 / z

        counts = []
        for r in range(kk):
            hits = (jnp.where(a[r] + b_lo >= tau, 1.0, 0.0) + jnp.where(a[r] + b_hi >= tau, 1.0, 0.0))
            counts.append(all_sublanes_sum(hits))
        for v in range(PEER_N_KEYS // sub):
            rows = pl.ds(v * sub, sub)
            cnt = jnp.zeros((sub, LANES), F32)
            rank = jnp.full((sub, LANES), float(kk), F32)
            for r in range(kk):
                cnt = jnp.where(s1[v] == a[r], counts[r], cnt)
                rank = jnp.where(s2[v] == b[r], float(r), rank)
            c1_ref[h, g, rows, :] = cnt
            e1_ref[h, g, rows, :] = jnp.exp(s1[v] - a[0]) * inv_z
            s2[v] = (rank, jnp.exp(s2[v] - b[0]))
        rank2 = jnp.concatenate([rv for rv, _ in s2], axis=0).astype(BF16)
        e2 = jnp.concatenate([ev for _, ev in s2], axis=0).astype(BF16)
        r2_ref[h, g] = pltpu.bitcast(rank2, jnp.uint32)
        e2_ref[h, g] = pltpu.bitcast(e2, jnp.uint32)
        return 0

    lax.fori_loop(0, groups * PEER_HEADS, per_tile, 0)


def _peer_topk(s1t, s2t, tn=512):
    nh, ng, nk, _ = s1t.shape
    groups = tn // LANES
    blk = pl.BlockSpec((nh, groups, nk, LANES), lambda i: (0, i, 0, 0))
    f32 = jax.ShapeDtypeStruct(s1t.shape, F32)
    b16 = jax.ShapeDtypeStruct((nh, ng, nk // 2, LANES), jnp.uint32)
    pblk = pl.BlockSpec((nh, groups, nk // 2, LANES), lambda i: (0, i, 0, 0))
    return pl.pallas_call(
        functools.partial(_topk_kernel, groups=groups),
        out_shape=(f32, f32, b16, b16),
        grid=(ng // groups,),
        in_specs=[blk, blk],
        out_specs=(blk, blk, pblk, pblk),
        compiler_params=_params("arbitrary"),
        name="peer_topk",
    )(s1t, s2t)


class _Pieces(list):
    every = 1
    phase = 0


def _peer_kernel(h2_ref, u_ref, vt_ref, c1_ref, e1_ref, r2_ref, e2_ref, x1_ref, mod_ref, o_ref,
                 acc_ref, act0_ref, act1_ref, p0_ref, p1_ref, *, tn, te, n_tiles):
    k = pl.program_id(0)
    n_items = pl.num_programs(0) - 2
    nk = PEER_N_KEYS
    ngroups = tn // LANES

    @pl.when(k == 0)
    def _():
        acc_ref[...] = jnp.zeros_like(acc_ref)
        for ref in (act0_ref, act1_ref, p0_ref, p1_ref):
            ref[...] = jnp.zeros_like(ref)

    ni = te // nk
    i0 = pl.multiple_of((jnp.clip(k - 1, 0, n_items - 1) % n_tiles) * ni, ni)
    tile_c = jnp.clip(k - 2, 0, n_items - 1) % n_tiles

    d_model = acc_ref.shape[0]
    ksplit = 4
    msplit = 2
    halves = [(mh, nh, kh) for mh in range(msplit) for nh in range(ngroups // 2) for kh in range(ksplit)]

    def stage_c(p_a, mh, nh, kh):
        rows = slice(mh * (d_model // msplit), (mh + 1) * (d_model // msplit))
        cols = slice(nh * 2 * LANES, (nh + 1) * 2 * LANES)
        ks = slice(kh * (te // ksplit), (kh + 1) * (te // ksplit))
        p_prev = jnp.concatenate([p_a[2 * nh, ks, :], p_a[2 * nh + 1, ks, :]], axis=1)
        acc_ref[rows, cols] += _dot(vt_ref[rows, ks], p_prev)

    def stage_a(act_a, mh, nh, kh):
        rows = slice(mh * (te // msplit), (mh + 1) * (te // msplit))
        ks = slice(kh * (d_model // ksplit), (kh + 1) * (d_model // ksplit))
        act = _dot(u_ref[rows, ks], h2_ref[nh * 2 * LANES:(nh + 1) * 2 * LANES, ks], NT)
        if kh == 0:
            act_a[2 * nh, rows, :] = act[:, :LANES]
            act_a[2 * nh + 1, rows, :] = act[:, LANES:]
        else:
            act_a[2 * nh, rows, :] += act[:, :LANES]
            act_a[2 * nh + 1, rows, :] += act[:, LANES:]

    def rows_bf16(row):
        tile = jnp.broadcast_to(row, (BF16_ROWS, LANES)).astype(BF16)
        return jnp.concatenate([tile] * (nk // BF16_ROWS), axis=0)

    quad = 4

    def stage_b(act_b, p_b, tg, iq, pieces):
        gates = [jnp.zeros((nk, LANES), BF16) for _ in range(quad)]
        for h in range(PEER_HEADS):
            if h % pieces.every == pieces.phase and pieces:
                pieces.pop(0)()
            c1 = c1_ref[h, tg, pl.ds(i0, ni), :]
            e1 = e1_ref[h, tg, pl.ds(i0, ni), :]
            rank2 = pltpu.bitcast(r2_ref[h, tg], BF16)
            e2 = pltpu.bitcast(e2_ref[h, tg], BF16)
            for q in range(quad):
                il = iq * quad + q
                sel = rank2 < rows_bf16(c1[il:il + 1, :])
                gates[q] = gates[q] + jnp.where(sel, e2 * rows_bf16(e1[il:il + 1, :]), jnp.zeros_like(e2))
        for q in range(quad):
            il = iq * quad + q
            a = act_b[tg, il * nk:(il + 1) * nk, :]
            gelu = 0.5 * a * (1.0 + lax.erf(a * (2.0 ** -0.5)))
            p_b[tg, il * nk:(il + 1) * nk, :] = gates[q] * gelu.astype(BF16)

    def stages(act_a, act_b, p_a, p_b):
        pieces = _Pieces([functools.partial(stage_c, p_a, *hv) for hv in halves]
                         + [functools.partial(stage_a, act_a, *hv) for hv in halves])
        n_blocks = ngroups * (ni // quad)
        pieces.every = max(1, PEER_HEADS * n_blocks // len(pieces))
        pieces.phase = 0
        for tg in range(ngroups):
            for iq in range(ni // quad):
                stage_b(act_b, p_b, tg, iq, pieces)
        while pieces:
            pieces.pop(0)()

    @pl.when(k % 2 == 0)
    def _():
        stages(act0_ref, act1_ref, p0_ref, p1_ref)

    @pl.when(k % 2 == 1)
    def _():
        stages(act1_ref, act0_ref, p1_ref, p0_ref)

    @pl.when((tile_c == n_tiles - 1) & (k >= 2))
    def _():
        o_ref[...] = x1_ref[...] + mod_ref[0, 5:6, :] * acc_ref[...].T
        acc_ref[...] = jnp.zeros_like(acc_ref)


def _peer_ffn(h2, u_bf, vt_bf, c1, e1, r2, e2, x1, mod6, seq, tn=512, te=1024):
    t, d = h2.shape
    ne = u_bf.shape[0]
    ngroups = tn // LANES
    n_tiles = ne // te
    n_items = (t // tn) * n_tiles

    def item(k, lag):
        w = jnp.clip(k - lag, 0, n_items - 1)
        return w // n_tiles, w % n_tiles

    stat = pl.BlockSpec((PEER_HEADS, ngroups, PEER_N_KEYS, LANES), lambda k: (0, item(k, 1)[0], 0, 0))
    pstat = pl.BlockSpec((PEER_HEADS, ngroups, PEER_N_KEYS // 2, LANES), lambda k: (0, item(k, 1)[0], 0, 0))
    return pl.pallas_call(
        functools.partial(_peer_kernel, tn=tn, te=te, n_tiles=n_tiles),
        out_shape=jax.ShapeDtypeStruct((t, d), F32),
        grid=(n_items + 2,),
        in_specs=[pl.BlockSpec((tn, d), lambda k: (item(k, 0)[0], 0)),
                  pl.BlockSpec((te, d), lambda k: (item(k, 0)[1], 0)),
                  pl.BlockSpec((d, te), lambda k: (0, item(k, 2)[1])),
                  stat, stat, pstat, pstat,
                  pl.BlockSpec((tn, d), lambda k: (item(k, 2)[0], 0)),
                  pl.BlockSpec((1, 6, d), lambda k: ((item(k, 2)[0] * tn) // seq, 0, 0))],
        out_specs=pl.BlockSpec((tn, d), lambda k: (item(k, 2)[0], 0)),
        scratch_shapes=[pltpu.VMEM((d, tn), F32),
                        pltpu.VMEM((ngroups, te, LANES), F32), pltpu.VMEM((ngroups, te, LANES), F32),
                        pltpu.VMEM((ngroups, te, LANES), BF16), pltpu.VMEM((ngroups, te, LANES), BF16)],
        compiler_params=_params("arbitrary"),
        name="peer_ffn",
    )(h2, u_bf, vt_bf, c1, e1, r2, e2, x1, mod6)


def _head_block_diag(n):
    idx = jnp.arange(n, dtype=jnp.int32) // HEAD_DIM
    return (idx[:, None] == idx[None, :]).astype(BF16)


def _layer(x, c, ada_w, ada_b, norm1_g, w_in, mu_prev, mu_next, q_norm_g, k_norm_g, w_decay0, w_decay_up,
           a_gate0, a_gate_up, g_up, k_k, k_a, r_k, lnx_g, lnx_b, w_out, norm2_g, peer_w_query,
           peer_sub_keys1, peer_sub_keys2, peer_u, peer_v):
    bsz, s, d = x.shape
    n_att_heads = ATT_WIDTH // HEAD_DIM
    bd512 = _head_block_diag(ATT_WIDTH)
    bd128 = _head_block_diag(LANES)

    mod6 = _ada(c, ada_w, ada_b).reshape(bsz, 6, d)
    gq = jnp.tile(q_norm_g, n_att_heads).reshape(1, ATT_WIDTH)
    gk = jnp.tile(k_norm_g, n_att_heads).reshape(1, ATT_WIDTH)
    qkv, zrw = _inproj(x, mod6, norm1_g.reshape(1, d), w_in.astype(BF16), gq, gk, bd512)

    att = _attention(qkv, n_att_heads)

    zero_w = jnp.zeros((2, LANES - DECAY_LORA, RWKV_WIDTH), F32)
    wup_pad = jnp.concatenate([w_decay_up, zero_w], axis=1).astype(BF16)
    aup_pad = jnp.concatenate([zero_w, a_gate_up], axis=1).astype(BF16)
    pk, plw = _rwkv_prep(zrw, mu_prev.reshape(1, -1), mu_next.reshape(1, -1), wup_pad, aup_pad, g_up.astype(BF16),
                         w_decay0, a_gate0, k_k.reshape(1, -1), k_a.reshape(1, -1), r_k.reshape(1, -1), bd512)
    rp, qp, gm, hm = _rwkv_p1(pk, plw)
    rw = _rwkv_p2(rp, qp, gm, hm, pk, lnx_g.reshape(1, -1), lnx_b.reshape(1, -1), bd128)

    x1, h2, s1t, s2t = _outproj(att, rw, x, mod6, norm2_g.reshape(1, d), w_out.astype(BF16),
                                peer_w_query.astype(BF16), peer_sub_keys1.astype(BF16),
                                peer_sub_keys2.astype(BF16))
    c1, e1, r2, e2 = _peer_topk(s1t, s2t)
    out = _peer_ffn(h2.reshape(bsz * s, d), peer_u.astype(BF16), peer_v.T.astype(BF16), c1, e1, r2, e2,
                    x1.reshape(bsz * s, d), mod6, s)
    return out.reshape(bsz, s, d)


def kernel(x, c, ada_w, ada_b, norm1_g, w_in, mu_prev, mu_next, q_norm_g, k_norm_g, w_decay0, w_decay_up,
           a_gate0, a_gate_up, g_up, k_k, k_a, r_k, lnx_g, lnx_b, w_out, norm2_g, peer_w_query,
           peer_sub_keys1, peer_sub_keys2, peer_u, peer_v):
    depth = ada_w.shape[0]
    for l in range(depth):
        x = _layer(x, c, ada_w[l], ada_b[l], norm1_g[l], w_in[l], mu_prev[l], mu_next[l], q_norm_g[l],
                   k_norm_g[l], w_decay0[l], w_decay_up[l], a_gate0[l], a_gate_up[l], g_up[l], k_k[l], k_a[l],
                   r_k[l], lnx_g[l], lnx_b[l], w_out[l], norm2_g[l], peer_w_query[l], peer_sub_keys1[l],
                   peer_sub_keys2[l], peer_u[l], peer_v[l])
    return x
```

```python
import functools
import math

import jax
import jax.numpy as jnp
from jax import lax
from jax.experimental import pallas as pl
from jax.experimental.pallas import tpu as pltpu

F32 = jnp.float32
BF16 = jnp.bfloat16

HEAD_DIM = 64
ATT_WIDTH = 512
RWKV_WIDTH = 512
DECAY_LORA = 64
AAA_LORA = 64
GATE_LORA = 128
SHIFT_WIDTH = 3 * RWKV_WIDTH + DECAY_LORA + AAA_LORA + GATE_LORA
PEER_N_KEYS = 128
PEER_HEADS = 8
PEER_TOPK = 16
NORM_EPS = 1e-6
LNX_EPS = 64e-5
MASK_VALUE = -1e30
LOG2E = 1.4426950408889634
ATT_HALF_WINDOWS = ((1, 64), (4, 256), (16, 1024))

LANES = 128
BF16_ROWS = 16
CHUNK = 64
ATT_BLOCK = 128
VMEM_LIMIT_BYTES = 56 * 1024 * 1024
WKV_PASSES_CHAIN = 1

NN = (((1,), (0,)), ((), ()))
NT = (((1,), (1,)), ((), ()))
TN = (((0,), (0,)), ((), ()))

SLAB_R, SLAB_V, SLAB_KK, SLAB_DIR0, SLAB_G, SLAB_BONUS, NUM_SLABS = 0, 1, 2, 3, 7, 8, 9


def _params(*sem):
    return pltpu.CompilerParams(dimension_semantics=sem, vmem_limit_bytes=VMEM_LIMIT_BYTES)


def _dot(a, b, dims=NN):
    return lax.dot_general(a, b, dims, preferred_element_type=F32)


def _split(a):
    hi = a.astype(BF16)
    lo = (a - hi.astype(F32)).astype(BF16)
    return hi, lo


def _mm(a, b, dims=NN, passes=3):
    if passes == 1:
        return _dot(a.astype(BF16), b.astype(BF16), dims)
    ah, al = _split(a)
    bh, bl = _split(b)
    return _dot(ah, bh, dims) + (_dot(ah, bl, dims) + _dot(al, bh, dims))


def _mm_exact_rhs(a, b_exact):
    ah, al = _split(a)
    return _dot(ah, b_exact) + _dot(al, b_exact)


def _sigmoid(x):
    return 1.0 / (1.0 + jnp.exp(-x))


def _ada_kernel(c_ref, w_ref, b_ref, o_ref):
    c = c_ref[...]
    o_ref[...] = _mm(c * _sigmoid(c), w_ref[...]) + b_ref[...]


def _ada(c, ada_w, ada_b):
    bsz, d = c.shape
    n = ada_w.shape[1]
    tn = 1024
    return pl.pallas_call(
        _ada_kernel,
        out_shape=jax.ShapeDtypeStruct((bsz, n), F32),
        grid=(n // tn,),
        in_specs=[pl.BlockSpec((bsz, d), lambda j: (0, 0)),
                  pl.BlockSpec((d, tn), lambda j: (0, j)),
                  pl.BlockSpec((1, tn), lambda j: (0, j))],
        out_specs=pl.BlockSpec((bsz, tn), lambda j: (0, j)),
        compiler_params=_params("arbitrary"),
        name="ada",
    )(c, ada_w, ada_b.reshape(1, n))


def _inproj_kernel(x_ref, mod_ref, g1_ref, w_ref, gq_ref, gk_ref, bd_ref, qkv_ref, zrw_ref):
    x = x_ref[0]
    mod = mod_ref[0]
    ms = jnp.mean(x * x, axis=-1, keepdims=True)
    h = x * lax.rsqrt(ms + NORM_EPS) * g1_ref[...]
    h = (h * (1.0 + mod[1:2]) + mod[0:1]).astype(BF16)
    bd = bd_ref[...]

    def head_norm(z, g):
        ss = _mm_exact_rhs(z * z, bd) * (1.0 / HEAD_DIM)
        return z * lax.rsqrt(ss + NORM_EPS) * g

    a = ATT_WIDTH
    zq = _dot(h, w_ref[:, 0:a])
    qkv_ref[0, :, 0:a] = (head_norm(zq, gq_ref[...]) * (HEAD_DIM ** -0.5 * LOG2E)).astype(BF16)
    zk = _dot(h, w_ref[:, a:2 * a])
    qkv_ref[0, :, a:2 * a] = head_norm(zk, gk_ref[...]).astype(BF16)
    qkv_ref[0, :, 2 * a:3 * a] = _dot(h, w_ref[:, 2 * a:3 * a]).astype(BF16)
    zrw_ref[0] = _dot(h, w_ref[:, 3 * a:])


def _inproj(x, mod6, norm1_g, w_in_bf, gq, gk, bd512, tm=1024):
    bsz, s, d = x.shape
    nin = w_in_bf.shape[1]
    return pl.pallas_call(
        _inproj_kernel,
        out_shape=(jax.ShapeDtypeStruct((bsz, s, 3 * ATT_WIDTH), BF16),
                   jax.ShapeDtypeStruct((bsz, s, SHIFT_WIDTH), F32)),
        grid=(bsz, s // tm),
        in_specs=[pl.BlockSpec((1, tm, d), lambda b, i: (b, i, 0)),
                  pl.BlockSpec((1, 6, d), lambda b, i: (b, 0, 0)),
                  pl.BlockSpec((1, d), lambda b, i: (0, 0)),
                  pl.BlockSpec((d, nin), lambda b, i: (0, 0)),
                  pl.BlockSpec((1, ATT_WIDTH), lambda b, i: (0, 0)),
                  pl.BlockSpec((1, ATT_WIDTH), lambda b, i: (0, 0)),
                  pl.BlockSpec((ATT_WIDTH, ATT_WIDTH), lambda b, i: (0, 0))],
        out_specs=(pl.BlockSpec((1, tm, 3 * ATT_WIDTH), lambda b, i: (b, i, 0)),
                   pl.BlockSpec((1, tm, SHIFT_WIDTH), lambda b, i: (b, i, 0))),
        compiler_params=_params("arbitrary", "arbitrary"),
        name="inproj",
    )(x, mod6, norm1_g, w_in_bf, gq, gk, bd512)


ATT_FAR_DILATION, ATT_FAR_HALF = ATT_HALF_WINDOWS[-1]
ATT_NEAR_BLOCKS = 6


def _attn_near_window(nkb):
    return min(nkb, ATT_NEAR_BLOCKS)


def _attn_near_start(qb, nkb):
    return jnp.clip(((qb - 2) // 2) * 2, 0, nkb - _attn_near_window(nkb))


def _attn_far_bias(n_heads, nkb):
    per_block = ATT_BLOCK // ATT_FAR_DILATION
    i = jnp.arange(ATT_BLOCK, dtype=jnp.int32)[:, None]
    ip = jnp.arange(ATT_BLOCK, dtype=jnp.int32)[None, :]
    start = _attn_near_start(i // per_block, nkb)
    kb = ip // per_block
    dist = ATT_FAR_DILATION * jnp.abs(i - ip)
    far = (dist <= ATT_FAR_HALF) & ~((kb >= start) & (kb < start + _attn_near_window(nkb)))
    slopes = jnp.exp2(-8.0 * (jnp.arange(n_heads, dtype=F32) + 1.0) / n_heads) * LOG2E
    return jnp.where(far[None], -slopes[:, None, None] * dist.astype(F32)[None], MASK_VALUE)


def _attn_bias_table(n_heads, nkb):
    reach = _attn_near_window(nkb) - 1
    nd = 2 * reach + 1
    dd = jnp.arange(nd, dtype=jnp.int32)[None, :, None]
    r = jnp.arange(ATT_BLOCK, dtype=jnp.int32)[:, None, None]
    c = jnp.arange(ATT_BLOCK, dtype=jnp.int32)[None, None, :]
    dt = r - c - (dd - reach) * ATT_BLOCK
    adt = jnp.abs(dt)
    mult = jnp.zeros(dt.shape, F32)
    for dil, half in ATT_HALF_WINDOWS:
        mult = mult + ((adt <= half) & (dt % dil == 0)).astype(F32)
    logm = jnp.where(mult > 0, jnp.log2(jnp.maximum(mult, 1.0)), MASK_VALUE)
    slopes = jnp.exp2(-8.0 * (jnp.arange(n_heads, dtype=F32) + 1.0) / n_heads) * LOG2E
    bias = logm[None] - slopes[:, None, None, None] * adt.astype(F32)[None]
    return bias.reshape(n_heads, ATT_BLOCK, nd * ATT_BLOCK)


def _attn_kernel(q_ref, k_ref, v_ref, bias_ref, far_ref, o_ref, s_ref, v1_ref, g_ref, accf_ref, mf_ref,
                 *, nkb, win):
    blk = ATT_BLOCK
    reach = win - 1
    has_far = win < nkb
    lane = lax.broadcasted_iota(jnp.int32, (blk, LANES), 1)
    first = lane < HEAD_DIM

    def two_heads(q):
        zero = jnp.zeros_like(q)
        return jnp.concatenate([jnp.where(first, q, zero), jnp.where(first, zero, q)], axis=0)

    def with_ones(v, hh):
        is_own = (lax.broadcasted_iota(jnp.int32, v.shape, 1) < HEAD_DIM) == (hh == 0)
        return jnp.where(is_own, v, jnp.ones_like(v))

    v = v_ref[0]
    for hh in range(2):
        v1_ref[hh] = with_ones(v, hh)

    if has_far:
        g_ref[0] = q_ref[0].astype(F32)
        g_ref[1] = k_ref[0].astype(F32)
        g_ref[2] = v.astype(F32)
        n_far = nkb * blk // ATT_FAR_DILATION

        def residue(r, _):
            rows = pl.ds(r, n_far, stride=ATT_FAR_DILATION)
            qr = g_ref[0, rows, :].astype(BF16)
            kr = g_ref[1, rows, :].astype(BF16)
            vr = g_ref[2, rows, :].astype(BF16)
            s = _dot(two_heads(qr), kr, NT)
            for hh in range(2):
                sb = s[hh * blk:(hh + 1) * blk] + far_ref[hh]
                mx = jnp.max(sb, axis=-1, keepdims=True)
                p = jnp.exp2(sb - mx).astype(BF16)
                accf_ref[hh, rows, :] = _dot(p, with_ones(vr, hh))
                mf_ref[hh, rows, :] = jnp.broadcast_to(mx, (blk, LANES))
            return 0

        lax.fori_loop(0, ATT_FAR_DILATION, residue, 0, unroll=8)

    group = s_ref.shape[0]
    slabs = range(0, win, 2)

    def query_block_group(i, _):
        us = range(group)
        qis = [group * i + u for u in us]
        rows = [pl.ds(pl.multiple_of(qi * blk, blk), blk) for qi in qis]
        qab = [two_heads(q_ref[0, rows[u], :]) for u in us]
        ks = [_attn_near_start(qi, nkb) if has_far else 0 for qi in qis]

        m = [[jnp.full((blk, LANES), -jnp.inf, F32) for _ in range(2)] for _ in us]
        for w0 in slabs:
            for u in us:
                kj0 = ks[u] + w0
                kslab = k_ref[0, pl.ds(pl.multiple_of(kj0 * blk, blk), 2 * blk), :]
                s = _dot(qab[u], kslab, NT)
                x0 = pl.multiple_of((kj0 - qis[u] + reach) * blk, blk)
                for hh in range(2):
                    sb = s[hh * blk:(hh + 1) * blk] + bias_ref[hh, :, pl.ds(x0, 2 * blk)]
                    s_ref[u, hh, :, w0 * blk:(w0 + 2) * blk] = sb
                    m[u][hh] = jnp.maximum(m[u][hh], jnp.maximum(sb[:, :blk], sb[:, blk:]))
        mrow = [[jnp.max(m[u][hh], axis=-1, keepdims=True) for hh in range(2)] for u in us]

        if has_far:
            acc = [[None, None] for _ in us]
            for u in us:
                for hh in range(2):
                    m_far = mf_ref[hh, rows[u], :][:, 0:1]
                    m_all = jnp.maximum(mrow[u][hh], m_far)
                    acc[u][hh] = accf_ref[hh, rows[u], :] * jnp.exp2(m_far - m_all)
                    mrow[u][hh] = m_all
        else:
            acc = [[jnp.zeros((blk, LANES), F32) for _ in range(2)] for _ in us]
        for w0 in slabs:
            for u in us:
                keys = pl.ds(pl.multiple_of((ks[u] + w0) * blk, blk), 2 * blk)
                for hh in range(2):
                    p = jnp.exp2(s_ref[u, hh, :, w0 * blk:(w0 + 2) * blk] - mrow[u][hh]).astype(BF16)
                    acc[u][hh] = acc[u][hh] + _dot(p, v1_ref[hh, keys, :])
        for u in us:
            num = jnp.where(first, acc[u][0], acc[u][1])
            den = jnp.where(first, pltpu.roll(acc[u][0], HEAD_DIM, axis=1), pltpu.roll(acc[u][1], HEAD_DIM, axis=1))
            o_ref[0, rows[u], :] = (num / den).astype(o_ref.dtype)
        return 0

    lax.fori_loop(0, nkb // group, query_block_group, 0)


def _attention(qkv, n_heads):
    bsz, s, _ = qkv.shape
    nkb = s // ATT_BLOCK
    win = _attn_near_window(nkb)
    assert nkb % 2 == 0 and (win == nkb or s // ATT_FAR_DILATION == ATT_BLOCK)
    bias = _attn_bias_table(n_heads, nkb)
    far = _attn_far_bias(n_heads, nkb)
    npair = ATT_WIDTH // LANES
    return pl.pallas_call(
        functools.partial(_attn_kernel, nkb=nkb, win=win),
        out_shape=jax.ShapeDtypeStruct((bsz, s, ATT_WIDTH), BF16),
        grid=(npair, bsz),
        in_specs=[pl.BlockSpec((1, s, LANES), lambda hp, b: (b, 0, hp)),
                  pl.BlockSpec((1, s, LANES), lambda hp, b: (b, 0, npair + hp)),
                  pl.BlockSpec((1, s, LANES), lambda hp, b: (b, 0, 2 * npair + hp)),
                  pl.BlockSpec((2, ATT_BLOCK, bias.shape[2]), lambda hp, b: (hp, 0, 0)),
                  pl.BlockSpec((2, ATT_BLOCK, ATT_BLOCK), lambda hp, b: (hp, 0, 0))],
        out_specs=pl.BlockSpec((1, s, LANES), lambda hp, b: (b, 0, hp)),
        scratch_shapes=[pltpu.VMEM((math.gcd(nkb, 4), 2, ATT_BLOCK, win * ATT_BLOCK), F32), pltpu.VMEM((2, s, LANES), BF16),
                        pltpu.VMEM((3, s, LANES), F32), pltpu.VMEM((2, s, LANES), F32),
                        pltpu.VMEM((2, s, LANES), F32)],
        compiler_params=_params("arbitrary", "arbitrary"),
        name="attn",
    )(qkv, qkv, qkv, bias, far)


def _prep_kernel(z_ref, zp_ref, zn_ref, mup_ref, mun_ref, wup_ref, aup_ref, gup_ref, w0_ref, a0_ref,
                 kk_ref, ka_ref, rk_ref, bd_ref, o_ref, lw_ref, *, tq):
    i = pl.program_id(1)
    last = pl.num_programs(1) - 1
    z = z_ref[0]
    row = lax.broadcasted_iota(jnp.int32, (tq, 1), 0)
    prev_row = zp_ref[0, 7:8, :] * (i > 0).astype(F32)
    next_row = zn_ref[0, 0:1, :] * (i < last).astype(F32)
    zp = jnp.where(row == 0, prev_row, pltpu.roll(z, 1, axis=0))
    zn = jnp.where(row == tq - 1, next_row, pltpu.roll(z, tq - 1, axis=0))
    zs = z + mup_ref[...] * (zp - z) + mun_ref[...] * (zn - z)

    w = RWKV_WIDTH
    r = zs[:, 0:w]
    k = zs[:, w:2 * w]
    v = zs[:, 2 * w:3 * w]
    xwa = zs[:, 3 * w:3 * w + LANES]
    xg = zs[:, 3 * w + LANES:]
    bd = bd_ref[...]

    g = _dot(_sigmoid(xg).astype(BF16), gup_ref[...])
    kk = k * kk_ref[...]
    ss = _mm_exact_rhs(kk * kk, bd)
    kk = kk * lax.rsqrt(jnp.maximum(ss, 1e-12))
    ka = ka_ref[...]
    txw = jnp.tanh(xwa).astype(BF16)
    xab = xwa.astype(BF16)

    def put(slab, val):
        o_ref[0, :, slab * w:(slab + 1) * w] = val.astype(o_ref.dtype)

    put(SLAB_R, r)
    put(SLAB_V, v)
    put(SLAB_KK, kk)
    put(SLAB_G, g)
    a_sum = jnp.zeros_like(k)
    for d in range(2):
        y = w0_ref[d:d + 1, :] + _dot(txw, wup_ref[d])
        wlog = -(jnp.maximum(-y, 0.0) + jnp.log(1.0 + jnp.exp(-jnp.abs(y)))) - 0.5
        a = _sigmoid(a0_ref[d:d + 1, :] + _dot(xab, aup_ref[d]))
        a_sum = a_sum + a
        lw_ref[0, :, d * w:(d + 1) * w] = -jnp.exp(wlog)
        put(SLAB_DIR0 + 2 * d, k * (1.0 + (a - 1.0) * ka))
        put(SLAB_DIR0 + 2 * d + 1, kk * a)
    k_bonus = k * (1.0 + (0.5 * a_sum - 1.0) * ka)
    bsum = _mm_exact_rhs(r * k_bonus * rk_ref[...], bd)
    put(SLAB_BONUS, bsum * v)


def _rwkv_prep(zrw, mu_prev, mu_next, wup_pad, aup_pad, g_up_bf, w0, a0, k_k, k_a, r_k, bd512, tq=512):
    bsz, s, sw = zrw.shape
    w = RWKV_WIDTH
    nt = s // tq
    row = lambda b, i: (0, 0)
    return pl.pallas_call(
        functools.partial(_prep_kernel, tq=tq),
        out_shape=(jax.ShapeDtypeStruct((bsz, s, NUM_SLABS * w), BF16),
                   jax.ShapeDtypeStruct((bsz, s, 2 * w), F32)),
        grid=(bsz, nt),
        in_specs=[pl.BlockSpec((1, tq, sw), lambda b, i: (b, i, 0)),
                  pl.BlockSpec((1, 8, sw), lambda b, i: (b, jnp.maximum(i * (tq // 8) - 1, 0), 0)),
                  pl.BlockSpec((1, 8, sw), lambda b, i: (b, jnp.minimum((i + 1) * (tq // 8), s // 8 - 1), 0)),
                  pl.BlockSpec((1, sw), row),
                  pl.BlockSpec((1, sw), row),
                  pl.BlockSpec((2, LANES, w), lambda b, i: (0, 0, 0)),
                  pl.BlockSpec((2, LANES, w), lambda b, i: (0, 0, 0)),
                  pl.BlockSpec((GATE_LORA, w), row),
                  pl.BlockSpec((2, w), row),
                  pl.BlockSpec((2, w), row),
                  pl.BlockSpec((1, w), row),
                  pl.BlockSpec((1, w), row),
                  pl.BlockSpec((1, w), row),
                  pl.BlockSpec((w, w), row)],
        out_specs=(pl.BlockSpec((1, tq, NUM_SLABS * w), lambda b, i: (b, i, 0)),
                   pl.BlockSpec((1, tq, 2 * w), lambda b, i: (b, i, 0))),
        compiler_params=_params("arbitrary", "arbitrary"),
        name="rwkv_prep",
    )(zrw, zrw, zrw, mu_prev, mu_next, wup_pad, aup_pad, g_up_bf, w0, a0, k_k, k_a, r_k, bd512)


def _block_diag(x, first):
    zero = jnp.zeros_like(x)
    return jnp.concatenate([jnp.where(first, x, zero), jnp.where(first, zero, x)], axis=0)


def _p1_kernel(r_ref, v_ref, kk_ref, lw_ref, kd_ref, be_ref, rp_ref, qp_ref, g_ref, h_ref, *, cpb):
    c = CHUNK
    sign = 1 - 2 * pl.program_id(2)
    rowi = lax.broadcasted_iota(jnp.int32, (c, LANES), 0)
    lane = lax.broadcasted_iota(jnp.int32, (c, LANES), 1)
    coli = lane & (c - 1)
    first = lane < HEAD_DIM
    before = sign * (rowi - coli)
    strict2 = before > 0
    incl2 = before >= 0
    eye2 = (coli == rowi).astype(F32)
    r64 = lax.broadcasted_iota(jnp.int32, (c, c), 0)
    c64 = lax.broadcasted_iota(jnp.int32, (c, c), 1)
    cum = (sign * (r64 - c64) >= 0).astype(BF16)

    js = range(cpb)
    sls = [pl.ds(j * c, c) for j in js]
    bdg = lambda x: _block_diag(x, first)
    r = [r_ref[0, sl, :].astype(F32) for sl in sls]
    v = [v_ref[0, sl, :].astype(F32) for sl in sls]
    kk = [kk_ref[0, sl, :].astype(F32) for sl in sls]
    lw = [lw_ref[0, sl, :] for sl in sls]
    kd = [kd_ref[0, sl, :].astype(F32) for sl in sls]
    be = [be_ref[0, sl, :].astype(F32) for sl in sls]

    one = lambda x, y, dims=NN: _mm(x, y, dims, 1)
    rows2 = lambda x, y: jnp.concatenate([x, y], axis=0)
    cols2 = lambda x, y: jnp.concatenate([x, y], axis=1)

    def cumsum(x):
        l1 = x.astype(BF16)
        rem = x - l1.astype(F32)
        l2 = rem.astype(BF16)
        l3 = (rem - l2.astype(F32)).astype(BF16)
        y = _dot(cum, jnp.concatenate([l1, l2, l3], axis=1))
        return y[:, :LANES] + (y[:, LANES:2 * LANES] + y[:, 2 * LANES:])

    cs = [cumsum(lw[j]) for j in js]
    g_inv = [jnp.exp(-cs[j]) for j in js]
    g_tot = [jnp.exp(jnp.sum(lw[j], axis=0, keepdims=True)) for j in js]
    ab = [-kk[j] * jnp.exp(cs[j] - lw[j]) for j in js]
    rb = [r[j] * jnp.exp(cs[j]) for j in js]
    bt = [be[j] * g_inv[j] for j in js]
    kt = [kd[j] * g_inv[j] for j in js]

    sc = [one(rows2(ab[j], rb[j]), rows2(bdg(bt[j]), bdg(kt[j])), NT) for j in js]
    m_ab = [jnp.where(strict2, sc[j][:c, :LANES], 0.0) for j in js]
    n_rb = [jnp.where(incl2, sc[j][c:, :LANES], 0.0) for j in js]
    m_ak = [jnp.where(strict2, sc[j][:c, LANES:], 0.0) for j in js]
    n_rk = [jnp.where(incl2, sc[j][c:, LANES:], 0.0) for j in js]

    levels = int(math.log2(c)) - 1
    t = [eye2 + m_ab[j] for j in js]
    p = [one(m_ab[j], bdg(m_ab[j])) for j in js]
    for _ in range(1, levels - 1):
        y = [one(rows2(p[j], t[j]), bdg(p[j])) for j in js]
        p = [y[j][:c] for j in js]
        t = [t[j] + y[j][c:] for j in js]
    t = [t[j] + one(t[j], bdg(p[j])) for j in js]

    def residual(a, tt):
        ah, al = _split(a)
        th, tl = _split(bdg(tt))
        y = _dot(rows2(ah, al), th)
        return eye2 - (y[:c] + (y[c:] + _dot(ah, tl)))

    res = [residual(eye2 - m_ab[j], t[j]) for j in js]
    t = [t[j] + one(t[j], bdg(res[j])) for j in js]

    vbd = [bdg(v[j]) for j in js]
    mv = [one(rows2(m_ak[j], n_rk[j]), vbd[j]) for j in js]
    tp = [one(t[j], cols2(bdg(ab[j]), bdg(mv[j][:c]))) for j in js]
    ap = [tp[j][:, :LANES] for j in js]
    pp = [tp[j][:, LANES:] for j in js]
    nr = [one(n_rb[j], cols2(bdg(ap[j]), bdg(pp[j]))) for j in js]
    for j in js:
        rp_ref[0, 0, 0, sls[j], :] = rb[j] + nr[j][:, :LANES]
        qp_ref[0, 0, 0, sls[j], :] = nr[j][:, LANES:] + mv[j][c:]
    heads = lambda x: jnp.where(first, x[:HEAD_DIM], x[HEAD_DIM:])
    for j in js:
        st = one(cols2(ap[j], pp[j]), bt[j], TN)
        g_ref[0, 0, 0, j] = (eye2 + heads(st[:LANES])) * g_tot[j]
        h_ref[0, 0, 0, j] = heads(st[LANES:] + one(v[j], kt[j], TN)) * g_tot[j]


def _rwkv_p1(pk, plw, cpb=32):
    bsz, s, _ = pk.shape
    nc = s // CHUNK
    npair = RWKV_WIDTH // LANES
    rows = cpb * CHUNK

    def slab(sidx):
        return pl.BlockSpec((1, rows, LANES), lambda b, hp, d, ci: (b, ci, sidx * npair + hp))

    def dslab(off):
        return pl.BlockSpec((1, rows, LANES),
                            lambda b, hp, d, ci: (b, ci, (SLAB_DIR0 + 2 * d + off) * npair + hp))

    lw_spec = pl.BlockSpec((1, rows, LANES), lambda b, hp, d, ci: (b, ci, d * npair + hp))

    seq = pl.BlockSpec((1, 1, 1, rows, LANES), lambda b, hp, d, ci: (b, hp, d, ci, 0))
    mat = pl.BlockSpec((1, 1, 1, cpb, HEAD_DIM, LANES), lambda b, hp, d, ci: (b, hp, d, ci, 0, 0))
    return pl.pallas_call(
        functools.partial(_p1_kernel, cpb=cpb),
        out_shape=(jax.ShapeDtypeStruct((bsz, npair, 2, s, LANES), F32),
                   jax.ShapeDtypeStruct((bsz, npair, 2, s, LANES), F32),
                   jax.ShapeDtypeStruct((bsz, npair, 2, nc, HEAD_DIM, LANES), F32),
                   jax.ShapeDtypeStruct((bsz, npair, 2, nc, HEAD_DIM, LANES), F32)),
        grid=(bsz, npair, 2, nc // cpb),
        in_specs=[slab(SLAB_R), slab(SLAB_V), slab(SLAB_KK), lw_spec, dslab(0), dslab(1)],
        out_specs=(seq, seq, mat, mat),
        compiler_params=_params("arbitrary", "arbitrary", "arbitrary", "arbitrary"),
        name="rwkv_p1",
    )(pk, pk, pk, plw, pk, pk)


def _p2_kernel(rp_ref, qp_ref, g_ref, h_ref, gate_ref, bonus_ref, lng_ref, lnb_ref, bd_ref, o_ref,
               y_ref, s_ref, *, nc, te, hpb):
    c = CHUNK
    s_ref[...] = jnp.zeros_like(s_ref)
    first = lax.broadcasted_iota(jnp.int32, (HEAD_DIM, LANES), 1) < HEAD_DIM
    chains = [(hq, d) for hq in range(hpb) for d in range(2)]

    def step(j, _):
        for hq, d in chains:
            jc = j if d == 0 else nc - 1 - j
            sl = pl.ds(pl.multiple_of(jc * c, c), c)
            st = s_ref[hq, d]
            y_ref[hq, d, sl, :] = (_mm(rp_ref[0, hq, d, sl, :], st, NT, WKV_PASSES_CHAIN)
                                   + qp_ref[0, hq, d, sl, :])
            gmat = _block_diag(g_ref[0, hq, d, jc], first)
            hmat = _block_diag(h_ref[0, hq, d, jc], first)
            s_ref[hq, d] = _mm(st, gmat, NN, WKV_PASSES_CHAIN) + hmat
        return 0

    lax.fori_loop(0, nc, step, 0)

    bd = bd_ref[...]
    inv = 1.0 / HEAD_DIM

    def epi(i, _):
        sl = pl.ds(pl.multiple_of(i * te, te), te)
        for hq in range(hpb):
            lanes = slice(hq * LANES, (hq + 1) * LANES)
            y = y_ref[hq, 0, sl, :] + y_ref[hq, 1, sl, :]
            mean = _mm_exact_rhs(y, bd) * inv
            yc = y - mean
            var = _mm_exact_rhs(yc * yc, bd) * inv
            yn = yc * lax.rsqrt(var + LNX_EPS) * lng_ref[:, lanes] + lnb_ref[:, lanes]
            o_ref[0, sl, lanes] = ((yn + bonus_ref[0, sl, lanes].astype(F32))
                                   * gate_ref[0, sl, lanes].astype(F32)).astype(o_ref.dtype)
        return 0

    lax.fori_loop(0, (nc * c) // te, epi, 0)


def _rwkv_p2(rp, qp, gm, hm, pk, lnx_g, lnx_b, bd128, te=1024, hpb=2):
    bsz, npair, _, s, _ = rp.shape
    nc = s // CHUNK
    width = hpb * LANES
    nq = npair // hpb
    seq = pl.BlockSpec((1, hpb, 2, s, LANES), lambda b, q: (b, q, 0, 0, 0))
    mat = pl.BlockSpec((1, hpb, 2, nc, HEAD_DIM, LANES), lambda b, q: (b, q, 0, 0, 0, 0))
    return pl.pallas_call(
        functools.partial(_p2_kernel, nc=nc, te=min(te, s), hpb=hpb),
        out_shape=jax.ShapeDtypeStruct((bsz, s, RWKV_WIDTH), BF16),
        grid=(bsz, nq),
        in_specs=[seq, seq, mat, mat,
                  pl.BlockSpec((1, s, width), lambda b, q: (b, 0, SLAB_G * nq + q)),
                  pl.BlockSpec((1, s, width), lambda b, q: (b, 0, SLAB_BONUS * nq + q)),
                  pl.BlockSpec((1, width), lambda b, q: (0, q)),
                  pl.BlockSpec((1, width), lambda b, q: (0, q)),
                  pl.BlockSpec((LANES, LANES), lambda b, q: (0, 0))],
        out_specs=pl.BlockSpec((1, s, width), lambda b, q: (b, 0, q)),
        scratch_shapes=[pltpu.VMEM((hpb, 2, s, LANES), F32), pltpu.VMEM((hpb, 2, LANES, LANES), F32)],
        compiler_params=_params("arbitrary", "arbitrary"),
        name="rwkv_p2",
    )(rp, qp, gm, hm, pk, pk, lnx_g, lnx_b, bd128)


def _outproj_kernel(att_ref, rw_ref, x_ref, mod_ref, g2_ref, wo_ref, wq_ref, k1_ref, k2_ref,
                    x1_ref, h2_ref, s1_ref, s2_ref):
    mod = mod_ref[0]
    a = ATT_WIDTH
    mixed = _dot(att_ref[0], wo_ref[0:a, :]) + _dot(rw_ref[0], wo_ref[a:, :])
    x1 = x_ref[0] + mod[2:3] * mixed
    x1_ref[0] = x1
    ms = jnp.mean(x1 * x1, axis=-1, keepdims=True)
    h2 = x1 * lax.rsqrt(ms + NORM_EPS) * g2_ref[...]
    h2 = (h2 * (1.0 + mod[4:5]) + mod[3:4]).astype(BF16)
    h2_ref[0] = h2
    q = _dot(h2, wq_ref[...])
    k1 = k1_ref[...]
    k2 = k2_ref[...]
    for h in range(PEER_HEADS):
        base = h * 2 * LANES
        s1 = _dot(k1, q[:, base:base + LANES].astype(BF16), NT)
        s2 = _dot(k2, q[:, base + LANES:base + 2 * LANES].astype(BF16), NT)
        for g in range(s1.shape[1] // LANES):
            s1_ref[h, g] = s1[:, g * LANES:(g + 1) * LANES]
            s2_ref[h, g] = s2[:, g * LANES:(g + 1) * LANES]


def _outproj(att, rw, x, mod6, norm2_g, w_out_bf, wq_bf, k1_bf, k2_bf, tm=512):
    bsz, s, d = x.shape
    nt = s // tm
    t = bsz * s
    nq = wq_bf.shape[1]
    const = lambda b, i: (0, 0)
    gpt = tm // LANES
    tok = pl.BlockSpec((PEER_HEADS, gpt, PEER_N_KEYS, LANES), lambda b, i: (0, b * nt + i, 0, 0))
    stat = jax.ShapeDtypeStruct((PEER_HEADS, t // LANES, PEER_N_KEYS, LANES), F32)
    return pl.pallas_call(
        _outproj_kernel,
        out_shape=(jax.ShapeDtypeStruct((bsz, s, d), F32),
                   jax.ShapeDtypeStruct((bsz, s, d), BF16),
                   stat, stat),
        grid=(bsz, nt),
        in_specs=[pl.BlockSpec((1, tm, ATT_WIDTH), lambda b, i: (b, i, 0)),
                  pl.BlockSpec((1, tm, RWKV_WIDTH), lambda b, i: (b, i, 0)),
                  pl.BlockSpec((1, tm, d), lambda b, i: (b, i, 0)),
                  pl.BlockSpec((1, 6, d), lambda b, i: (b, 0, 0)),
                  pl.BlockSpec((1, d), const),
                  pl.BlockSpec((d, d), const),
                  pl.BlockSpec((d, nq), const),
                  pl.BlockSpec((PEER_N_KEYS, LANES), const),
                  pl.BlockSpec((PEER_N_KEYS, LANES), const)],
        out_specs=(pl.BlockSpec((1, tm, d), lambda b, i: (b, i, 0)),
                   pl.BlockSpec((1, tm, d), lambda b, i: (b, i, 0)),
                   tok, tok),
        compiler_params=_params("arbitrary", "arbitrary"),
        name="outproj",
    )(att, rw, x, mod6, norm2_g, w_out_bf, wq_bf, k1_bf, k2_bf)


def _oddeven_sort_pairs(n):
    pairs = []
    p = 1
    while p < n:
        k = p
        while k >= 1:
            for j in range(k % p, n - k, 2 * k):
                for i in range(min(k, n - j - k)):
                    if (i + j) // (2 * p) == (i + j + k) // (2 * p):
                        pairs.append((i + j, i + j + k))
            k //= 2
        p *= 2
    return pairs


def _bitonic_merge_pairs(n):
    pairs = []
    stride = n // 2
    while stride >= 1:
        pairs += [(i, i + stride) for i in range(n) if not i & stride]
        stride //= 2
    return pairs


def _compare_exchange(vals, pairs):
    vals = list(vals)
    for i, j in pairs:
        a, b = vals[i], vals[j]
        if b is None:
            continue
        if a is None:
            vals[i], vals[j] = b, None
        else:
            vals[i], vals[j] = jnp.maximum(a, b), jnp.minimum(a, b)
    return vals


def _top16_sorted(vregs):
    kk = PEER_TOPK
    vals = list(vregs) + [None] * (kk - len(vregs))
    vals = _compare_exchange(vals, _oddeven_sort_pairs(kk))
    for shift in (4, 2, 1):
        other = [None if v is None else pltpu.roll(v, shift, axis=0) for v in vals]
        merged = []
        for k in range(kk):
            a, b = vals[k], other[kk - 1 - k]
            merged.append(b if a is None else a if b is None else jnp.maximum(a, b))
        vals = _compare_exchange(merged, _bitonic_merge_pairs(kk))
    return vals


def _topk_kernel(s1_ref, s2_ref, c1_ref, e1_ref, r2_ref, e2_ref, *, groups):
    kk = PEER_TOPK
    sub = 8
    neg = -jnp.inf
    rows8 = lax.broadcasted_iota(jnp.int32, (sub, LANES), 0)

    def rows_of(reps):
        out = reps[0]
        for r in range(1, sub):
            out = jnp.where(rows8 == r, reps[r], out)
        return out

    def all_sublanes_sum(x):
        for shift in (4, 2, 1):
            x = x + pltpu.roll(x, shift, axis=0)
        return x

    def per_tile(it_idx, _):
        g = it_idx // PEER_HEADS
        h = it_idx % PEER_HEADS
        s1 = [s1_ref[h, g, pl.ds(v * sub, sub), :] for v in range(PEER_N_KEYS // sub)]
        s2 = [s2_ref[h, g, pl.ds(v * sub, sub), :] for v in range(PEER_N_KEYS // sub)]
        a = _top16_sorted(s1)
        b = _top16_sorted(s2)
        a_lo, a_hi = rows_of(a[:sub]), rows_of(a[sub:])
        b_lo, b_hi = rows_of(b[:sub]), rows_of(b[sub:])

        cand = [a[0] + b_lo, a[0] + b_hi, a[1] + b_lo]
        for i in range(2, sub):
            cand.append(jnp.where(rows8 < kk // (i + 1), a[i] + b_lo, neg))
        cand.append(a_hi + b[0])
        top = _top16_sorted(cand)
        tau = top[kk - 1]
        z = None
        for t in top:
            ez = jnp.exp(t - top[0])
            z = ez if z is None else z + ez
        inv_z = 1.0 / z

        counts = []
        for r in range(kk):
            hits = (jnp.where(a[r] + b_lo >= tau, 1.0, 0.0) + jnp.where(a[r] + b_hi >= tau, 1.0, 0.0))
            counts.append(all_sublanes_sum(hits))
        for v in range(PEER_N_KEYS // sub):
            rows = pl.ds(v * sub, sub)
            cnt = jnp.zeros((sub, LANES), F32)
            rank = jnp.full((sub, LANES), float(kk), F32)
            for r in range(kk):
                cnt = jnp.where(s1[v] == a[r], counts[r], cnt)
                rank = jnp.where(s2[v] == b[r], float(r), rank)
            c1_ref[h, g, rows, :] = cnt
            e1_ref[h, g, rows, :] = jnp.exp(s1[v] - a[0]) * inv_z
            s2[v] = (rank, jnp.exp(s2[v] - b[0]))
        rank2 = jnp.concatenate([rv for rv, _ in s2], axis=0).astype(BF16)
        e2 = jnp.concatenate([ev for _, ev in s2], axis=0).astype(BF16)
        r2_ref[h, g] = pltpu.bitcast(rank2, jnp.uint32)
        e2_ref[h, g] = pltpu.bitcast(e2, jnp.uint32)
        return 0

    lax.fori_loop(0, groups * PEER_HEADS, per_tile, 0)


def _peer_topk(s1t, s2t, tn=512):
    nh, ng, nk, _ = s1t.shape
    groups = tn // LANES
    blk = pl.BlockSpec((nh, groups, nk, LANES), lambda i: (0, i, 0, 0))
    f32 = jax.ShapeDtypeStruct(s1t.shape, F32)
    b16 = jax.ShapeDtypeStruct((nh, ng, nk // 2, LANES), jnp.uint32)
    pblk = pl.BlockSpec((nh, groups, nk // 2, LANES), lambda i: (0, i, 0, 0))
    return pl.pallas_call(
        functools.partial(_topk_kernel, groups=groups),
        out_shape=(f32, f32, b16, b16),
        grid=(ng // groups,),
        in_specs=[blk, blk],
        out_specs=(blk, blk, pblk, pblk),
        compiler_params=_params("arbitrary"),
        name="peer_topk",
    )(s1t, s2t)


class _Pieces(list):
    every = 1
    phase = 0


def _peer_kernel(h2_ref, u_ref, vt_ref, c1_ref, e1_ref, r2_ref, e2_ref, x1_ref, mod_ref, o_ref,
                 acc_ref, act0_ref, act1_ref, p0_ref, p1_ref, *, tn, te, n_tiles):
    k = pl.program_id(0)
    n_items = pl.num_programs(0) - 2
    nk = PEER_N_KEYS
    ngroups = tn // LANES

    @pl.when(k == 0)
    def _():
        acc_ref[...] = jnp.zeros_like(acc_ref)
        for ref in (act0_ref, act1_ref, p0_ref, p1_ref):
            ref[...] = jnp.zeros_like(ref)

    ni = te // nk
    i0 = pl.multiple_of((jnp.clip(k - 1, 0, n_items - 1) % n_tiles) * ni, ni)
    tile_c = jnp.clip(k - 2, 0, n_items - 1) % n_tiles

    d_model = acc_ref.shape[0]
    ksplit = 4
    msplit = 2
    halves = [(mh, nh, kh) for mh in range(msplit) for nh in range(ngroups // 2) for kh in range(ksplit)]

    def stage_c(p_a, mh, nh, kh):
        rows = slice(mh * (d_model // msplit), (mh + 1) * (d_model // msplit))
        cols = slice(nh * 2 * LANES, (nh + 1) * 2 * LANES)
        ks = slice(kh * (te // ksplit), (kh + 1) * (te // ksplit))
        p_prev = jnp.concatenate([p_a[2 * nh, ks, :], p_a[2 * nh + 1, ks, :]], axis=1)
        acc_ref[rows, cols] += _dot(vt_ref[ks, rows], p_prev, TN)

    def stage_a(act_a, mh, nh, kh):
        rows = slice(mh * (te // msplit), (mh + 1) * (te // msplit))
        ks = slice(kh * (d_model // ksplit), (kh + 1) * (d_model // ksplit))
        act = _dot(u_ref[rows, ks], h2_ref[nh * 2 * LANES:(nh + 1) * 2 * LANES, ks], NT)
        if kh == 0:
            act_a[2 * nh, rows, :] = act[:, :LANES]
            act_a[2 * nh + 1, rows, :] = act[:, LANES:]
        else:
            act_a[2 * nh, rows, :] += act[:, :LANES]
            act_a[2 * nh + 1, rows, :] += act[:, LANES:]

    def rows_bf16(row):
        tile = jnp.broadcast_to(row, (BF16_ROWS, LANES)).astype(BF16)
        return jnp.concatenate([tile] * (nk // BF16_ROWS), axis=0)

    quad = 4

    def stage_b(act_b, p_b, tg, iq, pieces):
        gates = [jnp.zeros((nk, LANES), BF16) for _ in range(quad)]
        for h in range(PEER_HEADS):
            if h % pieces.every == pieces.phase and pieces:
                pieces.pop(0)()
            c1 = c1_ref[h, tg, pl.ds(i0, ni), :]
            e1 = e1_ref[h, tg, pl.ds(i0, ni), :]
            rank2 = pltpu.bitcast(r2_ref[h, tg], BF16)
            e2 = pltpu.bitcast(e2_ref[h, tg], BF16)
            for q in range(quad):
                il = iq * quad + q
                sel = rank2 < rows_bf16(c1[il:il + 1, :])
                gates[q] = gates[q] + jnp.where(sel, e2 * rows_bf16(e1[il:il + 1, :]), jnp.zeros_like(e2))
        for q in range(quad):
            il = iq * quad + q
            a = act_b[tg, il * nk:(il + 1) * nk, :]
            gelu = 0.5 * a * (1.0 + lax.erf(a * (2.0 ** -0.5)))
            p_b[tg, il * nk:(il + 1) * nk, :] = gates[q] * gelu.astype(BF16)

    def stages(act_a, act_b, p_a, p_b):
        pieces = _Pieces([functools.partial(stage_c, p_a, *hv) for hv in halves]
                         + [functools.partial(stage_a, act_a, *hv) for hv in halves])
        n_blocks = ngroups * (ni // quad)
        pieces.every = max(1, PEER_HEADS * n_blocks // len(pieces))
        pieces.phase = 0
        for tg in range(ngroups):
            for iq in range(ni // quad):
                stage_b(act_b, p_b, tg, iq, pieces)
        while pieces:
            pieces.pop(0)()

    @pl.when(k % 2 == 0)
    def _():
        stages(act0_ref, act1_ref, p0_ref, p1_ref)

    @pl.when(k % 2 == 1)
    def _():
        stages(act1_ref, act0_ref, p1_ref, p0_ref)

    @pl.when((tile_c == n_tiles - 1) & (k >= 2))
    def _():
        o_ref[...] = x1_ref[...] + mod_ref[0, 5:6, :] * acc_ref[...].T
        acc_ref[...] = jnp.zeros_like(acc_ref)


def _peer_ffn(h2, u_bf, vt_bf, c1, e1, r2, e2, x1, mod6, seq, tn=512, te=1024):
    t, d = h2.shape
    ne = u_bf.shape[0]
    ngroups = tn // LANES
    n_tiles = ne // te
    n_items = (t // tn) * n_tiles

    def item(k, lag):
        w = jnp.clip(k - lag, 0, n_items - 1)
        return w // n_tiles, w % n_tiles

    stat = pl.BlockSpec((PEER_HEADS, ngroups, PEER_N_KEYS, LANES), lambda k: (0, item(k, 1)[0], 0, 0))
    pstat = pl.BlockSpec((PEER_HEADS, ngroups, PEER_N_KEYS // 2, LANES), lambda k: (0, item(k, 1)[0], 0, 0))
    return pl.pallas_call(
        functools.partial(_peer_kernel, tn=tn, te=te, n_tiles=n_tiles),
        out_shape=jax.ShapeDtypeStruct((t, d), F32),
        grid=(n_items + 2,),
        in_specs=[pl.BlockSpec((tn, d), lambda k: (item(k, 0)[0], 0)),
                  pl.BlockSpec((te, d), lambda k: (item(k, 0)[1], 0)),
                  pl.BlockSpec((te, d), lambda k: (item(k, 2)[1], 0)),
                  stat, stat, pstat, pstat,
                  pl.BlockSpec((tn, d), lambda k: (item(k, 2)[0], 0)),
                  pl.BlockSpec((1, 6, d), lambda k: ((item(k, 2)[0] * tn) // seq, 0, 0))],
        out_specs=pl.BlockSpec((tn, d), lambda k: (item(k, 2)[0], 0)),
        scratch_shapes=[pltpu.VMEM((d, tn), F32),
                        pltpu.VMEM((ngroups, te, LANES), F32), pltpu.VMEM((ngroups, te, LANES), F32),
                        pltpu.VMEM((ngroups, te, LANES), BF16), pltpu.VMEM((ngroups, te, LANES), BF16)],
        compiler_params=_params("arbitrary"),
        name="peer_ffn",
    )(h2, u_bf, vt_bf, c1, e1, r2, e2, x1, mod6)


def _head_block_diag(n):
    idx = jnp.arange(n, dtype=jnp.int32) // HEAD_DIM
    return (idx[:, None] == idx[None, :]).astype(BF16)


def _layer(x, c, ada_w, ada_b, norm1_g, w_in, mu_prev, mu_next, q_norm_g, k_norm_g, w_decay0, w_decay_up,
           a_gate0, a_gate_up, g_up, k_k, k_a, r_k, lnx_g, lnx_b, w_out, norm2_g, peer_w_query,
           peer_sub_keys1, peer_sub_keys2, peer_u, peer_v):
    bsz, s, d = x.shape
    n_att_heads = ATT_WIDTH // HEAD_DIM
    bd512 = _head_block_diag(ATT_WIDTH)
    bd128 = _head_block_diag(LANES)

    mod6 = _ada(c, ada_w, ada_b).reshape(bsz, 6, d)
    gq = jnp.tile(q_norm_g, n_att_heads).reshape(1, ATT_WIDTH)
    gk = jnp.tile(k_norm_g, n_att_heads).reshape(1, ATT_WIDTH)
    qkv, zrw = _inproj(x, mod6, norm1_g.reshape(1, d), w_in.astype(BF16), gq, gk, bd512)

    att = _attention(qkv, n_att_heads)

    zero_w = jnp.zeros((2, LANES - DECAY_LORA, RWKV_WIDTH), F32)
    wup_pad = jnp.concatenate([w_decay_up, zero_w], axis=1).astype(BF16)
    aup_pad = jnp.concatenate([zero_w, a_gate_up], axis=1).astype(BF16)
    pk, plw = _rwkv_prep(zrw, mu_prev.reshape(1, -1), mu_next.reshape(1, -1), wup_pad, aup_pad, g_up.astype(BF16),
                         w_decay0, a_gate0, k_k.reshape(1, -1), k_a.reshape(1, -1), r_k.reshape(1, -1), bd512)
    rp, qp, gm, hm = _rwkv_p1(pk, plw)
    rw = _rwkv_p2(rp, qp, gm, hm, pk, lnx_g.reshape(1, -1), lnx_b.reshape(1, -1), bd128)

    x1, h2, s1t, s2t = _outproj(att, rw, x, mod6, norm2_g.reshape(1, d), w_out.astype(BF16),
                                peer_w_query.astype(BF16), peer_sub_keys1.astype(BF16),
                                peer_sub_keys2.astype(BF16))
    c1, e1, r2, e2 = _peer_topk(s1t, s2t)
    out = _peer_ffn(h2.reshape(bsz * s, d), peer_u.astype(BF16), peer_v.astype(BF16), c1, e1, r2, e2,
                    x1.reshape(bsz * s, d), mod6, s)
    return out.reshape(bsz, s, d)


def kernel(x, c, ada_w, ada_b, norm1_g, w_in, mu_prev, mu_next, q_norm_g, k_norm_g, w_decay0, w_decay_up,
           a_gate0, a_gate_up, g_up, k_k, k_a, r_k, lnx_g, lnx_b, w_out, norm2_g, peer_w_query,
           peer_sub_keys1, peer_sub_keys2, peer_u, peer_v):
    depth = ada_w.shape[0]
    for l in range(depth):
        x = _layer(x, c, ada_w[l], ada_b[l], norm1_g[l], w_in[l], mu_prev[l], mu_next[l], q_norm_g[l],
                   k_norm_g[l], w_decay0[l], w_decay_up[l], a_gate0[l], a_gate_up[l], g_up[l], k_k[l], k_a[l],
                   r_k[l], lnx_g[l], lnx_b[l], w_out[l], norm2_g[l], peer_w_query[l], peer_sub_keys1[l],
                   peer_sub_keys2[l], peer_u[l], peer_v[l])
    return x
```
